```python
import jax, jax.numpy as jnp
from jax import lax
import numpy as np

D_MODEL = 1024
BATCH = 8
SEQ = 2048
DEPTH = 2
DEC_BATCH = 128
DEC_SEQ = 8
PAST_LEN = 16384
PAGE_SIZE = 128

MIX_W = D_MODEL
LRU_W = MIX_W // 2
LRU_BLOCKS = 8
LRU_BLK = LRU_W // LRU_BLOCKS
LRU_C = 8.0
CONV_K = 4
GLA_HEADS = 4
GLA_DV = (MIX_W // 4) // GLA_HEADS
GLA_DK = GLA_DV // 2
GLA_RANK = 16
GLA_TAU = 16.0
GLA_CHUNK = 16
GDN_HEADS = 4
GDN_DK = (MIX_W // 4) // GDN_HEADS
GDN_DV = GDN_DK
GDN_CHUNK = 64
GDN_CONV_W = 2 * GDN_HEADS * GDN_DK + GDN_HEADS * GDN_DV
IN_SPLITS = (LRU_W, LRU_W,
             GLA_HEADS * GLA_DK, GLA_HEADS * GLA_DK, GLA_HEADS * GLA_DV, GLA_RANK, GLA_HEADS * GLA_DV,
             GDN_HEADS * GDN_DK, GDN_HEADS * GDN_DK, GDN_HEADS * GDN_DV, GDN_HEADS, GDN_HEADS,
             GDN_HEADS * GDN_DV)
IN_W = sum(IN_SPLITS)
MEM_LEN = 256
MEM_HEADS = 4
MEM_HD = D_MODEL // MEM_HEADS
D_FF = 7 * D_MODEL // 2
N_EXPERTS = 8
TOP_K = 2
MOE_BLOCK = 128
N_DENSE = (DEPTH + 1) // 2
N_MOE = DEPTH // 2
EPS = 1e-6

kernel_name = 'hymba_lru_gla_gdn_decode_step'

F32 = jnp.float32


def rmsnorm(x, g):
    xf = x.astype(F32)
    y = xf * lax.rsqrt(jnp.mean(xf * xf, axis=-1, keepdims=True) + EPS)
    return (y * g.astype(F32)).astype(x.dtype)


def l2norm(x):
    return x * lax.rsqrt(jnp.sum(x * x, axis=-1, keepdims=True) + EPS)


def split_cols(x, sizes):
    out, start = [], 0
    for s in sizes:
        out.append(x[..., start:start + s])
        start += s
    return out


def causal_conv(x, buf, w):
    L = x.shape[1]
    xp = jnp.concatenate([buf.astype(x.dtype), x], axis=1)
    y = xp[:, 0:L] * w[0]
    for k in range(1, CONV_K):
        y = y + xp[:, k:k + L] * w[k]
    return y, xp[:, L:]


def to_chunks(x, C):
    B, L = x.shape[:2]
    n = -(-L // C)
    x = jnp.pad(x, [(0, 0), (0, n * C - L)] + [(0, 0)] * (x.ndim - 2))
    x = x.reshape((B, n, C) + x.shape[2:])
    return jnp.swapaxes(jnp.moveaxis(x, 1, 0), 2, 3)


def from_chunks(o, L):
    o = jnp.moveaxis(jnp.swapaxes(o, 2, 3), 0, 1)
    B, n, C = o.shape[:3]
    return o.reshape((B, n * C) + o.shape[3:])[:, :L]


def lru_combine(c1, c2):
    a1, b1 = c1
    a2, b2 = c2
    return a1 * a2, a2 * b1 + b2


def gla_chunked(q, k, v, gk, S0):
    L = q.shape[1]
    C = min(GLA_CHUNK, L)
    q = to_chunks(q.astype(F32) * GLA_DK ** -0.5, C)
    k = to_chunks(k.astype(F32), C)
    v = to_chunks(v.astype(F32), C)
    gk = to_chunks(gk.astype(F32), C)
    mask = jnp.tril(jnp.ones((C, C), bool))[:, :, None]

    def step(S, inp):
        qc, kc, vc, gc = inp
        G = jnp.cumsum(gc, axis=2)
        diff = jnp.where(mask, G[:, :, :, None, :] - G[:, :, None, :, :], -jnp.inf)
        A = jnp.einsum('bhid,bhjd,bhijd->bhij', qc, kc, jnp.exp(diff))
        o = jnp.einsum('bhij,bhjv->bhiv', A, vc) + jnp.einsum('bhid,bhdv->bhiv', qc * jnp.exp(G), S)
        G_last = G[:, :, -1:, :]
        S = jnp.exp(G_last[:, :, 0, :])[..., None] * S + jnp.einsum('bhjd,bhjv->bhdv', kc * jnp.exp(G_last - G), vc)
        return S, o

    S, o = lax.scan(step, S0.astype(F32), (q, k, v, gk))
    return from_chunks(o, L), S


def gdn_chunked(q, k, v, g, beta, S0):
    L = q.shape[1]
    C = min(GDN_CHUNK, L)
    q = to_chunks(q.astype(F32) * GDN_DK ** -0.5, C)
    k = to_chunks(k.astype(F32), C)
    v = to_chunks(v.astype(F32), C)
    g = to_chunks(g.astype(F32), C)
    beta = to_chunks(beta.astype(F32), C)
    incl = jnp.tril(jnp.ones((C, C), bool))
    strict = jnp.tril(jnp.ones((C, C), bool), -1)
    eye = jnp.eye(C, dtype=F32)

    def step(S, inp):
        qc, kc, vc, gc, bc = inp
        G = jnp.cumsum(gc, axis=-1)
        decay = jnp.exp(jnp.where(incl, G[..., :, None] - G[..., None, :], -jnp.inf))
        kb = kc * bc[..., None]
        M = jnp.where(strict, jnp.einsum('bhid,bhjd->bhij', kb, kc) * decay, 0.0)
        rhs = jnp.concatenate([vc * bc[..., None], kb * jnp.exp(G)[..., None]], axis=-1)
        sol = lax.linalg.triangular_solve(eye + M, rhs, left_side=True, lower=True, unit_diagonal=True)
        u, w = sol[..., :GDN_DV], sol[..., GDN_DV:]
        v_new = u - jnp.einsum('bhid,bhdv->bhiv', w, S)
        A = jnp.einsum('bhid,bhjd->bhij', qc, kc) * decay
        o = jnp.einsum('bhid,bhdv->bhiv', qc * jnp.exp(G)[..., None], S) + jnp.einsum('bhij,bhjv->bhiv', A, v_new)
        S = jnp.exp(G[..., -1])[..., None, None] * S + jnp.einsum('bhjd,bhjv->bhdv', kc * jnp.exp(G[..., -1:] - G)[..., None], v_new)
        return S, o

    S, o = lax.scan(step, S0.astype(F32), (q, k, v, g, beta))
    return from_chunks(o, L), S


def mixer(h, lru_h0, lru_buf0, gla_S0, gdn_S0, gdn_buf0,
          w_in, lru_conv_w, lru_conv_b, lru_a_w, lru_a_b, lru_x_w, lru_x_b, lru_lam,
          gla_gk_w2, gla_gk_b, gla_norm, gdn_conv_w, gdn_a_log, gdn_dt_bias, gdn_norm, w_out):
    B, L, _ = h.shape
    dt = h.dtype
    (lru_x, lru_gate, gla_q, gla_k, gla_v, gla_lr, gla_g,
     gdn_q, gdn_k, gdn_v, gdn_b, gdn_a, gdn_z) = split_cols(h @ w_in, IN_SPLITS)
    xc, lru_buf = causal_conv(lru_x, lru_buf0, lru_conv_w)
    xc = (xc + lru_conv_b).astype(F32)
    xb = xc.reshape(B, L, LRU_BLOCKS, LRU_BLK)
    r = jax.nn.sigmoid(jnp.einsum('blnc,ncd->blnd', xb, lru_a_w.astype(F32)).reshape(B, L, LRU_W) + lru_a_b)
    ig = jax.nn.sigmoid(jnp.einsum('blnc,ncd->blnd', xb, lru_x_w.astype(F32)).reshape(B, L, LRU_W) + lru_x_b)
    log_a = -LRU_C * r * jax.nn.softplus(-lru_lam.astype(F32))
    a = jnp.exp(log_a)
    b = jnp.sqrt(-jnp.expm1(2.0 * log_a)) * (ig * xc)
    b = b.at[:, 0].add(a[:, 0] * lru_h0.astype(F32))
    _, hs = lax.associative_scan(lru_combine, (a, b), axis=1)
    y_lru = hs * jax.nn.gelu(lru_gate.astype(F32))
    gk = jax.nn.log_sigmoid((gla_lr @ gla_gk_w2 + gla_gk_b).astype(F32)) / GLA_TAU
    o_gla, gla_S = gla_chunked(gla_q.reshape(B, L, GLA_HEADS, GLA_DK), gla_k.reshape(B, L, GLA_HEADS, GLA_DK),
                               gla_v.reshape(B, L, GLA_HEADS, GLA_DV), gk.reshape(B, L, GLA_HEADS, GLA_DK), gla_S0)
    y_gla = rmsnorm(o_gla, gla_norm).reshape(B, L, -1) * jax.nn.silu(gla_g.astype(F32))
    qkv, gdn_buf = causal_conv(jnp.concatenate([gdn_q, gdn_k, gdn_v], axis=-1), gdn_buf0, gdn_conv_w)
    qkv = jax.nn.silu(qkv.astype(F32))
    cq, ck, cv = split_cols(qkv, (GDN_HEADS * GDN_DK, GDN_HEADS * GDN_DK, GDN_HEADS * GDN_DV))
    cq = l2norm(cq.reshape(B, L, GDN_HEADS, GDN_DK))
    ck = l2norm(ck.reshape(B, L, GDN_HEADS, GDN_DK))
    cv = cv.reshape(B, L, GDN_HEADS, GDN_DV)
    beta = jax.nn.sigmoid(gdn_b.astype(F32))
    g = -jnp.exp(gdn_a_log.astype(F32)) * jax.nn.softplus(gdn_a.astype(F32) + gdn_dt_bias.astype(F32))
    o_gdn, gdn_S = gdn_chunked(cq, ck, cv, g, beta, gdn_S0)
    y_gdn = rmsnorm(o_gdn, gdn_norm).reshape(B, L, -1) * jax.nn.silu(gdn_z.astype(F32))
    y = jnp.concatenate([y_lru, y_gla, y_gdn], axis=-1).astype(dt) @ w_out
    return y, (hs[:, -1].astype(dt), lru_buf.astype(dt), gla_S.astype(dt), gdn_S.astype(dt), gdn_buf.astype(dt))


def mem_kv(mem, g, w_k, w_v):
    B, M, _ = mem.shape
    m = rmsnorm(mem, g)
    return (m @ w_k).reshape(B, M, MEM_HEADS, MEM_HD), (m @ w_v).reshape(B, M, MEM_HEADS, MEM_HD)


def mem_attend(h, k, v, w_q, w_o):
    B, L, _ = h.shape
    q = (h @ w_q).reshape(B, L, MEM_HEADS, MEM_HD).astype(F32)
    s = jnp.einsum('blhd,bmhd->bhlm', q, k.astype(F32)) * MEM_HD ** -0.5
    p = jax.nn.softmax(s, axis=-1)
    o = jnp.einsum('bhlm,bmhd->blhd', p, v.astype(F32)).reshape(B, L, D_MODEL)
    return o.astype(h.dtype) @ w_o


def swiglu(h, w_g, w_u, w_d):
    return (jax.nn.silu(h @ w_g) * (h @ w_u)) @ w_d


def moe_ffn(h, w_router, w_gate, w_up, w_down):
    B, L, D = h.shape
    T = B * L
    x = h.reshape(T, D)
    logits = (x @ w_router).astype(F32)
    top_logit, top_idx = lax.top_k(logits, TOP_K)
    gates = jax.nn.softmax(top_logit, axis=-1)
    e_flat = top_idx.reshape(-1)
    tok_flat = jnp.repeat(jnp.arange(T, dtype=jnp.int32), TOP_K)
    g_flat = gates.reshape(-1)
    order = jnp.argsort(e_flat)
    e_sorted = e_flat[order]
    counts = jnp.bincount(e_flat, length=N_EXPERTS)
    padded = (counts + MOE_BLOCK - 1) // MOE_BLOCK * MOE_BLOCK
    pad_end = jnp.cumsum(padded)
    start = jnp.cumsum(counts) - counts
    slot = (pad_end - padded)[e_sorted] + jnp.arange(T * TOP_K) - start[e_sorted]
    n_blocks = -(-(T * TOP_K) // MOE_BLOCK) + N_EXPERTS
    P = n_blocks * MOE_BLOCK
    slot_tok = jnp.full((P,), T, jnp.int32).at[slot].set(tok_flat[order])
    slot_gate = jnp.zeros((P,), F32).at[slot].set(g_flat[order])
    blk_exp = jnp.minimum(jnp.searchsorted(pad_end, jnp.arange(n_blocks) * MOE_BLOCK, side='right'), N_EXPERTS - 1)
    xs = jnp.concatenate([x, jnp.zeros((1, D), x.dtype)], axis=0)[slot_tok].reshape(n_blocks, MOE_BLOCK, D)

    def expert_block(args):
        xb, e = args
        return (jax.nn.silu(xb @ w_gate[e]) * (xb @ w_up[e])) @ w_down[e]

    ys = lax.map(expert_block, (xs, blk_exp)).reshape(P, D).astype(F32) * slot_gate[:, None]
    out = jnp.zeros((T + 1, D), F32).at[slot_tok].add(ys)[:T]
    return out.astype(h.dtype).reshape(B, L, D)


def setup_inputs(seed: int = 0) -> dict:
    key = jax.random.key(seed)
    ks = iter(jax.random.split(key, 48))

    def nrm(shape, scale=1.0):
        return jax.random.normal(next(ks), shape, F32) * scale

    def gain(shape):
        return 1.0 + nrm(shape, 0.02)

    def unif(shape, lo, hi):
        return jax.random.uniform(next(ks), shape, F32, lo, hi)

    D = D_MODEL
    a0 = unif((DEPTH, LRU_W), 0.9, 0.999)
    p = a0 ** (1.0 / LRU_C)
    dtv = jnp.exp(unif((DEPTH, GDN_HEADS), float(np.log(1e-3)), float(np.log(1e-1))))
    return {
        'x_prompt': nrm((BATCH, SEQ, D)),
        'x_sample': nrm((DEC_BATCH, DEC_SEQ, D)),
        'mem_prompt': nrm((BATCH, MEM_LEN, D)),
        'state_lru_h': nrm((DEPTH, DEC_BATCH, LRU_W), 0.5),
        'state_lru_conv': nrm((DEPTH, DEC_BATCH, CONV_K - 1, LRU_W)),
        'state_gla': nrm((DEPTH, DEC_BATCH, GLA_HEADS, GLA_DK, GLA_DV), 0.5),
        'state_gdn': nrm((DEPTH, DEC_BATCH, GDN_HEADS, GDN_DK, GDN_DV), 0.3),
        'state_gdn_conv': nrm((DEPTH, DEC_BATCH, CONV_K - 1, GDN_CONV_W)),
        'cache_mem_k': nrm((DEPTH, DEC_BATCH, MEM_LEN, MEM_HEADS, MEM_HD)),
        'cache_mem_v': nrm((DEPTH, DEC_BATCH, MEM_LEN, MEM_HEADS, MEM_HD)),
        'norm_mix': gain((DEPTH, D)),
        'w_in': nrm((DEPTH, D, IN_W), D ** -0.5),
        'lru_conv_w': nrm((DEPTH, CONV_K, LRU_W), CONV_K ** -0.5),
        'lru_conv_b': nrm((DEPTH, LRU_W), 0.02),
        'lru_a_w': nrm((DEPTH, LRU_BLOCKS, LRU_BLK, LRU_BLK), LRU_BLK ** -0.5),
        'lru_a_b': nrm((DEPTH, LRU_W), 0.02),
        'lru_x_w': nrm((DEPTH, LRU_BLOCKS, LRU_BLK, LRU_BLK), LRU_BLK ** -0.5),
        'lru_x_b': nrm((DEPTH, LRU_W), 0.02),
        'lru_lam': jnp.log(p) - jnp.log1p(-p),
        'gla_gk_w2': nrm((DEPTH, GLA_RANK, GLA_HEADS * GLA_DK), GLA_RANK ** -0.5),
        'gla_gk_b': nrm((DEPTH, GLA_HEADS * GLA_DK), 0.02),
        'gla_norm': gain((DEPTH, GLA_DV)),
        'gdn_conv_w': nrm((DEPTH, CONV_K, GDN_CONV_W), CONV_K ** -0.5),
        'gdn_a_log': jnp.log(unif((DEPTH, GDN_HEADS), 1.0, 16.0)),
        'gdn_dt_bias': dtv + jnp.log(-jnp.expm1(-dtv)),
        'gdn_norm': gain((DEPTH, GDN_DV)),
        'w_out': nrm((DEPTH, MIX_W, D), MIX_W ** -0.5),
        'norm_xq': gain((DEPTH, D)),
        'norm_mem': gain((DEPTH, D)),
        'w_mq': nrm((DEPTH, D, D), D ** -0.5),
        'w_mk': nrm((DEPTH, D, D), D ** -0.5),
        'w_mv': nrm((DEPTH, D, D), D ** -0.5),
        'w_mo': nrm((DEPTH, D, D), D ** -0.5),
        'norm_ffn': gain((DEPTH, D)),
        'w_ff_gate': nrm((N_DENSE, D, D_FF), D ** -0.5),
        'w_ff_up': nrm((N_DENSE, D, D_FF), D ** -0.5),
        'w_ff_down': nrm((N_DENSE, D_FF, D), D_FF ** -0.5),
        'w_router': nrm((N_MOE, D, N_EXPERTS), D ** -0.5),
        'w_e_gate': nrm((N_MOE, N_EXPERTS, D, D_FF), D ** -0.5),
        'w_e_up': nrm((N_MOE, N_EXPERTS, D, D_FF), D ** -0.5),
        'w_e_down': nrm((N_MOE, N_EXPERTS, D_FF, D), D_FF ** -0.5),
        'norm_final': gain((D,)),
    }


def reference(x_prompt, x_sample, mem_prompt, state_lru_h, state_lru_conv, state_gla, state_gdn,
              state_gdn_conv, cache_mem_k, cache_mem_v, norm_mix, w_in, lru_conv_w, lru_conv_b,
              lru_a_w, lru_a_b, lru_x_w, lru_x_b, lru_lam, gla_gk_w2, gla_gk_b, gla_norm,
              gdn_conv_w, gdn_a_log, gdn_dt_bias, gdn_norm, w_out, norm_xq, norm_mem, w_mq, w_mk,
              w_mv, w_mo, norm_ffn, w_ff_gate, w_ff_up, w_ff_down, w_router, w_e_gate, w_e_up,
              w_e_down, norm_final):
    def run(x, mem_k, mem_v, states):
        new_states = []
        for l in range(DEPTH):
            y, st = mixer(rmsnorm(x, norm_mix[l]), *states[l],
                          w_in[l], lru_conv_w[l], lru_conv_b[l], lru_a_w[l], lru_a_b[l], lru_x_w[l],
                          lru_x_b[l], lru_lam[l], gla_gk_w2[l], gla_gk_b[l], gla_norm[l], gdn_conv_w[l],
                          gdn_a_log[l], gdn_dt_bias[l], gdn_norm[l], w_out[l])
            x = x + y
            x = x + mem_attend(rmsnorm(x, norm_xq[l]), mem_k[l], mem_v[l], w_mq[l], w_mo[l])
            h = rmsnorm(x, norm_ffn[l])
            j = l // 2
            if l % 2 == 0:
                x = x + swiglu(h, w_ff_gate[j], w_ff_up[j], w_ff_down[j])
            else:
                x = x + moe_ffn(h, w_router[j], w_e_gate[j], w_e_up[j], w_e_down[j])
            new_states.append(st)
        return rmsnorm(x, norm_final), [jnp.stack(s) for s in zip(*new_states)]

    Bp = x_prompt.shape[0]
    zero_state = (jnp.zeros((Bp, LRU_W), F32), jnp.zeros((Bp, CONV_K - 1, LRU_W), F32),
                  jnp.zeros((Bp, GLA_HEADS, GLA_DK, GLA_DV), F32),
                  jnp.zeros((Bp, GDN_HEADS, GDN_DK, GDN_DV), F32),
                  jnp.zeros((Bp, CONV_K - 1, GDN_CONV_W), F32))
    mkv = [mem_kv(mem_prompt, norm_mem[l], w_mk[l], w_mv[l]) for l in range(DEPTH)]
    pk = [kv[0] for kv in mkv]
    pv = [kv[1] for kv in mkv]
    y_prompt, p_st = run(x_prompt, pk, pv, [zero_state] * DEPTH)
    p_lru_h, p_lru_conv, p_gla, p_gdn, p_gdn_conv = p_st
    p_mem_k = jnp.stack(pk)
    p_mem_v = jnp.stack(pv)

    s_in = [(state_lru_h[l], state_lru_conv[l], state_gla[l], state_gdn[l], state_gdn_conv[l])
            for l in range(DEPTH)]
    y_sample, s_st = run(x_sample, cache_mem_k, cache_mem_v, s_in)
    s_lru_h, s_lru_conv, s_gla, s_gdn, s_gdn_conv = s_st

    return (y_prompt, y_sample, p_lru_h, p_lru_conv, p_gla, p_gdn, p_gdn_conv, p_mem_k, p_mem_v,
            s_lru_h, s_lru_conv, s_gla, s_gdn, s_gdn_conv)
```

```python
import functools

import jax
import jax.numpy as jnp
from jax import lax
from jax.experimental import pallas as pl
from jax.experimental.pallas import tpu as pltpu

F32 = jnp.float32
BF16 = jnp.bfloat16
EPS = 1e-6

D_MODEL = 1024
LRU_W = 512
LRU_BLOCKS = 8
LRU_C = 8.0
CONV_K = 4
GLA_HEADS = 4
GLA_DK = 32
GLA_DV = 64
GLA_RANK = 16
GLA_TAU = 16.0
GLA_SUB = 16
GDN_HEADS = 4
GDN_DK = 64
GDN_DV = 64
MIX_CHUNK = 64
MEM_HEADS = 4
MEM_HD = 256
N_EXPERTS = 8
LANES = 128
VMEM_LIMIT_BYTES = 48 * 1024 * 1024


def _cparams(n_axes):
    return pltpu.CompilerParams(dimension_semantics=("arbitrary",) * n_axes,
                                vmem_limit_bytes=VMEM_LIMIT_BYTES)


def _mm(a, b):
    return jnp.dot(a.astype(BF16), b.astype(BF16), preferred_element_type=F32)


def _mm_nt(a, b):
    return lax.dot_general(a.astype(BF16), b.astype(BF16), (((1,), (1,)), ((), ())),
                           preferred_element_type=F32)


def _mm_tn(a, b):
    return lax.dot_general(a.astype(BF16), b.astype(BF16), (((0,), (0,)), ((), ())),
                           preferred_element_type=F32)


def _rms_rows(x, g):
    ms = jnp.mean(x * x, axis=-1, keepdims=True)
    return (x * lax.rsqrt(ms + EPS)) * g


def _softplus(x):
    return jnp.maximum(x, 0.0) + jnp.log1p(jnp.exp(-jnp.abs(x)))


def _sigmoid(x):
    return 1.0 / (1.0 + jnp.exp(-x))


def _silu(x):
    return x * _sigmoid(x)


def _gelu_tanh(x):
    c = 0.7978845608028654
    return x * (0.5 * (1.0 + jnp.tanh(c * (x + 0.044715 * (x * x * x)))))


def _seg_cumsum_rows(x, seg):
    rows = x.shape[0]
    tpos = lax.broadcasted_iota(jnp.int32, (rows, 1), 0) & (seg - 1)
    d = 1
    while d < seg:
        x = x + jnp.where(tpos >= d, pltpu.roll(x, d, axis=0), 0.0)
        d *= 2
    return x


def _head_rms(o, gain, n_heads, width):
    lane_head = lax.broadcasted_iota(jnp.int32, (1, n_heads * width), 1) // width
    sq = o * o
    inv = jnp.zeros_like(o)
    for h in range(n_heads):
        m = lane_head == h
        ms = jnp.sum(jnp.where(m, sq, 0.0), axis=-1, keepdims=True) * (1.0 / width)
        inv = jnp.where(m, lax.rsqrt(ms + EPS), inv)
    return (o * inv) * gain


def _stack_heads(x, n_heads, width):
    c = x.shape[0]
    t = jnp.concatenate([x] * n_heads, axis=0)
    row_head = lax.broadcasted_iota(jnp.int32, (n_heads * c, 1), 0) // c
    lane_head = lax.broadcasted_iota(jnp.int32, (1, n_heads * width), 1) // width
    return jnp.where(row_head == lane_head, t, 0.0)


def _unstack_heads(x, n_heads):
    c = x.shape[0] // n_heads
    o = x[0:c]
    for h in range(1, n_heads):
        o = o + x[h * c:(h + 1) * c]
    return o


def _rms_matmul_kernel(x_ref, g_ref, w_ref, *o_refs):
    h = _rms_rows(x_ref[...], g_ref[...]).astype(BF16)
    start = 0
    for o_ref in o_refs:
        n = o_ref.shape[1]
        o_ref[...] = jnp.dot(h, w_ref[:, start:start + n], preferred_element_type=F32)
        start += n


def _rms_matmul(x, g, w, widths, name, tm=512):
    t, d = x.shape
    n = w.shape[1]
    tm = min(tm, t)
    return pl.pallas_call(
        _rms_matmul_kernel,
        grid=(t // tm,),
        in_specs=[pl.BlockSpec((tm, d), lambda i: (i, 0)),
                  pl.BlockSpec((1, d), lambda i: (0, 0)),
                  pl.BlockSpec((d, n), lambda i: (0, 0))],
        out_specs=[pl.BlockSpec((tm, wd), lambda i: (i, 0)) for wd in widths],
        out_shape=[jax.ShapeDtypeStruct((t, wd), F32) for wd in widths],
        compiler_params=_cparams(1),
        name=name,
    )(x, g.reshape(1, d), w)


def _lru_kernel(z_ref, h0_ref, buf0_ref, cw_ref, cb_ref, aw_ref, ab_ref, xw_ref, xb_ref, lam_ref,
                y_ref, hout_ref, bufout_ref, xs_ref, hc_ref, *, bt, tl, nt):
    w = LRU_W
    rows = bt * tl
    j = pl.program_id(0) % nt

    @pl.when(j == 0)
    def _():
        xs_ref[:, 5:8, :] = buf0_ref[...]
        hc_ref[...] = h0_ref[...]

    xs_ref[:, 8:, :] = z_ref[:, :w].reshape(bt, tl, w)
    gate = z_ref[:, w:]
    cw = cw_ref[...]
    xc = cb_ref[...] + xs_ref[:, 5:5 + tl, :] * cw[0:1]
    for k in range(1, CONV_K):
        xc = xc + xs_ref[:, 5 + k:5 + k + tl, :] * cw[k:k + 1]
    tail = xs_ref[:, 5 + tl:8 + tl, :]
    xs_ref[:, 5:8, :] = tail
    bufout_ref[...] = tail

    xc = xc.reshape(rows, w)
    r = _sigmoid(_mm(xc, aw_ref[...]) + ab_ref[...])
    ig = _sigmoid(_mm(xc, xw_ref[...]) + xb_ref[...])
    log_a = (-LRU_C * r) * _softplus(-lam_ref[...])
    a = jnp.exp(log_a)
    th = jnp.tanh(log_a)
    b = jnp.sqrt((-2.0 * th) / (1.0 - th)) * (ig * xc)

    tpos = lax.broadcasted_iota(jnp.int32, (rows, 1), 0) & (tl - 1)
    d = 1
    while d < tl:
        m = tpos >= d
        b = jnp.where(m, a * pltpu.roll(b, d, axis=0) + b, b)
        a = jnp.where(m, a * pltpu.roll(a, d, axis=0), a)
        d *= 2
    hc = jnp.broadcast_to(hc_ref[...], (bt, tl, w)).reshape(rows, w)
    h = b + a * hc
    hlast = h.reshape(bt, tl, w)[:, tl - 1:tl, :]
    hc_ref[...] = hlast
    hout_ref[...] = hlast
    y_ref[...] = h * _gelu_tanh(gate)


def _lru_mixer(z, h0, buf0, cw, cb, aw, ab, xw, xb, lam, batch, seq):
    w = LRU_W
    tl = min(seq, 256)
    bt = min(batch, max(1, 256 // seq))
    nt = seq // tl
    rows = bt * tl
    grid = (batch * seq // rows,)
    if nt > 1:
        sidx = lambda i: (i // nt, 0, 0)
    else:
        sidx = lambda i: (i, 0, 0)
    full2 = lambda i: (0, 0)
    return pl.pallas_call(
        functools.partial(_lru_kernel, bt=bt, tl=tl, nt=nt),
        grid=grid,
        in_specs=[pl.BlockSpec((rows, 2 * w), lambda i: (i, 0)),
                  pl.BlockSpec((bt, 1, w), sidx),
                  pl.BlockSpec((bt, CONV_K - 1, w), sidx),
                  pl.BlockSpec((CONV_K, w), full2),
                  pl.BlockSpec((1, w), full2),
                  pl.BlockSpec((w, w), full2),
                  pl.BlockSpec((1, w), full2),
                  pl.BlockSpec((w, w), full2),
                  pl.BlockSpec((1, w), full2),
                  pl.BlockSpec((1, w), full2)],
        out_specs=[pl.BlockSpec((rows, w), lambda i: (i, 0)),
                   pl.BlockSpec((bt, 1, w), sidx),
                   pl.BlockSpec((bt, CONV_K - 1, w), sidx)],
        out_shape=[jax.ShapeDtypeStruct((batch * seq, w), F32),
                   jax.ShapeDtypeStruct((batch, 1, w), F32),
                   jax.ShapeDtypeStruct((batch, CONV_K - 1, w), F32)],
        scratch_shapes=[pltpu.VMEM((bt, 8 + tl, w), F32), pltpu.VMEM((bt, 1, w), F32)],
        compiler_params=_cparams(1),
        name="lru_mixer",
    )(z, h0.reshape(batch, 1, w), buf0, cw, cb.reshape(1, w), aw, ab.reshape(1, w), xw,
      xb.reshape(1, w), lam.reshape(1, w))


def _gla_kernel(z_ref, s0_ref, w2_ref, gb_ref, gn_ref, y_ref, sout_ref, s_ref, *, c, sc, nt):
    nh = GLA_HEADS
    kw = nh * GLA_DK
    vw = nh * GLA_DV
    j = pl.program_id(0) % nt

    @pl.when(j == 0)
    def _():
        s_ref[...] = s0_ref[...]

    q = z_ref[:, 0:kw] * (GLA_DK ** -0.5)
    k = z_ref[:, kw:2 * kw]
    v = z_ref[:, 2 * kw:2 * kw + vw]
    g = z_ref[:, 2 * kw + vw:2 * kw + 2 * vw]
    lr = z_ref[:, 2 * kw + 2 * vw:]
    gk = -_softplus(-(_mm(lr, w2_ref[...]) + gb_ref[...])) / GLA_TAU
    gcum = _seg_cumsum_rows(gk, sc)
    qp = q * jnp.exp(gcum)
    kp = k * jnp.exp(-gcum)

    qs = _stack_heads(qp, nh, GLA_DK)
    ks = _stack_heads(kp, nh, GLA_DK)
    a = _mm_nt(qs, ks)
    ri = lax.broadcasted_iota(jnp.int32, (nh * c, nh * c), 0)
    ci = lax.broadcasted_iota(jnp.int32, (nh * c, nh * c), 1)
    a = jnp.where((ri // sc == ci // sc) & (ci <= ri), a, 0.0)
    o = _unstack_heads(_mm(a, _stack_heads(v, nh, GLA_DV)), nh)

    s = s_ref[...]
    eye = (lax.broadcasted_iota(jnp.int32, (kw, kw), 0) ==
           lax.broadcasted_iota(jnp.int32, (kw, kw), 1))
    bd = (lax.broadcasted_iota(jnp.int32, (kw, vw), 0) // GLA_DK ==
          lax.broadcasted_iota(jnp.int32, (kw, vw), 1) // GLA_DV)
    o_inter = []
    for i in range(c // sc):
        lo, hi = i * sc, (i + 1) * sc
        o_inter.append(_mm(qp[lo:hi], s))
        glast = gcum[hi - 1:hi]
        kpp = k[lo:hi] * jnp.exp(glast - gcum[lo:hi])
        u = _mm_tn(kpp, v[lo:hi])
        dcol = jnp.sum(jnp.where(eye, jnp.exp(glast), 0.0), axis=1, keepdims=True)
        s = dcol * s + jnp.where(bd, u, 0.0)
    s_ref[...] = s
    sout_ref[...] = s
    o = o + jnp.concatenate(o_inter, axis=0)
    y_ref[...] = _head_rms(o, gn_ref[...], nh, GLA_DV) * _silu(g)


def _gla_mixer(z, s0, w2, gb, gn, batch, seq):
    c = min(seq, MIX_CHUNK)
    sc = min(c, GLA_SUB)
    nt = seq // c
    kw, vw = GLA_HEADS * GLA_DK, GLA_HEADS * GLA_DV
    zw = z.shape[1]
    full2 = lambda i: (0, 0)
    sidx = lambda i: (i // nt, 0, 0)
    return pl.pallas_call(
        functools.partial(_gla_kernel, c=c, sc=sc, nt=nt),
        grid=(batch * nt,),
        in_specs=[pl.BlockSpec((c, zw), lambda i: (i, 0)),
                  pl.BlockSpec((None, kw, vw), sidx),
                  pl.BlockSpec((LANES, kw), full2),
                  pl.BlockSpec((1, kw), full2),
                  pl.BlockSpec((1, vw), full2)],
        out_specs=[pl.BlockSpec((c, vw), lambda i: (i, 0)),
                   pl.BlockSpec((None, kw, vw), sidx)],
        out_shape=[jax.ShapeDtypeStruct((batch * seq, vw), F32),
                   jax.ShapeDtypeStruct((batch, kw, vw), F32)],
        scratch_shapes=[pltpu.VMEM((kw, vw), F32)],
        compiler_params=_cparams(1),
        name="gla_mixer",
    )(z, s0, w2, gb.reshape(1, kw), gn.reshape(1, vw))


def _gdn_kernel(z_ref, s0_ref, buf0_ref, cw_ref, alog_ref, dtb_ref, gn_ref,
                y_ref, sout_ref, bufout_ref, xs_ref, s_ref, *, c, nt):
    nh = GDN_HEADS
    hw = nh * GDN_DK
    cw3 = 3 * hw
    j = pl.program_id(0) % nt

    @pl.when(j == 0)
    def _():
        xs_ref[5:8, :] = buf0_ref[...]
        s_ref[...] = s0_ref[...]

    xs_ref[8:, :] = z_ref[:, 0:cw3]
    cw = cw_ref[...]
    qkv = xs_ref[5:5 + c, :] * cw[0:1]
    for kk in range(1, CONV_K):
        qkv = qkv + xs_ref[5 + kk:5 + kk + c, :] * cw[kk:kk + 1]
    tail = xs_ref[5 + c:8 + c, :]
    xs_ref[5:8, :] = tail
    bufout_ref[...] = tail
    qkv = _silu(qkv)
    zg = z_ref[:, cw3:cw3 + hw]
    sm = z_ref[:, cw3 + hw:]

    lane_head = lax.broadcasted_iota(jnp.int32, (1, hw), 1) // GDN_DK

    def l2n(x):
        sq = x * x
        inv = jnp.zeros_like(x)
        for h in range(nh):
            m = lane_head == h
            ss = jnp.sum(jnp.where(m, sq, 0.0), axis=-1, keepdims=True)
            inv = jnp.where(m, lax.rsqrt(ss + EPS), inv)
        return x * inv

    q = l2n(qkv[:, 0:hw]) * (GDN_DK ** -0.5)
    k = l2n(qkv[:, hw:2 * hw])
    v = qkv[:, 2 * hw:3 * hw]
    beta = _sigmoid(sm)
    glog = -jnp.exp(alog_ref[...]) * _softplus(sm + dtb_ref[...])
    gcum = _seg_cumsum_rows(glog, c)

    n = nh * c
    bcol = jnp.concatenate([beta[:, h:h + 1] for h in range(nh)], axis=0)
    gcol = jnp.concatenate([gcum[:, nh + h:nh + h + 1] for h in range(nh)], axis=0)
    glast = jnp.concatenate(
        [jnp.broadcast_to(gcum[c - 1:c, nh + h:nh + h + 1], (c, 1)) for h in range(nh)], axis=0)
    ri = lax.broadcasted_iota(jnp.int32, (n, n), 0)
    ci = lax.broadcasted_iota(jnp.int32, (n, n), 1)
    grow = jnp.sum(jnp.where(ri == ci, gcol, 0.0), axis=0, keepdims=True)
    same = ri // c == ci // c
    incl = same & (ci <= ri)
    strict = same & (ci < ri)
    dec = jnp.where(incl, jnp.exp(jnp.where(incl, gcol - grow, 0.0)), 0.0)

    ks = _stack_heads(k, nh, GDN_DK)
    qs = _stack_heads(q, nh, GDN_DK)
    vs = _stack_heads(v, nh, GDN_DV)
    kkt = _mm_nt(ks, ks)
    qkt = _mm_nt(qs, ks)

    x = jnp.where(strict, -(bcol * kkt) * dec, 0.0)
    tinv = jnp.where(ri == ci, 1.0, 0.0) + x
    p = x
    span = 2
    while span < c:
        p = _mm(p, p)
        tinv = tinv + _mm(tinv, p)
        span *= 2

    s = s_ref[...]
    u = _mm(tinv, vs * bcol)
    w = _mm(tinv, ks * (bcol * jnp.exp(gcol)))
    vnew = u - _mm(w, s)
    a = jnp.where(incl, qkt * dec, 0.0)
    o = _unstack_heads(_mm(qs * jnp.exp(gcol), s) + _mm(a, vnew), nh)

    sdec = jnp.concatenate(
        [jnp.broadcast_to(jnp.exp(gcum[c - 1:c, nh + h:nh + h + 1]), (GDN_DK, 1)) for h in range(nh)],
        axis=0)
    s = sdec * s + _mm_tn(ks * jnp.exp(glast - gcol), vnew)
    s_ref[...] = s
    sout_ref[...] = s
    y_ref[...] = _head_rms(o, gn_ref[...], nh, GDN_DV) * _silu(zg)


def _gdn_mixer(z, s0, buf0, cw, alog, dtb, gn, batch, seq):
    c = min(seq, MIX_CHUNK)
    nt = seq // c
    hw = GDN_HEADS * GDN_DK
    zw = z.shape[1]
    full2 = lambda i: (0, 0)
    sidx = lambda i: (i // nt, 0, 0)
    return pl.pallas_call(
        functools.partial(_gdn_kernel, c=c, nt=nt),
        grid=(batch * nt,),
        in_specs=[pl.BlockSpec((c, zw), lambda i: (i, 0)),
                  pl.BlockSpec((None, hw, hw), sidx),
                  pl.BlockSpec((None, CONV_K - 1, 3 * hw), sidx),
                  pl.BlockSpec((CONV_K, 3 * hw), full2),
                  pl.BlockSpec((1, LANES), full2),
                  pl.BlockSpec((1, LANES), full2),
                  pl.BlockSpec((1, hw), full2)],
        out_specs=[pl.BlockSpec((c, hw), lambda i: (i, 0)),
                   pl.BlockSpec((None, hw, hw), sidx),
                   pl.BlockSpec((None, CONV_K - 1, 3 * hw), sidx)],
        out_shape=[jax.ShapeDtypeStruct((batch * seq, hw), F32),
                   jax.ShapeDtypeStruct((batch, hw, hw), F32),
                   jax.ShapeDtypeStruct((batch, CONV_K - 1, 3 * hw), F32)],
        scratch_shapes=[pltpu.VMEM((8 + c, 3 * hw), F32), pltpu.VMEM((hw, hw), F32)],
        compiler_params=_cparams(1),
        name="gdn_mixer",
    )(z, s0, buf0, cw, alog, dtb, gn.reshape(1, hw))


def _outproj_kernel(x_ref, yl_ref, yg_ref, yd_ref, wo_ref, gq_ref, wq_ref, xn_ref, q_ref):
    lw = LRU_W
    gw = GLA_HEADS * GLA_DV
    y = _mm(yl_ref[...], wo_ref[0:lw, :])
    y = y + _mm(yg_ref[...], wo_ref[lw:lw + gw, :])
    y = y + _mm(yd_ref[...], wo_ref[lw + gw:, :])
    xn = x_ref[...] + y
    xn_ref[...] = xn
    q_ref[...] = _mm(_rms_rows(xn, gq_ref[...]), wq_ref[...])


def _outproj(x, yl, yg, yd, wo, gq, wq, tm=512):
    t, d = x.shape
    tm = min(tm, t)
    row = lambda i: (i, 0)
    full2 = lambda i: (0, 0)
    return pl.pallas_call(
        _outproj_kernel,
        grid=(t // tm,),
        in_specs=[pl.BlockSpec((tm, d), row),
                  pl.BlockSpec((tm, yl.shape[1]), row),
                  pl.BlockSpec((tm, yg.shape[1]), row),
                  pl.BlockSpec((tm, yd.shape[1]), row),
                  pl.BlockSpec((d, d), full2),
                  pl.BlockSpec((1, d), full2),
                  pl.BlockSpec((d, d), full2)],
        out_specs=[pl.BlockSpec((tm, d), row), pl.BlockSpec((tm, d), row)],
        out_shape=[jax.ShapeDtypeStruct((t, d), F32), jax.ShapeDtypeStruct((t, d), F32)],
        compiler_params=_cparams(1),
        name="outproj_qproj",
    )(x, yl, yg, yd, wo, gq.reshape(1, d), wq)


def _attn_kernel(x_ref, q_ref, k_ref, v_ref, wo_ref, o_ref):
    hd = MEM_HD
    acc = x_ref[...]
    for h in range(MEM_HEADS):
        sl = slice(h * hd, (h + 1) * hd)
        s = _mm_nt(q_ref[:, sl], k_ref[:, sl]) * (hd ** -0.5)
        m = jnp.max(s, axis=-1, keepdims=True)
        p = jnp.exp(s - m)
        l = jnp.sum(p, axis=-1, keepdims=True)
        oh = _mm(p, v_ref[:, sl]) / l
        acc = acc + _mm(oh, wo_ref[sl, :])
    o_ref[...] = acc


def _attention(x, q, k, v, wo, batch, seq):
    t, d = x.shape
    tl = min(seq, 512)
    nl = seq // tl
    m = k.shape[1]
    row = lambda b, j: (b * nl + j, 0)
    return pl.pallas_call(
        _attn_kernel,
        grid=(batch, nl),
        in_specs=[pl.BlockSpec((tl, d), row),
                  pl.BlockSpec((tl, d), row),
                  pl.BlockSpec((None, m, d), lambda b, j: (b, 0, 0)),
                  pl.BlockSpec((None, m, d), lambda b, j: (b, 0, 0)),
                  pl.BlockSpec((d, d), lambda b, j: (0, 0))],
        out_specs=pl.BlockSpec((tl, d), row),
        out_shape=jax.ShapeDtypeStruct((t, d), F32),
        compiler_params=_cparams(2),
        name="mem_attention",
    )(x, q, k, v, wo)


def _ffn_kernel(x_ref, g_ref, wg_ref, wu_ref, wd_ref, o_ref, h_ref, acc_ref):
    f = pl.program_id(1)

    @pl.when(f == 0)
    def _():
        h_ref[...] = _rms_rows(x_ref[...], g_ref[...]).astype(BF16)
        acc_ref[...] = jnp.zeros_like(acc_ref)

    h = h_ref[...]
    a = jnp.dot(h, wg_ref[...], preferred_element_type=F32)
    u = jnp.dot(h, wu_ref[...], preferred_element_type=F32)
    acc_ref[...] += _mm(_silu(a) * u, wd_ref[...])

    @pl.when(f == pl.num_programs(1) - 1)
    def _():
        o_ref[...] = x_ref[...] + acc_ref[...]


def _ffn_dense(x, g, wg, wu, wd, tm=1024, tf=512):
    t, d = x.shape
    dff = wg.shape[1]
    tm = min(tm, t)
    return pl.pallas_call(
        _ffn_kernel,
        grid=(t // tm, dff // tf),
        in_specs=[pl.BlockSpec((tm, d), lambda i, f: (i, 0)),
                  pl.BlockSpec((1, d), lambda i, f: (0, 0)),
                  pl.BlockSpec((d, tf), lambda i, f: (0, f)),
                  pl.BlockSpec((d, tf), lambda i, f: (0, f)),
                  pl.BlockSpec((tf, d), lambda i, f: (f, 0))],
        out_specs=pl.BlockSpec((tm, d), lambda i, f: (i, 0)),
        out_shape=jax.ShapeDtypeStruct((t, d), F32),
        scratch_shapes=[pltpu.VMEM((tm, d), BF16), pltpu.VMEM((tm, d), F32)],
        compiler_params=_cparams(2),
        name="ffn_dense",
    )(x, g.reshape(1, d), wg, wu, wd)


def _router_kernel(x_ref, g_ref, whi_ref, wlo_ref, h_ref, info_ref, cnt_ref, carry_ref, *, tm):
    i = pl.program_id(0)

    @pl.when(i == 0)
    def _():
        carry_ref[...] = jnp.zeros_like(carry_ref)

    h = _rms_rows(x_ref[...], g_ref[...])
    h_ref[...] = h
    hhi = h.astype(BF16)
    hlo = (h - hhi.astype(F32)).astype(BF16)
    whi = whi_ref[...]
    logits = (jnp.dot(hhi, whi, preferred_element_type=F32)
              + jnp.dot(hlo, whi, preferred_element_type=F32)
              + jnp.dot(hhi, wlo_ref[...], preferred_element_type=F32))
    lane = lax.broadcasted_iota(jnp.int32, (tm, LANES), 1)
    neg = jnp.float32(-jnp.inf)
    logits = jnp.where(lane < N_EXPERTS, logits, neg)
    m1 = jnp.max(logits, axis=-1, keepdims=True)
    i1 = jnp.min(jnp.where(logits == m1, lane, LANES), axis=-1, keepdims=True)
    rest = jnp.where(lane == i1, neg, logits)
    m2 = jnp.max(rest, axis=-1, keepdims=True)
    i2 = jnp.min(jnp.where(rest == m2, lane, LANES), axis=-1, keepdims=True)
    e = jnp.exp(m2 - m1)
    g1 = 1.0 / (1.0 + e)
    g2 = e / (1.0 + e)
    oh1 = jnp.where(lane == i1, 1.0, 0.0)
    oh2 = jnp.where(lane == i2, 1.0, 0.0)
    oh = oh1 + oh2
    ri = lax.broadcasted_iota(jnp.int32, (tm, tm), 0)
    ci = lax.broadcasted_iota(jnp.int32, (tm, tm), 1)
    tri = jnp.where(ci < ri, 1.0, 0.0)
    before = _mm(tri, oh) + carry_ref[0:1, :]
    r1 = jnp.sum(oh1 * before, axis=-1, keepdims=True)
    r2 = jnp.sum(oh2 * before, axis=-1, keepdims=True)
    carry = carry_ref[0:1, :] + jnp.sum(oh, axis=0, keepdims=True)
    carry_ref[...] = jnp.broadcast_to(carry, carry_ref.shape)
    cnt_ref[...] = jnp.broadcast_to(carry, cnt_ref.shape)
    info = jnp.where(lane == 0, i1.astype(F32), 0.0)
    info = jnp.where(lane == 1, i2.astype(F32), info)
    info = jnp.where(lane == 2, r1, info)
    info = jnp.where(lane == 3, r2, info)
    info = jnp.where(lane == 4, g1, info)
    info = jnp.where(lane == 5, g2, info)
    info_ref[...] = info


def _router(x, g, whi, wlo, tm=512):
    t, d = x.shape
    tm = min(tm, t)
    return pl.pallas_call(
        functools.partial(_router_kernel, tm=tm),
        grid=(t // tm,),
        in_specs=[pl.BlockSpec((tm, d), lambda i: (i, 0)),
                  pl.BlockSpec((1, d), lambda i: (0, 0)),
                  pl.BlockSpec((d, LANES), lambda i: (0, 0)),
                  pl.BlockSpec((d, LANES), lambda i: (0, 0))],
        out_specs=[pl.BlockSpec((tm, d), lambda i: (i, 0)),
                   pl.BlockSpec((tm, LANES), lambda i: (i, 0)),
                   pl.BlockSpec((8, LANES), lambda i: (0, 0))],
        out_shape=[jax.ShapeDtypeStruct((t, d), F32),
                   jax.ShapeDtypeStruct((t, LANES), F32),
                   jax.ShapeDtypeStruct((8, LANES), F32)],
        scratch_shapes=[pltpu.VMEM((8, LANES), F32)],
        compiler_params=_cparams(1),
        name="moe_router",
    )(x, g.reshape(1, d), whi, wlo)


def _gather_kernel(tok_ref, h_hbm, o_ref, sem, *, rows):
    base = pl.program_id(0) * rows

    def issue(r, carry):
        t = tok_ref[base + r]
        pltpu.make_async_copy(h_hbm.at[pl.ds(t, 1)], o_ref.at[pl.ds(r, 1)], sem).start()
        return carry

    lax.fori_loop(0, rows, issue, 0)
    pltpu.make_async_copy(h_hbm.at[pl.ds(0, rows)], o_ref, sem).wait()


def _gather_rows(tok, h, n_rows, rows=256):
    d = h.shape[1]
    return pl.pallas_call(
        functools.partial(_gather_kernel, rows=rows),
        grid_spec=pltpu.PrefetchScalarGridSpec(
            num_scalar_prefetch=1,
            grid=(n_rows // rows,),
            in_specs=[pl.BlockSpec(memory_space=pl.ANY)],
            out_specs=pl.BlockSpec((rows, d), lambda i, tok: (i, 0)),
            scratch_shapes=[pltpu.SemaphoreType.DMA(())]),
        out_shape=jax.ShapeDtypeStruct((n_rows, d), F32),
        compiler_params=_cparams(1),
        name="moe_gather",
    )(tok, h)


def _expert_kernel(be_ref, nu_ref, x_ref, wg_ref, wu_ref, wd_ref, o_ref, h_ref, acc_ref):
    i = pl.program_id(0)
    f = pl.program_id(1)
    nf = pl.num_programs(1)
    used = i < nu_ref[0]

    @pl.when(used & (f == 0))
    def _():
        h_ref[...] = x_ref[...].astype(BF16)
        acc_ref[...] = jnp.zeros_like(acc_ref)

    @pl.when(used)
    def _():
        h = h_ref[...]
        a = jnp.dot(h, wg_ref[...], preferred_element_type=F32)
        u = jnp.dot(h, wu_ref[...], preferred_element_type=F32)
        acc_ref[...] += _mm(_silu(a) * u, wd_ref[...])

    @pl.when(used & (f == nf - 1))
    def _():
        o_ref[...] = acc_ref[...]

    @pl.when(jnp.logical_not(used) & (f == nf - 1))
    def _():
        o_ref[...] = jnp.zeros_like(o_ref)


def _expert_ffn(blk_exp, n_used, xs, wg, wu, wd, tb, tf=512):
    p, d = xs.shape
    dff = wg.shape[2]
    return pl.pallas_call(
        _expert_kernel,
        grid_spec=pltpu.PrefetchScalarGridSpec(
            num_scalar_prefetch=2,
            grid=(p // tb, dff // tf),
            in_specs=[pl.BlockSpec((tb, d), lambda i, f, be, nu: (i, 0)),
                      pl.BlockSpec((None, d, tf), lambda i, f, be, nu: (be[i], 0, f)),
                      pl.BlockSpec((None, d, tf), lambda i, f, be, nu: (be[i], 0, f)),
                      pl.BlockSpec((None, tf, d), lambda i, f, be, nu: (be[i], f, 0))],
            out_specs=pl.BlockSpec((tb, d), lambda i, f, be, nu: (i, 0)),
            scratch_shapes=[pltpu.VMEM((tb, d), BF16), pltpu.VMEM((tb, d), F32)]),
        out_shape=jax.ShapeDtypeStruct((p, d), F32),
        compiler_params=_cparams(2),
        name="moe_experts",
    )(blk_exp, n_used, xs, wg, wu, wd)


def _combine_kernel(s1_ref, s2_ref, x_ref, info_ref, ys_hbm, gf_ref, o_ref, b1_ref, b2_ref, sem,
                    *, rows, final_norm):
    base = pl.program_id(0) * rows

    def issue(r, carry):
        pltpu.make_async_copy(ys_hbm.at[pl.ds(s1_ref[base + r], 1)], b1_ref.at[pl.ds(r, 1)], sem).start()
        pltpu.make_async_copy(ys_hbm.at[pl.ds(s2_ref[base + r], 1)], b2_ref.at[pl.ds(r, 1)], sem).start()
        return carry

    lax.fori_loop(0, rows, issue, 0)
    pltpu.make_async_copy(ys_hbm.at[pl.ds(0, rows)], b1_ref, sem).wait()
    pltpu.make_async_copy(ys_hbm.at[pl.ds(0, rows)], b2_ref, sem).wait()
    info = info_ref[...]
    y = b1_ref[...] * info[:, 4:5] + b2_ref[...] * info[:, 5:6]
    out = x_ref[...] + y
    if final_norm:
        out = _rms_rows(out, gf_ref[...])
    o_ref[...] = out


def _combine(slot1, slot2, x, info, ys, gf, final_norm, rows=256):
    t, d = x.shape
    rows = min(rows, t)
    return pl.pallas_call(
        functools.partial(_combine_kernel, rows=rows, final_norm=final_norm),
        grid_spec=pltpu.PrefetchScalarGridSpec(
            num_scalar_prefetch=2,
            grid=(t // rows,),
            in_specs=[pl.BlockSpec((rows, d), lambda i, a, b: (i, 0)),
                      pl.BlockSpec((rows, LANES), lambda i, a, b: (i, 0)),
                      pl.BlockSpec(memory_space=pl.ANY),
                      pl.BlockSpec((1, d), lambda i, a, b: (0, 0))],
            out_specs=pl.BlockSpec((rows, d), lambda i, a, b: (i, 0)),
            scratch_shapes=[pltpu.VMEM((rows, d), F32), pltpu.VMEM((rows, d), F32),
                            pltpu.SemaphoreType.DMA(())]),
        out_shape=jax.ShapeDtypeStruct((t, d), F32),
        compiler_params=_cparams(1),
        name="moe_combine",
    )(slot1, slot2, x, info, ys, gf.reshape(1, d))


def _moe_ffn(x, g, whi, wlo, wg, wu, wd, gf, final_norm, tb=512):
    t, d = x.shape
    h, info, cnt = _router(x, g, whi, wlo)
    e = info[:, 0:2].astype(jnp.int32)
    rank = info[:, 2:4].astype(jnp.int32)
    counts = cnt[0, :N_EXPERTS].astype(jnp.int32)
    padded = (counts + tb - 1) // tb * tb
    pad_end = jnp.cumsum(padded)
    pad_start = pad_end - padded
    slot = pad_start[e] + rank
    n_blocks = (t * 2) // tb + N_EXPERTS
    p = n_blocks * tb
    tok = jnp.repeat(jnp.arange(t, dtype=jnp.int32), 2)
    slot_tok = jnp.zeros((p,), jnp.int32).at[slot.reshape(-1)].set(tok)
    blk_exp = jnp.minimum(
        jnp.searchsorted(pad_end, jnp.arange(n_blocks, dtype=jnp.int32) * tb, side="right"),
        N_EXPERTS - 1).astype(jnp.int32)
    n_used = (pad_end[-1:] // tb).astype(jnp.int32)
    xs = _gather_rows(slot_tok, h, p)
    ys = _expert_ffn(blk_exp, n_used, xs, wg, wu, wd, tb)
    return _combine(slot[:, 0], slot[:, 1], x, info, ys, gf, final_norm)


def _final_norm_kernel(x_ref, g_ref, o_ref):
    o_ref[...] = _rms_rows(x_ref[...], g_ref[...])


def _final_norm(x, g, tm=512):
    t, d = x.shape
    tm = min(tm, t)
    return pl.pallas_call(
        _final_norm_kernel,
        grid=(t // tm,),
        in_specs=[pl.BlockSpec((tm, d), lambda i: (i, 0)), pl.BlockSpec((1, d), lambda i: (0, 0))],
        out_specs=pl.BlockSpec((tm, d), lambda i: (i, 0)),
        out_shape=jax.ShapeDtypeStruct((t, d), F32),
        compiler_params=_cparams(1),
        name="final_norm",
    )(x, g.reshape(1, d))


def _block_diag(w):
    n, c, d = w.shape
    eye = jnp.eye(n, dtype=w.dtype)
    return jnp.einsum("ncd,nm->ncmd", w, eye).reshape(n * c, n * d)


def _state_to_bd(s):
    b, h, dk, dv = s.shape
    eye = jnp.eye(h, dtype=s.dtype)
    return jnp.einsum("bhde,hg->bhdge", s, eye).reshape(b, h * dk, h * dv)


def _state_from_bd(s, h):
    b, r, c = s.shape
    dk, dv = r // h, c // h
    s = s.reshape(b, h, dk, h, dv)
    return jnp.stack([s[:, i, :, i, :] for i in range(h)], axis=1)


def _pad_cols(w, n):
    return jnp.pad(w, ((0, 0), (0, n - w.shape[1])))


def _layer_params(l, p):
    (norm_mix, w_in, lru_conv_w, lru_conv_b, lru_a_w, lru_a_b, lru_x_w, lru_x_b, lru_lam, gla_gk_w2,
     gla_gk_b, gla_norm, gdn_conv_w, gdn_a_log, gdn_dt_bias, gdn_norm, w_out, norm_xq, norm_mem,
     w_mq, w_mk, w_mv, w_mo, norm_ffn) = [a[l] for a in p]
    lw = LRU_W
    gk, gv = GLA_HEADS * GLA_DK, GLA_HEADS * GLA_DV
    dh = GDN_HEADS * GDN_DK
    offs = [0]
    for s in (lw, lw, gk, gk, gv, GLA_RANK, gv, dh, dh, dh, GDN_HEADS, GDN_HEADS, dh):
        offs.append(offs[-1] + s)
    col = lambda i: w_in[:, offs[i]:offs[i + 1]]
    w_lru = jnp.concatenate([col(0), col(1)], axis=1)
    w_gla = jnp.concatenate([col(2), col(3), col(4), col(6), _pad_cols(col(5), LANES)], axis=1)
    w_gdn = jnp.concatenate(
        [col(7), col(8), col(9), col(12), _pad_cols(jnp.concatenate([col(10), col(11)], axis=1), LANES)],
        axis=1)
    w_cat = jnp.concatenate([w_lru, w_gla, w_gdn], axis=1).astype(BF16)
    widths = (w_lru.shape[1], w_gla.shape[1], w_gdn.shape[1])
    alog = jnp.zeros((1, LANES), F32).at[0, GDN_HEADS:2 * GDN_HEADS].set(gdn_a_log)
    dtb = jnp.zeros((1, LANES), F32).at[0, GDN_HEADS:2 * GDN_HEADS].set(gdn_dt_bias)
    return dict(
        norm_mix=norm_mix, w_cat=w_cat, widths=widths,
        lru_conv_w=lru_conv_w, lru_conv_b=lru_conv_b,
        lru_a=_block_diag(lru_a_w).astype(BF16), lru_a_b=lru_a_b,
        lru_x=_block_diag(lru_x_w).astype(BF16), lru_x_b=lru_x_b, lru_lam=lru_lam,
        gla_w2=jnp.pad(gla_gk_w2, ((0, LANES - GLA_RANK), (0, 0))).astype(BF16), gla_gk_b=gla_gk_b,
        gla_norm=jnp.tile(gla_norm, GLA_HEADS),
        gdn_conv_w=gdn_conv_w, gdn_alog=alog, gdn_dtb=dtb, gdn_norm=jnp.tile(gdn_norm, GDN_HEADS),
        w_out=w_out.astype(BF16), norm_xq=norm_xq, norm_mem=norm_mem, w_mq=w_mq.astype(BF16),
        w_mkv=jnp.concatenate([w_mk, w_mv], axis=1).astype(BF16), w_mo=w_mo.astype(BF16),
        norm_ffn=norm_ffn)


def _run_group(x3, mem_k, mem_v, states, layers, ffn, norm_final):
    batch, seq, d = x3.shape
    x = x3.reshape(batch * seq, d)
    new_states = []
    for l, lp in enumerate(layers):
        lru_h0, lru_buf0, gla_s0, gdn_s0, gdn_buf0 = states[l]
        z_lru, z_gla, z_gdn = _rms_matmul(x, lp["norm_mix"], lp["w_cat"], lp["widths"], "in_proj")
        y_lru, lru_h, lru_buf = _lru_mixer(
            z_lru, lru_h0, lru_buf0, lp["lru_conv_w"], lp["lru_conv_b"], lp["lru_a"], lp["lru_a_b"],
            lp["lru_x"], lp["lru_x_b"], lp["lru_lam"], batch, seq)
        y_gla, gla_s = _gla_mixer(z_gla, _state_to_bd(gla_s0), lp["gla_w2"], lp["gla_gk_b"],
                                  lp["gla_norm"], batch, seq)
        y_gdn, gdn_s, gdn_buf = _gdn_mixer(z_gdn, _state_to_bd(gdn_s0), gdn_buf0, lp["gdn_conv_w"],
                                           lp["gdn_alog"], lp["gdn_dtb"], lp["gdn_norm"], batch, seq)
        x, q = _outproj(x, y_lru, y_gla, y_gdn, lp["w_out"], lp["norm_xq"], lp["w_mq"])
        x = _attention(x, q, mem_k[l], mem_v[l], lp["w_mo"], batch, seq)
        last = l == len(layers) - 1
        kind, fp = ffn[l]
        if kind == "dense":
            x = _ffn_dense(x, lp["norm_ffn"], *fp)
            if last:
                x = _final_norm(x, norm_final)
        else:
            x = _moe_ffn(x, lp["norm_ffn"], *fp, norm_final, last)
        new_states.append((lru_h.reshape(batch, LRU_W), lru_buf, _state_from_bd(gla_s, GLA_HEADS),
                           _state_from_bd(gdn_s, GDN_HEADS), gdn_buf))
    return x.reshape(batch, seq, d), [jnp.stack(s) for s in zip(*new_states)]


def kernel(x_prompt, x_sample, mem_prompt, state_lru_h, state_lru_conv, state_gla, state_gdn, state_gdn_conv, cache_mem_k, cache_mem_v, norm_mix, w_in, lru_conv_w, lru_conv_b, lru_a_w, lru_a_b, lru_x_w, lru_x_b, lru_lam, gla_gk_w2, gla_gk_b, gla_norm, gdn_conv_w, gdn_a_log, gdn_dt_bias, gdn_norm, w_out, norm_xq, norm_mem, w_mq, w_mk, w_mv, w_mo, norm_ffn, w_ff_gate, w_ff_up, w_ff_down, w_router, w_e_gate, w_e_up, w_e_down, norm_final):
    depth = norm_mix.shape[0]
    per_layer = (norm_mix, w_in, lru_conv_w, lru_conv_b, lru_a_w, lru_a_b, lru_x_w, lru_x_b, lru_lam,
                 gla_gk_w2, gla_gk_b, gla_norm, gdn_conv_w, gdn_a_log, gdn_dt_bias, gdn_norm, w_out,
                 norm_xq, norm_mem, w_mq, w_mk, w_mv, w_mo, norm_ffn)
    layers = [_layer_params(l, per_layer) for l in range(depth)]
    ffn = []
    for l in range(depth):
        j = l // 2
        if l % 2 == 0:
            ffn.append(("dense", (w_ff_gate[j].astype(BF16), w_ff_up[j].astype(BF16),
                                  w_ff_down[j].astype(BF16))))
        else:
            wr = _pad_cols(w_router[j], LANES)
            whi = wr.astype(BF16)
            wlo = (wr - whi.astype(F32)).astype(BF16)
            ffn.append(("moe", (whi, wlo, w_e_gate[j].astype(BF16), w_e_up[j].astype(BF16),
                                w_e_down[j].astype(BF16))))

    bp, mlen, d = mem_prompt.shape
    mem2 = mem_prompt.reshape(bp * mlen, d)
    pk, pv = [], []
    for lp in layers:
        k2, v2 = _rms_matmul(mem2, lp["norm_mem"], lp["w_mkv"], (d, d), "mem_kv")
        pk.append(k2.reshape(bp, mlen, d))
        pv.append(v2.reshape(bp, mlen, d))
    zero_state = (jnp.zeros((bp, LRU_W), F32), jnp.zeros((bp, CONV_K - 1, LRU_W), F32),
                  jnp.zeros((bp, GLA_HEADS, GLA_DK, GLA_DV), F32),
                  jnp.zeros((bp, GDN_HEADS, GDN_DK, GDN_DV), F32),
                  jnp.zeros((bp, CONV_K - 1, 3 * GDN_HEADS * GDN_DK), F32))
    y_prompt, p_st = _run_group(x_prompt, pk, pv, [zero_state] * depth, layers, ffn, norm_final)
    p_mem_k = jnp.stack(pk).reshape(depth, bp, mlen, MEM_HEADS, MEM_HD)
    p_mem_v = jnp.stack(pv).reshape(depth, bp, mlen, MEM_HEADS, MEM_HD)

    bs = x_sample.shape[0]
    s_in = [(state_lru_h[l], state_lru_conv[l], state_gla[l], state_gdn[l], state_gdn_conv[l])
            for l in range(depth)]
    ck = [cache_mem_k[l].reshape(bs, mlen, d) for l in range(depth)]
    cv = [cache_mem_v[l].reshape(bs, mlen, d) for l in range(depth)]
    y_sample, s_st = _run_group(x_sample, ck, cv, s_in, layers, ffn, norm_final)

    return (y_prompt, y_sample, p_st[0], p_st[1], p_st[2], p_st[3], p_st[4], p_mem_k, p_mem_v,
            s_st[0], s_st[1], s_st[2], s_st[3], s_st[4])
```

```python
import functools

import jax
import jax.numpy as jnp
from jax import lax
from jax.experimental import pallas as pl
from jax.experimental.pallas import tpu as pltpu

F32 = jnp.float32
BF16 = jnp.bfloat16
EPS = 1e-6

D_MODEL = 1024
LRU_W = 512
LRU_BLOCKS = 8
LRU_C = 8.0
CONV_K = 4
GLA_HEADS = 4
GLA_DK = 32
GLA_DV = 64
GLA_RANK = 16
GLA_TAU = 16.0
GLA_SUB = 16
GDN_HEADS = 4
GDN_DK = 64
GDN_DV = 64
MIX_CHUNK = 64
MEM_HEADS = 4
MEM_HD = 256
N_EXPERTS = 8
LANES = 128
VMEM_LIMIT_BYTES = 48 * 1024 * 1024
EXPERT_VMEM_LIMIT_BYTES = 56 * 1024 * 1024


def _cparams(n_axes):
    return pltpu.CompilerParams(dimension_semantics=("arbitrary",) * n_axes,
                                vmem_limit_bytes=VMEM_LIMIT_BYTES)


def _mm(a, b):
    return jnp.dot(a.astype(BF16), b.astype(BF16), preferred_element_type=F32)


def _mm_nt(a, b):
    return lax.dot_general(a.astype(BF16), b.astype(BF16), (((1,), (1,)), ((), ())),
                           preferred_element_type=F32)


def _mm_tn(a, b):
    return lax.dot_general(a.astype(BF16), b.astype(BF16), (((0,), (0,)), ((), ())),
                           preferred_element_type=F32)


def _rms_rows(x, g):
    ms = jnp.mean(x * x, axis=-1, keepdims=True)
    return (x * lax.rsqrt(ms + EPS)) * g


def _softplus(x):
    return jnp.maximum(x, 0.0) + jnp.log1p(jnp.exp(-jnp.abs(x)))


def _sigmoid(x):
    return 1.0 / (1.0 + jnp.exp(-x))


def _silu(x):
    return x * _sigmoid(x)


def _gelu_tanh(x):
    c = 0.7978845608028654
    return x * (0.5 * (1.0 + jnp.tanh(c * (x + 0.044715 * (x * x * x)))))


def _seg_cumsum_rows(x, seg):
    rows = x.shape[0]
    tpos = lax.broadcasted_iota(jnp.int32, (rows, 1), 0) & (seg - 1)
    d = 1
    while d < seg:
        x = x + jnp.where(tpos >= d, pltpu.roll(x, d, axis=0), 0.0)
        d *= 2
    return x


def _head_rms(o, gain, n_heads, width):
    lane_head = lax.broadcasted_iota(jnp.int32, (1, n_heads * width), 1) // width
    sq = o * o
    inv = jnp.zeros_like(o)
    for h in range(n_heads):
        m = lane_head == h
        ms = jnp.sum(jnp.where(m, sq, 0.0), axis=-1, keepdims=True) * (1.0 / width)
        inv = jnp.where(m, lax.rsqrt(ms + EPS), inv)
    return (o * inv) * gain


def _stack_heads(x, n_heads, width):
    c = x.shape[0]
    t = jnp.concatenate([x] * n_heads, axis=0)
    row_head = lax.broadcasted_iota(jnp.int32, (n_heads * c, 1), 0) // c
    lane_head = lax.broadcasted_iota(jnp.int32, (1, n_heads * width), 1) // width
    return jnp.where(row_head == lane_head, t, 0.0)


def _unstack_heads(x, n_heads):
    c = x.shape[0] // n_heads
    o = x[0:c]
    for h in range(1, n_heads):
        o = o + x[h * c:(h + 1) * c]
    return o


def _rms_matmul_kernel(x_ref, g_ref, w_ref, *o_refs):
    h = _rms_rows(x_ref[...], g_ref[...]).astype(BF16)
    start = 0
    for o_ref in o_refs:
        n = o_ref.shape[1]
        o_ref[...] = jnp.dot(h, w_ref[:, start:start + n], preferred_element_type=F32)
        start += n


def _rms_matmul(x, g, w, widths, name, tm=512):
    t, d = x.shape
    n = w.shape[1]
    tm = min(tm, t)
    return pl.pallas_call(
        _rms_matmul_kernel,
        grid=(t // tm,),
        in_specs=[pl.BlockSpec((tm, d), lambda i: (i, 0)),
                  pl.BlockSpec((1, d), lambda i: (0, 0)),
                  pl.BlockSpec((d, n), lambda i: (0, 0))],
        out_specs=[pl.BlockSpec((tm, wd), lambda i: (i, 0)) for wd in widths],
        out_shape=[jax.ShapeDtypeStruct((t, wd), F32) for wd in widths],
        compiler_params=_cparams(1),
        name=name,
    )(x, g.reshape(1, d), w)


def _lru_kernel(z_ref, h0_ref, buf0_ref, cw_ref, cb_ref, aw_ref, ab_ref, xw_ref, xb_ref, lam_ref,
                y_ref, hout_ref, bufout_ref, xs_ref, hc_ref, *, bt, tl, nt):
    w = LRU_W
    rows = bt * tl
    j = pl.program_id(0) % nt

    @pl.when(j == 0)
    def _():
        xs_ref[:, 5:8, :] = buf0_ref[...]
        hc_ref[...] = h0_ref[...]

    xs_ref[:, 8:, :] = z_ref[:, :w].reshape(bt, tl, w)
    gate = z_ref[:, w:]
    cw = cw_ref[...]
    xc = cb_ref[...] + xs_ref[:, 5:5 + tl, :] * cw[0:1]
    for k in range(1, CONV_K):
        xc = xc + xs_ref[:, 5 + k:5 + k + tl, :] * cw[k:k + 1]
    tail = xs_ref[:, 5 + tl:8 + tl, :]
    xs_ref[:, 5:8, :] = tail
    bufout_ref[...] = tail

    xc = xc.reshape(rows, w)
    r = _sigmoid(_mm(xc, aw_ref[...]) + ab_ref[...])
    ig = _sigmoid(_mm(xc, xw_ref[...]) + xb_ref[...])
    log_a = (-LRU_C * r) * _softplus(-lam_ref[...])
    a = jnp.exp(log_a)
    th = jnp.tanh(log_a)
    b = jnp.sqrt((-2.0 * th) / (1.0 - th)) * (ig * xc)

    tpos = lax.broadcasted_iota(jnp.int32, (rows, 1), 0) & (tl - 1)
    d = 1
    while d < tl:
        m = tpos >= d
        b = jnp.where(m, a * pltpu.roll(b, d, axis=0) + b, b)
        a = jnp.where(m, a * pltpu.roll(a, d, axis=0), a)
        d *= 2
    hc = jnp.broadcast_to(hc_ref[...], (bt, tl, w)).reshape(rows, w)
    h = b + a * hc
    hlast = h.reshape(bt, tl, w)[:, tl - 1:tl, :]
    hc_ref[...] = hlast
    hout_ref[...] = hlast
    y_ref[...] = h * _gelu_tanh(gate)


def _lru_mixer(z, h0, buf0, layer, cw, cb, aw, ab, xw, xb, lam, batch, seq):
    w = LRU_W
    tl = min(seq, 256)
    bt = min(batch, max(1, 256 // seq))
    nt = seq // tl
    rows = bt * tl
    grid = (batch * seq // rows,)
    if nt > 1:
        sidx = lambda i: (i // nt, 0, 0)
        lidx = lambda i: (layer, i // nt, 0, 0)
    else:
        sidx = lambda i: (i, 0, 0)
        lidx = lambda i: (layer, i, 0, 0)
    full2 = lambda i: (0, 0)
    return pl.pallas_call(
        functools.partial(_lru_kernel, bt=bt, tl=tl, nt=nt),
        grid=grid,
        in_specs=[pl.BlockSpec((rows, 2 * w), lambda i: (i, 0)),
                  pl.BlockSpec((None, bt, 1, w), lidx),
                  pl.BlockSpec((None, bt, CONV_K - 1, w), lidx),
                  pl.BlockSpec((CONV_K, w), full2),
                  pl.BlockSpec((1, w), full2),
                  pl.BlockSpec((w, w), full2),
                  pl.BlockSpec((1, w), full2),
                  pl.BlockSpec((w, w), full2),
                  pl.BlockSpec((1, w), full2),
                  pl.BlockSpec((1, w), full2)],
        out_specs=[pl.BlockSpec((rows, w), lambda i: (i, 0)),
                   pl.BlockSpec((bt, 1, w), sidx),
                   pl.BlockSpec((bt, CONV_K - 1, w), sidx)],
        out_shape=[jax.ShapeDtypeStruct((batch * seq, w), F32),
                   jax.ShapeDtypeStruct((batch, 1, w), F32),
                   jax.ShapeDtypeStruct((batch, CONV_K - 1, w), F32)],
        scratch_shapes=[pltpu.VMEM((bt, 8 + tl, w), F32), pltpu.VMEM((bt, 1, w), F32)],
        compiler_params=_cparams(1),
        name="lru_mixer",
    )(z, h0.reshape(h0.shape[0], batch, 1, w), buf0, cw, cb.reshape(1, w), aw, ab.reshape(1, w), xw,
      xb.reshape(1, w), lam.reshape(1, w))


def _gla_kernel(z_ref, s0_ref, w2_ref, gb_ref, gn_ref, y_ref, sout_ref, s_ref, *, c, sc, nt):
    nh = GLA_HEADS
    kw = nh * GLA_DK
    vw = nh * GLA_DV
    j = pl.program_id(0) % nt

    @pl.when(j == 0)
    def _():
        s_ref[...] = jnp.zeros_like(s_ref)
        for h in range(nh):
            s_ref[h * GLA_DK:(h + 1) * GLA_DK, h * GLA_DV:(h + 1) * GLA_DV] = s0_ref[h]

    q = z_ref[:, 0:kw] * (GLA_DK ** -0.5)
    k = z_ref[:, kw:2 * kw]
    v = z_ref[:, 2 * kw:2 * kw + vw]
    g = z_ref[:, 2 * kw + vw:2 * kw + 2 * vw]
    lr = z_ref[:, 2 * kw + 2 * vw:]
    gk = -_softplus(-(_mm(lr, w2_ref[...]) + gb_ref[...])) / GLA_TAU
    gcum = _seg_cumsum_rows(gk, sc)
    qp = q * jnp.exp(gcum)
    kp = k * jnp.exp(-gcum)

    qs = _stack_heads(qp, nh, GLA_DK)
    ks = _stack_heads(kp, nh, GLA_DK)
    a = _mm_nt(qs, ks)
    ri = lax.broadcasted_iota(jnp.int32, (nh * c, nh * c), 0)
    ci = lax.broadcasted_iota(jnp.int32, (nh * c, nh * c), 1)
    a = jnp.where((ri // sc == ci // sc) & (ci <= ri), a, 0.0)
    o = _unstack_heads(_mm(a, _stack_heads(v, nh, GLA_DV)), nh)

    s = s_ref[...]
    eye = (lax.broadcasted_iota(jnp.int32, (kw, kw), 0) ==
           lax.broadcasted_iota(jnp.int32, (kw, kw), 1))
    bd = (lax.broadcasted_iota(jnp.int32, (kw, vw), 0) // GLA_DK ==
          lax.broadcasted_iota(jnp.int32, (kw, vw), 1) // GLA_DV)
    o_inter = []
    for i in range(c // sc):
        lo, hi = i * sc, (i + 1) * sc
        o_inter.append(_mm(qp[lo:hi], s))
        glast = gcum[hi - 1:hi]
        kpp = k[lo:hi] * jnp.exp(glast - gcum[lo:hi])
        u = _mm_tn(kpp, v[lo:hi])
        dcol = jnp.sum(jnp.where(eye, jnp.exp(glast), 0.0), axis=1, keepdims=True)
        s = dcol * s + jnp.where(bd, u, 0.0)
    s_ref[...] = s
    for h in range(nh):
        sout_ref[h] = s[h * GLA_DK:(h + 1) * GLA_DK, h * GLA_DV:(h + 1) * GLA_DV]
    o = o + jnp.concatenate(o_inter, axis=0)
    y_ref[...] = _head_rms(o, gn_ref[...], nh, GLA_DV) * _silu(g)


def _gla_mixer(z, s0, layer, w2, gb, gn, batch, seq):
    c = min(seq, MIX_CHUNK)
    sc = min(c, GLA_SUB)
    nt = seq // c
    kw, vw = GLA_HEADS * GLA_DK, GLA_HEADS * GLA_DV
    zw = z.shape[1]
    full2 = lambda i: (0, 0)
    sblk = (None, GLA_HEADS, GLA_DK, GLA_DV)
    return pl.pallas_call(
        functools.partial(_gla_kernel, c=c, sc=sc, nt=nt),
        grid=(batch * nt,),
        in_specs=[pl.BlockSpec((c, zw), lambda i: (i, 0)),
                  pl.BlockSpec((None,) + sblk, lambda i: (layer, i // nt, 0, 0, 0)),
                  pl.BlockSpec((LANES, kw), full2),
                  pl.BlockSpec((1, kw), full2),
                  pl.BlockSpec((1, vw), full2)],
        out_specs=[pl.BlockSpec((c, vw), lambda i: (i, 0)),
                   pl.BlockSpec(sblk, lambda i: (i // nt, 0, 0, 0))],
        out_shape=[jax.ShapeDtypeStruct((batch * seq, vw), F32),
                   jax.ShapeDtypeStruct((batch, GLA_HEADS, GLA_DK, GLA_DV), F32)],
        scratch_shapes=[pltpu.VMEM((kw, vw), F32)],
        compiler_params=_cparams(1),
        name="gla_mixer",
    )(z, s0, w2, gb.reshape(1, kw), gn.reshape(1, vw))


def _gdn_kernel(z_ref, s0_ref, buf0_ref, cw_ref, alog_ref, dtb_ref, gn_ref,
                y_ref, sout_ref, bufout_ref, xs_ref, s_ref, *, c, nt):
    nh = GDN_HEADS
    hw = nh * GDN_DK
    cw3 = 3 * hw
    j = pl.program_id(0) % nt

    @pl.when(j == 0)
    def _():
        xs_ref[5:8, :] = buf0_ref[...]
        s_ref[...] = jnp.zeros_like(s_ref)
        for h in range(nh):
            s_ref[h * GDN_DK:(h + 1) * GDN_DK, h * GDN_DV:(h + 1) * GDN_DV] = s0_ref[h]

    xs_ref[8:, :] = z_ref[:, 0:cw3]
    cw = cw_ref[...]
    qkv = xs_ref[5:5 + c, :] * cw[0:1]
    for kk in range(1, CONV_K):
        qkv = qkv + xs_ref[5 + kk:5 + kk + c, :] * cw[kk:kk + 1]
    tail = xs_ref[5 + c:8 + c, :]
    xs_ref[5:8, :] = tail
    bufout_ref[...] = tail
    qkv = _silu(qkv)
    zg = z_ref[:, cw3:cw3 + hw]
    sm = z_ref[:, cw3 + hw:]

    lane_head = lax.broadcasted_iota(jnp.int32, (1, hw), 1) // GDN_DK

    def l2n(x):
        sq = x * x
        inv = jnp.zeros_like(x)
        for h in range(nh):
            m = lane_head == h
            ss = jnp.sum(jnp.where(m, sq, 0.0), axis=-1, keepdims=True)
            inv = jnp.where(m, lax.rsqrt(ss + EPS), inv)
        return x * inv

    q = l2n(qkv[:, 0:hw]) * (GDN_DK ** -0.5)
    k = l2n(qkv[:, hw:2 * hw])
    v = qkv[:, 2 * hw:3 * hw]
    beta = _sigmoid(sm)
    glog = -jnp.exp(alog_ref[...]) * _softplus(sm + dtb_ref[...])
    gcum = _seg_cumsum_rows(glog, c)

    n = nh * c
    bcol = jnp.concatenate([beta[:, h:h + 1] for h in range(nh)], axis=0)
    gcol = jnp.concatenate([gcum[:, nh + h:nh + h + 1] for h in range(nh)], axis=0)
    glast = jnp.concatenate(
        [jnp.broadcast_to(gcum[c - 1:c, nh + h:nh + h + 1], (c, 1)) for h in range(nh)], axis=0)
    ri = lax.broadcasted_iota(jnp.int32, (n, n), 0)
    ci = lax.broadcasted_iota(jnp.int32, (n, n), 1)
    grow = jnp.sum(jnp.where(ri == ci, gcol, 0.0), axis=0, keepdims=True)
    same = ri // c == ci // c
    incl = same & (ci <= ri)
    strict = same & (ci < ri)
    dec = jnp.where(incl, jnp.exp(jnp.where(incl, gcol - grow, 0.0)), 0.0)

    ks = _stack_heads(k, nh, GDN_DK)
    qs = _stack_heads(q, nh, GDN_DK)
    vs = _stack_heads(v, nh, GDN_DV)
    kkt = _mm_nt(ks, ks)
    qkt = _mm_nt(qs, ks)

    x = jnp.where(strict, -(bcol * kkt) * dec, 0.0)
    tinv = jnp.where(ri == ci, 1.0, 0.0) + x
    p = x
    span = 2
    while span < c:
        p = _mm(p, p)
        tinv = tinv + _mm(tinv, p)
        span *= 2

    s = s_ref[...]
    u = _mm(tinv, vs * bcol)
    w = _mm(tinv, ks * (bcol * jnp.exp(gcol)))
    vnew = u - _mm(w, s)
    a = jnp.where(incl, qkt * dec, 0.0)
    o = _unstack_heads(_mm(qs * jnp.exp(gcol), s) + _mm(a, vnew), nh)

    sdec = jnp.concatenate(
        [jnp.broadcast_to(jnp.exp(gcum[c - 1:c, nh + h:nh + h + 1]), (GDN_DK, 1)) for h in range(nh)],
        axis=0)
    s = sdec * s + _mm_tn(ks * jnp.exp(glast - gcol), vnew)
    s_ref[...] = s
    for h in range(nh):
        sout_ref[h] = s[h * GDN_DK:(h + 1) * GDN_DK, h * GDN_DV:(h + 1) * GDN_DV]
    y_ref[...] = _head_rms(o, gn_ref[...], nh, GDN_DV) * _silu(zg)


def _gdn_mixer(z, s0, buf0, layer, cw, alog, dtb, gn, batch, seq):
    c = min(seq, MIX_CHUNK)
    nt = seq // c
    hw = GDN_HEADS * GDN_DK
    zw = z.shape[1]
    full2 = lambda i: (0, 0)
    sidx = lambda i: (i // nt, 0, 0)
    sblk = (None, GDN_HEADS, GDN_DK, GDN_DV)
    return pl.pallas_call(
        functools.partial(_gdn_kernel, c=c, nt=nt),
        grid=(batch * nt,),
        in_specs=[pl.BlockSpec((c, zw), lambda i: (i, 0)),
                  pl.BlockSpec((None,) + sblk, lambda i: (layer, i // nt, 0, 0, 0)),
                  pl.BlockSpec((None, None, CONV_K - 1, 3 * hw), lambda i: (layer, i // nt, 0, 0)),
                  pl.BlockSpec((CONV_K, 3 * hw), full2),
                  pl.BlockSpec((1, LANES), full2),
                  pl.BlockSpec((1, LANES), full2),
                  pl.BlockSpec((1, hw), full2)],
        out_specs=[pl.BlockSpec((c, hw), lambda i: (i, 0)),
                   pl.BlockSpec(sblk, lambda i: (i // nt, 0, 0, 0)),
                   pl.BlockSpec((None, CONV_K - 1, 3 * hw), sidx)],
        out_shape=[jax.ShapeDtypeStruct((batch * seq, hw), F32),
                   jax.ShapeDtypeStruct((batch, GDN_HEADS, GDN_DK, GDN_DV), F32),
                   jax.ShapeDtypeStruct((batch, CONV_K - 1, 3 * hw), F32)],
        scratch_shapes=[pltpu.VMEM((8 + c, 3 * hw), F32), pltpu.VMEM((hw, hw), F32)],
        compiler_params=_cparams(1),
        name="gdn_mixer",
    )(z, s0, buf0, cw, alog, dtb, gn.reshape(1, hw))


def _outproj_kernel(x_ref, yl_ref, yg_ref, yd_ref, wo_ref, gq_ref, wq_ref, xn_ref, q_ref):
    lw = LRU_W
    gw = GLA_HEADS * GLA_DV
    y = _mm(yl_ref[...], wo_ref[0:lw, :])
    y = y + _mm(yg_ref[...], wo_ref[lw:lw + gw, :])
    y = y + _mm(yd_ref[...], wo_ref[lw + gw:, :])
    xn = x_ref[...] + y
    xn_ref[...] = xn
    q_ref[...] = _mm(_rms_rows(xn, gq_ref[...]), wq_ref[...])


def _outproj(x, yl, yg, yd, wo, gq, wq, tm=512):
    t, d = x.shape
    tm = min(tm, t)
    row = lambda i: (i, 0)
    full2 = lambda i: (0, 0)
    return pl.pallas_call(
        _outproj_kernel,
        grid=(t // tm,),
        in_specs=[pl.BlockSpec((tm, d), row),
                  pl.BlockSpec((tm, yl.shape[1]), row),
                  pl.BlockSpec((tm, yg.shape[1]), row),
                  pl.BlockSpec((tm, yd.shape[1]), row),
                  pl.BlockSpec((d, d), full2),
                  pl.BlockSpec((1, d), full2),
                  pl.BlockSpec((d, d), full2)],
        out_specs=[pl.BlockSpec((tm, d), row), pl.BlockSpec((tm, d), row)],
        out_shape=[jax.ShapeDtypeStruct((t, d), F32), jax.ShapeDtypeStruct((t, d), F32)],
        compiler_params=_cparams(1),
        name="outproj_qproj",
    )(x, yl, yg, yd, wo, gq.reshape(1, d), wq)


def _attn_kernel(x_ref, q_ref, k_ref, v_ref, wo_ref, o_ref, *, head_axis):
    hd = MEM_HD
    acc = x_ref[...]
    for h in range(MEM_HEADS):
        sl = slice(h * hd, (h + 1) * hd)
        if head_axis:
            kh, vh = k_ref[:, h, :], v_ref[:, h, :]
        else:
            kh, vh = k_ref[:, sl], v_ref[:, sl]
        s = _mm_nt(q_ref[:, sl], kh) * (hd ** -0.5)
        m = jnp.max(s, axis=-1, keepdims=True)
        p = jnp.exp(s - m)
        l = jnp.sum(p, axis=-1, keepdims=True)
        oh = _mm(p, vh) / l
        acc = acc + _mm(oh, wo_ref[sl, :])
    o_ref[...] = acc


def _attention(x, q, k, v, layer, wo, batch, seq):
    t, d = x.shape
    tl = min(seq, 512)
    nl = seq // tl
    head_axis = k.ndim == 5
    kblk = (None, None) + k.shape[2:]
    kidx = lambda b, j: (layer, b) + (0,) * (k.ndim - 2)
    row = lambda b, j: (b * nl + j, 0)
    return pl.pallas_call(
        functools.partial(_attn_kernel, head_axis=head_axis),
        grid=(batch, nl),
        in_specs=[pl.BlockSpec((tl, d), row),
                  pl.BlockSpec((tl, d), row),
                  pl.BlockSpec(kblk, kidx),
                  pl.BlockSpec(kblk, kidx),
                  pl.BlockSpec((d, d), lambda b, j: (0, 0))],
        out_specs=pl.BlockSpec((tl, d), row),
        out_shape=jax.ShapeDtypeStruct((t, d), F32),
        compiler_params=_cparams(2),
        name="mem_attention",
    )(x, q, k, v, wo)


def _ffn_kernel(x_ref, g_ref, wg_ref, wu_ref, wd_ref, o_ref, h_ref, acc_ref):
    f = pl.program_id(1)

    @pl.when(f == 0)
    def _():
        h_ref[...] = _rms_rows(x_ref[...], g_ref[...]).astype(BF16)
        acc_ref[...] = jnp.zeros_like(acc_ref)

    h = h_ref[...]
    a = _mm(h, wg_ref[...])
    u = _mm(h, wu_ref[...])
    acc_ref[...] += _mm(_silu(a) * u, wd_ref[...])

    @pl.when(f == pl.num_programs(1) - 1)
    def _():
        o_ref[...] = x_ref[...] + acc_ref[...]


def _ffn_dense(x, g, wg, wu, wd, tm=1024, tf=512):
    t, d = x.shape
    dff = wg.shape[1]
    tm = min(tm, t)
    return pl.pallas_call(
        _ffn_kernel,
        grid=(t // tm, dff // tf),
        in_specs=[pl.BlockSpec((tm, d), lambda i, f: (i, 0)),
                  pl.BlockSpec((1, d), lambda i, f: (0, 0)),
                  pl.BlockSpec((d, tf), lambda i, f: (0, f)),
                  pl.BlockSpec((d, tf), lambda i, f: (0, f)),
                  pl.BlockSpec((tf, d), lambda i, f: (f, 0))],
        out_specs=pl.BlockSpec((tm, d), lambda i, f: (i, 0)),
        out_shape=jax.ShapeDtypeStruct((t, d), F32),
        scratch_shapes=[pltpu.VMEM((tm, d), BF16), pltpu.VMEM((tm, d), F32)],
        compiler_params=_cparams(2),
        name="ffn_dense",
    )(x, g.reshape(1, d), wg, wu, wd)


def _router_kernel(x_ref, g_ref, whi_ref, wlo_ref, h_ref, info_ref, cnt_ref, carry_ref, *, tm):
    i = pl.program_id(0)

    @pl.when(i == 0)
    def _():
        carry_ref[...] = jnp.zeros_like(carry_ref)

    h = _rms_rows(x_ref[...], g_ref[...])
    h_ref[...] = h
    hhi = h.astype(BF16)
    hlo = (h - hhi.astype(F32)).astype(BF16)
    whi = whi_ref[...]
    logits = (jnp.dot(hhi, whi, preferred_element_type=F32)
              + jnp.dot(hlo, whi, preferred_element_type=F32)
              + jnp.dot(hhi, wlo_ref[...], preferred_element_type=F32))
    lane = lax.broadcasted_iota(jnp.int32, (tm, LANES), 1)
    neg = jnp.float32(-jnp.inf)
    logits = jnp.where(lane < N_EXPERTS, logits, neg)
    m1 = jnp.max(logits, axis=-1, keepdims=True)
    i1 = jnp.min(jnp.where(logits == m1, lane, LANES), axis=-1, keepdims=True)
    rest = jnp.where(lane == i1, neg, logits)
    m2 = jnp.max(rest, axis=-1, keepdims=True)
    i2 = jnp.min(jnp.where(rest == m2, lane, LANES), axis=-1, keepdims=True)
    e = jnp.exp(m2 - m1)
    g1 = 1.0 / (1.0 + e)
    g2 = e / (1.0 + e)
    oh1 = jnp.where(lane == i1, 1.0, 0.0)
    oh2 = jnp.where(lane == i2, 1.0, 0.0)
    oh = oh1 + oh2
    ri = lax.broadcasted_iota(jnp.int32, (tm, tm), 0)
    ci = lax.broadcasted_iota(jnp.int32, (tm, tm), 1)
    tri = jnp.where(ci < ri, 1.0, 0.0)
    before = _mm(tri, oh) + carry_ref[0:1, :]
    r1 = jnp.sum(oh1 * before, axis=-1, keepdims=True)
    r2 = jnp.sum(oh2 * before, axis=-1, keepdims=True)
    carry = carry_ref[0:1, :] + jnp.sum(oh, axis=0, keepdims=True)
    carry_ref[...] = jnp.broadcast_to(carry, carry_ref.shape)
    cnt_ref[...] = jnp.broadcast_to(carry, cnt_ref.shape)
    info = jnp.where(lane == 0, i1.astype(F32), 0.0)
    info = jnp.where(lane == 1, i2.astype(F32), info)
    info = jnp.where(lane == 2, r1, info)
    info = jnp.where(lane == 3, r2, info)
    info = jnp.where(lane == 4, g1, info)
    info = jnp.where(lane == 5, g2, info)
    info_ref[...] = info


def _router(x, g, whi, wlo, tm=512):
    t, d = x.shape
    tm = min(tm, t)
    return pl.pallas_call(
        functools.partial(_router_kernel, tm=tm),
        grid=(t // tm,),
        in_specs=[pl.BlockSpec((tm, d), lambda i: (i, 0)),
                  pl.BlockSpec((1, d), lambda i: (0, 0)),
                  pl.BlockSpec((d, LANES), lambda i: (0, 0)),
                  pl.BlockSpec((d, LANES), lambda i: (0, 0))],
        out_specs=[pl.BlockSpec((tm, d), lambda i: (i, 0)),
                   pl.BlockSpec((tm, LANES), lambda i: (i, 0)),
                   pl.BlockSpec((8, LANES), lambda i: (0, 0))],
        out_shape=[jax.ShapeDtypeStruct((t, d), F32),
                   jax.ShapeDtypeStruct((t, LANES), F32),
                   jax.ShapeDtypeStruct((8, LANES), F32)],
        scratch_shapes=[pltpu.VMEM((8, LANES), F32)],
        compiler_params=_cparams(1),
        name="moe_router",
    )(x, g.reshape(1, d), whi, wlo)


def _dispatch_kernel(s1_ref, s2_ref, zs_ref, zf_ref, h_ref, xs_hbm, zbuf_ref, zsem, sem, *, rows, sub,
                     n_blk):
    i = pl.program_id(0)

    @pl.when(i == 0)
    def _():
        zbuf_ref[...] = jnp.zeros_like(zbuf_ref)
        for e in range(N_EXPERTS):
            dst = xs_hbm.at[pl.ds(pl.multiple_of(zs_ref[e], 8), sub)]
            pltpu.make_async_copy(zbuf_ref, dst, zsem).start()
            pltpu.make_async_copy(zbuf_ref, dst, zsem).wait()

        def fill(j, carry):
            @pl.when(zf_ref[j] == 1)
            def _():
                dst = xs_hbm.at[pl.ds(pl.multiple_of(j * sub, sub), sub)]
                pltpu.make_async_copy(zbuf_ref, dst, zsem).start()
            return carry

        def drain(j, carry):
            @pl.when(zf_ref[j] == 1)
            def _():
                dst = xs_hbm.at[pl.ds(pl.multiple_of(j * sub, sub), sub)]
                pltpu.make_async_copy(zbuf_ref, dst, zsem).wait()
            return carry

        lax.fori_loop(0, n_blk, fill, 0)
        lax.fori_loop(0, n_blk, drain, 0)

    base = i * rows

    def issue(r, carry):
        src = h_ref.at[pl.ds(r, 1)]
        pltpu.make_async_copy(src, xs_hbm.at[pl.ds(s1_ref[base + r], 1)], sem).start()
        pltpu.make_async_copy(src, xs_hbm.at[pl.ds(s2_ref[base + r], 1)], sem).start()
        return carry

    lax.fori_loop(0, rows, issue, 0, unroll=8)
    for _ in range(2):
        pltpu.make_async_copy(h_ref, xs_hbm.at[pl.ds(0, rows)], sem).wait()


def _dispatch(slot1, slot2, zero_start, zero_blk, h, sub, rows=256):
    t, d = h.shape
    rows = min(rows, t)
    n_blk = zero_blk.shape[0]
    n_rows = n_blk * sub
    return pl.pallas_call(
        functools.partial(_dispatch_kernel, rows=rows, sub=sub, n_blk=n_blk),
        grid_spec=pltpu.PrefetchScalarGridSpec(
            num_scalar_prefetch=4,
            grid=(t // rows,),
            in_specs=[pl.BlockSpec((rows, d), lambda i, a, b, c, e: (i, 0))],
            out_specs=pl.BlockSpec(memory_space=pl.ANY),
            scratch_shapes=[pltpu.VMEM((sub, d), F32), pltpu.SemaphoreType.DMA(()),
                            pltpu.SemaphoreType.DMA(())]),
        out_shape=jax.ShapeDtypeStruct((n_rows, d), F32),
        compiler_params=_cparams(1),
        name="moe_dispatch",
    )(slot1, slot2, zero_start, zero_blk, h)


def _expert_kernel(be_ref, ns_ref, xi_ref, x_ref, wg_ref, wu_ref, wd_ref, o_ref, *, sub, n_sub):
    i = pl.program_id(0)
    f = pl.program_id(1)
    ns = ns_ref[i]

    @pl.when(f == 0)
    def _():
        o_ref[...] = jnp.zeros_like(o_ref)

    for s in range(n_sub):
        @pl.when(s < ns)
        def _():
            rows = slice(s * sub, (s + 1) * sub)
            h = x_ref[rows, :]
            a = _mm(h, wg_ref[...])
            u = _mm(h, wu_ref[...])
            o_ref[rows, :] += _mm(_silu(a) * u, wd_ref[...])


def _expert_ffn(blk_exp, n_valid_sub, x_blk, xs, wg, wu, wd, sb, sub, tf=512):
    p, d = xs.shape
    dff = wg.shape[2]
    n_super = n_valid_sub.shape[0]
    nf = dff // tf

    def fidx(i, f, ns):
        used = jnp.minimum(ns[i], 1)
        return f * used + (nf - 1) * (1 - used)

    return pl.pallas_call(
        functools.partial(_expert_kernel, sub=sub, n_sub=sb // sub),
        grid_spec=pltpu.PrefetchScalarGridSpec(
            num_scalar_prefetch=3,
            grid=(n_super, nf),
            in_specs=[pl.BlockSpec((sb, d), lambda i, f, be, ns, xi: (xi[i], 0)),
                      pl.BlockSpec((None, d, tf), lambda i, f, be, ns, xi: (be[i], 0, fidx(i, f, ns))),
                      pl.BlockSpec((None, d, tf), lambda i, f, be, ns, xi: (be[i], 0, fidx(i, f, ns))),
                      pl.BlockSpec((None, tf, d), lambda i, f, be, ns, xi: (be[i], fidx(i, f, ns), 0))],
            out_specs=pl.BlockSpec((sb, d), lambda i, f, be, ns, xi: (i, 0))),
        out_shape=jax.ShapeDtypeStruct((n_super * sb, d), F32),
        compiler_params=pltpu.CompilerParams(dimension_semantics=("arbitrary", "arbitrary"),
                                             vmem_limit_bytes=EXPERT_VMEM_LIMIT_BYTES),
        name="moe_experts",
    )(blk_exp, n_valid_sub, x_blk, xs, wg, wu, wd)


def _combine_kernel(s1_ref, s2_ref, x_ref, info_ref, ys_hbm, gf_ref, o_ref, b1_ref, b2_ref, sem,
                    *, rows, final_norm):
    base = pl.program_id(0) * rows

    def issue(r, carry):
        pltpu.make_async_copy(ys_hbm.at[pl.ds(s1_ref[base + r], 1)], b1_ref.at[pl.ds(r, 1)], sem).start()
        pltpu.make_async_copy(ys_hbm.at[pl.ds(s2_ref[base + r], 1)], b2_ref.at[pl.ds(r, 1)], sem).start()
        return carry

    lax.fori_loop(0, rows, issue, 0, unroll=8)
    pltpu.make_async_copy(ys_hbm.at[pl.ds(0, rows)], b1_ref, sem).wait()
    pltpu.make_async_copy(ys_hbm.at[pl.ds(0, rows)], b2_ref, sem).wait()
    info = info_ref[...]
    y = b1_ref[...] * info[:, 4:5] + b2_ref[...] * info[:, 5:6]
    out = x_ref[...] + y
    if final_norm:
        out = _rms_rows(out, gf_ref[...])
    o_ref[...] = out


def _combine(slot1, slot2, x, info, ys, gf, final_norm, rows=256):
    t, d = x.shape
    rows = min(rows, t)
    return pl.pallas_call(
        functools.partial(_combine_kernel, rows=rows, final_norm=final_norm),
        grid_spec=pltpu.PrefetchScalarGridSpec(
            num_scalar_prefetch=2,
            grid=(t // rows,),
            in_specs=[pl.BlockSpec((rows, d), lambda i, a, b: (i, 0)),
                      pl.BlockSpec((rows, LANES), lambda i, a, b: (i, 0)),
                      pl.BlockSpec(memory_space=pl.ANY),
                      pl.BlockSpec((1, d), lambda i, a, b: (0, 0))],
            out_specs=pl.BlockSpec((rows, d), lambda i, a, b: (i, 0)),
            scratch_shapes=[pltpu.VMEM((rows, d), F32), pltpu.VMEM((rows, d), F32),
                            pltpu.SemaphoreType.DMA(())]),
        out_shape=jax.ShapeDtypeStruct((t, d), F32),
        compiler_params=_cparams(1),
        name="moe_combine",
    )(slot1, slot2, x, info, ys, gf.reshape(1, d))


def _moe_ffn(x, g, whi, wlo, wg, wu, wd, gf, final_norm, sub=512):
    t, d = x.shape
    sb = 2048 if 2 * t >= 16 * 1024 else sub
    h, info, cnt = _router(x, g, whi, wlo)
    e = info[:, 0:2].astype(jnp.int32)
    rank = info[:, 2:4].astype(jnp.int32)
    counts = cnt[0, :N_EXPERTS].astype(jnp.int32)
    n_sb = (counts + sb - 1) // sb
    sb_end = jnp.cumsum(n_sb)
    sb_start = sb_end - n_sb
    row_start = sb_start * sb
    eid = jnp.arange(N_EXPERTS, dtype=jnp.int32)
    slot = jnp.sum(jnp.where(e[:, :, None] == eid, row_start, 0), axis=-1) + rank
    n_super = (2 * t) // sb + N_EXPERTS
    blk = jnp.arange(n_super, dtype=jnp.int32)
    n_used = sb_end[-1]
    used = blk < n_used
    blk_c = jnp.minimum(blk, n_used - 1)
    be = jnp.minimum(jnp.sum((blk_c[:, None] >= sb_end[None, :]).astype(jnp.int32), axis=-1),
                     N_EXPERTS - 1)
    valid = jnp.clip(counts[be] - (blk_c - sb_start[be]) * sb, 0, sb)
    n_valid_sub = jnp.where(used, (valid + sub - 1) // sub, 0).astype(jnp.int32)
    zero_start = ((row_start + counts) // 8 * 8).astype(jnp.int32)
    per = sb // sub
    sub_in_blk = jnp.arange(per, dtype=jnp.int32)
    zero_blk = (sub_in_blk[None, :] >= n_valid_sub[:, None]).astype(jnp.int32).reshape(-1)
    zero_blk = jnp.concatenate([zero_blk, jnp.ones((1,), jnp.int32)])
    xs = _dispatch(slot[:, 0], slot[:, 1], zero_start, zero_blk, h, sub)
    ys = _expert_ffn(be.astype(jnp.int32), n_valid_sub, blk_c.astype(jnp.int32), xs, wg, wu, wd, sb, sub)
    return _combine(slot[:, 0], slot[:, 1], x, info, ys, gf, final_norm)


def _final_norm_kernel(x_ref, g_ref, o_ref):
    o_ref[...] = _rms_rows(x_ref[...], g_ref[...])


def _final_norm(x, g, tm=512):
    t, d = x.shape
    tm = min(tm, t)
    return pl.pallas_call(
        _final_norm_kernel,
        grid=(t // tm,),
        in_specs=[pl.BlockSpec((tm, d), lambda i: (i, 0)), pl.BlockSpec((1, d), lambda i: (0, 0))],
        out_specs=pl.BlockSpec((tm, d), lambda i: (i, 0)),
        out_shape=jax.ShapeDtypeStruct((t, d), F32),
        compiler_params=_cparams(1),
        name="final_norm",
    )(x, g.reshape(1, d))


def _block_diag(w):
    n, c, d = w.shape
    eye = jnp.eye(n, dtype=w.dtype)
    return jnp.einsum("ncd,nm->ncmd", w, eye).reshape(n * c, n * d)


def _pad_cols(w, n):
    return jnp.pad(w, ((0, 0), (0, n - w.shape[1])))


def _layer_params(l, p):
    (norm_mix, w_in, lru_conv_w, lru_conv_b, lru_a_w, lru_a_b, lru_x_w, lru_x_b, lru_lam, gla_gk_w2,
     gla_gk_b, gla_norm, gdn_conv_w, gdn_a_log, gdn_dt_bias, gdn_norm, w_out, norm_xq, norm_mem,
     w_mq, w_mk, w_mv, w_mo, norm_ffn) = [a[l] for a in p]
    lw = LRU_W
    gk, gv = GLA_HEADS * GLA_DK, GLA_HEADS * GLA_DV
    dh = GDN_HEADS * GDN_DK
    offs = [0]
    for s in (lw, lw, gk, gk, gv, GLA_RANK, gv, dh, dh, dh, GDN_HEADS, GDN_HEADS, dh):
        offs.append(offs[-1] + s)
    col = lambda i: w_in[:, offs[i]:offs[i + 1]]
    w_lru = jnp.concatenate([col(0), col(1)], axis=1)
    w_gla = jnp.concatenate([col(2), col(3), col(4), col(6), _pad_cols(col(5), LANES)], axis=1)
    w_gdn = jnp.concatenate(
        [col(7), col(8), col(9), col(12), _pad_cols(jnp.concatenate([col(10), col(11)], axis=1), LANES)],
        axis=1)
    w_cat = jnp.concatenate([w_lru, w_gla, w_gdn], axis=1).astype(BF16)
    widths = (w_lru.shape[1], w_gla.shape[1], w_gdn.shape[1])
    alog = jnp.zeros((1, LANES), F32).at[0, GDN_HEADS:2 * GDN_HEADS].set(gdn_a_log)
    dtb = jnp.zeros((1, LANES), F32).at[0, GDN_HEADS:2 * GDN_HEADS].set(gdn_dt_bias)
    return dict(
        norm_mix=norm_mix, w_cat=w_cat, widths=widths,
        lru_conv_w=lru_conv_w, lru_conv_b=lru_conv_b,
        lru_a=_block_diag(lru_a_w).astype(BF16), lru_a_b=lru_a_b,
        lru_x=_block_diag(lru_x_w).astype(BF16), lru_x_b=lru_x_b, lru_lam=lru_lam,
        gla_w2=jnp.pad(gla_gk_w2, ((0, LANES - GLA_RANK), (0, 0))).astype(BF16), gla_gk_b=gla_gk_b,
        gla_norm=jnp.tile(gla_norm, GLA_HEADS),
        gdn_conv_w=gdn_conv_w, gdn_alog=alog, gdn_dtb=dtb, gdn_norm=jnp.tile(gdn_norm, GDN_HEADS),
        w_out=w_out.astype(BF16), norm_xq=norm_xq, norm_mem=norm_mem, w_mq=w_mq.astype(BF16),
        w_mkv=jnp.concatenate([w_mk, w_mv], axis=1).astype(BF16), w_mo=w_mo.astype(BF16),
        norm_ffn=norm_ffn)


def _run_group(x3, mem_k, mem_v, states, layers, ffn, norm_final):
    batch, seq, d = x3.shape
    x = x3.reshape(batch * seq, d)
    lru_h0, lru_buf0, gla_s0, gdn_s0, gdn_buf0, state_layer = states
    new_states = []
    for l, lp in enumerate(layers):
        sl = state_layer(l)
        z_lru, z_gla, z_gdn = _rms_matmul(x, lp["norm_mix"], lp["w_cat"], lp["widths"], "in_proj")
        y_lru, lru_h, lru_buf = _lru_mixer(
            z_lru, lru_h0, lru_buf0, sl, lp["lru_conv_w"], lp["lru_conv_b"], lp["lru_a"], lp["lru_a_b"],
            lp["lru_x"], lp["lru_x_b"], lp["lru_lam"], batch, seq)
        y_gla, gla_s = _gla_mixer(z_gla, gla_s0, sl, lp["gla_w2"], lp["gla_gk_b"], lp["gla_norm"],
                                  batch, seq)
        y_gdn, gdn_s, gdn_buf = _gdn_mixer(z_gdn, gdn_s0, gdn_buf0, sl, lp["gdn_conv_w"],
                                           lp["gdn_alog"], lp["gdn_dtb"], lp["gdn_norm"], batch, seq)
        x, q = _outproj(x, y_lru, y_gla, y_gdn, lp["w_out"], lp["norm_xq"], lp["w_mq"])
        x = _attention(x, q, mem_k[l][0], mem_v[l][0], mem_k[l][1], lp["w_mo"], batch, seq)
        last = l == len(layers) - 1
        kind, fp = ffn[l]
        if kind == "dense":
            x = _ffn_dense(x, lp["norm_ffn"], *fp)
            if last:
                x = _final_norm(x, norm_final)
        else:
            x = _moe_ffn(x, lp["norm_ffn"], *fp, norm_final, last)
        new_states.append((lru_h.reshape(batch, LRU_W), lru_buf, gla_s, gdn_s, gdn_buf))
    return x.reshape(batch, seq, d), [jnp.stack(s) for s in zip(*new_states)]


def kernel(x_prompt, x_sample, mem_prompt, state_lru_h, state_lru_conv, state_gla, state_gdn, state_gdn_conv, cache_mem_k, cache_mem_v, norm_mix, w_in, lru_conv_w, lru_conv_b, lru_a_w, lru_a_b, lru_x_w, lru_x_b, lru_lam, gla_gk_w2, gla_gk_b, gla_norm, gdn_conv_w, gdn_a_log, gdn_dt_bias, gdn_norm, w_out, norm_xq, norm_mem, w_mq, w_mk, w_mv, w_mo, norm_ffn, w_ff_gate, w_ff_up, w_ff_down, w_router, w_e_gate, w_e_up, w_e_down, norm_final):
    depth = norm_mix.shape[0]
    per_layer = (norm_mix, w_in, lru_conv_w, lru_conv_b, lru_a_w, lru_a_b, lru_x_w, lru_x_b, lru_lam,
                 gla_gk_w2, gla_gk_b, gla_norm, gdn_conv_w, gdn_a_log, gdn_dt_bias, gdn_norm, w_out,
                 norm_xq, norm_mem, w_mq, w_mk, w_mv, w_mo, norm_ffn)
    layers = [_layer_params(l, per_layer) for l in range(depth)]
    ffn = []
    for l in range(depth):
        j = l // 2
        if l % 2 == 0:
            ffn.append(("dense", (w_ff_gate[j], w_ff_up[j], w_ff_down[j])))
        else:
            wr = _pad_cols(w_router[j], LANES)
            whi = wr.astype(BF16)
            wlo = (wr - whi.astype(F32)).astype(BF16)
            ffn.append(("moe", (whi, wlo, w_e_gate[j], w_e_up[j], w_e_down[j])))

    bp, mlen, d = mem_prompt.shape
    mem2 = mem_prompt.reshape(bp * mlen, d)
    pk, pv = [], []
    for lp in layers:
        k2, v2 = _rms_matmul(mem2, lp["norm_mem"], lp["w_mkv"], (d, d), "mem_kv")
        pk.append(k2.reshape(1, bp, mlen, d))
        pv.append(v2.reshape(1, bp, mlen, d))
    zero_state = (jnp.zeros((1, bp, LRU_W), F32), jnp.zeros((1, bp, CONV_K - 1, LRU_W), F32),
                  jnp.zeros((1, bp, GLA_HEADS, GLA_DK, GLA_DV), F32),
                  jnp.zeros((1, bp, GDN_HEADS, GDN_DK, GDN_DV), F32),
                  jnp.zeros((1, bp, CONV_K - 1, 3 * GDN_HEADS * GDN_DK), F32),
                  lambda l: 0)
    y_prompt, p_st = _run_group(x_prompt, [(k, 0) for k in pk], [(v, 0) for v in pv], zero_state,
                                layers, ffn, norm_final)
    p_mem_k = jnp.concatenate(pk, axis=0).reshape(depth, bp, mlen, MEM_HEADS, MEM_HD)
    p_mem_v = jnp.concatenate(pv, axis=0).reshape(depth, bp, mlen, MEM_HEADS, MEM_HD)

    s_in = (state_lru_h, state_lru_conv, state_gla, state_gdn, state_gdn_conv, lambda l: l)
    y_sample, s_st = _run_group(x_sample, [(cache_mem_k, l) for l in range(depth)],
                                [(cache_mem_v, l) for l in range(depth)], s_in, layers, ffn, norm_final)

    return (y_prompt, y_sample, p_st[0], p_st[1], p_st[2], p_st[3], p_st[4], p_mem_k, p_mem_v,
            s_st[0], s_st[1], s_st[2], s_st[3], s_st[4])
```

```python
import functools

import jax
import jax.numpy as jnp
from jax import lax
from jax.experimental import pallas as pl
from jax.experimental.pallas import tpu as pltpu

F32 = jnp.float32
BF16 = jnp.bfloat16
EPS = 1e-6

D_MODEL = 1024
LRU_W = 512
LRU_BLOCKS = 8
LRU_C = 8.0
CONV_K = 4
GLA_HEADS = 4
GLA_DK = 32
GLA_DV = 64
GLA_RANK = 16
GLA_TAU = 16.0
GLA_SUB = 16
GDN_HEADS = 4
GDN_DK = 64
GDN_DV = 64
MIX_CHUNK = 64
MEM_HEADS = 4
MEM_HD = 256
N_EXPERTS = 8
LANES = 128
VMEM_LIMIT_BYTES = 48 * 1024 * 1024
EXPERT_VMEM_LIMIT_BYTES = 56 * 1024 * 1024


def _cparams(n_axes):
    return pltpu.CompilerParams(dimension_semantics=("arbitrary",) * n_axes,
                                vmem_limit_bytes=VMEM_LIMIT_BYTES)


def _mm(a, b):
    return jnp.dot(a.astype(BF16), b.astype(BF16), preferred_element_type=F32)


def _mm_nt(a, b):
    return lax.dot_general(a.astype(BF16), b.astype(BF16), (((1,), (1,)), ((), ())),
                           preferred_element_type=F32)


def _mm_tn(a, b):
    return lax.dot_general(a.astype(BF16), b.astype(BF16), (((0,), (0,)), ((), ())),
                           preferred_element_type=F32)


def _rms_rows(x, g):
    ms = jnp.mean(x * x, axis=-1, keepdims=True)
    return (x * lax.rsqrt(ms + EPS)) * g


def _softplus(x):
    return jnp.maximum(x, 0.0) + jnp.log1p(jnp.exp(-jnp.abs(x)))


def _sigmoid(x):
    return 1.0 / (1.0 + jnp.exp(-x))


def _silu(x):
    return x * _sigmoid(x)


def _gelu_tanh(x):
    c = 0.7978845608028654
    return x * (0.5 * (1.0 + jnp.tanh(c * (x + 0.044715 * (x * x * x)))))


def _seg_cumsum_rows(x, seg):
    rows = x.shape[0]
    tpos = lax.broadcasted_iota(jnp.int32, (rows, 1), 0) & (seg - 1)
    d = 1
    while d < seg:
        x = x + jnp.where(tpos >= d, pltpu.roll(x, d, axis=0), 0.0)
        d *= 2
    return x


def _head_rms(o, gain, n_heads, width):
    lane_head = lax.broadcasted_iota(jnp.int32, (1, n_heads * width), 1) // width
    sq = o * o
    inv = jnp.zeros_like(o)
    for h in range(n_heads):
        m = lane_head == h
        ms = jnp.sum(jnp.where(m, sq, 0.0), axis=-1, keepdims=True) * (1.0 / width)
        inv = jnp.where(m, lax.rsqrt(ms + EPS), inv)
    return (o * inv) * gain


def _stack_heads(x, n_heads, width):
    c = x.shape[0]
    t = jnp.concatenate([x] * n_heads, axis=0)
    row_head = lax.broadcasted_iota(jnp.int32, (n_heads * c, 1), 0) // c
    lane_head = lax.broadcasted_iota(jnp.int32, (1, n_heads * width), 1) // width
    return jnp.where(row_head == lane_head, t, 0.0)


def _unstack_heads(x, n_heads):
    c = x.shape[0] // n_heads
    o = x[0:c]
    for h in range(1, n_heads):
        o = o + x[h * c:(h + 1) * c]
    return o


def _rms_matmul_kernel(x_ref, g_ref, w_ref, *o_refs):
    h = _rms_rows(x_ref[...], g_ref[...]).astype(BF16)
    start = 0
    for o_ref in o_refs:
        n = o_ref.shape[1]
        o_ref[...] = jnp.dot(h, w_ref[:, start:start + n], preferred_element_type=F32)
        start += n


def _rms_matmul(x, g, w, widths, name, tm=512):
    t, d = x.shape
    n = w.shape[1]
    tm = min(tm, t)
    return pl.pallas_call(
        _rms_matmul_kernel,
        grid=(t // tm,),
        in_specs=[pl.BlockSpec((tm, d), lambda i: (i, 0)),
                  pl.BlockSpec((1, d), lambda i: (0, 0)),
                  pl.BlockSpec((d, n), lambda i: (0, 0))],
        out_specs=[pl.BlockSpec((tm, wd), lambda i: (i, 0)) for wd in widths],
        out_shape=[jax.ShapeDtypeStruct((t, wd), F32) for wd in widths],
        compiler_params=_cparams(1),
        name=name,
    )(x, g.reshape(1, d), w)


def _lru_kernel(z_ref, h0_ref, buf0_ref, cw_ref, cb_ref, aw_ref, ab_ref, xw_ref, xb_ref, lam_ref,
                y_ref, hout_ref, bufout_ref, xs_ref, hc_ref, *, bt, tl, nt):
    w = LRU_W
    rows = bt * tl
    j = pl.program_id(0) % nt

    @pl.when(j == 0)
    def _():
        xs_ref[:, 5:8, :] = buf0_ref[...]
        hc_ref[...] = h0_ref[...]

    xs_ref[:, 8:, :] = z_ref[:, :w].reshape(bt, tl, w)
    gate = z_ref[:, w:]
    cw = cw_ref[...]
    xc = cb_ref[...] + xs_ref[:, 5:5 + tl, :] * cw[0:1]
    for k in range(1, CONV_K):
        xc = xc + xs_ref[:, 5 + k:5 + k + tl, :] * cw[k:k + 1]
    tail = xs_ref[:, 5 + tl:8 + tl, :]
    xs_ref[:, 5:8, :] = tail
    bufout_ref[...] = tail

    xc = xc.reshape(rows, w)
    r = _sigmoid(_mm(xc, aw_ref[...]) + ab_ref[...])
    ig = _sigmoid(_mm(xc, xw_ref[...]) + xb_ref[...])
    log_a = (-LRU_C * r) * _softplus(-lam_ref[...])
    a = jnp.exp(log_a)
    th = jnp.tanh(log_a)
    b = jnp.sqrt((-2.0 * th) / (1.0 - th)) * (ig * xc)

    sub = 8
    gps = tl // sub
    a3 = a.reshape(rows // sub, sub, w)
    b3 = b.reshape(rows // sub, sub, w)
    spos = lax.broadcasted_iota(jnp.int32, (1, sub, 1), 1)
    d = 1
    while d < sub:
        m = spos >= d
        b3 = jnp.where(m, a3 * pltpu.roll(b3, d, axis=1) + b3, b3)
        a3 = jnp.where(m, a3 * pltpu.roll(a3, d, axis=1), a3)
        d *= 2
    a4 = a3.reshape(bt, gps, sub, w)
    b4 = b3.reshape(bt, gps, sub, w)
    carry = hc_ref[...]
    hs = []
    for r in range(gps):
        hr = b4[:, r] + a4[:, r] * carry
        hs.append(hr)
        carry = hr[:, sub - 1:sub, :]
    h = jnp.stack(hs, axis=1).reshape(rows, w)
    hlast = carry
    hc_ref[...] = hlast
    hout_ref[...] = hlast
    y_ref[...] = h * _gelu_tanh(gate)


def _lru_mixer(z, h0, buf0, layer, cw, cb, aw, ab, xw, xb, lam, batch, seq):
    w = LRU_W
    tl = min(seq, 256)
    bt = min(batch, max(1, 256 // seq))
    nt = seq // tl
    rows = bt * tl
    grid = (batch * seq // rows,)
    if nt > 1:
        sidx = lambda i: (i // nt, 0, 0)
        lidx = lambda i: (layer, i // nt, 0, 0)
    else:
        sidx = lambda i: (i, 0, 0)
        lidx = lambda i: (layer, i, 0, 0)
    full2 = lambda i: (0, 0)
    return pl.pallas_call(
        functools.partial(_lru_kernel, bt=bt, tl=tl, nt=nt),
        grid=grid,
        in_specs=[pl.BlockSpec((rows, 2 * w), lambda i: (i, 0)),
                  pl.BlockSpec((None, bt, 1, w), lidx),
                  pl.BlockSpec((None, bt, CONV_K - 1, w), lidx),
                  pl.BlockSpec((CONV_K, w), full2),
                  pl.BlockSpec((1, w), full2),
                  pl.BlockSpec((w, w), full2),
                  pl.BlockSpec((1, w), full2),
                  pl.BlockSpec((w, w), full2),
                  pl.BlockSpec((1, w), full2),
                  pl.BlockSpec((1, w), full2)],
        out_specs=[pl.BlockSpec((rows, w), lambda i: (i, 0)),
                   pl.BlockSpec((bt, 1, w), sidx),
                   pl.BlockSpec((bt, CONV_K - 1, w), sidx)],
        out_shape=[jax.ShapeDtypeStruct((batch * seq, w), F32),
                   jax.ShapeDtypeStruct((batch, 1, w), F32),
                   jax.ShapeDtypeStruct((batch, CONV_K - 1, w), F32)],
        scratch_shapes=[pltpu.VMEM((bt, 8 + tl, w), F32), pltpu.VMEM((bt, 1, w), F32)],
        compiler_params=_cparams(1),
        name="lru_mixer",
    )(z, h0.reshape(h0.shape[0], batch, 1, w), buf0, cw, cb.reshape(1, w), aw, ab.reshape(1, w), xw,
      xb.reshape(1, w), lam.reshape(1, w))


def _gla_chunk(z_ref, w2_ref, gb_ref, gn_ref, y_ref, sout_ref, s_ref, *, c, sc):
    nh = GLA_HEADS
    kw = nh * GLA_DK
    vw = nh * GLA_DV

    q = z_ref[:, 0:kw] * (GLA_DK ** -0.5)
    k = z_ref[:, kw:2 * kw]
    v = z_ref[:, 2 * kw:2 * kw + vw]
    g = z_ref[:, 2 * kw + vw:2 * kw + 2 * vw]
    lr = z_ref[:, 2 * kw + 2 * vw:]
    gk = -_softplus(-(_mm(lr, w2_ref[...]) + gb_ref[...])) / GLA_TAU
    gcum = _seg_cumsum_rows(gk, sc)
    qp = q * jnp.exp(gcum)
    kp = k * jnp.exp(-gcum)

    qs = _stack_heads(qp, nh, GLA_DK)
    ks = _stack_heads(kp, nh, GLA_DK)
    a = _mm_nt(qs, ks)
    ri = lax.broadcasted_iota(jnp.int32, (nh * c, nh * c), 0)
    ci = lax.broadcasted_iota(jnp.int32, (nh * c, nh * c), 1)
    a = jnp.where((ri // sc == ci // sc) & (ci <= ri), a, 0.0)
    o = _unstack_heads(_mm(a, _stack_heads(v, nh, GLA_DV)), nh)

    s = s_ref[...]
    eye = (lax.broadcasted_iota(jnp.int32, (kw, kw), 0) ==
           lax.broadcasted_iota(jnp.int32, (kw, kw), 1))
    bd = (lax.broadcasted_iota(jnp.int32, (kw, vw), 0) // GLA_DK ==
          lax.broadcasted_iota(jnp.int32, (kw, vw), 1) // GLA_DV)
    o_inter = []
    for i in range(c // sc):
        lo, hi = i * sc, (i + 1) * sc
        o_inter.append(_mm(qp[lo:hi], s))
        glast = gcum[hi - 1:hi]
        kpp = k[lo:hi] * jnp.exp(glast - gcum[lo:hi])
        u = _mm_tn(kpp, v[lo:hi])
        dcol = jnp.sum(jnp.where(eye, jnp.exp(glast), 0.0), axis=1, keepdims=True)
        s = dcol * s + jnp.where(bd, u, 0.0)
    s_ref[...] = s
    for h in range(nh):
        sout_ref[h] = s[h * GLA_DK:(h + 1) * GLA_DK, h * GLA_DV:(h + 1) * GLA_DV]
    o = o + jnp.concatenate(o_inter, axis=0)
    y_ref[...] = _head_rms(o, gn_ref[...], nh, GLA_DV) * _silu(g)


def _gla_kernel(z_ref, s0_ref, w2_ref, gb_ref, gn_ref, y_ref, sout_ref, s_ref, *, c, sc, g):
    @pl.when(pl.program_id(1) == 0)
    def _():
        s_ref[...] = jnp.zeros_like(s_ref)
        for b in range(g):
            for h in range(GLA_HEADS):
                s_ref[b, h * GLA_DK:(h + 1) * GLA_DK, h * GLA_DV:(h + 1) * GLA_DV] = s0_ref[b, h]

    for b in range(g):
        _gla_chunk(z_ref.at[b], w2_ref, gb_ref, gn_ref, y_ref.at[b], sout_ref.at[b], s_ref.at[b],
                   c=c, sc=sc)


def _gla_mixer(z, s0, layer, w2, gb, gn, batch, seq):
    c = min(seq, MIX_CHUNK)
    sc = min(c, GLA_SUB)
    nt = seq // c
    g = _mixer_group(batch, c)
    kw, vw = GLA_HEADS * GLA_DK, GLA_HEADS * GLA_DV
    zw = z.shape[2]
    full2 = lambda i, j: (0, 0)
    sblk = (g, GLA_HEADS, GLA_DK, GLA_DV)
    return pl.pallas_call(
        functools.partial(_gla_kernel, c=c, sc=sc, g=g),
        grid=(batch // g, nt),
        in_specs=[pl.BlockSpec((g, c, zw), lambda i, j: (i, j, 0)),
                  pl.BlockSpec((None,) + sblk, lambda i, j: (layer, i, 0, 0, 0)),
                  pl.BlockSpec((LANES, kw), full2),
                  pl.BlockSpec((1, kw), full2),
                  pl.BlockSpec((1, vw), full2)],
        out_specs=[pl.BlockSpec((g, c, vw), lambda i, j: (i, j, 0)),
                   pl.BlockSpec(sblk, lambda i, j: (i, 0, 0, 0))],
        out_shape=[jax.ShapeDtypeStruct((batch, seq, vw), F32),
                   jax.ShapeDtypeStruct((batch, GLA_HEADS, GLA_DK, GLA_DV), F32)],
        scratch_shapes=[pltpu.VMEM((g, kw, vw), F32)],
        compiler_params=_cparams(2),
        name="gla_mixer",
    )(z, s0, w2, gb.reshape(1, kw), gn.reshape(1, vw))


def _gdn_prep(z_ref, cw_ref, alog_ref, dtb_ref, bufout_ref, xs_ref, *, c):
    nh = GDN_HEADS
    hw = nh * GDN_DK
    cw3 = 3 * hw

    xs_ref[8:, :] = z_ref[:, 0:cw3]
    cw = cw_ref[...]
    qkv = xs_ref[5:5 + c, :] * cw[0:1]
    for kk in range(1, CONV_K):
        qkv = qkv + xs_ref[5 + kk:5 + kk + c, :] * cw[kk:kk + 1]
    tail = xs_ref[5 + c:8 + c, :]
    xs_ref[5:8, :] = tail
    bufout_ref[...] = tail
    qkv = _silu(qkv)
    zg = z_ref[:, cw3:cw3 + hw]
    sm = z_ref[:, cw3 + hw:]

    lane_head = lax.broadcasted_iota(jnp.int32, (1, hw), 1) // GDN_DK

    def l2n(x):
        sq = x * x
        inv = jnp.zeros_like(x)
        for h in range(nh):
            m = lane_head == h
            ss = jnp.sum(jnp.where(m, sq, 0.0), axis=-1, keepdims=True)
            inv = jnp.where(m, lax.rsqrt(ss + EPS), inv)
        return x * inv

    q = l2n(qkv[:, 0:hw]) * (GDN_DK ** -0.5)
    k = l2n(qkv[:, hw:2 * hw])
    v = qkv[:, 2 * hw:3 * hw]
    beta = _sigmoid(sm)
    glog = -jnp.exp(alog_ref[...]) * _softplus(sm + dtb_ref[...])
    gcum = _seg_cumsum_rows(glog, c)

    n = nh * c
    bcol = jnp.concatenate([beta[:, h:h + 1] for h in range(nh)], axis=0)
    gcol = jnp.concatenate([gcum[:, nh + h:nh + h + 1] for h in range(nh)], axis=0)
    glast = jnp.concatenate(
        [jnp.broadcast_to(gcum[c - 1:c, nh + h:nh + h + 1], (c, 1)) for h in range(nh)], axis=0)
    ri = lax.broadcasted_iota(jnp.int32, (n, n), 0)
    ci = lax.broadcasted_iota(jnp.int32, (n, n), 1)
    grow = jnp.sum(jnp.where(ri == ci, gcol, 0.0), axis=0, keepdims=True)
    same = ri // c == ci // c
    incl = same & (ci <= ri)
    strict = same & (ci < ri)
    dec = jnp.where(incl, jnp.exp(jnp.where(incl, gcol - grow, 0.0)), 0.0)

    sdec = jnp.concatenate(
        [jnp.broadcast_to(jnp.exp(gcum[c - 1:c, nh + h:nh + h + 1]), (GDN_DK, 1)) for h in range(nh)],
        axis=0)
    return dict(ks=_stack_heads(k, nh, GDN_DK), qs=_stack_heads(q, nh, GDN_DK),
                vs=_stack_heads(v, nh, GDN_DV), bcol=bcol, egc=jnp.exp(gcol), dec=dec,
                kdec=jnp.exp(glast - gcol), sdec=sdec, zg=zg)


def _gdn_kernel(z_ref, s0_ref, buf0_ref, cw_ref, alog_ref, dtb_ref, gn_ref,
                y_ref, sout_ref, bufout_ref, xs_ref, s_ref, *, c, g):
    nh = GDN_HEADS
    hw = nh * GDN_DK
    n = nh * c

    @pl.when(pl.program_id(1) == 0)
    def _():
        xs_ref[:, 5:8, :] = buf0_ref[...]
        s_ref[...] = jnp.zeros_like(s_ref)
        for b in range(g):
            for h in range(nh):
                s_ref[b, h * GDN_DK:(h + 1) * GDN_DK, h * GDN_DV:(h + 1) * GDN_DV] = s0_ref[b, h]

    bs = range(g)
    pr = [_gdn_prep(z_ref.at[b], cw_ref, alog_ref, dtb_ref, bufout_ref.at[b], xs_ref.at[b], c=c)
          for b in bs]
    ri = lax.broadcasted_iota(jnp.int32, (n, n), 0)
    ci = lax.broadcasted_iota(jnp.int32, (n, n), 1)
    same = ri // c == ci // c
    incl = same & (ci <= ri)
    strict = same & (ci < ri)
    kq = [_mm_nt(jnp.concatenate([pr[b]["ks"], pr[b]["qs"]], axis=0), pr[b]["ks"]) for b in bs]

    p = [jnp.where(strict, -(pr[b]["bcol"] * kq[b][0:n]) * pr[b]["dec"], 0.0) for b in bs]
    tinv = [jnp.where(ri == ci, 1.0, 0.0) + p[b] for b in bs]
    span = 2
    while span < c:
        p = [_mm(p[b], p[b]) for b in bs]
        tinv = [tinv[b] + _mm(tinv[b], p[b]) for b in bs]
        span *= 2

    s = [s_ref[b] for b in bs]
    uw = [_mm(tinv[b], jnp.concatenate([pr[b]["vs"] * pr[b]["bcol"],
                                        pr[b]["ks"] * (pr[b]["bcol"] * pr[b]["egc"])], axis=1))
          for b in bs]
    qw = [_mm(jnp.concatenate([pr[b]["qs"] * pr[b]["egc"], uw[b][:, hw:]], axis=0), s[b])
          for b in bs]
    vnew = [uw[b][:, 0:hw] - qw[b][n:] for b in bs]
    av = [_mm(jnp.where(incl, kq[b][n:] * pr[b]["dec"], 0.0), vnew[b]) for b in bs]
    kv = [_mm_tn(pr[b]["ks"] * pr[b]["kdec"], vnew[b]) for b in bs]
    for b in bs:
        o = _unstack_heads(qw[b][0:n] + av[b], nh)
        sn = pr[b]["sdec"] * s[b] + kv[b]
        s_ref[b] = sn
        for h in range(nh):
            sout_ref[b, h] = sn[h * GDN_DK:(h + 1) * GDN_DK, h * GDN_DV:(h + 1) * GDN_DV]
        y_ref[b] = _head_rms(o, gn_ref[...], nh, GDN_DV) * _silu(pr[b]["zg"])


def _mixer_group(batch, c):
    return min(batch, max(4, 128 // c))


def _gdn_mixer(z, s0, buf0, layer, cw, alog, dtb, gn, batch, seq):
    c = min(seq, MIX_CHUNK)
    nt = seq // c
    g = _mixer_group(batch, c)
    hw = GDN_HEADS * GDN_DK
    zw = z.shape[2]
    full2 = lambda i, j: (0, 0)
    sblk = (g, GDN_HEADS, GDN_DK, GDN_DV)
    return pl.pallas_call(
        functools.partial(_gdn_kernel, c=c, g=g),
        grid=(batch // g, nt),
        in_specs=[pl.BlockSpec((g, c, zw), lambda i, j: (i, j, 0)),
                  pl.BlockSpec((None,) + sblk, lambda i, j: (layer, i, 0, 0, 0)),
                  pl.BlockSpec((None, g, CONV_K - 1, 3 * hw), lambda i, j: (layer, i, 0, 0)),
                  pl.BlockSpec((CONV_K, 3 * hw), full2),
                  pl.BlockSpec((1, LANES), full2),
                  pl.BlockSpec((1, LANES), full2),
                  pl.BlockSpec((1, hw), full2)],
        out_specs=[pl.BlockSpec((g, c, hw), lambda i, j: (i, j, 0)),
                   pl.BlockSpec(sblk, lambda i, j: (i, 0, 0, 0)),
                   pl.BlockSpec((g, CONV_K - 1, 3 * hw), lambda i, j: (i, 0, 0))],
        out_shape=[jax.ShapeDtypeStruct((batch, seq, hw), F32),
                   jax.ShapeDtypeStruct((batch, GDN_HEADS, GDN_DK, GDN_DV), F32),
                   jax.ShapeDtypeStruct((batch, CONV_K - 1, 3 * hw), F32)],
        scratch_shapes=[pltpu.VMEM((g, 8 + c, 3 * hw), F32), pltpu.VMEM((g, hw, hw), F32)],
        compiler_params=_cparams(2),
        name="gdn_mixer",
    )(z, s0, buf0, cw, alog, dtb, gn.reshape(1, hw))


def _outproj_kernel(x_ref, yl_ref, yg_ref, yd_ref, wo_ref, gq_ref, wq_ref, xn_ref, q_ref):
    lw = LRU_W
    gw = GLA_HEADS * GLA_DV
    y = _mm(yl_ref[...], wo_ref[0:lw, :])
    y = y + _mm(yg_ref[...], wo_ref[lw:lw + gw, :])
    y = y + _mm(yd_ref[...], wo_ref[lw + gw:, :])
    xn = x_ref[...] + y
    xn_ref[...] = xn
    q_ref[...] = _mm(_rms_rows(xn, gq_ref[...]), wq_ref[...])


def _outproj(x, yl, yg, yd, wo, gq, wq, tm=512):
    t, d = x.shape
    tm = min(tm, t)
    row = lambda i: (i, 0)
    full2 = lambda i: (0, 0)
    return pl.pallas_call(
        _outproj_kernel,
        grid=(t // tm,),
        in_specs=[pl.BlockSpec((tm, d), row),
                  pl.BlockSpec((tm, yl.shape[1]), row),
                  pl.BlockSpec((tm, yg.shape[1]), row),
                  pl.BlockSpec((tm, yd.shape[1]), row),
                  pl.BlockSpec((d, d), full2),
                  pl.BlockSpec((1, d), full2),
                  pl.BlockSpec((d, d), full2)],
        out_specs=[pl.BlockSpec((tm, d), row), pl.BlockSpec((tm, d), row)],
        out_shape=[jax.ShapeDtypeStruct((t, d), F32), jax.ShapeDtypeStruct((t, d), F32)],
        compiler_params=_cparams(1),
        name="outproj_qproj",
    )(x, yl, yg, yd, wo, gq.reshape(1, d), wq)


def _attn_kernel(x_ref, q_ref, k_ref, v_ref, wo_ref, o_ref):
    hd = MEM_HD
    acc = x_ref[...]
    for h in range(MEM_HEADS):
        sl = slice(h * hd, (h + 1) * hd)
        s = _mm_nt(q_ref[:, sl], k_ref[:, sl]) * (hd ** -0.5)
        m = jnp.max(s, axis=-1, keepdims=True)
        p = jnp.exp(s - m)
        l = jnp.sum(p, axis=-1, keepdims=True)
        oh = _mm(p, v_ref[:, sl]) / l
        acc = acc + _mm(oh, wo_ref[sl, :])
    o_ref[...] = acc


def _attn_heads_kernel(q_ref, k_ref, v_ref, o_ref, *, gb, tl):
    nh, hd = MEM_HEADS, MEM_HD
    m = k_ref.shape[1]
    row_head = lax.broadcasted_iota(jnp.int32, (nh * tl, 1), 0) // tl
    col_head = lax.broadcasted_iota(jnp.int32, (1, m * nh), 1) % nh
    for b in range(gb):
        kf = k_ref[b].reshape(m * nh, hd)
        vf = v_ref[b].reshape(m * nh, hd)
        rows = slice(b * tl, (b + 1) * tl)
        qs = jnp.concatenate([q_ref[rows, h * hd:(h + 1) * hd] for h in range(nh)], axis=0)
        s = _mm_nt(qs, kf) * (hd ** -0.5)
        s = jnp.where(row_head == col_head, s, -jnp.inf)
        mx = jnp.max(s, axis=-1, keepdims=True)
        p = jnp.exp(s - mx)
        l = jnp.sum(p, axis=-1, keepdims=True)
        o = _mm(p, vf) / l
        for h in range(nh):
            o_ref[rows, h * hd:(h + 1) * hd] = o[h * tl:(h + 1) * tl]


def _oproj_kernel(x_ref, a_ref, wo_ref, o_ref):
    o_ref[...] = x_ref[...] + _mm(a_ref[...], wo_ref[...])


def _attention(x, q, k, v, layer, wo, batch, seq):
    t, d = x.shape
    if k.ndim == 4:
        tl = min(seq, 512)
        nl = seq // tl
        kblk = (None, None) + k.shape[2:]
        kidx = lambda b, j: (layer, b, 0, 0)
        row = lambda b, j: (b * nl + j, 0)
        return pl.pallas_call(
            _attn_kernel,
            grid=(batch, nl),
            in_specs=[pl.BlockSpec((tl, d), row),
                      pl.BlockSpec((tl, d), row),
                      pl.BlockSpec(kblk, kidx),
                      pl.BlockSpec(kblk, kidx),
                      pl.BlockSpec((d, d), lambda b, j: (0, 0))],
            out_specs=pl.BlockSpec((tl, d), row),
            out_shape=jax.ShapeDtypeStruct((t, d), F32),
            compiler_params=_cparams(2),
            name="mem_attention",
        )(x, q, k, v, wo)
    gb = 2
    kblk = (None, gb) + k.shape[2:]
    kidx = lambda i: (layer, i, 0, 0, 0)
    att = pl.pallas_call(
        functools.partial(_attn_heads_kernel, gb=gb, tl=seq),
        grid=(batch // gb,),
        in_specs=[pl.BlockSpec((gb * seq, d), lambda i: (i, 0)),
                  pl.BlockSpec(kblk, kidx),
                  pl.BlockSpec(kblk, kidx)],
        out_specs=pl.BlockSpec((gb * seq, d), lambda i: (i, 0)),
        out_shape=jax.ShapeDtypeStruct((t, d), F32),
        compiler_params=_cparams(1),
        name="mem_attention_heads",
    )(q, k, v)
    tm = min(t, 512)
    return pl.pallas_call(
        _oproj_kernel,
        grid=(t // tm,),
        in_specs=[pl.BlockSpec((tm, d), lambda i: (i, 0)),
                  pl.BlockSpec((tm, d), lambda i: (i, 0)),
                  pl.BlockSpec((d, d), lambda i: (0, 0))],
        out_specs=pl.BlockSpec((tm, d), lambda i: (i, 0)),
        out_shape=jax.ShapeDtypeStruct((t, d), F32),
        compiler_params=_cparams(1),
        name="mem_oproj",
    )(x, att, wo)


def _ffn_kernel(x_ref, g_ref, wg_ref, wu_ref, wd_ref, o_ref, h_ref, acc_ref):
    f = pl.program_id(1)

    @pl.when(f == 0)
    def _():
        h_ref[...] = _rms_rows(x_ref[...], g_ref[...]).astype(BF16)
        acc_ref[...] = jnp.zeros_like(acc_ref)

    h = h_ref[...]
    a = _mm(h, wg_ref[...])
    u = _mm(h, wu_ref[...])
    acc_ref[...] += _mm(_silu(a) * u, wd_ref[...])

    @pl.when(f == pl.num_programs(1) - 1)
    def _():
        o_ref[...] = x_ref[...] + acc_ref[...]


def _ffn_dense(x, g, wg, wu, wd, tm=1024, tf=512):
    t, d = x.shape
    dff = wg.shape[1]
    tm = min(tm, t)
    return pl.pallas_call(
        _ffn_kernel,
        grid=(t // tm, dff // tf),
        in_specs=[pl.BlockSpec((tm, d), lambda i, f: (i, 0)),
                  pl.BlockSpec((1, d), lambda i, f: (0, 0)),
                  pl.BlockSpec((d, tf), lambda i, f: (0, f)),
                  pl.BlockSpec((d, tf), lambda i, f: (0, f)),
                  pl.BlockSpec((tf, d), lambda i, f: (f, 0))],
        out_specs=pl.BlockSpec((tm, d), lambda i, f: (i, 0)),
        out_shape=jax.ShapeDtypeStruct((t, d), F32),
        scratch_shapes=[pltpu.VMEM((tm, d), BF16), pltpu.VMEM((tm, d), F32)],
        compiler_params=_cparams(2),
        name="ffn_dense",
    )(x, g.reshape(1, d), wg, wu, wd)


def _router_kernel(x_ref, g_ref, whi_ref, wlo_ref, h_ref, info_ref, cnt_ref, carry_ref, *, tm):
    i = pl.program_id(0)

    @pl.when(i == 0)
    def _():
        carry_ref[...] = jnp.zeros_like(carry_ref)

    h = _rms_rows(x_ref[...], g_ref[...])
    h_ref[...] = h
    hhi = h.astype(BF16)
    hlo = (h - hhi.astype(F32)).astype(BF16)
    whi = whi_ref[...]
    logits = (jnp.dot(hhi, whi, preferred_element_type=F32)
              + jnp.dot(hlo, whi, preferred_element_type=F32)
              + jnp.dot(hhi, wlo_ref[...], preferred_element_type=F32))
    lane = lax.broadcasted_iota(jnp.int32, (tm, LANES), 1)
    neg = jnp.float32(-jnp.inf)
    logits = jnp.where(lane < N_EXPERTS, logits, neg)
    m1 = jnp.max(logits, axis=-1, keepdims=True)
    i1 = jnp.min(jnp.where(logits == m1, lane, LANES), axis=-1, keepdims=True)
    rest = jnp.where(lane == i1, neg, logits)
    m2 = jnp.max(rest, axis=-1, keepdims=True)
    i2 = jnp.min(jnp.where(rest == m2, lane, LANES), axis=-1, keepdims=True)
    e = jnp.exp(m2 - m1)
    g1 = 1.0 / (1.0 + e)
    g2 = e / (1.0 + e)
    oh1 = jnp.where(lane == i1, 1.0, 0.0)
    oh2 = jnp.where(lane == i2, 1.0, 0.0)
    oh = oh1 + oh2
    ri = lax.broadcasted_iota(jnp.int32, (tm, tm), 0)
    ci = lax.broadcasted_iota(jnp.int32, (tm, tm), 1)
    tri = jnp.where(ci < ri, 1.0, 0.0)
    before = _mm(tri, oh) + carry_ref[0:1, :]
    r1 = jnp.sum(oh1 * before, axis=-1, keepdims=True)
    r2 = jnp.sum(oh2 * before, axis=-1, keepdims=True)
    carry = carry_ref[0:1, :] + jnp.sum(oh, axis=0, keepdims=True)
    carry_ref[...] = jnp.broadcast_to(carry, carry_ref.shape)
    cnt_ref[...] = jnp.broadcast_to(carry, cnt_ref.shape)
    info = jnp.where(lane == 0, i1.astype(F32), 0.0)
    info = jnp.where(lane == 1, i2.astype(F32), info)
    info = jnp.where(lane == 2, r1, info)
    info = jnp.where(lane == 3, r2, info)
    info = jnp.where(lane == 4, g1, info)
    info = jnp.where(lane == 5, g2, info)
    info_ref[...] = info


def _router(x, g, whi, wlo, tm=512):
    t, d = x.shape
    tm = min(tm, t)
    return pl.pallas_call(
        functools.partial(_router_kernel, tm=tm),
        grid=(t // tm,),
        in_specs=[pl.BlockSpec((tm, d), lambda i: (i, 0)),
                  pl.BlockSpec((1, d), lambda i: (0, 0)),
                  pl.BlockSpec((d, LANES), lambda i: (0, 0)),
                  pl.BlockSpec((d, LANES), lambda i: (0, 0))],
        out_specs=[pl.BlockSpec((tm, d), lambda i: (i, 0)),
                   pl.BlockSpec((tm, LANES), lambda i: (i, 0)),
                   pl.BlockSpec((8, LANES), lambda i: (0, 0))],
        out_shape=[jax.ShapeDtypeStruct((t, d), F32),
                   jax.ShapeDtypeStruct((t, LANES), F32),
                   jax.ShapeDtypeStruct((8, LANES), F32)],
        scratch_shapes=[pltpu.VMEM((8, LANES), F32)],
        compiler_params=_cparams(1),
        name="moe_router",
    )(x, g.reshape(1, d), whi, wlo)


def _dispatch_kernel(s1_ref, s2_ref, zs_ref, zf_ref, h_ref, xs_hbm, zbuf_ref, zsem, sem, *, rows, sub,
                     n_blk):
    i = pl.program_id(0)

    @pl.when(i == 0)
    def _():
        zbuf_ref[...] = jnp.zeros_like(zbuf_ref)
        for e in range(N_EXPERTS):
            dst = xs_hbm.at[pl.ds(pl.multiple_of(zs_ref[e], 8), sub)]
            pltpu.make_async_copy(zbuf_ref, dst, zsem).start()
            pltpu.make_async_copy(zbuf_ref, dst, zsem).wait()

        def fill(j, carry):
            @pl.when(zf_ref[j] == 1)
            def _():
                dst = xs_hbm.at[pl.ds(pl.multiple_of(j * sub, sub), sub)]
                pltpu.make_async_copy(zbuf_ref, dst, zsem).start()
            return carry

        def drain(j, carry):
            @pl.when(zf_ref[j] == 1)
            def _():
                dst = xs_hbm.at[pl.ds(pl.multiple_of(j * sub, sub), sub)]
                pltpu.make_async_copy(zbuf_ref, dst, zsem).wait()
            return carry

        lax.fori_loop(0, n_blk, fill, 0)
        lax.fori_loop(0, n_blk, drain, 0)

    base = i * rows

    def issue(r, carry):
        src = h_ref.at[pl.ds(r, 1)]
        pltpu.make_async_copy(src, xs_hbm.at[pl.ds(s1_ref[base + r], 1)], sem).start()
        pltpu.make_async_copy(src, xs_hbm.at[pl.ds(s2_ref[base + r], 1)], sem).start()
        return carry

    lax.fori_loop(0, rows, issue, 0, unroll=8)
    for _ in range(2):
        pltpu.make_async_copy(h_ref, xs_hbm.at[pl.ds(0, rows)], sem).wait()


def _dispatch(slot1, slot2, zero_start, zero_blk, h, sub, rows=256):
    t, d = h.shape
    rows = min(rows, t)
    n_blk = zero_blk.shape[0]
    n_rows = n_blk * sub
    return pl.pallas_call(
        functools.partial(_dispatch_kernel, rows=rows, sub=sub, n_blk=n_blk),
        grid_spec=pltpu.PrefetchScalarGridSpec(
            num_scalar_prefetch=4,
            grid=(t // rows,),
            in_specs=[pl.BlockSpec((rows, d), lambda i, a, b, c, e: (i, 0))],
            out_specs=pl.BlockSpec(memory_space=pl.ANY),
            scratch_shapes=[pltpu.VMEM((sub, d), F32), pltpu.SemaphoreType.DMA(()),
                            pltpu.SemaphoreType.DMA(())]),
        out_shape=jax.ShapeDtypeStruct((n_rows, d), F32),
        compiler_params=_cparams(1),
        name="moe_dispatch",
    )(slot1, slot2, zero_start, zero_blk, h)


def _expert_kernel(be_ref, ns_ref, xi_ref, x_ref, wg_ref, wu_ref, wd_ref, o_ref, *, sub, n_sub):
    i = pl.program_id(0)
    f = pl.program_id(1)
    ns = ns_ref[i]

    @pl.when(f == 0)
    def _():
        o_ref[...] = jnp.zeros_like(o_ref)

    for s in range(n_sub):
        @pl.when(s < ns)
        def _():
            rows = slice(s * sub, (s + 1) * sub)
            h = x_ref[rows, :]
            a = _mm(h, wg_ref[...])
            u = _mm(h, wu_ref[...])
            o_ref[rows, :] += _mm(_silu(a) * u, wd_ref[...])


def _expert_ffn(blk_exp, n_valid_sub, x_blk, xs, wg, wu, wd, sb, sub, tf=512):
    p, d = xs.shape
    dff = wg.shape[2]
    n_super = n_valid_sub.shape[0]
    nf = dff // tf

    def fidx(i, f, ns):
        used = jnp.minimum(ns[i], 1)
        return f * used + (nf - 1) * (1 - used)

    return pl.pallas_call(
        functools.partial(_expert_kernel, sub=sub, n_sub=sb // sub),
        grid_spec=pltpu.PrefetchScalarGridSpec(
            num_scalar_prefetch=3,
            grid=(n_super, nf),
            in_specs=[pl.BlockSpec((sb, d), lambda i, f, be, ns, xi: (xi[i], 0)),
                      pl.BlockSpec((None, d, tf), lambda i, f, be, ns, xi: (be[i], 0, fidx(i, f, ns))),
                      pl.BlockSpec((None, d, tf), lambda i, f, be, ns, xi: (be[i], 0, fidx(i, f, ns))),
                      pl.BlockSpec((None, tf, d), lambda i, f, be, ns, xi: (be[i], fidx(i, f, ns), 0))],
            out_specs=pl.BlockSpec((sb, d), lambda i, f, be, ns, xi: (i, 0))),
        out_shape=jax.ShapeDtypeStruct((n_super * sb, d), F32),
        compiler_params=pltpu.CompilerParams(dimension_semantics=("arbitrary", "arbitrary"),
                                             vmem_limit_bytes=EXPERT_VMEM_LIMIT_BYTES),
        name="moe_experts",
    )(blk_exp, n_valid_sub, x_blk, xs, wg, wu, wd)


def _combine_kernel(s1_ref, s2_ref, x_ref, info_ref, ys_hbm, gf_ref, o_ref, b1_ref, b2_ref, sem,
                    *, rows, final_norm):
    base = pl.program_id(0) * rows

    def issue(r, carry):
        pltpu.make_async_copy(ys_hbm.at[pl.ds(s1_ref[base + r], 1)], b1_ref.at[pl.ds(r, 1)], sem).start()
        pltpu.make_async_copy(ys_hbm.at[pl.ds(s2_ref[base + r], 1)], b2_ref.at[pl.ds(r, 1)], sem).start()
        return carry

    lax.fori_loop(0, rows, issue, 0, unroll=8)
    pltpu.make_async_copy(ys_hbm.at[pl.ds(0, rows)], b1_ref, sem).wait()
    pltpu.make_async_copy(ys_hbm.at[pl.ds(0, rows)], b2_ref, sem).wait()
    info = info_ref[...]
    y = b1_ref[...] * info[:, 4:5] + b2_ref[...] * info[:, 5:6]
    out = x_ref[...] + y
    if final_norm:
        out = _rms_rows(out, gf_ref[...])
    o_ref[...] = out


def _combine(slot1, slot2, x, info, ys, gf, final_norm, rows=256):
    t, d = x.shape
    rows = min(rows, t)
    return pl.pallas_call(
        functools.partial(_combine_kernel, rows=rows, final_norm=final_norm),
        grid_spec=pltpu.PrefetchScalarGridSpec(
            num_scalar_prefetch=2,
            grid=(t // rows,),
            in_specs=[pl.BlockSpec((rows, d), lambda i, a, b: (i, 0)),
                      pl.BlockSpec((rows, LANES), lambda i, a, b: (i, 0)),
                      pl.BlockSpec(memory_space=pl.ANY),
                      pl.BlockSpec((1, d), lambda i, a, b: (0, 0))],
            out_specs=pl.BlockSpec((rows, d), lambda i, a, b: (i, 0)),
            scratch_shapes=[pltpu.VMEM((rows, d), F32), pltpu.VMEM((rows, d), F32),
                            pltpu.SemaphoreType.DMA(())]),
        out_shape=jax.ShapeDtypeStruct((t, d), F32),
        compiler_params=_cparams(1),
        name="moe_combine",
    )(slot1, slot2, x, info, ys, gf.reshape(1, d))


def _moe_ffn(x, g, whi, wlo, wg, wu, wd, gf, final_norm, sub=512):
    t, d = x.shape
    sb = 2048 if 2 * t >= 16 * 1024 else sub
    h, info, cnt = _router(x, g, whi, wlo)
    e = info[:, 0:2].astype(jnp.int32)
    rank = info[:, 2:4].astype(jnp.int32)
    counts = cnt[0, :N_EXPERTS].astype(jnp.int32)
    n_sb = (counts + sb - 1) // sb
    sb_end = jnp.cumsum(n_sb)
    sb_start = sb_end - n_sb
    row_start = sb_start * sb
    eid = jnp.arange(N_EXPERTS, dtype=jnp.int32)
    slot = jnp.sum(jnp.where(e[:, :, None] == eid, row_start, 0), axis=-1) + rank
    n_super = (2 * t) // sb + N_EXPERTS
    blk = jnp.arange(n_super, dtype=jnp.int32)
    n_used = sb_end[-1]
    used = blk < n_used
    blk_c = jnp.minimum(blk, n_used - 1)
    be = jnp.minimum(jnp.sum((blk_c[:, None] >= sb_end[None, :]).astype(jnp.int32), axis=-1),
                     N_EXPERTS - 1)
    valid = jnp.clip(counts[be] - (blk_c - sb_start[be]) * sb, 0, sb)
    n_valid_sub = jnp.where(used, (valid + sub - 1) // sub, 0).astype(jnp.int32)
    zero_start = ((row_start + counts) // 8 * 8).astype(jnp.int32)
    per = sb // sub
    sub_in_blk = jnp.arange(per, dtype=jnp.int32)
    zero_blk = (sub_in_blk[None, :] >= n_valid_sub[:, None]).astype(jnp.int32).reshape(-1)
    zero_blk = jnp.concatenate([zero_blk, jnp.ones((1,), jnp.int32)])
    xs = _dispatch(slot[:, 0], slot[:, 1], zero_start, zero_blk, h, sub)
    ys = _expert_ffn(be.astype(jnp.int32), n_valid_sub, blk_c.astype(jnp.int32), xs, wg, wu, wd, sb, sub)
    return _combine(slot[:, 0], slot[:, 1], x, info, ys, gf, final_norm)


def _final_norm_kernel(x_ref, g_ref, o_ref):
    o_ref[...] = _rms_rows(x_ref[...], g_ref[...])


def _final_norm(x, g, tm=512):
    t, d = x.shape
    tm = min(tm, t)
    return pl.pallas_call(
        _final_norm_kernel,
        grid=(t // tm,),
        in_specs=[pl.BlockSpec((tm, d), lambda i: (i, 0)), pl.BlockSpec((1, d), lambda i: (0, 0))],
        out_specs=pl.BlockSpec((tm, d), lambda i: (i, 0)),
        out_shape=jax.ShapeDtypeStruct((t, d), F32),
        compiler_params=_cparams(1),
        name="final_norm",
    )(x, g.reshape(1, d))


def _block_diag(w):
    n, c, d = w.shape
    eye = jnp.eye(n, dtype=w.dtype)
    return jnp.einsum("ncd,nm->ncmd", w, eye).reshape(n * c, n * d)


def _pad_cols(w, n):
    return jnp.pad(w, ((0, 0), (0, n - w.shape[1])))


def _layer_params(l, p):
    (norm_mix, w_in, lru_conv_w, lru_conv_b, lru_a_w, lru_a_b, lru_x_w, lru_x_b, lru_lam, gla_gk_w2,
     gla_gk_b, gla_norm, gdn_conv_w, gdn_a_log, gdn_dt_bias, gdn_norm, w_out, norm_xq, norm_mem,
     w_mq, w_mk, w_mv, w_mo, norm_ffn) = [a[l] for a in p]
    lw = LRU_W
    gk, gv = GLA_HEADS * GLA_DK, GLA_HEADS * GLA_DV
    dh = GDN_HEADS * GDN_DK
    offs = [0]
    for s in (lw, lw, gk, gk, gv, GLA_RANK, gv, dh, dh, dh, GDN_HEADS, GDN_HEADS, dh):
        offs.append(offs[-1] + s)
    col = lambda i: w_in[:, offs[i]:offs[i + 1]]
    w_lru = jnp.concatenate([col(0), col(1)], axis=1)
    w_gla = jnp.concatenate([col(2), col(3), col(4), col(6), _pad_cols(col(5), LANES)], axis=1)
    w_gdn = jnp.concatenate(
        [col(7), col(8), col(9), col(12), _pad_cols(jnp.concatenate([col(10), col(11)], axis=1), LANES)],
        axis=1)
    w_cat = jnp.concatenate([w_lru, w_gla, w_gdn], axis=1).astype(BF16)
    widths = (w_lru.shape[1], w_gla.shape[1], w_gdn.shape[1])
    alog = jnp.zeros((1, LANES), F32).at[0, GDN_HEADS:2 * GDN_HEADS].set(gdn_a_log)
    dtb = jnp.zeros((1, LANES), F32).at[0, GDN_HEADS:2 * GDN_HEADS].set(gdn_dt_bias)
    return dict(
        norm_mix=norm_mix, w_cat=w_cat, widths=widths,
        lru_conv_w=lru_conv_w, lru_conv_b=lru_conv_b,
        lru_a=_block_diag(lru_a_w).astype(BF16), lru_a_b=lru_a_b,
        lru_x=_block_diag(lru_x_w).astype(BF16), lru_x_b=lru_x_b, lru_lam=lru_lam,
        gla_w2=jnp.pad(gla_gk_w2, ((0, LANES - GLA_RANK), (0, 0))).astype(BF16), gla_gk_b=gla_gk_b,
        gla_norm=jnp.tile(gla_norm, GLA_HEADS),
        gdn_conv_w=gdn_conv_w, gdn_alog=alog, gdn_dtb=dtb, gdn_norm=jnp.tile(gdn_norm, GDN_HEADS),
        w_out=w_out.astype(BF16), norm_xq=norm_xq, norm_mem=norm_mem, w_mq=w_mq.astype(BF16),
        w_mkv=jnp.concatenate([w_mk, w_mv], axis=1).astype(BF16), w_mo=w_mo.astype(BF16),
        norm_ffn=norm_ffn)


def _run_group(x3, mem_k, mem_v, states, layers, ffn, norm_final):
    batch, seq, d = x3.shape
    x = x3.reshape(batch * seq, d)
    lru_h0, lru_buf0, gla_s0, gdn_s0, gdn_buf0, state_layer = states
    new_states = []
    for l, lp in enumerate(layers):
        sl = state_layer(l)
        z_lru, z_gla, z_gdn = _rms_matmul(x, lp["norm_mix"], lp["w_cat"], lp["widths"], "in_proj")
        y_lru, lru_h, lru_buf = _lru_mixer(
            z_lru, lru_h0, lru_buf0, sl, lp["lru_conv_w"], lp["lru_conv_b"], lp["lru_a"], lp["lru_a_b"],
            lp["lru_x"], lp["lru_x_b"], lp["lru_lam"], batch, seq)
        y_gla, gla_s = _gla_mixer(z_gla.reshape(batch, seq, -1), gla_s0, sl, lp["gla_w2"],
                                  lp["gla_gk_b"], lp["gla_norm"], batch, seq)
        y_gdn, gdn_s, gdn_buf = _gdn_mixer(z_gdn.reshape(batch, seq, -1), gdn_s0, gdn_buf0, sl,
                                           lp["gdn_conv_w"], lp["gdn_alog"], lp["gdn_dtb"],
                                           lp["gdn_norm"], batch, seq)
        x, q = _outproj(x, y_lru, y_gla.reshape(batch * seq, -1), y_gdn.reshape(batch * seq, -1),
                        lp["w_out"], lp["norm_xq"], lp["w_mq"])
        x = _attention(x, q, mem_k[l][0], mem_v[l][0], mem_k[l][1], lp["w_mo"], batch, seq)
        last = l == len(layers) - 1
        kind, fp = ffn[l]
        if kind == "dense":
            x = _ffn_dense(x, lp["norm_ffn"], *fp)
            if last:
                x = _final_norm(x, norm_final)
        else:
            x = _moe_ffn(x, lp["norm_ffn"], *fp, norm_final, last)
        new_states.append((lru_h.reshape(batch, LRU_W), lru_buf, gla_s, gdn_s, gdn_buf))
    return x.reshape(batch, seq, d), [jnp.stack(s) for s in zip(*new_states)]


def kernel(x_prompt, x_sample, mem_prompt, state_lru_h, state_lru_conv, state_gla, state_gdn, state_gdn_conv, cache_mem_k, cache_mem_v, norm_mix, w_in, lru_conv_w, lru_conv_b, lru_a_w, lru_a_b, lru_x_w, lru_x_b, lru_lam, gla_gk_w2, gla_gk_b, gla_norm, gdn_conv_w, gdn_a_log, gdn_dt_bias, gdn_norm, w_out, norm_xq, norm_mem, w_mq, w_mk, w_mv, w_mo, norm_ffn, w_ff_gate, w_ff_up, w_ff_down, w_router, w_e_gate, w_e_up, w_e_down, norm_final):
    depth = norm_mix.shape[0]
    per_layer = (norm_mix, w_in, lru_conv_w, lru_conv_b, lru_a_w, lru_a_b, lru_x_w, lru_x_b, lru_lam,
                 gla_gk_w2, gla_gk_b, gla_norm, gdn_conv_w, gdn_a_log, gdn_dt_bias, gdn_norm, w_out,
                 norm_xq, norm_mem, w_mq, w_mk, w_mv, w_mo, norm_ffn)
    layers = [_layer_params(l, per_layer) for l in range(depth)]
    ffn = []
    for l in range(depth):
        j = l // 2
        if l % 2 == 0:
            ffn.append(("dense", (w_ff_gate[j], w_ff_up[j], w_ff_down[j])))
        else:
            wr = _pad_cols(w_router[j], LANES)
            whi = wr.astype(BF16)
            wlo = (wr - whi.astype(F32)).astype(BF16)
            ffn.append(("moe", (whi, wlo, w_e_gate[j], w_e_up[j], w_e_down[j])))

    bp, mlen, d = mem_prompt.shape
    mem2 = mem_prompt.reshape(bp * mlen, d)
    pk, pv = [], []
    for lp in layers:
        k2, v2 = _rms_matmul(mem2, lp["norm_mem"], lp["w_mkv"], (d, d), "mem_kv")
        pk.append(k2.reshape(1, bp, mlen, d))
        pv.append(v2.reshape(1, bp, mlen, d))
    zero_state = (jnp.zeros((1, bp, LRU_W), F32), jnp.zeros((1, bp, CONV_K - 1, LRU_W), F32),
                  jnp.zeros((1, bp, GLA_HEADS, GLA_DK, GLA_DV), F32),
                  jnp.zeros((1, bp, GDN_HEADS, GDN_DK, GDN_DV), F32),
                  jnp.zeros((1, bp, CONV_K - 1, 3 * GDN_HEADS * GDN_DK), F32),
                  lambda l: 0)
    y_prompt, p_st = _run_group(x_prompt, [(k, 0) for k in pk], [(v, 0) for v in pv], zero_state,
                                layers, ffn, norm_final)
    p_mem_k = jnp.concatenate(pk, axis=0).reshape(depth, bp, mlen, MEM_HEADS, MEM_HD)
    p_mem_v = jnp.concatenate(pv, axis=0).reshape(depth, bp, mlen, MEM_HEADS, MEM_HD)

    s_in = (state_lru_h, state_lru_conv, state_gla, state_gdn, state_gdn_conv, lambda l: l)
    y_sample, s_st = _run_group(x_sample, [(cache_mem_k, l) for l in range(depth)],
                                [(cache_mem_v, l) for l in range(depth)], s_in, layers, ffn, norm_final)

    return (y_prompt, y_sample, p_st[0], p_st[1], p_st[2], p_st[3], p_st[4], p_mem_k, p_mem_v,
            s_st[0], s_st[1], s_st[2], s_st[3], s_st[4])
```

```python
import functools

import jax
import jax.numpy as jnp
from jax import lax
from jax.experimental import pallas as pl
from jax.experimental.pallas import tpu as pltpu

F32 = jnp.float32
BF16 = jnp.bfloat16
EPS = 1e-6

D_MODEL = 1024
LRU_W = 512
LRU_BLOCKS = 8
LRU_C = 8.0
CONV_K = 4
GLA_HEADS = 4
GLA_DK = 32
GLA_DV = 64
GLA_RANK = 16
GLA_TAU = 16.0
GLA_SUB = 16
GDN_HEADS = 4
GDN_DK = 64
GDN_DV = 64
MIX_CHUNK = 64
MEM_HEADS = 4
MEM_HD = 256
N_EXPERTS = 8
LANES = 128
VMEM_LIMIT_BYTES = 48 * 1024 * 1024
EXPERT_VMEM_LIMIT_BYTES = 56 * 1024 * 1024


def _cparams(n_axes):
    return pltpu.CompilerParams(dimension_semantics=("arbitrary",) * n_axes,
                                vmem_limit_bytes=VMEM_LIMIT_BYTES)


def _mm(a, b):
    return jnp.dot(a.astype(BF16), b.astype(BF16), preferred_element_type=F32)


def _mm_nt(a, b):
    return lax.dot_general(a.astype(BF16), b.astype(BF16), (((1,), (1,)), ((), ())),
                           preferred_element_type=F32)


def _mm_tn(a, b):
    return lax.dot_general(a.astype(BF16), b.astype(BF16), (((0,), (0,)), ((), ())),
                           preferred_element_type=F32)


def _rms_rows(x, g):
    ms = jnp.mean(x * x, axis=-1, keepdims=True)
    return (x * lax.rsqrt(ms + EPS)) * g


def _softplus(x):
    return jnp.maximum(x, 0.0) + jnp.log1p(jnp.exp(-jnp.abs(x)))


def _sigmoid(x):
    return 1.0 / (1.0 + jnp.exp(-x))


def _silu(x):
    return x * _sigmoid(x)


def _gelu_tanh(x):
    c = 0.7978845608028654
    return x * (0.5 * (1.0 + jnp.tanh(c * (x + 0.044715 * (x * x * x)))))


def _seg_cumsum_rows(x, seg):
    rows = x.shape[0]
    tpos = lax.broadcasted_iota(jnp.int32, (rows, 1), 0) & (seg - 1)
    d = 1
    while d < seg:
        x = x + jnp.where(tpos >= d, pltpu.roll(x, d, axis=0), 0.0)
        d *= 2
    return x


def _head_rms(o, gain, n_heads, width):
    lane_head = lax.broadcasted_iota(jnp.int32, (1, n_heads * width), 1) // width
    sq = o * o
    inv = jnp.zeros_like(o)
    for h in range(n_heads):
        m = lane_head == h
        ms = jnp.sum(jnp.where(m, sq, 0.0), axis=-1, keepdims=True) * (1.0 / width)
        inv = jnp.where(m, lax.rsqrt(ms + EPS), inv)
    return (o * inv) * gain


def _stack_heads(x, n_heads, width):
    c = x.shape[0]
    t = jnp.concatenate([x] * n_heads, axis=0)
    row_head = lax.broadcasted_iota(jnp.int32, (n_heads * c, 1), 0) // c
    lane_head = lax.broadcasted_iota(jnp.int32, (1, n_heads * width), 1) // width
    return jnp.where(row_head == lane_head, t, 0.0)


def _unstack_heads(x, n_heads):
    c = x.shape[0] // n_heads
    o = x[0:c]
    for h in range(1, n_heads):
        o = o + x[h * c:(h + 1) * c]
    return o


def _rms_matmul_kernel(x_ref, g_ref, w_ref, *o_refs):
    h = _rms_rows(x_ref[...], g_ref[...]).astype(BF16)
    start = 0
    for o_ref in o_refs:
        n = o_ref.shape[1]
        o_ref[...] = jnp.dot(h, w_ref[:, start:start + n], preferred_element_type=F32)
        start += n


def _rms_matmul(x, g, w, widths, name, tm=512):
    t, d = x.shape
    n = w.shape[1]
    tm = min(tm, t)
    return pl.pallas_call(
        _rms_matmul_kernel,
        grid=(t // tm,),
        in_specs=[pl.BlockSpec((tm, d), lambda i: (i, 0)),
                  pl.BlockSpec((1, d), lambda i: (0, 0)),
                  pl.BlockSpec((d, n), lambda i: (0, 0))],
        out_specs=[pl.BlockSpec((tm, wd), lambda i: (i, 0)) for wd in widths],
        out_shape=[jax.ShapeDtypeStruct((t, wd), F32) for wd in widths],
        compiler_params=_cparams(1),
        name=name,
    )(x, g.reshape(1, d), w)


def _lru_kernel(z_ref, h0_ref, buf0_ref, cw_ref, cb_ref, aw_ref, ab_ref, xw_ref, xb_ref, lam_ref,
                y_ref, hout_ref, bufout_ref, xs_ref, hc_ref, *, bt, tl, nt):
    w = LRU_W
    rows = bt * tl
    j = pl.program_id(0) % nt

    @pl.when(j == 0)
    def _():
        xs_ref[:, 5:8, :] = buf0_ref[...]
        hc_ref[...] = h0_ref[...]

    xs_ref[:, 8:, :] = z_ref[:, :w].reshape(bt, tl, w)
    gate = z_ref[:, w:]
    cw = cw_ref[...]
    xc = cb_ref[...] + xs_ref[:, 5:5 + tl, :] * cw[0:1]
    for k in range(1, CONV_K):
        xc = xc + xs_ref[:, 5 + k:5 + k + tl, :] * cw[k:k + 1]
    tail = xs_ref[:, 5 + tl:8 + tl, :]
    xs_ref[:, 5:8, :] = tail
    bufout_ref[...] = tail

    xc = xc.reshape(rows, w)
    r = _sigmoid(_mm(xc, aw_ref[...]) + ab_ref[...])
    ig = _sigmoid(_mm(xc, xw_ref[...]) + xb_ref[...])
    log_a = (-LRU_C * r) * _softplus(-lam_ref[...])
    a = jnp.exp(log_a)
    th = jnp.tanh(log_a)
    b = jnp.sqrt((-2.0 * th) / (1.0 - th)) * (ig * xc)

    sub = 8
    gps = tl // sub
    a3 = a.reshape(rows // sub, sub, w)
    b3 = b.reshape(rows // sub, sub, w)
    spos = lax.broadcasted_iota(jnp.int32, (1, sub, 1), 1)
    d = 1
    while d < sub:
        m = spos >= d
        b3 = jnp.where(m, a3 * pltpu.roll(b3, d, axis=1) + b3, b3)
        a3 = jnp.where(m, a3 * pltpu.roll(a3, d, axis=1), a3)
        d *= 2
    a4 = a3.reshape(bt, gps, sub, w)
    b4 = b3.reshape(bt, gps, sub, w)
    carry = hc_ref[...]
    hs = []
    for r in range(gps):
        hr = b4[:, r] + a4[:, r] * carry
        hs.append(hr)
        carry = hr[:, sub - 1:sub, :]
    h = jnp.stack(hs, axis=1).reshape(rows, w)
    hlast = carry
    hc_ref[...] = hlast
    hout_ref[...] = hlast
    y_ref[...] = h * _gelu_tanh(gate)


def _lru_mixer(z, h0, buf0, layer, cw, cb, aw, ab, xw, xb, lam, batch, seq):
    w = LRU_W
    tl = min(seq, 256)
    bt = min(batch, max(1, 256 // seq))
    nt = seq // tl
    rows = bt * tl
    grid = (batch * seq // rows,)
    if nt > 1:
        sidx = lambda i: (i // nt, 0, 0)
        lidx = lambda i: (layer, i // nt, 0, 0)
    else:
        sidx = lambda i: (i, 0, 0)
        lidx = lambda i: (layer, i, 0, 0)
    full2 = lambda i: (0, 0)
    return pl.pallas_call(
        functools.partial(_lru_kernel, bt=bt, tl=tl, nt=nt),
        grid=grid,
        in_specs=[pl.BlockSpec((rows, 2 * w), lambda i: (i, 0)),
                  pl.BlockSpec((None, bt, 1, w), lidx),
                  pl.BlockSpec((None, bt, CONV_K - 1, w), lidx),
                  pl.BlockSpec((CONV_K, w), full2),
                  pl.BlockSpec((1, w), full2),
                  pl.BlockSpec((w, w), full2),
                  pl.BlockSpec((1, w), full2),
                  pl.BlockSpec((w, w), full2),
                  pl.BlockSpec((1, w), full2),
                  pl.BlockSpec((1, w), full2)],
        out_specs=[pl.BlockSpec((rows, w), lambda i: (i, 0)),
                   pl.BlockSpec((bt, 1, w), sidx),
                   pl.BlockSpec((bt, CONV_K - 1, w), sidx)],
        out_shape=[jax.ShapeDtypeStruct((batch * seq, w), F32),
                   jax.ShapeDtypeStruct((batch, 1, w), F32),
                   jax.ShapeDtypeStruct((batch, CONV_K - 1, w), F32)],
        scratch_shapes=[pltpu.VMEM((bt, 8 + tl, w), F32), pltpu.VMEM((bt, 1, w), F32)],
        compiler_params=_cparams(1),
        name="lru_mixer",
    )(z, h0.reshape(h0.shape[0], batch, 1, w), buf0, cw, cb.reshape(1, w), aw, ab.reshape(1, w), xw,
      xb.reshape(1, w), lam.reshape(1, w))


def _gla_kernel(z_ref, s0_ref, w2_ref, gb_ref, gn_ref, y_ref, sout_ref, s_ref, *, c, sc, g):
    nh = GLA_HEADS
    kw = nh * GLA_DK
    vw = nh * GLA_DV

    @pl.when(pl.program_id(1) == 0)
    def _():
        s_ref[...] = jnp.zeros_like(s_ref)
        for b in range(g):
            for h in range(nh):
                s_ref[b, h * GLA_DK:(h + 1) * GLA_DK, h * GLA_DV:(h + 1) * GLA_DV] = s0_ref[b, h]

    bs = range(g)
    q = [z_ref[b, :, 0:kw] * (GLA_DK ** -0.5) for b in bs]
    k = [z_ref[b, :, kw:2 * kw] for b in bs]
    v = [z_ref[b, :, 2 * kw:2 * kw + vw] for b in bs]
    gk = [-_softplus(-(_mm(z_ref[b, :, 2 * kw + 2 * vw:], w2_ref[...]) + gb_ref[...])) / GLA_TAU
          for b in bs]
    gcum = [_seg_cumsum_rows(gk[b], sc) for b in bs]
    qp = [q[b] * jnp.exp(gcum[b]) for b in bs]

    ri = lax.broadcasted_iota(jnp.int32, (nh * c, nh * c), 0)
    ci = lax.broadcasted_iota(jnp.int32, (nh * c, nh * c), 1)
    keep = (ri // sc == ci // sc) & (ci <= ri)
    a = [_mm_nt(_stack_heads(qp[b], nh, GLA_DK), _stack_heads(k[b] * jnp.exp(-gcum[b]), nh, GLA_DK))
         for b in bs]
    o = [_unstack_heads(_mm(jnp.where(keep, a[b], 0.0), _stack_heads(v[b], nh, GLA_DV)), nh)
         for b in bs]

    s = [s_ref[b] for b in bs]
    eye = (lax.broadcasted_iota(jnp.int32, (kw, kw), 0) ==
           lax.broadcasted_iota(jnp.int32, (kw, kw), 1))
    bd = (lax.broadcasted_iota(jnp.int32, (kw, vw), 0) // GLA_DK ==
          lax.broadcasted_iota(jnp.int32, (kw, vw), 1) // GLA_DV)
    o_inter = [[] for _ in bs]
    for i in range(c // sc):
        lo, hi = i * sc, (i + 1) * sc
        for b in bs:
            o_inter[b].append(_mm(qp[b][lo:hi], s[b]))
            glast = gcum[b][hi - 1:hi]
            kpp = k[b][lo:hi] * jnp.exp(glast - gcum[b][lo:hi])
            u = _mm_tn(kpp, v[b][lo:hi])
            dcol = jnp.sum(jnp.where(eye, jnp.exp(glast), 0.0), axis=1, keepdims=True)
            s[b] = dcol * s[b] + jnp.where(bd, u, 0.0)
    for b in bs:
        s_ref[b] = s[b]
        for h in range(nh):
            sout_ref[b, h] = s[b][h * GLA_DK:(h + 1) * GLA_DK, h * GLA_DV:(h + 1) * GLA_DV]
        ob = o[b] + jnp.concatenate(o_inter[b], axis=0)
        gate = z_ref[b, :, 2 * kw + vw:2 * kw + 2 * vw]
        y_ref[b] = _head_rms(ob, gn_ref[...], nh, GLA_DV) * _silu(gate)


def _gla_mixer(z, s0, layer, w2, gb, gn, batch, seq):
    c = min(seq, MIX_CHUNK)
    sc = min(c, GLA_SUB)
    nt = seq // c
    g = _mixer_group(batch, c)
    kw, vw = GLA_HEADS * GLA_DK, GLA_HEADS * GLA_DV
    zw = z.shape[2]
    full2 = lambda i, j: (0, 0)
    sblk = (g, GLA_HEADS, GLA_DK, GLA_DV)
    return pl.pallas_call(
        functools.partial(_gla_kernel, c=c, sc=sc, g=g),
        grid=(batch // g, nt),
        in_specs=[pl.BlockSpec((g, c, zw), lambda i, j: (i, j, 0)),
                  pl.BlockSpec((None,) + sblk, lambda i, j: (layer, i, 0, 0, 0)),
                  pl.BlockSpec((LANES, kw), full2),
                  pl.BlockSpec((1, kw), full2),
                  pl.BlockSpec((1, vw), full2)],
        out_specs=[pl.BlockSpec((g, c, vw), lambda i, j: (i, j, 0)),
                   pl.BlockSpec(sblk, lambda i, j: (i, 0, 0, 0))],
        out_shape=[jax.ShapeDtypeStruct((batch, seq, vw), F32),
                   jax.ShapeDtypeStruct((batch, GLA_HEADS, GLA_DK, GLA_DV), F32)],
        scratch_shapes=[pltpu.VMEM((g, kw, vw), F32)],
        compiler_params=_cparams(2),
        name="gla_mixer",
    )(z, s0, w2, gb.reshape(1, kw), gn.reshape(1, vw))


def _gdn_prep(z_ref, cw_ref, alog_ref, dtb_ref, bufout_ref, xs_ref, *, c):
    nh = GDN_HEADS
    hw = nh * GDN_DK
    cw3 = 3 * hw

    xs_ref[8:, :] = z_ref[:, 0:cw3]
    cw = cw_ref[...]
    qkv = xs_ref[5:5 + c, :] * cw[0:1]
    for kk in range(1, CONV_K):
        qkv = qkv + xs_ref[5 + kk:5 + kk + c, :] * cw[kk:kk + 1]
    tail = xs_ref[5 + c:8 + c, :]
    xs_ref[5:8, :] = tail
    bufout_ref[...] = tail
    qkv = _silu(qkv)
    zg = z_ref[:, cw3:cw3 + hw]
    sm = z_ref[:, cw3 + hw:]

    lane_head = lax.broadcasted_iota(jnp.int32, (1, hw), 1) // GDN_DK

    def l2n(x):
        sq = x * x
        inv = jnp.zeros_like(x)
        for h in range(nh):
            m = lane_head == h
            ss = jnp.sum(jnp.where(m, sq, 0.0), axis=-1, keepdims=True)
            inv = jnp.where(m, lax.rsqrt(ss + EPS), inv)
        return x * inv

    q = l2n(qkv[:, 0:hw]) * (GDN_DK ** -0.5)
    k = l2n(qkv[:, hw:2 * hw])
    v = qkv[:, 2 * hw:3 * hw]
    beta = _sigmoid(sm)
    glog = -jnp.exp(alog_ref[...]) * _softplus(sm + dtb_ref[...])
    gcum = _seg_cumsum_rows(glog, c)

    n = nh * c
    bcol = jnp.concatenate([beta[:, h:h + 1] for h in range(nh)], axis=0)
    gcol = jnp.concatenate([gcum[:, nh + h:nh + h + 1] for h in range(nh)], axis=0)
    glast = jnp.concatenate(
        [jnp.broadcast_to(gcum[c - 1:c, nh + h:nh + h + 1], (c, 1)) for h in range(nh)], axis=0)
    ri = lax.broadcasted_iota(jnp.int32, (n, n), 0)
    ci = lax.broadcasted_iota(jnp.int32, (n, n), 1)
    grow = jnp.sum(jnp.where(ri == ci, gcol, 0.0), axis=0, keepdims=True)
    same = ri // c == ci // c
    incl = same & (ci <= ri)
    strict = same & (ci < ri)
    dec = jnp.where(incl, jnp.exp(jnp.where(incl, gcol - grow, 0.0)), 0.0)

    sdec = jnp.concatenate(
        [jnp.broadcast_to(jnp.exp(gcum[c - 1:c, nh + h:nh + h + 1]), (GDN_DK, 1)) for h in range(nh)],
        axis=0)
    return dict(ks=_stack_heads(k, nh, GDN_DK), qs=_stack_heads(q, nh, GDN_DK),
                vs=_stack_heads(v, nh, GDN_DV), bcol=bcol, egc=jnp.exp(gcol), dec=dec,
                kdec=jnp.exp(glast - gcol), sdec=sdec, zg=zg)


def _gdn_kernel(z_ref, s0_ref, buf0_ref, cw_ref, alog_ref, dtb_ref, gn_ref,
                y_ref, sout_ref, bufout_ref, xs_ref, s_ref, *, c, g):
    nh = GDN_HEADS
    hw = nh * GDN_DK
    n = nh * c

    @pl.when(pl.program_id(1) == 0)
    def _():
        xs_ref[:, 5:8, :] = buf0_ref[...]
        s_ref[...] = jnp.zeros_like(s_ref)
        for b in range(g):
            for h in range(nh):
                s_ref[b, h * GDN_DK:(h + 1) * GDN_DK, h * GDN_DV:(h + 1) * GDN_DV] = s0_ref[b, h]

    bs = range(g)
    pr = [_gdn_prep(z_ref.at[b], cw_ref, alog_ref, dtb_ref, bufout_ref.at[b], xs_ref.at[b], c=c)
          for b in bs]
    ri = lax.broadcasted_iota(jnp.int32, (n, n), 0)
    ci = lax.broadcasted_iota(jnp.int32, (n, n), 1)
    same = ri // c == ci // c
    incl = same & (ci <= ri)
    strict = same & (ci < ri)
    kq = [_mm_nt(jnp.concatenate([pr[b]["ks"], pr[b]["qs"]], axis=0), pr[b]["ks"]) for b in bs]

    p = [jnp.where(strict, -(pr[b]["bcol"] * kq[b][0:n]) * pr[b]["dec"], 0.0) for b in bs]
    tinv = [jnp.where(ri == ci, 1.0, 0.0) + p[b] for b in bs]
    span = 2
    while span < c:
        p = [_mm(p[b], p[b]) for b in bs]
        tinv = [tinv[b] + _mm(tinv[b], p[b]) for b in bs]
        span *= 2

    s = [s_ref[b] for b in bs]
    uw = [_mm(tinv[b], jnp.concatenate([pr[b]["vs"] * pr[b]["bcol"],
                                        pr[b]["ks"] * (pr[b]["bcol"] * pr[b]["egc"])], axis=1))
          for b in bs]
    qw = [_mm(jnp.concatenate([pr[b]["qs"] * pr[b]["egc"], uw[b][:, hw:]], axis=0), s[b])
          for b in bs]
    vnew = [uw[b][:, 0:hw] - qw[b][n:] for b in bs]
    av = [_mm(jnp.where(incl, kq[b][n:] * pr[b]["dec"], 0.0), vnew[b]) for b in bs]
    kv = [_mm_tn(pr[b]["ks"] * pr[b]["kdec"], vnew[b]) for b in bs]
    for b in bs:
        o = _unstack_heads(qw[b][0:n] + av[b], nh)
        sn = pr[b]["sdec"] * s[b] + kv[b]
        s_ref[b] = sn
        for h in range(nh):
            sout_ref[b, h] = sn[h * GDN_DK:(h + 1) * GDN_DK, h * GDN_DV:(h + 1) * GDN_DV]
        y_ref[b] = _head_rms(o, gn_ref[...], nh, GDN_DV) * _silu(pr[b]["zg"])


def _mixer_group(batch, c):
    return min(batch, max(4, 128 // c))


def _gdn_mixer(z, s0, buf0, layer, cw, alog, dtb, gn, batch, seq):
    c = min(seq, MIX_CHUNK)
    nt = seq // c
    g = _mixer_group(batch, c)
    hw = GDN_HEADS * GDN_DK
    zw = z.shape[2]
    full2 = lambda i, j: (0, 0)
    sblk = (g, GDN_HEADS, GDN_DK, GDN_DV)
    return pl.pallas_call(
        functools.partial(_gdn_kernel, c=c, g=g),
        grid=(batch // g, nt),
        in_specs=[pl.BlockSpec((g, c, zw), lambda i, j: (i, j, 0)),
                  pl.BlockSpec((None,) + sblk, lambda i, j: (layer, i, 0, 0, 0)),
                  pl.BlockSpec((None, g, CONV_K - 1, 3 * hw), lambda i, j: (layer, i, 0, 0)),
                  pl.BlockSpec((CONV_K, 3 * hw), full2),
                  pl.BlockSpec((1, LANES), full2),
                  pl.BlockSpec((1, LANES), full2),
                  pl.BlockSpec((1, hw), full2)],
        out_specs=[pl.BlockSpec((g, c, hw), lambda i, j: (i, j, 0)),
                   pl.BlockSpec(sblk, lambda i, j: (i, 0, 0, 0)),
                   pl.BlockSpec((g, CONV_K - 1, 3 * hw), lambda i, j: (i, 0, 0))],
        out_shape=[jax.ShapeDtypeStruct((batch, seq, hw), F32),
                   jax.ShapeDtypeStruct((batch, GDN_HEADS, GDN_DK, GDN_DV), F32),
                   jax.ShapeDtypeStruct((batch, CONV_K - 1, 3 * hw), F32)],
        scratch_shapes=[pltpu.VMEM((g, 8 + c, 3 * hw), F32), pltpu.VMEM((g, hw, hw), F32)],
        compiler_params=_cparams(2),
        name="gdn_mixer",
    )(z, s0, buf0, cw, alog, dtb, gn.reshape(1, hw))


def _outproj_kernel(x_ref, yl_ref, yg_ref, yd_ref, wo_ref, gq_ref, wq_ref, xn_ref, q_ref):
    lw = LRU_W
    gw = GLA_HEADS * GLA_DV
    y = _mm(yl_ref[...], wo_ref[0:lw, :])
    y = y + _mm(yg_ref[...], wo_ref[lw:lw + gw, :])
    y = y + _mm(yd_ref[...], wo_ref[lw + gw:, :])
    xn = x_ref[...] + y
    xn_ref[...] = xn
    q_ref[...] = _mm(_rms_rows(xn, gq_ref[...]), wq_ref[...])


def _outproj(x, yl, yg, yd, wo, gq, wq, tm=512):
    t, d = x.shape
    tm = min(tm, t)
    row = lambda i: (i, 0)
    full2 = lambda i: (0, 0)
    return pl.pallas_call(
        _outproj_kernel,
        grid=(t // tm,),
        in_specs=[pl.BlockSpec((tm, d), row),
                  pl.BlockSpec((tm, yl.shape[1]), row),
                  pl.BlockSpec((tm, yg.shape[1]), row),
                  pl.BlockSpec((tm, yd.shape[1]), row),
                  pl.BlockSpec((d, d), full2),
                  pl.BlockSpec((1, d), full2),
                  pl.BlockSpec((d, d), full2)],
        out_specs=[pl.BlockSpec((tm, d), row), pl.BlockSpec((tm, d), row)],
        out_shape=[jax.ShapeDtypeStruct((t, d), F32), jax.ShapeDtypeStruct((t, d), F32)],
        compiler_params=_cparams(1),
        name="outproj_qproj",
    )(x, yl, yg, yd, wo, gq.reshape(1, d), wq)


def _attn_kernel(x_ref, q_ref, k_ref, v_ref, wo_ref, o_ref):
    hd = MEM_HD
    acc = x_ref[...]
    for h in range(MEM_HEADS):
        sl = slice(h * hd, (h + 1) * hd)
        s = _mm_nt(q_ref[:, sl], k_ref[:, sl]) * (hd ** -0.5)
        m = jnp.max(s, axis=-1, keepdims=True)
        p = jnp.exp(s - m)
        l = jnp.sum(p, axis=-1, keepdims=True)
        oh = _mm(p, v_ref[:, sl]) / l
        acc = acc + _mm(oh, wo_ref[sl, :])
    o_ref[...] = acc


def _attn_heads_kernel(q_ref, k_ref, v_ref, o_ref, *, gb, tl):
    nh, hd = MEM_HEADS, MEM_HD
    m = k_ref.shape[1]
    row_head = lax.broadcasted_iota(jnp.int32, (nh * tl, 1), 0) // tl
    col_head = lax.broadcasted_iota(jnp.int32, (1, m * nh), 1) % nh
    for b in range(gb):
        kf = k_ref[b].reshape(m * nh, hd)
        vf = v_ref[b].reshape(m * nh, hd)
        rows = slice(b * tl, (b + 1) * tl)
        qs = jnp.concatenate([q_ref[rows, h * hd:(h + 1) * hd] for h in range(nh)], axis=0)
        s = _mm_nt(qs, kf) * (hd ** -0.5)
        s = jnp.where(row_head == col_head, s, -jnp.inf)
        mx = jnp.max(s, axis=-1, keepdims=True)
        p = jnp.exp(s - mx)
        l = jnp.sum(p, axis=-1, keepdims=True)
        o = _mm(p, vf) / l
        for h in range(nh):
            o_ref[rows, h * hd:(h + 1) * hd] = o[h * tl:(h + 1) * tl]


def _oproj_kernel(x_ref, a_ref, wo_ref, o_ref):
    o_ref[...] = x_ref[...] + _mm(a_ref[...], wo_ref[...])


def _attention(x, q, k, v, layer, wo, batch, seq):
    t, d = x.shape
    if k.ndim == 4:
        tl = min(seq, 512)
        nl = seq // tl
        kblk = (None, None) + k.shape[2:]
        kidx = lambda b, j: (layer, b, 0, 0)
        row = lambda b, j: (b * nl + j, 0)
        return pl.pallas_call(
            _attn_kernel,
            grid=(batch, nl),
            in_specs=[pl.BlockSpec((tl, d), row),
                      pl.BlockSpec((tl, d), row),
                      pl.BlockSpec(kblk, kidx),
                      pl.BlockSpec(kblk, kidx),
                      pl.BlockSpec((d, d), lambda b, j: (0, 0))],
            out_specs=pl.BlockSpec((tl, d), row),
            out_shape=jax.ShapeDtypeStruct((t, d), F32),
            compiler_params=_cparams(2),
            name="mem_attention",
        )(x, q, k, v, wo)
    gb = 2
    kblk = (None, gb) + k.shape[2:]
    kidx = lambda i: (layer, i, 0, 0, 0)
    att = pl.pallas_call(
        functools.partial(_attn_heads_kernel, gb=gb, tl=seq),
        grid=(batch // gb,),
        in_specs=[pl.BlockSpec((gb * seq, d), lambda i: (i, 0)),
                  pl.BlockSpec(kblk, kidx),
                  pl.BlockSpec(kblk, kidx)],
        out_specs=pl.BlockSpec((gb * seq, d), lambda i: (i, 0)),
        out_shape=jax.ShapeDtypeStruct((t, d), F32),
        compiler_params=_cparams(1),
        name="mem_attention_heads",
    )(q, k, v)
    tm = min(t, 512)
    return pl.pallas_call(
        _oproj_kernel,
        grid=(t // tm,),
        in_specs=[pl.BlockSpec((tm, d), lambda i: (i, 0)),
                  pl.BlockSpec((tm, d), lambda i: (i, 0)),
                  pl.BlockSpec((d, d), lambda i: (0, 0))],
        out_specs=pl.BlockSpec((tm, d), lambda i: (i, 0)),
        out_shape=jax.ShapeDtypeStruct((t, d), F32),
        compiler_params=_cparams(1),
        name="mem_oproj",
    )(x, att, wo)


def _ffn_kernel(x_ref, g_ref, wg_ref, wu_ref, wd_ref, o_ref, h_ref, acc_ref):
    f = pl.program_id(1)

    @pl.when(f == 0)
    def _():
        h_ref[...] = _rms_rows(x_ref[...], g_ref[...]).astype(BF16)
        acc_ref[...] = jnp.zeros_like(acc_ref)

    h = h_ref[...]
    a = _mm(h, wg_ref[...])
    u = _mm(h, wu_ref[...])
    acc_ref[...] += _mm(_silu(a) * u, wd_ref[...])

    @pl.when(f == pl.num_programs(1) - 1)
    def _():
        o_ref[...] = x_ref[...] + acc_ref[...]


def _ffn_dense(x, g, wg, wu, wd, tm=1024, tf=512):
    t, d = x.shape
    dff = wg.shape[1]
    tm = min(tm, t)
    return pl.pallas_call(
        _ffn_kernel,
        grid=(t // tm, dff // tf),
        in_specs=[pl.BlockSpec((tm, d), lambda i, f: (i, 0)),
                  pl.BlockSpec((1, d), lambda i, f: (0, 0)),
                  pl.BlockSpec((d, tf), lambda i, f: (0, f)),
                  pl.BlockSpec((d, tf), lambda i, f: (0, f)),
                  pl.BlockSpec((tf, d), lambda i, f: (f, 0))],
        out_specs=pl.BlockSpec((tm, d), lambda i, f: (i, 0)),
        out_shape=jax.ShapeDtypeStruct((t, d), F32),
        scratch_shapes=[pltpu.VMEM((tm, d), BF16), pltpu.VMEM((tm, d), F32)],
        compiler_params=_cparams(2),
        name="ffn_dense",
    )(x, g.reshape(1, d), wg, wu, wd)


def _router_kernel(x_ref, g_ref, whi_ref, wlo_ref, info_ref, cnt_ref, carry_ref, *, tm):
    i = pl.program_id(0)

    @pl.when(i == 0)
    def _():
        carry_ref[...] = jnp.zeros_like(carry_ref)

    h = _rms_rows(x_ref[...], g_ref[...])
    hhi = h.astype(BF16)
    hlo = (h - hhi.astype(F32)).astype(BF16)
    whi = whi_ref[...]
    logits = (jnp.dot(hhi, whi, preferred_element_type=F32)
              + jnp.dot(hlo, whi, preferred_element_type=F32)
              + jnp.dot(hhi, wlo_ref[...], preferred_element_type=F32))
    lane = lax.broadcasted_iota(jnp.int32, (tm, LANES), 1)
    neg = jnp.float32(-jnp.inf)
    logits = jnp.where(lane < N_EXPERTS, logits, neg)
    m1 = jnp.max(logits, axis=-1, keepdims=True)
    i1 = jnp.min(jnp.where(logits == m1, lane, LANES), axis=-1, keepdims=True)
    rest = jnp.where(lane == i1, neg, logits)
    m2 = jnp.max(rest, axis=-1, keepdims=True)
    i2 = jnp.min(jnp.where(rest == m2, lane, LANES), axis=-1, keepdims=True)
    e = jnp.exp(m2 - m1)
    g1 = 1.0 / (1.0 + e)
    g2 = e / (1.0 + e)
    oh1 = jnp.where(lane == i1, 1.0, 0.0)
    oh2 = jnp.where(lane == i2, 1.0, 0.0)
    oh = oh1 + oh2
    ri = lax.broadcasted_iota(jnp.int32, (tm, tm), 0)
    ci = lax.broadcasted_iota(jnp.int32, (tm, tm), 1)
    tri = jnp.where(ci < ri, 1.0, 0.0)
    before = _mm(tri, oh) + carry_ref[0:1, :]
    r1 = jnp.sum(oh1 * before, axis=-1, keepdims=True)
    r2 = jnp.sum(oh2 * before, axis=-1, keepdims=True)
    carry = carry_ref[0:1, :] + jnp.sum(oh, axis=0, keepdims=True)
    carry_ref[...] = jnp.broadcast_to(carry, carry_ref.shape)
    cnt_ref[...] = jnp.broadcast_to(carry, cnt_ref.shape)
    info = jnp.where(lane == 0, i1.astype(F32), 0.0)
    info = jnp.where(lane == 1, i2.astype(F32), info)
    info = jnp.where(lane == 2, r1, info)
    info = jnp.where(lane == 3, r2, info)
    info = jnp.where(lane == 4, g1, info)
    info = jnp.where(lane == 5, g2, info)
    info_ref[...] = info


def _router(x, g, whi, wlo, tm=512):
    t, d = x.shape
    tm = min(tm, t)
    return pl.pallas_call(
        functools.partial(_router_kernel, tm=tm),
        grid=(t // tm,),
        in_specs=[pl.BlockSpec((tm, d), lambda i: (i, 0)),
                  pl.BlockSpec((1, d), lambda i: (0, 0)),
                  pl.BlockSpec((d, LANES), lambda i: (0, 0)),
                  pl.BlockSpec((d, LANES), lambda i: (0, 0))],
        out_specs=[pl.BlockSpec((tm, LANES), lambda i: (i, 0)),
                   pl.BlockSpec((8, LANES), lambda i: (0, 0))],
        out_shape=[jax.ShapeDtypeStruct((t, LANES), F32),
                   jax.ShapeDtypeStruct((8, LANES), F32)],
        scratch_shapes=[pltpu.VMEM((8, LANES), F32)],
        compiler_params=_cparams(1),
        name="moe_router",
    )(x, g.reshape(1, d), whi, wlo)


def _dispatch_kernel(s1_ref, s2_ref, zs_ref, zf_ref, x_ref, g_ref, xs_hbm, h_ref, zbuf_ref, zsem, sem,
                     *, rows, sub, n_blk):
    i = pl.program_id(0)

    @pl.when(i == 0)
    def _():
        zbuf_ref[...] = jnp.zeros_like(zbuf_ref)
        for e in range(N_EXPERTS):
            dst = xs_hbm.at[pl.ds(pl.multiple_of(zs_ref[e], 8), sub)]
            pltpu.make_async_copy(zbuf_ref, dst, zsem).start()
            pltpu.make_async_copy(zbuf_ref, dst, zsem).wait()

        def fill(j, carry):
            @pl.when(zf_ref[j] == 1)
            def _():
                dst = xs_hbm.at[pl.ds(pl.multiple_of(j * sub, sub), sub)]
                pltpu.make_async_copy(zbuf_ref, dst, zsem).start()
            return carry

        def drain(j, carry):
            @pl.when(zf_ref[j] == 1)
            def _():
                dst = xs_hbm.at[pl.ds(pl.multiple_of(j * sub, sub), sub)]
                pltpu.make_async_copy(zbuf_ref, dst, zsem).wait()
            return carry

        lax.fori_loop(0, n_blk, fill, 0)
        lax.fori_loop(0, n_blk, drain, 0)

    base = i * rows
    h_ref[...] = _rms_rows(x_ref[...], g_ref[...])

    def issue(r, carry):
        src = h_ref.at[pl.ds(r, 1)]
        pltpu.make_async_copy(src, xs_hbm.at[pl.ds(s1_ref[base + r], 1)], sem).start(priority=0)
        pltpu.make_async_copy(src, xs_hbm.at[pl.ds(s2_ref[base + r], 1)], sem).start(priority=1)
        return carry

    lax.fori_loop(0, rows, issue, 0, unroll=8)
    for _ in range(2):
        pltpu.make_async_copy(h_ref, xs_hbm.at[pl.ds(0, rows)], sem).wait()


def _dispatch(slot1, slot2, zero_start, zero_blk, x, g, sub, rows=256):
    t, d = x.shape
    rows = min(rows, t)
    n_blk = zero_blk.shape[0]
    n_rows = n_blk * sub
    return pl.pallas_call(
        functools.partial(_dispatch_kernel, rows=rows, sub=sub, n_blk=n_blk),
        grid_spec=pltpu.PrefetchScalarGridSpec(
            num_scalar_prefetch=4,
            grid=(t // rows,),
            in_specs=[pl.BlockSpec((rows, d), lambda i, a, b, c, e: (i, 0)),
                      pl.BlockSpec((1, d), lambda i, a, b, c, e: (0, 0))],
            out_specs=pl.BlockSpec(memory_space=pl.ANY),
            scratch_shapes=[pltpu.VMEM((rows, d), F32), pltpu.VMEM((sub, d), F32),
                            pltpu.SemaphoreType.DMA(()), pltpu.SemaphoreType.DMA(())]),
        out_shape=jax.ShapeDtypeStruct((n_rows, d), F32),
        compiler_params=_cparams(1),
        name="moe_dispatch",
    )(slot1, slot2, zero_start, zero_blk, x, g.reshape(1, d))


def _expert_kernel(be_ref, ns_ref, xi_ref, x_ref, wg_ref, wu_ref, wd_ref, o_ref, *, sub, n_sub):
    i = pl.program_id(0)
    f = pl.program_id(1)
    ns = ns_ref[i]

    @pl.when(f == 0)
    def _():
        o_ref[...] = jnp.zeros_like(o_ref)

    def swiglu(rows):
        h = x_ref[rows, :]
        a = _mm(h, wg_ref[...])
        u = _mm(h, wu_ref[...])
        o_ref[rows, :] += _mm(_silu(a) * u, wd_ref[...])

    for s in range(0, n_sub, 2):
        if s + 2 <= n_sub:
            @pl.when(s + 2 <= ns)
            def _():
                swiglu(slice(s * sub, (s + 2) * sub))

        @pl.when(s + 1 == ns)
        def _():
            swiglu(slice(s * sub, (s + 1) * sub))


def _expert_ffn(blk_exp, n_valid_sub, x_blk, xs, wg, wu, wd, sb, sub, tf=512):
    p, d = xs.shape
    dff = wg.shape[2]
    n_super = n_valid_sub.shape[0]
    nf = dff // tf

    def fidx(i, f, ns):
        used = jnp.minimum(ns[i], 1)
        return f * used + (nf - 1) * (1 - used)

    return pl.pallas_call(
        functools.partial(_expert_kernel, sub=sub, n_sub=sb // sub),
        grid_spec=pltpu.PrefetchScalarGridSpec(
            num_scalar_prefetch=3,
            grid=(n_super, nf),
            in_specs=[pl.BlockSpec((sb, d), lambda i, f, be, ns, xi: (xi[i], 0)),
                      pl.BlockSpec((None, d, tf), lambda i, f, be, ns, xi: (be[i], 0, fidx(i, f, ns))),
                      pl.BlockSpec((None, d, tf), lambda i, f, be, ns, xi: (be[i], 0, fidx(i, f, ns))),
                      pl.BlockSpec((None, tf, d), lambda i, f, be, ns, xi: (be[i], fidx(i, f, ns), 0))],
            out_specs=pl.BlockSpec((sb, d), lambda i, f, be, ns, xi: (i, 0))),
        out_shape=jax.ShapeDtypeStruct((n_super * sb, d), F32),
        compiler_params=pltpu.CompilerParams(dimension_semantics=("arbitrary", "arbitrary"),
                                             vmem_limit_bytes=EXPERT_VMEM_LIMIT_BYTES),
        name="moe_experts",
    )(blk_exp, n_valid_sub, x_blk, xs, wg, wu, wd)


def _combine_kernel(s1_ref, s2_ref, x_ref, info_ref, ys_hbm, gf_ref, o_ref, b1_ref, b2_ref, sem,
                    *, rows, final_norm):
    base = pl.program_id(0) * rows

    def issue(r, carry):
        c1 = pltpu.make_async_copy(ys_hbm.at[pl.ds(s1_ref[base + r], 1)], b1_ref.at[pl.ds(r, 1)], sem)
        c2 = pltpu.make_async_copy(ys_hbm.at[pl.ds(s2_ref[base + r], 1)], b2_ref.at[pl.ds(r, 1)], sem)
        c1.start(priority=0)
        c2.start(priority=1)
        return carry

    lax.fori_loop(0, rows, issue, 0, unroll=8)
    pltpu.make_async_copy(ys_hbm.at[pl.ds(0, rows)], b1_ref, sem).wait()
    pltpu.make_async_copy(ys_hbm.at[pl.ds(0, rows)], b2_ref, sem).wait()
    info = info_ref[...]
    y = b1_ref[...] * info[:, 4:5] + b2_ref[...] * info[:, 5:6]
    out = x_ref[...] + y
    if final_norm:
        out = _rms_rows(out, gf_ref[...])
    o_ref[...] = out


def _combine(slot1, slot2, x, info, ys, gf, final_norm, rows=256):
    t, d = x.shape
    rows = min(rows, t)
    return pl.pallas_call(
        functools.partial(_combine_kernel, rows=rows, final_norm=final_norm),
        grid_spec=pltpu.PrefetchScalarGridSpec(
            num_scalar_prefetch=2,
            grid=(t // rows,),
            in_specs=[pl.BlockSpec((rows, d), lambda i, a, b: (i, 0)),
                      pl.BlockSpec((rows, LANES), lambda i, a, b: (i, 0)),
                      pl.BlockSpec(memory_space=pl.ANY),
                      pl.BlockSpec((1, d), lambda i, a, b: (0, 0))],
            out_specs=pl.BlockSpec((rows, d), lambda i, a, b: (i, 0)),
            scratch_shapes=[pltpu.VMEM((rows, d), F32), pltpu.VMEM((rows, d), F32),
                            pltpu.SemaphoreType.DMA(())]),
        out_shape=jax.ShapeDtypeStruct((t, d), F32),
        compiler_params=_cparams(1),
        name="moe_combine",
    )(slot1, slot2, x, info, ys, gf.reshape(1, d))


def _moe_ffn(x, g, whi, wlo, wg, wu, wd, gf, final_norm, sub=512):
    t, d = x.shape
    sb = 2048 if 2 * t >= 16 * 1024 else sub
    info, cnt = _router(x, g, whi, wlo)
    e = info[:, 0:2].astype(jnp.int32)
    rank = info[:, 2:4].astype(jnp.int32)
    counts = cnt[0, :N_EXPERTS].astype(jnp.int32)
    n_sb = (counts + sb - 1) // sb
    sb_end = jnp.cumsum(n_sb)
    sb_start = sb_end - n_sb
    row_start = sb_start * sb
    eid = jnp.arange(N_EXPERTS, dtype=jnp.int32)
    slot = jnp.sum(jnp.where(e[:, :, None] == eid, row_start, 0), axis=-1) + rank
    n_super = (2 * t) // sb + N_EXPERTS
    blk = jnp.arange(n_super, dtype=jnp.int32)
    n_used = sb_end[-1]
    used = blk < n_used
    blk_c = jnp.minimum(blk, n_used - 1)
    be = jnp.minimum(jnp.sum((blk_c[:, None] >= sb_end[None, :]).astype(jnp.int32), axis=-1),
                     N_EXPERTS - 1)
    valid = jnp.clip(counts[be] - (blk_c - sb_start[be]) * sb, 0, sb)
    n_valid_sub = jnp.where(used, (valid + sub - 1) // sub, 0).astype(jnp.int32)
    zero_start = ((row_start + counts) // 8 * 8).astype(jnp.int32)
    per = sb // sub
    sub_in_blk = jnp.arange(per, dtype=jnp.int32)
    zero_blk = (sub_in_blk[None, :] >= n_valid_sub[:, None]).astype(jnp.int32).reshape(-1)
    zero_blk = jnp.concatenate([zero_blk, jnp.ones((1,), jnp.int32)])
    xs = _dispatch(slot[:, 0], slot[:, 1], zero_start, zero_blk, x, g, sub)
    ys = _expert_ffn(be.astype(jnp.int32), n_valid_sub, blk_c.astype(jnp.int32), xs, wg, wu, wd, sb, sub)
    return _combine(slot[:, 0], slot[:, 1], x, info, ys, gf, final_norm)


def _final_norm_kernel(x_ref, g_ref, o_ref):
    o_ref[...] = _rms_rows(x_ref[...], g_ref[...])


def _final_norm(x, g, tm=512):
    t, d = x.shape
    tm = min(tm, t)
    return pl.pallas_call(
        _final_norm_kernel,
        grid=(t // tm,),
        in_specs=[pl.BlockSpec((tm, d), lambda i: (i, 0)), pl.BlockSpec((1, d), lambda i: (0, 0))],
        out_specs=pl.BlockSpec((tm, d), lambda i: (i, 0)),
        out_shape=jax.ShapeDtypeStruct((t, d), F32),
        compiler_params=_cparams(1),
        name="final_norm",
    )(x, g.reshape(1, d))


def _block_diag(w):
    n, c, d = w.shape
    eye = jnp.eye(n, dtype=w.dtype)
    return jnp.einsum("ncd,nm->ncmd", w, eye).reshape(n * c, n * d)


def _pad_cols(w, n):
    return jnp.pad(w, ((0, 0), (0, n - w.shape[1])))


def _layer_params(l, p):
    (norm_mix, w_in, lru_conv_w, lru_conv_b, lru_a_w, lru_a_b, lru_x_w, lru_x_b, lru_lam, gla_gk_w2,
     gla_gk_b, gla_norm, gdn_conv_w, gdn_a_log, gdn_dt_bias, gdn_norm, w_out, norm_xq, norm_mem,
     w_mq, w_mk, w_mv, w_mo, norm_ffn) = [a[l] for a in p]
    lw = LRU_W
    gk, gv = GLA_HEADS * GLA_DK, GLA_HEADS * GLA_DV
    dh = GDN_HEADS * GDN_DK
    offs = [0]
    for s in (lw, lw, gk, gk, gv, GLA_RANK, gv, dh, dh, dh, GDN_HEADS, GDN_HEADS, dh):
        offs.append(offs[-1] + s)
    col = lambda i: w_in[:, offs[i]:offs[i + 1]]
    w_lru = jnp.concatenate([col(0), col(1)], axis=1)
    w_gla = jnp.concatenate([col(2), col(3), col(4), col(6), _pad_cols(col(5), LANES)], axis=1)
    w_gdn = jnp.concatenate(
        [col(7), col(8), col(9), col(12), _pad_cols(jnp.concatenate([col(10), col(11)], axis=1), LANES)],
        axis=1)
    w_cat = jnp.concatenate([w_lru, w_gla, w_gdn], axis=1).astype(BF16)
    widths = (w_lru.shape[1], w_gla.shape[1], w_gdn.shape[1])
    alog = jnp.zeros((1, LANES), F32).at[0, GDN_HEADS:2 * GDN_HEADS].set(gdn_a_log)
    dtb = jnp.zeros((1, LANES), F32).at[0, GDN_HEADS:2 * GDN_HEADS].set(gdn_dt_bias)
    return dict(
        norm_mix=norm_mix, w_cat=w_cat, widths=widths,
        lru_conv_w=lru_conv_w, lru_conv_b=lru_conv_b,
        lru_a=_block_diag(lru_a_w).astype(BF16), lru_a_b=lru_a_b,
        lru_x=_block_diag(lru_x_w).astype(BF16), lru_x_b=lru_x_b, lru_lam=lru_lam,
        gla_w2=jnp.pad(gla_gk_w2, ((0, LANES - GLA_RANK), (0, 0))).astype(BF16), gla_gk_b=gla_gk_b,
        gla_norm=jnp.tile(gla_norm, GLA_HEADS),
        gdn_conv_w=gdn_conv_w, gdn_alog=alog, gdn_dtb=dtb, gdn_norm=jnp.tile(gdn_norm, GDN_HEADS),
        w_out=w_out.astype(BF16), norm_xq=norm_xq, norm_mem=norm_mem, w_mq=w_mq.astype(BF16),
        w_mkv=jnp.concatenate([w_mk, w_mv], axis=1).astype(BF16), w_mo=w_mo.astype(BF16),
        norm_ffn=norm_ffn)


def _run_group(x3, mem_k, mem_v, states, layers, ffn, norm_final):
    batch, seq, d = x3.shape
    x = x3.reshape(batch * seq, d)
    lru_h0, lru_buf0, gla_s0, gdn_s0, gdn_buf0, state_layer = states
    new_states = []
    for l, lp in enumerate(layers):
        sl = state_layer(l)
        z_lru, z_gla, z_gdn = _rms_matmul(x, lp["norm_mix"], lp["w_cat"], lp["widths"], "in_proj")
        y_lru, lru_h, lru_buf = _lru_mixer(
            z_lru, lru_h0, lru_buf0, sl, lp["lru_conv_w"], lp["lru_conv_b"], lp["lru_a"], lp["lru_a_b"],
            lp["lru_x"], lp["lru_x_b"], lp["lru_lam"], batch, seq)
        y_gla, gla_s = _gla_mixer(z_gla.reshape(batch, seq, -1), gla_s0, sl, lp["gla_w2"],
                                  lp["gla_gk_b"], lp["gla_norm"], batch, seq)
        y_gdn, gdn_s, gdn_buf = _gdn_mixer(z_gdn.reshape(batch, seq, -1), gdn_s0, gdn_buf0, sl,
                                           lp["gdn_conv_w"], lp["gdn_alog"], lp["gdn_dtb"],
                                           lp["gdn_norm"], batch, seq)
        x, q = _outproj(x, y_lru, y_gla.reshape(batch * seq, -1), y_gdn.reshape(batch * seq, -1),
                        lp["w_out"], lp["norm_xq"], lp["w_mq"])
        x = _attention(x, q, mem_k[l][0], mem_v[l][0], mem_k[l][1], lp["w_mo"], batch, seq)
        last = l == len(layers) - 1
        kind, fp = ffn[l]
        if kind == "dense":
            x = _ffn_dense(x, lp["norm_ffn"], *fp)
            if last:
                x = _final_norm(x, norm_final)
        else:
            x = _moe_ffn(x, lp["norm_ffn"], *fp, norm_final, last)
        new_states.append((lru_h.reshape(batch, LRU_W), lru_buf, gla_s, gdn_s, gdn_buf))
    return x.reshape(batch, seq, d), [jnp.stack(s) for s in zip(*new_states)]


def kernel(x_prompt, x_sample, mem_prompt, state_lru_h, state_lru_conv, state_gla, state_gdn, state_gdn_conv, cache_mem_k, cache_mem_v, norm_mix, w_in, lru_conv_w, lru_conv_b, lru_a_w, lru_a_b, lru_x_w, lru_x_b, lru_lam, gla_gk_w2, gla_gk_b, gla_norm, gdn_conv_w, gdn_a_log, gdn_dt_bias, gdn_norm, w_out, norm_xq, norm_mem, w_mq, w_mk, w_mv, w_mo, norm_ffn, w_ff_gate, w_ff_up, w_ff_down, w_router, w_e_gate, w_e_up, w_e_down, norm_final):
    depth = norm_mix.shape[0]
    per_layer = (norm_mix, w_in, lru_conv_w, lru_conv_b, lru_a_w, lru_a_b, lru_x_w, lru_x_b, lru_lam,
                 gla_gk_w2, gla_gk_b, gla_norm, gdn_conv_w, gdn_a_log, gdn_dt_bias, gdn_norm, w_out,
                 norm_xq, norm_mem, w_mq, w_mk, w_mv, w_mo, norm_ffn)
    layers = [_layer_params(l, per_layer) for l in range(depth)]
    ffn = []
    for l in range(depth):
        j = l // 2
        if l % 2 == 0:
            ffn.append(("dense", (w_ff_gate[j], w_ff_up[j], w_ff_down[j])))
        else:
            wr = _pad_cols(w_router[j], LANES)
            whi = wr.astype(BF16)
            wlo = (wr - whi.astype(F32)).astype(BF16)
            ffn.append(("moe", (whi, wlo, w_e_gate[j], w_e_up[j], w_e_down[j])))

    bp, mlen, d = mem_prompt.shape
    mem2 = mem_prompt.reshape(bp * mlen, d)
    pk, pv = [], []
    for lp in layers:
        k2, v2 = _rms_matmul(mem2, lp["norm_mem"], lp["w_mkv"], (d, d), "mem_kv")
        pk.append(k2.reshape(1, bp, mlen, d))
        pv.append(v2.reshape(1, bp, mlen, d))
    zero_state = (jnp.zeros((1, bp, LRU_W), F32), jnp.zeros((1, bp, CONV_K - 1, LRU_W), F32),
                  jnp.zeros((1, bp, GLA_HEADS, GLA_DK, GLA_DV), F32),
                  jnp.zeros((1, bp, GDN_HEADS, GDN_DK, GDN_DV), F32),
                  jnp.zeros((1, bp, CONV_K - 1, 3 * GDN_HEADS * GDN_DK), F32),
                  lambda l: 0)
    y_prompt, p_st = _run_group(x_prompt, [(k, 0) for k in pk], [(v, 0) for v in pv], zero_state,
                                layers, ffn, norm_final)
    p_mem_k = jnp.concatenate(pk, axis=0).reshape(depth, bp, mlen, MEM_HEADS, MEM_HD)
    p_mem_v = jnp.concatenate(pv, axis=0).reshape(depth, bp, mlen, MEM_HEADS, MEM_HD)

    s_in = (state_lru_h, state_lru_conv, state_gla, state_gdn, state_gdn_conv, lambda l: l)
    y_sample, s_st = _run_group(x_sample, [(cache_mem_k, l) for l in range(depth)],
                                [(cache_mem_v, l) for l in range(depth)], s_in, layers, ffn, norm_final)

    return (y_prompt, y_sample, p_st[0], p_st[1], p_st[2], p_st[3], p_st[4], p_mem_k, p_mem_v,
            s_st[0], s_st[1], s_st[2], s_st[3], s_st[4])
```

```python
import functools

import jax
import jax.numpy as jnp
from jax import lax
from jax.experimental import pallas as pl
from jax.experimental.pallas import tpu as pltpu

F32 = jnp.float32
BF16 = jnp.bfloat16
EPS = 1e-6

D_MODEL = 1024
LRU_W = 512
LRU_BLOCKS = 8
LRU_C = 8.0
CONV_K = 4
GLA_HEADS = 4
GLA_DK = 32
GLA_DV = 64
GLA_RANK = 16
GLA_TAU = 16.0
GLA_SUB = 16
GDN_HEADS = 4
GDN_DK = 64
GDN_DV = 64
MIX_CHUNK = 64
MEM_HEADS = 4
MEM_HD = 256
N_EXPERTS = 8
LANES = 128
VMEM_LIMIT_BYTES = 48 * 1024 * 1024
EXPERT_VMEM_LIMIT_BYTES = 56 * 1024 * 1024


def _cparams(n_axes):
    return pltpu.CompilerParams(dimension_semantics=("arbitrary",) * n_axes,
                                vmem_limit_bytes=VMEM_LIMIT_BYTES)


def _mm(a, b):
    return jnp.dot(a.astype(BF16), b.astype(BF16), preferred_element_type=F32)


def _mm_nt(a, b):
    return lax.dot_general(a.astype(BF16), b.astype(BF16), (((1,), (1,)), ((), ())),
                           preferred_element_type=F32)


def _mm_tn(a, b):
    return lax.dot_general(a.astype(BF16), b.astype(BF16), (((0,), (0,)), ((), ())),
                           preferred_element_type=F32)


def _rms_rows(x, g):
    ms = jnp.mean(x * x, axis=-1, keepdims=True)
    return (x * lax.rsqrt(ms + EPS)) * g


def _softplus(x):
    return jnp.maximum(x, 0.0) + jnp.log1p(jnp.exp(-jnp.abs(x)))


def _sigmoid(x):
    return 1.0 / (1.0 + jnp.exp(-x))


def _silu(x):
    return x * _sigmoid(x)


def _gelu_tanh(x):
    c = 0.7978845608028654
    return x * (0.5 * (1.0 + jnp.tanh(c * (x + 0.044715 * (x * x * x)))))


def _seg_cumsum_rows(x, seg):
    rows = x.shape[0]
    tpos = lax.broadcasted_iota(jnp.int32, (rows, 1), 0) & (seg - 1)
    d = 1
    while d < seg:
        x = x + jnp.where(tpos >= d, pltpu.roll(x, d, axis=0), 0.0)
        d *= 2
    return x


def _head_rms(o, gain, n_heads, width):
    lane_head = lax.broadcasted_iota(jnp.int32, (1, n_heads * width), 1) // width
    sq = o * o
    inv = jnp.zeros_like(o)
    for h in range(n_heads):
        m = lane_head == h
        ms = jnp.sum(jnp.where(m, sq, 0.0), axis=-1, keepdims=True) * (1.0 / width)
        inv = jnp.where(m, lax.rsqrt(ms + EPS), inv)
    return (o * inv) * gain


def _stack_heads(x, n_heads, width):
    c = x.shape[0]
    t = jnp.concatenate([x] * n_heads, axis=0)
    row_head = lax.broadcasted_iota(jnp.int32, (n_heads * c, 1), 0) // c
    lane_head = lax.broadcasted_iota(jnp.int32, (1, n_heads * width), 1) // width
    return jnp.where(row_head == lane_head, t, 0.0)


def _unstack_heads(x, n_heads):
    c = x.shape[0] // n_heads
    o = x[0:c]
    for h in range(1, n_heads):
        o = o + x[h * c:(h + 1) * c]
    return o


def _rms_matmul_kernel(x_ref, g_ref, w_ref, *o_refs):
    h = _rms_rows(x_ref[...], g_ref[...]).astype(BF16)
    start = 0
    for o_ref in o_refs:
        n = o_ref.shape[1]
        o_ref[...] = jnp.dot(h, w_ref[:, start:start + n], preferred_element_type=F32)
        start += n


def _row_tile(t, pref):
    tile = min(pref, t)
    while t % tile or tile % 8:
        tile -= 8
    return tile


def _rms_matmul(x, g, w, widths, name, tm=512, row_off=0, n_rows=None):
    d = x.shape[1]
    t = x.shape[0] if n_rows is None else n_rows
    n = w.shape[1]
    tm = _row_tile(t, tm)
    off = row_off // tm
    return pl.pallas_call(
        _rms_matmul_kernel,
        grid=(t // tm,),
        in_specs=[pl.BlockSpec((tm, d), lambda i: (i + off, 0)),
                  pl.BlockSpec((1, d), lambda i: (0, 0)),
                  pl.BlockSpec((d, n), lambda i: (0, 0))],
        out_specs=[pl.BlockSpec((tm, wd), lambda i: (i, 0)) for wd in widths],
        out_shape=[jax.ShapeDtypeStruct((t, wd), F32) for wd in widths],
        compiler_params=_cparams(1),
        name=name,
    )(x, g.reshape(1, d), w)


def _lru_kernel(z_ref, h0_ref, buf0_ref, cw_ref, cb_ref, aw_ref, ab_ref, xw_ref, xb_ref, lam_ref,
                y_ref, hout_ref, bufout_ref, xs_ref, hc_ref, *, bt, tl, nt):
    w = LRU_W
    rows = bt * tl
    j = pl.program_id(0) % nt

    @pl.when(j == 0)
    def _():
        xs_ref[:, 5:8, :] = buf0_ref[...]
        hc_ref[...] = h0_ref[...]

    xs_ref[:, 8:, :] = z_ref[:, :w].reshape(bt, tl, w)
    gate = z_ref[:, w:]
    cw = cw_ref[...]
    xc = cb_ref[...] + xs_ref[:, 5:5 + tl, :] * cw[0:1]
    for k in range(1, CONV_K):
        xc = xc + xs_ref[:, 5 + k:5 + k + tl, :] * cw[k:k + 1]
    tail = xs_ref[:, 5 + tl:8 + tl, :]
    xs_ref[:, 5:8, :] = tail
    bufout_ref[...] = tail

    xc = xc.reshape(rows, w)
    r = _sigmoid(_mm(xc, aw_ref[...]) + ab_ref[...])
    ig = _sigmoid(_mm(xc, xw_ref[...]) + xb_ref[...])
    log_a = (-LRU_C * r) * _softplus(-lam_ref[...])
    a = jnp.exp(log_a)
    th = jnp.tanh(log_a)
    b = jnp.sqrt((-2.0 * th) / (1.0 - th)) * (ig * xc)

    sub = 8
    gps = tl // sub
    a3 = a.reshape(rows // sub, sub, w)
    b3 = b.reshape(rows // sub, sub, w)
    spos = lax.broadcasted_iota(jnp.int32, (1, sub, 1), 1)
    d = 1
    while d < sub:
        m = spos >= d
        b3 = jnp.where(m, a3 * pltpu.roll(b3, d, axis=1) + b3, b3)
        a3 = jnp.where(m, a3 * pltpu.roll(a3, d, axis=1), a3)
        d *= 2
    a4 = a3.reshape(bt, gps, sub, w)
    b4 = b3.reshape(bt, gps, sub, w)
    carry = hc_ref[...]
    hs = []
    for r in range(gps):
        hr = b4[:, r] + a4[:, r] * carry
        hs.append(hr)
        carry = hr[:, sub - 1:sub, :]
    h = jnp.stack(hs, axis=1).reshape(rows, w)
    hlast = carry
    hc_ref[...] = hlast
    hout_ref[...] = hlast
    y_ref[...] = h * _gelu_tanh(gate)


def _lru_mixer(z, h0, buf0, layer, cw, cb, aw, ab, xw, xb, lam, batch, seq):
    w = LRU_W
    tl = min(seq, 256)
    bt = min(batch, max(1, 256 // seq))
    nt = seq // tl
    rows = bt * tl
    grid = (batch * seq // rows,)
    if nt > 1:
        sidx = lambda i: (i // nt, 0, 0)
        lidx = lambda i: (layer, i // nt, 0, 0)
    else:
        sidx = lambda i: (i, 0, 0)
        lidx = lambda i: (layer, i, 0, 0)
    full2 = lambda i: (0, 0)
    return pl.pallas_call(
        functools.partial(_lru_kernel, bt=bt, tl=tl, nt=nt),
        grid=grid,
        in_specs=[pl.BlockSpec((rows, 2 * w), lambda i: (i, 0)),
                  pl.BlockSpec((None, bt, 1, w), lidx),
                  pl.BlockSpec((None, bt, CONV_K - 1, w), lidx),
                  pl.BlockSpec((CONV_K, w), full2),
                  pl.BlockSpec((1, w), full2),
                  pl.BlockSpec((w, w), full2),
                  pl.BlockSpec((1, w), full2),
                  pl.BlockSpec((w, w), full2),
                  pl.BlockSpec((1, w), full2),
                  pl.BlockSpec((1, w), full2)],
        out_specs=[pl.BlockSpec((rows, w), lambda i: (i, 0)),
                   pl.BlockSpec((bt, 1, w), sidx),
                   pl.BlockSpec((bt, CONV_K - 1, w), sidx)],
        out_shape=[jax.ShapeDtypeStruct((batch * seq, w), F32),
                   jax.ShapeDtypeStruct((batch, 1, w), F32),
                   jax.ShapeDtypeStruct((batch, CONV_K - 1, w), F32)],
        scratch_shapes=[pltpu.VMEM((bt, 8 + tl, w), F32), pltpu.VMEM((bt, 1, w), F32)],
        compiler_params=_cparams(1),
        name="lru_mixer",
    )(z, h0.reshape(h0.shape[0], batch, 1, w), buf0, cw, cb.reshape(1, w), aw, ab.reshape(1, w), xw,
      xb.reshape(1, w), lam.reshape(1, w))


def _gla_kernel(z_ref, s0_ref, w2_ref, gb_ref, gn_ref, y_ref, sout_ref, s_ref, *, c, sc, g):
    nh = GLA_HEADS
    kw = nh * GLA_DK
    vw = nh * GLA_DV

    @pl.when(pl.program_id(1) == 0)
    def _():
        s_ref[...] = jnp.zeros_like(s_ref)
        for b in range(g):
            for h in range(nh):
                s_ref[b, h * GLA_DK:(h + 1) * GLA_DK, h * GLA_DV:(h + 1) * GLA_DV] = s0_ref[b, h]

    bs = range(g)
    q = [z_ref[b, :, 0:kw] * (GLA_DK ** -0.5) for b in bs]
    k = [z_ref[b, :, kw:2 * kw] for b in bs]
    v = [z_ref[b, :, 2 * kw:2 * kw + vw] for b in bs]
    gk = [-_softplus(-(_mm(z_ref[b, :, 2 * kw + 2 * vw:], w2_ref[...]) + gb_ref[...])) / GLA_TAU
          for b in bs]
    gcum = [_seg_cumsum_rows(gk[b], sc) for b in bs]
    qp = [q[b] * jnp.exp(gcum[b]) for b in bs]

    ri = lax.broadcasted_iota(jnp.int32, (nh * c, nh * c), 0)
    ci = lax.broadcasted_iota(jnp.int32, (nh * c, nh * c), 1)
    keep = (ri // sc == ci // sc) & (ci <= ri)
    a = [_mm_nt(_stack_heads(qp[b], nh, GLA_DK), _stack_heads(k[b] * jnp.exp(-gcum[b]), nh, GLA_DK))
         for b in bs]
    o = [_unstack_heads(_mm(jnp.where(keep, a[b], 0.0), _stack_heads(v[b], nh, GLA_DV)), nh)
         for b in bs]

    s = [s_ref[b] for b in bs]
    eye = (lax.broadcasted_iota(jnp.int32, (kw, kw), 0) ==
           lax.broadcasted_iota(jnp.int32, (kw, kw), 1))
    bd = (lax.broadcasted_iota(jnp.int32, (kw, vw), 0) // GLA_DK ==
          lax.broadcasted_iota(jnp.int32, (kw, vw), 1) // GLA_DV)
    o_inter = [[] for _ in bs]
    for i in range(c // sc):
        lo, hi = i * sc, (i + 1) * sc
        for b in bs:
            o_inter[b].append(_mm(qp[b][lo:hi], s[b]))
            glast = gcum[b][hi - 1:hi]
            kpp = k[b][lo:hi] * jnp.exp(glast - gcum[b][lo:hi])
            u = _mm_tn(kpp, v[b][lo:hi])
            dcol = jnp.sum(jnp.where(eye, jnp.exp(glast), 0.0), axis=1, keepdims=True)
            s[b] = dcol * s[b] + jnp.where(bd, u, 0.0)
    for b in bs:
        s_ref[b] = s[b]
        for h in range(nh):
            sout_ref[b, h] = s[b][h * GLA_DK:(h + 1) * GLA_DK, h * GLA_DV:(h + 1) * GLA_DV]
        ob = o[b] + jnp.concatenate(o_inter[b], axis=0)
        gate = z_ref[b, :, 2 * kw + vw:2 * kw + 2 * vw]
        y_ref[b] = _head_rms(ob, gn_ref[...], nh, GLA_DV) * _silu(gate)


def _gla_mixer(z, s0, layer, w2, gb, gn, batch, seq):
    c = min(seq, MIX_CHUNK)
    sc = min(c, GLA_SUB)
    nt = seq // c
    g = _mixer_group(batch, c)
    kw, vw = GLA_HEADS * GLA_DK, GLA_HEADS * GLA_DV
    zw = z.shape[2]
    full2 = lambda i, j: (0, 0)
    sblk = (g, GLA_HEADS, GLA_DK, GLA_DV)
    return pl.pallas_call(
        functools.partial(_gla_kernel, c=c, sc=sc, g=g),
        grid=(batch // g, nt),
        in_specs=[pl.BlockSpec((g, c, zw), lambda i, j: (i, j, 0)),
                  pl.BlockSpec((None,) + sblk, lambda i, j: (layer, i, 0, 0, 0)),
                  pl.BlockSpec((LANES, kw), full2),
                  pl.BlockSpec((1, kw), full2),
                  pl.BlockSpec((1, vw), full2)],
        out_specs=[pl.BlockSpec((g, c, vw), lambda i, j: (i, j, 0)),
                   pl.BlockSpec(sblk, lambda i, j: (i, 0, 0, 0))],
        out_shape=[jax.ShapeDtypeStruct((batch, seq, vw), F32),
                   jax.ShapeDtypeStruct((batch, GLA_HEADS, GLA_DK, GLA_DV), F32)],
        scratch_shapes=[pltpu.VMEM((g, kw, vw), F32)],
        compiler_params=_cparams(2),
        name="gla_mixer",
    )(z, s0, w2, gb.reshape(1, kw), gn.reshape(1, vw))


def _gdn_prep(z_ref, cw_ref, alog_ref, dtb_ref, bufout_ref, xs_ref, *, c):
    nh = GDN_HEADS
    hw = nh * GDN_DK
    cw3 = 3 * hw

    xs_ref[8:, :] = z_ref[:, 0:cw3]
    cw = cw_ref[...]
    qkv = xs_ref[5:5 + c, :] * cw[0:1]
    for kk in range(1, CONV_K):
        qkv = qkv + xs_ref[5 + kk:5 + kk + c, :] * cw[kk:kk + 1]
    tail = xs_ref[5 + c:8 + c, :]
    xs_ref[5:8, :] = tail
    bufout_ref[...] = tail
    qkv = _silu(qkv)
    zg = z_ref[:, cw3:cw3 + hw]
    sm = z_ref[:, cw3 + hw:]

    lane_head = lax.broadcasted_iota(jnp.int32, (1, hw), 1) // GDN_DK

    def l2n(x):
        sq = x * x
        inv = jnp.zeros_like(x)
        for h in range(nh):
            m = lane_head == h
            ss = jnp.sum(jnp.where(m, sq, 0.0), axis=-1, keepdims=True)
            inv = jnp.where(m, lax.rsqrt(ss + EPS), inv)
        return x * inv

    q = l2n(qkv[:, 0:hw]) * (GDN_DK ** -0.5)
    k = l2n(qkv[:, hw:2 * hw])
    v = qkv[:, 2 * hw:3 * hw]
    beta = _sigmoid(sm)
    glog = -jnp.exp(alog_ref[...]) * _softplus(sm + dtb_ref[...])
    gcum = _seg_cumsum_rows(glog, c)

    n = nh * c
    bcol = jnp.concatenate([beta[:, h:h + 1] for h in range(nh)], axis=0)
    gcol = jnp.concatenate([gcum[:, nh + h:nh + h + 1] for h in range(nh)], axis=0)
    glast = jnp.concatenate(
        [jnp.broadcast_to(gcum[c - 1:c, nh + h:nh + h + 1], (c, 1)) for h in range(nh)], axis=0)
    ri = lax.broadcasted_iota(jnp.int32, (n, n), 0)
    ci = lax.broadcasted_iota(jnp.int32, (n, n), 1)
    grow = jnp.sum(jnp.where(ri == ci, gcol, 0.0), axis=0, keepdims=True)
    same = ri // c == ci // c
    incl = same & (ci <= ri)
    strict = same & (ci < ri)
    dec = jnp.where(incl, jnp.exp(jnp.where(incl, gcol - grow, 0.0)), 0.0)

    sdec = jnp.concatenate(
        [jnp.broadcast_to(jnp.exp(gcum[c - 1:c, nh + h:nh + h + 1]), (GDN_DK, 1)) for h in range(nh)],
        axis=0)
    return dict(ks=_stack_heads(k, nh, GDN_DK), qs=_stack_heads(q, nh, GDN_DK),
                vs=_stack_heads(v, nh, GDN_DV), bcol=bcol, egc=jnp.exp(gcol), dec=dec,
                kdec=jnp.exp(glast - gcol), sdec=sdec, zg=zg)


def _gdn_kernel(z_ref, s0_ref, buf0_ref, cw_ref, alog_ref, dtb_ref, gn_ref,
                y_ref, sout_ref, bufout_ref, xs_ref, s_ref, *, c, g):
    nh = GDN_HEADS
    hw = nh * GDN_DK
    n = nh * c

    @pl.when(pl.program_id(1) == 0)
    def _():
        xs_ref[:, 5:8, :] = buf0_ref[...]
        s_ref[...] = jnp.zeros_like(s_ref)
        for b in range(g):
            for h in range(nh):
                s_ref[b, h * GDN_DK:(h + 1) * GDN_DK, h * GDN_DV:(h + 1) * GDN_DV] = s0_ref[b, h]

    bs = range(g)
    pr = [_gdn_prep(z_ref.at[b], cw_ref, alog_ref, dtb_ref, bufout_ref.at[b], xs_ref.at[b], c=c)
          for b in bs]
    ri = lax.broadcasted_iota(jnp.int32, (n, n), 0)
    ci = lax.broadcasted_iota(jnp.int32, (n, n), 1)
    same = ri // c == ci // c
    incl = same & (ci <= ri)
    strict = same & (ci < ri)
    kq = [_mm_nt(jnp.concatenate([pr[b]["ks"], pr[b]["qs"]], axis=0), pr[b]["ks"]) for b in bs]

    p = [jnp.where(strict, -(pr[b]["bcol"] * kq[b][0:n]) * pr[b]["dec"], 0.0) for b in bs]
    tinv = [jnp.where(ri == ci, 1.0, 0.0) + p[b] for b in bs]
    span = 2
    while span < c:
        p = [_mm(p[b], p[b]) for b in bs]
        tinv = [tinv[b] + _mm(tinv[b], p[b]) for b in bs]
        span *= 2

    s = [s_ref[b] for b in bs]
    uw = [_mm(tinv[b], jnp.concatenate([pr[b]["vs"] * pr[b]["bcol"],
                                        pr[b]["ks"] * (pr[b]["bcol"] * pr[b]["egc"])], axis=1))
          for b in bs]
    qw = [_mm(jnp.concatenate([pr[b]["qs"] * pr[b]["egc"], uw[b][:, hw:]], axis=0), s[b])
          for b in bs]
    vnew = [uw[b][:, 0:hw] - qw[b][n:] for b in bs]
    av = [_mm(jnp.where(incl, kq[b][n:] * pr[b]["dec"], 0.0), vnew[b]) for b in bs]
    kv = [_mm_tn(pr[b]["ks"] * pr[b]["kdec"], vnew[b]) for b in bs]
    for b in bs:
        o = _unstack_heads(qw[b][0:n] + av[b], nh)
        sn = pr[b]["sdec"] * s[b] + kv[b]
        s_ref[b] = sn
        for h in range(nh):
            sout_ref[b, h] = sn[h * GDN_DK:(h + 1) * GDN_DK, h * GDN_DV:(h + 1) * GDN_DV]
        y_ref[b] = _head_rms(o, gn_ref[...], nh, GDN_DV) * _silu(pr[b]["zg"])


def _mixer_group(batch, c):
    return min(batch, max(4, 128 // c))


def _gdn_mixer(z, s0, buf0, layer, cw, alog, dtb, gn, batch, seq):
    c = min(seq, MIX_CHUNK)
    nt = seq // c
    g = _mixer_group(batch, c)
    hw = GDN_HEADS * GDN_DK
    zw = z.shape[2]
    full2 = lambda i, j: (0, 0)
    sblk = (g, GDN_HEADS, GDN_DK, GDN_DV)
    return pl.pallas_call(
        functools.partial(_gdn_kernel, c=c, g=g),
        grid=(batch // g, nt),
        in_specs=[pl.BlockSpec((g, c, zw), lambda i, j: (i, j, 0)),
                  pl.BlockSpec((None,) + sblk, lambda i, j: (layer, i, 0, 0, 0)),
                  pl.BlockSpec((None, g, CONV_K - 1, 3 * hw), lambda i, j: (layer, i, 0, 0)),
                  pl.BlockSpec((CONV_K, 3 * hw), full2),
                  pl.BlockSpec((1, LANES), full2),
                  pl.BlockSpec((1, LANES), full2),
                  pl.BlockSpec((1, hw), full2)],
        out_specs=[pl.BlockSpec((g, c, hw), lambda i, j: (i, j, 0)),
                   pl.BlockSpec(sblk, lambda i, j: (i, 0, 0, 0)),
                   pl.BlockSpec((g, CONV_K - 1, 3 * hw), lambda i, j: (i, 0, 0))],
        out_shape=[jax.ShapeDtypeStruct((batch, seq, hw), F32),
                   jax.ShapeDtypeStruct((batch, GDN_HEADS, GDN_DK, GDN_DV), F32),
                   jax.ShapeDtypeStruct((batch, CONV_K - 1, 3 * hw), F32)],
        scratch_shapes=[pltpu.VMEM((g, 8 + c, 3 * hw), F32), pltpu.VMEM((g, hw, hw), F32)],
        compiler_params=_cparams(2),
        name="gdn_mixer",
    )(z, s0, buf0, cw, alog, dtb, gn.reshape(1, hw))


def _outproj_kernel(x_ref, yl_ref, yg_ref, yd_ref, wo_ref, gq_ref, wq_ref, xn_ref, q_ref):
    lw = LRU_W
    gw = GLA_HEADS * GLA_DV
    y = _mm(yl_ref[...], wo_ref[0:lw, :])
    y = y + _mm(yg_ref[...], wo_ref[lw:lw + gw, :])
    y = y + _mm(yd_ref[...], wo_ref[lw + gw:, :])
    xn = x_ref[...] + y
    xn_ref[...] = xn
    q_ref[...] = _mm(_rms_rows(xn, gq_ref[...]), wq_ref[...])


def _outproj(x, yl, yg, yd, wo, gq, wq, tm=512, row_off=0):
    d = x.shape[1]
    t = yl.shape[0]
    tm = _row_tile(t, tm)
    off = row_off // tm
    row = lambda i: (i, 0)
    full2 = lambda i: (0, 0)
    return pl.pallas_call(
        _outproj_kernel,
        grid=(t // tm,),
        in_specs=[pl.BlockSpec((tm, d), lambda i: (i + off, 0)),
                  pl.BlockSpec((tm, yl.shape[1]), row),
                  pl.BlockSpec((tm, yg.shape[1]), row),
                  pl.BlockSpec((tm, yd.shape[1]), row),
                  pl.BlockSpec((d, d), full2),
                  pl.BlockSpec((1, d), full2),
                  pl.BlockSpec((d, d), full2)],
        out_specs=[pl.BlockSpec((tm, d), row), pl.BlockSpec((tm, d), row)],
        out_shape=[jax.ShapeDtypeStruct((t, d), F32), jax.ShapeDtypeStruct((t, d), F32)],
        compiler_params=_cparams(1),
        name="outproj_qproj",
    )(x, yl, yg, yd, wo, gq.reshape(1, d), wq)


def _attn_kernel(x_ref, q_ref, k_ref, v_ref, wo_ref, o_ref):
    hd = MEM_HD
    acc = x_ref[...]
    for h in range(MEM_HEADS):
        sl = slice(h * hd, (h + 1) * hd)
        s = _mm_nt(q_ref[:, sl], k_ref[:, sl]) * (hd ** -0.5)
        m = jnp.max(s, axis=-1, keepdims=True)
        p = jnp.exp(s - m)
        l = jnp.sum(p, axis=-1, keepdims=True)
        oh = _mm(p, v_ref[:, sl]) / l
        acc = acc + _mm(oh, wo_ref[sl, :])
    o_ref[...] = acc


def _attn_heads_kernel(q_ref, k_ref, v_ref, o_ref, *, gb, tl):
    nh, hd = MEM_HEADS, MEM_HD
    m = k_ref.shape[1]
    row_head = lax.broadcasted_iota(jnp.int32, (nh * tl, 1), 0) // tl
    col_head = lax.broadcasted_iota(jnp.int32, (1, m * nh), 1) % nh
    for b in range(gb):
        kf = k_ref[b].reshape(m * nh, hd)
        vf = v_ref[b].reshape(m * nh, hd)
        rows = slice(b * tl, (b + 1) * tl)
        qs = jnp.concatenate([q_ref[rows, h * hd:(h + 1) * hd] for h in range(nh)], axis=0)
        s = _mm_nt(qs, kf) * (hd ** -0.5)
        s = jnp.where(row_head == col_head, s, -jnp.inf)
        mx = jnp.max(s, axis=-1, keepdims=True)
        p = jnp.exp(s - mx)
        l = jnp.sum(p, axis=-1, keepdims=True)
        o = _mm(p, vf) / l
        for h in range(nh):
            o_ref[rows, h * hd:(h + 1) * hd] = o[h * tl:(h + 1) * tl]


def _oproj_kernel(x_ref, a_ref, wo_ref, *rest):
    o_ref = rest[-1]
    o_ref[...] = x_ref[...] + _mm(a_ref[...], wo_ref[...])


def _attn_first_kernel(x_ref, q_ref, k_ref, v_ref, wo_ref, o_ref, *, n_real):
    i = pl.program_id(0)

    @pl.when(i < n_real)
    def _():
        _attn_kernel(x_ref, q_ref, k_ref, v_ref, wo_ref, o_ref)

    @pl.when(i >= n_real)
    def _():
        o_ref[...] = jnp.zeros_like(o_ref)


def _attn_into_kernel(x_ref, q_ref, k_ref, v_ref, wo_ref, joint_ref, o_ref):
    _attn_kernel(x_ref, q_ref, k_ref, v_ref, wo_ref, o_ref)


def _attention(x, q, k, v, layer, wo, batch, seq, joint, row_off, total_rows):
    t, d = x.shape
    out_shape = jax.ShapeDtypeStruct((total_rows, d), F32)
    extra_specs = [] if joint is None else [pl.BlockSpec(memory_space=pl.ANY)]
    extra_args = () if joint is None else (joint,)
    if k.ndim == 4:
        tl = min(seq, 512)
        nl = seq // tl
        off = row_off // tl
        kblk = (None, None) + k.shape[2:]
        n_real = batch * nl
        if joint is None:
            assert row_off == 0
            n_steps = pl.cdiv(total_rows, tl)
            body = functools.partial(_attn_first_kernel, n_real=n_real)
        else:
            n_steps = n_real
            body = _attn_into_kernel
        kidx = lambda i: (layer, jnp.minimum(i // nl, batch - 1), 0, 0)
        row = lambda i: (jnp.minimum(i, n_real - 1), 0)
        return pl.pallas_call(
            body,
            grid=(n_steps,),
            in_specs=[pl.BlockSpec((tl, d), row),
                      pl.BlockSpec((tl, d), row),
                      pl.BlockSpec(kblk, kidx),
                      pl.BlockSpec(kblk, kidx),
                      pl.BlockSpec((d, d), lambda i: (0, 0))] + extra_specs,
            out_specs=pl.BlockSpec((tl, d), lambda i: (i + off, 0)),
            out_shape=out_shape,
            input_output_aliases={} if joint is None else {5: 0},
            compiler_params=_cparams(1),
            name="mem_attention",
        )(x, q, k, v, wo, *extra_args)
    gb = 2
    kblk = (None, gb) + k.shape[2:]
    kidx = lambda i: (layer, i, 0, 0, 0)
    att = pl.pallas_call(
        functools.partial(_attn_heads_kernel, gb=gb, tl=seq),
        grid=(batch // gb,),
        in_specs=[pl.BlockSpec((gb * seq, d), lambda i: (i, 0)),
                  pl.BlockSpec(kblk, kidx),
                  pl.BlockSpec(kblk, kidx)],
        out_specs=pl.BlockSpec((gb * seq, d), lambda i: (i, 0)),
        out_shape=jax.ShapeDtypeStruct((t, d), F32),
        compiler_params=_cparams(1),
        name="mem_attention_heads",
    )(q, k, v)
    tm = _row_tile(t, 512)
    off = row_off // tm
    return pl.pallas_call(
        _oproj_kernel,
        grid=(t // tm,),
        in_specs=[pl.BlockSpec((tm, d), lambda i: (i, 0)),
                  pl.BlockSpec((tm, d), lambda i: (i, 0)),
                  pl.BlockSpec((d, d), lambda i: (0, 0))] + extra_specs,
        out_specs=pl.BlockSpec((tm, d), lambda i: (i + off, 0)),
        out_shape=out_shape,
        input_output_aliases={} if joint is None else {3: 0},
        compiler_params=_cparams(1),
        name="mem_oproj",
    )(x, att, wo, *extra_args)


def _ffn_kernel(x_ref, g_ref, wg_ref, wu_ref, wd_ref, o_ref, h_ref, acc_ref):
    f = pl.program_id(1)

    @pl.when(f == 0)
    def _():
        h_ref[...] = _rms_rows(x_ref[...], g_ref[...]).astype(BF16)
        acc_ref[...] = jnp.zeros_like(acc_ref)

    h = h_ref[...]
    a = _mm(h, wg_ref[...])
    u = _mm(h, wu_ref[...])
    acc_ref[...] += _mm(_silu(a) * u, wd_ref[...])

    @pl.when(f == pl.num_programs(1) - 1)
    def _():
        o_ref[...] = x_ref[...] + acc_ref[...]


def _ffn_dense(x, g, wg, wu, wd, tm=1024, tf=512):
    t, d = x.shape
    dff = wg.shape[1]
    tm = _row_tile(t, tm)
    return pl.pallas_call(
        _ffn_kernel,
        grid=(t // tm, dff // tf),
        in_specs=[pl.BlockSpec((tm, d), lambda i, f: (i, 0)),
                  pl.BlockSpec((1, d), lambda i, f: (0, 0)),
                  pl.BlockSpec((d, tf), lambda i, f: (0, f)),
                  pl.BlockSpec((d, tf), lambda i, f: (0, f)),
                  pl.BlockSpec((tf, d), lambda i, f: (f, 0))],
        out_specs=pl.BlockSpec((tm, d), lambda i, f: (i, 0)),
        out_shape=jax.ShapeDtypeStruct((t, d), F32),
        scratch_shapes=[pltpu.VMEM((tm, d), BF16), pltpu.VMEM((tm, d), F32)],
        compiler_params=_cparams(2),
        name="ffn_dense",
    )(x, g.reshape(1, d), wg, wu, wd)


def _router_kernel(x_ref, g_ref, whi_ref, wlo_ref, info_ref, cnt_ref, carry_ref, *, tm):
    i = pl.program_id(0)

    @pl.when(i == 0)
    def _():
        carry_ref[...] = jnp.zeros_like(carry_ref)

    h = _rms_rows(x_ref[...], g_ref[...])
    hhi = h.astype(BF16)
    hlo = (h - hhi.astype(F32)).astype(BF16)
    whi = whi_ref[...]
    logits = (jnp.dot(hhi, whi, preferred_element_type=F32)
              + jnp.dot(hlo, whi, preferred_element_type=F32)
              + jnp.dot(hhi, wlo_ref[...], preferred_element_type=F32))
    lane = lax.broadcasted_iota(jnp.int32, (tm, LANES), 1)
    neg = jnp.float32(-jnp.inf)
    logits = jnp.where(lane < N_EXPERTS, logits, neg)
    m1 = jnp.max(logits, axis=-1, keepdims=True)
    i1 = jnp.min(jnp.where(logits == m1, lane, LANES), axis=-1, keepdims=True)
    rest = jnp.where(lane == i1, neg, logits)
    m2 = jnp.max(rest, axis=-1, keepdims=True)
    i2 = jnp.min(jnp.where(rest == m2, lane, LANES), axis=-1, keepdims=True)
    e = jnp.exp(m2 - m1)
    g1 = 1.0 / (1.0 + e)
    g2 = e / (1.0 + e)
    oh1 = jnp.where(lane == i1, 1.0, 0.0)
    oh2 = jnp.where(lane == i2, 1.0, 0.0)
    oh = oh1 + oh2
    ri = lax.broadcasted_iota(jnp.int32, (tm, tm), 0)
    ci = lax.broadcasted_iota(jnp.int32, (tm, tm), 1)
    tri = jnp.where(ci < ri, 1.0, 0.0)
    before = _mm(tri, oh) + carry_ref[0:1, :]
    r1 = jnp.sum(oh1 * before, axis=-1, keepdims=True)
    r2 = jnp.sum(oh2 * before, axis=-1, keepdims=True)
    carry = carry_ref[0:1, :] + jnp.sum(oh, axis=0, keepdims=True)
    carry_ref[...] = jnp.broadcast_to(carry, carry_ref.shape)
    cnt_ref[...] = jnp.broadcast_to(carry, cnt_ref.shape)
    info = jnp.where(lane == 0, i1.astype(F32), 0.0)
    info = jnp.where(lane == 1, i2.astype(F32), info)
    info = jnp.where(lane == 2, r1, info)
    info = jnp.where(lane == 3, r2, info)
    info = jnp.where(lane == 4, g1, info)
    info = jnp.where(lane == 5, g2, info)
    info_ref[...] = info


def _router(x, g, whi, wlo, tm=512):
    t, d = x.shape
    tm = _row_tile(t, tm)
    return pl.pallas_call(
        functools.partial(_router_kernel, tm=tm),
        grid=(t // tm,),
        in_specs=[pl.BlockSpec((tm, d), lambda i: (i, 0)),
                  pl.BlockSpec((1, d), lambda i: (0, 0)),
                  pl.BlockSpec((d, LANES), lambda i: (0, 0)),
                  pl.BlockSpec((d, LANES), lambda i: (0, 0))],
        out_specs=[pl.BlockSpec((tm, LANES), lambda i: (i, 0)),
                   pl.BlockSpec((8, LANES), lambda i: (0, 0))],
        out_shape=[jax.ShapeDtypeStruct((t, LANES), F32),
                   jax.ShapeDtypeStruct((8, LANES), F32)],
        scratch_shapes=[pltpu.VMEM((8, LANES), F32)],
        compiler_params=_cparams(1),
        name="moe_router",
    )(x, g.reshape(1, d), whi, wlo)


def _dispatch_kernel(s1_ref, s2_ref, zs_ref, zf_ref, x_ref, g_ref, xs_hbm, h_ref, zbuf_ref, zsem, sem,
                     *, rows, sub, n_blk):
    i = pl.program_id(0)

    @pl.when(i == 0)
    def _():
        zbuf_ref[...] = jnp.zeros_like(zbuf_ref)
        for e in range(N_EXPERTS):
            dst = xs_hbm.at[pl.ds(pl.multiple_of(zs_ref[e], 8), sub)]
            pltpu.make_async_copy(zbuf_ref, dst, zsem).start()
            pltpu.make_async_copy(zbuf_ref, dst, zsem).wait()

        def fill(j, carry):
            @pl.when(zf_ref[j] == 1)
            def _():
                dst = xs_hbm.at[pl.ds(pl.multiple_of(j * sub, sub), sub)]
                pltpu.make_async_copy(zbuf_ref, dst, zsem).start()
            return carry

        def drain(j, carry):
            @pl.when(zf_ref[j] == 1)
            def _():
                dst = xs_hbm.at[pl.ds(pl.multiple_of(j * sub, sub), sub)]
                pltpu.make_async_copy(zbuf_ref, dst, zsem).wait()
            return carry

        lax.fori_loop(0, n_blk, fill, 0)
        lax.fori_loop(0, n_blk, drain, 0)

    base = i * rows
    h_ref[...] = _rms_rows(x_ref[...], g_ref[...])

    def issue(r, carry):
        src = h_ref.at[pl.ds(r, 1)]
        pltpu.make_async_copy(src, xs_hbm.at[pl.ds(s1_ref[base + r], 1)], sem).start(priority=0)
        pltpu.make_async_copy(src, xs_hbm.at[pl.ds(s2_ref[base + r], 1)], sem).start(priority=1)
        return carry

    lax.fori_loop(0, rows, issue, 0, unroll=8)
    for _ in range(2):
        pltpu.make_async_copy(h_ref, xs_hbm.at[pl.ds(0, rows)], sem).wait()


def _dispatch(slot1, slot2, zero_start, zero_blk, x, g, sub, rows=256):
    t, d = x.shape
    rows = _row_tile(t, rows)
    n_blk = zero_blk.shape[0]
    n_rows = n_blk * sub
    return pl.pallas_call(
        functools.partial(_dispatch_kernel, rows=rows, sub=sub, n_blk=n_blk),
        grid_spec=pltpu.PrefetchScalarGridSpec(
            num_scalar_prefetch=4,
            grid=(t // rows,),
            in_specs=[pl.BlockSpec((rows, d), lambda i, a, b, c, e: (i, 0)),
                      pl.BlockSpec((1, d), lambda i, a, b, c, e: (0, 0))],
            out_specs=pl.BlockSpec(memory_space=pl.ANY),
            scratch_shapes=[pltpu.VMEM((rows, d), F32), pltpu.VMEM((sub, d), F32),
                            pltpu.SemaphoreType.DMA(()), pltpu.SemaphoreType.DMA(())]),
        out_shape=jax.ShapeDtypeStruct((n_rows, d), F32),
        compiler_params=_cparams(1),
        name="moe_dispatch",
    )(slot1, slot2, zero_start, zero_blk, x, g.reshape(1, d))


def _expert_kernel(be_ref, ns_ref, xi_ref, x_ref, wg_ref, wu_ref, wd_ref, o_ref, *, sub, n_sub):
    i = pl.program_id(0)
    f = pl.program_id(1)
    ns = ns_ref[i]

    @pl.when(f == 0)
    def _():
        o_ref[...] = jnp.zeros_like(o_ref)

    def swiglu(rows):
        h = x_ref[rows, :]
        a = _mm(h, wg_ref[...])
        u = _mm(h, wu_ref[...])
        o_ref[rows, :] += _mm(_silu(a) * u, wd_ref[...])

    for s in range(0, n_sub, 2):
        if s + 2 <= n_sub:
            @pl.when(s + 2 <= ns)
            def _():
                swiglu(slice(s * sub, (s + 2) * sub))

        @pl.when(s + 1 == ns)
        def _():
            swiglu(slice(s * sub, (s + 1) * sub))


def _expert_ffn(blk_exp, n_valid_sub, x_blk, xs, wg, wu, wd, sb, sub, tf=512):
    p, d = xs.shape
    dff = wg.shape[2]
    n_super = n_valid_sub.shape[0]
    nf = dff // tf

    def fidx(i, f, ns):
        used = jnp.minimum(ns[i], 1)
        return f * used + (nf - 1) * (1 - used)

    return pl.pallas_call(
        functools.partial(_expert_kernel, sub=sub, n_sub=sb // sub),
        grid_spec=pltpu.PrefetchScalarGridSpec(
            num_scalar_prefetch=3,
            grid=(n_super, nf),
            in_specs=[pl.BlockSpec((sb, d), lambda i, f, be, ns, xi: (xi[i], 0)),
                      pl.BlockSpec((None, d, tf), lambda i, f, be, ns, xi: (be[i], 0, fidx(i, f, ns))),
                      pl.BlockSpec((None, d, tf), lambda i, f, be, ns, xi: (be[i], 0, fidx(i, f, ns))),
                      pl.BlockSpec((None, tf, d), lambda i, f, be, ns, xi: (be[i], fidx(i, f, ns), 0))],
            out_specs=pl.BlockSpec((sb, d), lambda i, f, be, ns, xi: (i, 0))),
        out_shape=jax.ShapeDtypeStruct((n_super * sb, d), F32),
        compiler_params=pltpu.CompilerParams(dimension_semantics=("arbitrary", "arbitrary"),
                                             vmem_limit_bytes=EXPERT_VMEM_LIMIT_BYTES),
        name="moe_experts",
    )(blk_exp, n_valid_sub, x_blk, xs, wg, wu, wd)


def _combine_kernel(s1_ref, s2_ref, x_ref, info_ref, ys_hbm, gf_ref, o_ref, b1_ref, b2_ref, sem,
                    *, rows, final_norm, blk_off):
    base = (pl.program_id(0) + blk_off) * rows

    def issue(r, carry):
        c1 = pltpu.make_async_copy(ys_hbm.at[pl.ds(s1_ref[base + r], 1)], b1_ref.at[pl.ds(r, 1)], sem)
        c2 = pltpu.make_async_copy(ys_hbm.at[pl.ds(s2_ref[base + r], 1)], b2_ref.at[pl.ds(r, 1)], sem)
        c1.start(priority=0)
        c2.start(priority=1)
        return carry

    lax.fori_loop(0, rows, issue, 0, unroll=8)
    pltpu.make_async_copy(ys_hbm.at[pl.ds(0, rows)], b1_ref, sem).wait()
    pltpu.make_async_copy(ys_hbm.at[pl.ds(0, rows)], b2_ref, sem).wait()
    info = info_ref[...]
    y = b1_ref[...] * info[:, 4:5] + b2_ref[...] * info[:, 5:6]
    out = x_ref[...] + y
    if final_norm:
        out = _rms_rows(out, gf_ref[...])
    o_ref[...] = out


def _combine(slot1, slot2, x, info, ys, gf, final_norm, row_off, n_rows, rows=256):
    d = x.shape[1]
    rows = _row_tile(n_rows, rows)
    off = row_off // rows
    return pl.pallas_call(
        functools.partial(_combine_kernel, rows=rows, final_norm=final_norm, blk_off=off),
        grid_spec=pltpu.PrefetchScalarGridSpec(
            num_scalar_prefetch=2,
            grid=(n_rows // rows,),
            in_specs=[pl.BlockSpec((rows, d), lambda i, a, b: (i + off, 0)),
                      pl.BlockSpec((rows, LANES), lambda i, a, b: (i + off, 0)),
                      pl.BlockSpec(memory_space=pl.ANY),
                      pl.BlockSpec((1, d), lambda i, a, b: (0, 0))],
            out_specs=pl.BlockSpec((rows, d), lambda i, a, b: (i, 0)),
            scratch_shapes=[pltpu.VMEM((rows, d), F32), pltpu.VMEM((rows, d), F32),
                            pltpu.SemaphoreType.DMA(())]),
        out_shape=jax.ShapeDtypeStruct((n_rows, d), F32),
        compiler_params=_cparams(1),
        name="moe_combine",
    )(slot1, slot2, x, info, ys, gf.reshape(1, d))


def _moe_ffn(x, g, whi, wlo, wg, wu, wd, gf, final_norm, groups, sub=512):
    t, d = x.shape
    sb = 2048 if 2 * t >= 16 * 1024 else sub
    info, cnt = _router(x, g, whi, wlo)
    e = info[:, 0:2].astype(jnp.int32)
    rank = info[:, 2:4].astype(jnp.int32)
    counts = cnt[0, :N_EXPERTS].astype(jnp.int32)
    n_sb = (counts + sb - 1) // sb
    sb_end = jnp.cumsum(n_sb)
    sb_start = sb_end - n_sb
    row_start = sb_start * sb
    eid = jnp.arange(N_EXPERTS, dtype=jnp.int32)
    slot = jnp.sum(jnp.where(e[:, :, None] == eid, row_start, 0), axis=-1) + rank
    n_super = (2 * t) // sb + N_EXPERTS
    blk = jnp.arange(n_super, dtype=jnp.int32)
    n_used = sb_end[-1]
    used = blk < n_used
    blk_c = jnp.minimum(blk, n_used - 1)
    be = jnp.minimum(jnp.sum((blk_c[:, None] >= sb_end[None, :]).astype(jnp.int32), axis=-1),
                     N_EXPERTS - 1)
    valid = jnp.clip(counts[be] - (blk_c - sb_start[be]) * sb, 0, sb)
    n_valid_sub = jnp.where(used, (valid + sub - 1) // sub, 0).astype(jnp.int32)
    zero_start = ((row_start + counts) // 8 * 8).astype(jnp.int32)
    per = sb // sub
    sub_in_blk = jnp.arange(per, dtype=jnp.int32)
    zero_blk = (sub_in_blk[None, :] >= n_valid_sub[:, None]).astype(jnp.int32).reshape(-1)
    zero_blk = jnp.concatenate([zero_blk, jnp.ones((1,), jnp.int32)])
    xs = _dispatch(slot[:, 0], slot[:, 1], zero_start, zero_blk, x, g, sub)
    ys = _expert_ffn(be.astype(jnp.int32), n_valid_sub, blk_c.astype(jnp.int32), xs, wg, wu, wd, sb, sub)
    s1, s2 = slot[:, 0], slot[:, 1]
    return [_combine(s1, s2, x, info, ys, gf, final_norm, off, n) for off, n in groups]


def _final_norm_kernel(x_ref, g_ref, o_ref):
    o_ref[...] = _rms_rows(x_ref[...], g_ref[...])


def _final_norm(x, g, row_off, n_rows, tm=512):
    d = x.shape[1]
    t = n_rows
    tm = _row_tile(t, tm)
    off = row_off // tm
    return pl.pallas_call(
        _final_norm_kernel,
        grid=(t // tm,),
        in_specs=[pl.BlockSpec((tm, d), lambda i: (i + off, 0)), pl.BlockSpec((1, d), lambda i: (0, 0))],
        out_specs=pl.BlockSpec((tm, d), lambda i: (i, 0)),
        out_shape=jax.ShapeDtypeStruct((t, d), F32),
        compiler_params=_cparams(1),
        name="final_norm",
    )(x, g.reshape(1, d))


def _block_diag(w):
    n, c, d = w.shape
    eye = jnp.eye(n, dtype=w.dtype)
    return jnp.einsum("ncd,nm->ncmd", w, eye).reshape(n * c, n * d)


def _pad_cols(w, n):
    return jnp.pad(w, ((0, 0), (0, n - w.shape[1])))


def _layer_params(l, p):
    (norm_mix, w_in, lru_conv_w, lru_conv_b, lru_a_w, lru_a_b, lru_x_w, lru_x_b, lru_lam, gla_gk_w2,
     gla_gk_b, gla_norm, gdn_conv_w, gdn_a_log, gdn_dt_bias, gdn_norm, w_out, norm_xq, norm_mem,
     w_mq, w_mk, w_mv, w_mo, norm_ffn) = [a[l] for a in p]
    lw = LRU_W
    gk, gv = GLA_HEADS * GLA_DK, GLA_HEADS * GLA_DV
    dh = GDN_HEADS * GDN_DK
    offs = [0]
    for s in (lw, lw, gk, gk, gv, GLA_RANK, gv, dh, dh, dh, GDN_HEADS, GDN_HEADS, dh):
        offs.append(offs[-1] + s)
    col = lambda i: w_in[:, offs[i]:offs[i + 1]]
    w_lru = jnp.concatenate([col(0), col(1)], axis=1)
    w_gla = jnp.concatenate([col(2), col(3), col(4), col(6), _pad_cols(col(5), LANES)], axis=1)
    w_gdn = jnp.concatenate(
        [col(7), col(8), col(9), col(12), _pad_cols(jnp.concatenate([col(10), col(11)], axis=1), LANES)],
        axis=1)
    w_cat = jnp.concatenate([w_lru, w_gla, w_gdn], axis=1).astype(BF16)
    widths = (w_lru.shape[1], w_gla.shape[1], w_gdn.shape[1])
    alog = jnp.zeros((1, LANES), F32).at[0, GDN_HEADS:2 * GDN_HEADS].set(gdn_a_log)
    dtb = jnp.zeros((1, LANES), F32).at[0, GDN_HEADS:2 * GDN_HEADS].set(gdn_dt_bias)
    return dict(
        norm_mix=norm_mix, w_cat=w_cat, widths=widths,
        lru_conv_w=lru_conv_w, lru_conv_b=lru_conv_b,
        lru_a=_block_diag(lru_a_w).astype(BF16), lru_a_b=lru_a_b,
        lru_x=_block_diag(lru_x_w).astype(BF16), lru_x_b=lru_x_b, lru_lam=lru_lam,
        gla_w2=jnp.pad(gla_gk_w2, ((0, LANES - GLA_RANK), (0, 0))).astype(BF16), gla_gk_b=gla_gk_b,
        gla_norm=jnp.tile(gla_norm, GLA_HEADS),
        gdn_conv_w=gdn_conv_w, gdn_alog=alog, gdn_dtb=dtb, gdn_norm=jnp.tile(gdn_norm, GDN_HEADS),
        w_out=w_out.astype(BF16), norm_xq=norm_xq, norm_mem=norm_mem, w_mq=w_mq.astype(BF16),
        w_mkv=jnp.concatenate([w_mk, w_mv], axis=1).astype(BF16), w_mo=w_mo.astype(BF16),
        norm_ffn=norm_ffn)


def _mix_and_attend(x, x_off, grp, l, lp, joint, total_rows):
    batch, seq, n = grp["batch"], grp["seq"], grp["batch"] * grp["seq"]
    lru_h0, lru_buf0, gla_s0, gdn_s0, gdn_buf0 = grp["states"]
    sl = grp["state_layer"](l)
    z_lru, z_gla, z_gdn = _rms_matmul(x, lp["norm_mix"], lp["w_cat"], lp["widths"], "in_proj",
                                      row_off=x_off, n_rows=n)
    y_lru, lru_h, lru_buf = _lru_mixer(
        z_lru, lru_h0, lru_buf0, sl, lp["lru_conv_w"], lp["lru_conv_b"], lp["lru_a"], lp["lru_a_b"],
        lp["lru_x"], lp["lru_x_b"], lp["lru_lam"], batch, seq)
    y_gla, gla_s = _gla_mixer(z_gla.reshape(batch, seq, -1), gla_s0, sl, lp["gla_w2"],
                              lp["gla_gk_b"], lp["gla_norm"], batch, seq)
    y_gdn, gdn_s, gdn_buf = _gdn_mixer(z_gdn.reshape(batch, seq, -1), gdn_s0, gdn_buf0, sl,
                                       lp["gdn_conv_w"], lp["gdn_alog"], lp["gdn_dtb"],
                                       lp["gdn_norm"], batch, seq)
    xn, q = _outproj(x, y_lru, y_gla.reshape(n, -1), y_gdn.reshape(n, -1),
                     lp["w_out"], lp["norm_xq"], lp["w_mq"], row_off=x_off)
    mem_k, mem_v, mem_layer = grp["mem"][l]
    joint = _attention(xn, q, mem_k, mem_v, mem_layer, lp["w_mo"], batch, seq, joint, grp["row_off"],
                       total_rows)
    return joint, (lru_h.reshape(batch, LRU_W), lru_buf, gla_s, gdn_s, gdn_buf)


def _run_layers(groups, layers, ffn, norm_final):
    total_rows = sum(g["batch"] * g["seq"] for g in groups)
    spans = [(g["row_off"], g["batch"] * g["seq"]) for g in groups]
    xs = [(g["x"], 0) for g in groups]
    new_states = [[] for _ in groups]
    outs = None
    for l, lp in enumerate(layers):
        joint = None
        for gi, grp in enumerate(groups):
            joint, st = _mix_and_attend(xs[gi][0], xs[gi][1], grp, l, lp, joint, total_rows)
            new_states[gi].append(st)
        last = l == len(layers) - 1
        kind, fp = ffn[l]
        if kind == "dense":
            joint = _ffn_dense(joint, lp["norm_ffn"], *fp)
            if last:
                outs = [_final_norm(joint, norm_final, off, n) for off, n in spans]
        elif last:
            outs = _moe_ffn(joint, lp["norm_ffn"], *fp, norm_final, True, spans)
        else:
            joint = _moe_ffn(joint, lp["norm_ffn"], *fp, norm_final, False, [(0, total_rows)])[0]
        xs = [(joint, g["row_off"]) for g in groups]
    states = [[jnp.stack(s) for s in zip(*ns)] for ns in new_states]
    return outs, states


def kernel(x_prompt, x_sample, mem_prompt, state_lru_h, state_lru_conv, state_gla, state_gdn, state_gdn_conv, cache_mem_k, cache_mem_v, norm_mix, w_in, lru_conv_w, lru_conv_b, lru_a_w, lru_a_b, lru_x_w, lru_x_b, lru_lam, gla_gk_w2, gla_gk_b, gla_norm, gdn_conv_w, gdn_a_log, gdn_dt_bias, gdn_norm, w_out, norm_xq, norm_mem, w_mq, w_mk, w_mv, w_mo, norm_ffn, w_ff_gate, w_ff_up, w_ff_down, w_router, w_e_gate, w_e_up, w_e_down, norm_final):
    depth = norm_mix.shape[0]
    per_layer = (norm_mix, w_in, lru_conv_w, lru_conv_b, lru_a_w, lru_a_b, lru_x_w, lru_x_b, lru_lam,
                 gla_gk_w2, gla_gk_b, gla_norm, gdn_conv_w, gdn_a_log, gdn_dt_bias, gdn_norm, w_out,
                 norm_xq, norm_mem, w_mq, w_mk, w_mv, w_mo, norm_ffn)
    layers = [_layer_params(l, per_layer) for l in range(depth)]
    ffn = []
    for l in range(depth):
        j = l // 2
        if l % 2 == 0:
            ffn.append(("dense", (w_ff_gate[j], w_ff_up[j], w_ff_down[j])))
        else:
            wr = _pad_cols(w_router[j], LANES)
            whi = wr.astype(BF16)
            wlo = (wr - whi.astype(F32)).astype(BF16)
            ffn.append(("moe", (whi, wlo, w_e_gate[j], w_e_up[j], w_e_down[j])))

    bp, mlen, d = mem_prompt.shape
    mem2 = mem_prompt.reshape(bp * mlen, d)
    pk, pv = [], []
    for lp in layers:
        k2, v2 = _rms_matmul(mem2, lp["norm_mem"], lp["w_mkv"], (d, d), "mem_kv")
        pk.append(k2.reshape(1, bp, mlen, d))
        pv.append(v2.reshape(1, bp, mlen, d))
    zero_state = (jnp.zeros((1, bp, LRU_W), F32), jnp.zeros((1, bp, CONV_K - 1, LRU_W), F32),
                  jnp.zeros((1, bp, GLA_HEADS, GLA_DK, GLA_DV), F32),
                  jnp.zeros((1, bp, GDN_HEADS, GDN_DK, GDN_DV), F32),
                  jnp.zeros((1, bp, CONV_K - 1, 3 * GDN_HEADS * GDN_DK), F32))
    sp = x_prompt.shape[1]
    bs, ss = x_sample.shape[0], x_sample.shape[1]
    groups = [
        dict(x=x_prompt.reshape(bp * sp, d), batch=bp, seq=sp, row_off=0, states=zero_state,
             state_layer=lambda l: 0, mem=[(pk[l], pv[l], 0) for l in range(depth)]),
        dict(x=x_sample.reshape(bs * ss, d), batch=bs, seq=ss, row_off=bp * sp,
             states=(state_lru_h, state_lru_conv, state_gla, state_gdn, state_gdn_conv),
             state_layer=lambda l: l, mem=[(cache_mem_k, cache_mem_v, l) for l in range(depth)]),
    ]
    (y_p, y_s), (p_st, s_st) = _run_layers(groups, layers, ffn, norm_final)
    p_mem_k = jnp.concatenate(pk, axis=0).reshape(depth, bp, mlen, MEM_HEADS, MEM_HD)
    p_mem_v = jnp.concatenate(pv, axis=0).reshape(depth, bp, mlen, MEM_HEADS, MEM_HD)

    return (y_p.reshape(bp, sp, d), y_s.reshape(bs, ss, d), p_st[0], p_st[1], p_st[2], p_st[3], p_st[4],
            p_mem_k, p_mem_v, s_st[0], s_st[1], s_st[2], s_st[3], s_st[4])
```

```python
import functools
import math

import jax
import jax.numpy as jnp
from jax import lax
from jax.experimental import pallas as pl
from jax.experimental.pallas import tpu as pltpu

F32 = jnp.float32
BF16 = jnp.bfloat16
EPS = 1e-6

D_MODEL = 1024
LRU_W = 512
LRU_BLOCKS = 8
LRU_C = 8.0
CONV_K = 4
GLA_HEADS = 4
GLA_DK = 32
GLA_DV = 64
GLA_RANK = 16
GLA_TAU = 16.0
GLA_SUB = 16
GDN_HEADS = 4
GDN_DK = 64
GDN_DV = 64
MIX_CHUNK = 64
MEM_HEADS = 4
MEM_HD = 256
N_EXPERTS = 8
LANES = 128
VMEM_LIMIT_BYTES = 48 * 1024 * 1024
EXPERT_VMEM_LIMIT_BYTES = 56 * 1024 * 1024


def _cparams(n_axes):
    return pltpu.CompilerParams(dimension_semantics=("arbitrary",) * n_axes,
                                vmem_limit_bytes=VMEM_LIMIT_BYTES)


def _mm(a, b):
    return jnp.dot(a.astype(BF16), b.astype(BF16), preferred_element_type=F32)


def _mm_nt(a, b):
    return lax.dot_general(a.astype(BF16), b.astype(BF16), (((1,), (1,)), ((), ())),
                           preferred_element_type=F32)


def _mm_tn(a, b):
    return lax.dot_general(a.astype(BF16), b.astype(BF16), (((0,), (0,)), ((), ())),
                           preferred_element_type=F32)


def _rms_rows(x, g):
    ms = jnp.mean(x * x, axis=-1, keepdims=True)
    return (x * lax.rsqrt(ms + EPS)) * g


def _softplus(x):
    return jnp.maximum(x, 0.0) + jnp.log1p(jnp.exp(-jnp.abs(x)))


def _sigmoid(x):
    return 1.0 / (1.0 + jnp.exp(-x))


def _silu(x):
    return x * _sigmoid(x)


def _gelu_tanh(x):
    c = 0.7978845608028654
    return x * (0.5 * (1.0 + jnp.tanh(c * (x + 0.044715 * (x * x * x)))))


def _seg_cumsum_rows(x, seg):
    rows = x.shape[0]
    tpos = lax.broadcasted_iota(jnp.int32, (rows, 1), 0) & (seg - 1)
    d = 1
    while d < seg:
        x = x + jnp.where(tpos >= d, pltpu.roll(x, d, axis=0), 0.0)
        d *= 2
    return x


def _head_rms(o, gain, n_heads, width):
    lane_head = lax.broadcasted_iota(jnp.int32, (1, n_heads * width), 1) // width
    sq = o * o
    inv = jnp.zeros_like(o)
    for h in range(n_heads):
        m = lane_head == h
        ms = jnp.sum(jnp.where(m, sq, 0.0), axis=-1, keepdims=True) * (1.0 / width)
        inv = jnp.where(m, lax.rsqrt(ms + EPS), inv)
    return (o * inv) * gain


def _stack_heads(x, n_heads, width):
    c = x.shape[0]
    t = jnp.concatenate([x] * n_heads, axis=0)
    row_head = lax.broadcasted_iota(jnp.int32, (n_heads * c, 1), 0) // c
    lane_head = lax.broadcasted_iota(jnp.int32, (1, n_heads * width), 1) // width
    return jnp.where(row_head == lane_head, t, 0.0)


def _unstack_heads(x, n_heads):
    c = x.shape[0] // n_heads
    o = x[0:c]
    for h in range(1, n_heads):
        o = o + x[h * c:(h + 1) * c]
    return o


def _rms_matmul_kernel(x_ref, g_ref, w_ref, *o_refs):
    h = _rms_rows(x_ref[...], g_ref[...]).astype(BF16)
    start = 0
    for o_ref in o_refs:
        n = o_ref.shape[1]
        o_ref[...] = jnp.dot(h, w_ref[:, start:start + n], preferred_element_type=F32)
        start += n


def _row_tile(t, pref):
    tile = min(pref, t)
    while t % tile or tile % 8:
        tile -= 8
    return tile


def _rms_matmul(x, g, w, widths, name, tm=512, row_off=0, n_rows=None):
    d = x.shape[1]
    t = x.shape[0] if n_rows is None else n_rows
    n = w.shape[1]
    tm = _row_tile(t, tm)
    off = row_off // tm
    return pl.pallas_call(
        _rms_matmul_kernel,
        grid=(t // tm,),
        in_specs=[pl.BlockSpec((tm, d), lambda i: (i + off, 0)),
                  pl.BlockSpec((1, d), lambda i: (0, 0)),
                  pl.BlockSpec((d, n), lambda i: (0, 0))],
        out_specs=[pl.BlockSpec((tm, wd), lambda i: (i, 0)) for wd in widths],
        out_shape=[jax.ShapeDtypeStruct((t, wd), F32) for wd in widths],
        compiler_params=_cparams(1),
        name=name,
    )(x, g.reshape(1, d), w)


def _lru_kernel(z_ref, h0_ref, buf0_ref, cw_ref, cb_ref, aw_ref, ab_ref, xw_ref, xb_ref, lam_ref,
                y_ref, hout_ref, bufout_ref, xs_ref, hc_ref, *, bt, tl, nt):
    w = LRU_W
    rows = bt * tl
    j = pl.program_id(0) % nt

    @pl.when(j == 0)
    def _():
        xs_ref[:, 5:8, :] = buf0_ref[...]
        hc_ref[...] = h0_ref[...]

    xs_ref[:, 8:, :] = z_ref[:, :w].reshape(bt, tl, w)
    gate = z_ref[:, w:]
    cw = cw_ref[...]
    xc = cb_ref[...] + xs_ref[:, 5:5 + tl, :] * cw[0:1]
    for k in range(1, CONV_K):
        xc = xc + xs_ref[:, 5 + k:5 + k + tl, :] * cw[k:k + 1]
    tail = xs_ref[:, 5 + tl:8 + tl, :]
    xs_ref[:, 5:8, :] = tail
    bufout_ref[...] = tail

    xc = xc.reshape(rows, w)
    r = _sigmoid(_mm(xc, aw_ref[...]) + ab_ref[...])
    ig = _sigmoid(_mm(xc, xw_ref[...]) + xb_ref[...])
    log_a = (-LRU_C * r) * _softplus(-lam_ref[...])
    a = jnp.exp(log_a)
    th = jnp.tanh(log_a)
    b = jnp.sqrt((-2.0 * th) / (1.0 - th)) * (ig * xc)

    sub = 8
    gps = tl // sub
    a3 = a.reshape(rows // sub, sub, w)
    b3 = b.reshape(rows // sub, sub, w)
    spos = lax.broadcasted_iota(jnp.int32, (1, sub, 1), 1)
    d = 1
    while d < sub:
        m = spos >= d
        b3 = jnp.where(m, a3 * pltpu.roll(b3, d, axis=1) + b3, b3)
        a3 = jnp.where(m, a3 * pltpu.roll(a3, d, axis=1), a3)
        d *= 2
    a4 = a3.reshape(bt, gps, sub, w)
    b4 = b3.reshape(bt, gps, sub, w)
    carry = hc_ref[...]
    hs = []
    for r in range(gps):
        hr = b4[:, r] + a4[:, r] * carry
        hs.append(hr)
        carry = hr[:, sub - 1:sub, :]
    h = jnp.stack(hs, axis=1).reshape(rows, w)
    hlast = carry
    hc_ref[...] = hlast
    hout_ref[...] = hlast
    y_ref[...] = h * _gelu_tanh(gate)


def _lru_mixer(z, h0, buf0, layer, cw, cb, aw, ab, xw, xb, lam, batch, seq):
    w = LRU_W
    tl = min(seq, 256)
    bt = min(batch, max(1, 256 // seq))
    nt = seq // tl
    rows = bt * tl
    grid = (batch * seq // rows,)
    if nt > 1:
        sidx = lambda i: (i // nt, 0, 0)
        lidx = lambda i: (layer, i // nt, 0, 0)
    else:
        sidx = lambda i: (i, 0, 0)
        lidx = lambda i: (layer, i, 0, 0)
    full2 = lambda i: (0, 0)
    return pl.pallas_call(
        functools.partial(_lru_kernel, bt=bt, tl=tl, nt=nt),
        grid=grid,
        in_specs=[pl.BlockSpec((rows, 2 * w), lambda i: (i, 0)),
                  pl.BlockSpec((None, bt, 1, w), lidx),
                  pl.BlockSpec((None, bt, CONV_K - 1, w), lidx),
                  pl.BlockSpec((CONV_K, w), full2),
                  pl.BlockSpec((1, w), full2),
                  pl.BlockSpec((w, w), full2),
                  pl.BlockSpec((1, w), full2),
                  pl.BlockSpec((w, w), full2),
                  pl.BlockSpec((1, w), full2),
                  pl.BlockSpec((1, w), full2)],
        out_specs=[pl.BlockSpec((rows, w), lambda i: (i, 0)),
                   pl.BlockSpec((bt, 1, w), sidx),
                   pl.BlockSpec((bt, CONV_K - 1, w), sidx)],
        out_shape=[jax.ShapeDtypeStruct((batch * seq, w), F32),
                   jax.ShapeDtypeStruct((batch, 1, w), F32),
                   jax.ShapeDtypeStruct((batch, CONV_K - 1, w), F32)],
        scratch_shapes=[pltpu.VMEM((bt, 8 + tl, w), F32), pltpu.VMEM((bt, 1, w), F32)],
        compiler_params=_cparams(1),
        name="lru_mixer",
    )(z, h0.reshape(h0.shape[0], batch, 1, w), buf0, cw, cb.reshape(1, w), aw, ab.reshape(1, w), xw,
      xb.reshape(1, w), lam.reshape(1, w))


def _gla_kernel(z_ref, s0_ref, w2_ref, gb_ref, gn_ref, y_ref, sout_ref, s_ref, *, c, sc, g):
    nh = GLA_HEADS
    kw = nh * GLA_DK
    vw = nh * GLA_DV

    @pl.when(pl.program_id(1) == 0)
    def _():
        s_ref[...] = jnp.zeros_like(s_ref)
        for b in range(g):
            for h in range(nh):
                s_ref[b, h * GLA_DK:(h + 1) * GLA_DK, h * GLA_DV:(h + 1) * GLA_DV] = s0_ref[b, h]

    bs = range(g)
    q = [z_ref[b, :, 0:kw] * (GLA_DK ** -0.5) for b in bs]
    k = [z_ref[b, :, kw:2 * kw] for b in bs]
    v = [z_ref[b, :, 2 * kw:2 * kw + vw] for b in bs]
    gk = [-_softplus(-(_mm(z_ref[b, :, 2 * kw + 2 * vw:], w2_ref[...]) + gb_ref[...])) / GLA_TAU
          for b in bs]
    gcum = [_seg_cumsum_rows(gk[b], sc) for b in bs]
    qp = [q[b] * jnp.exp(gcum[b]) for b in bs]

    ri = lax.broadcasted_iota(jnp.int32, (nh * c, nh * c), 0)
    ci = lax.broadcasted_iota(jnp.int32, (nh * c, nh * c), 1)
    keep = (ri // sc == ci // sc) & (ci <= ri)
    a = [_mm_nt(_stack_heads(qp[b], nh, GLA_DK), _stack_heads(k[b] * jnp.exp(-gcum[b]), nh, GLA_DK))
         for b in bs]
    o = [_unstack_heads(_mm(jnp.where(keep, a[b], 0.0), _stack_heads(v[b], nh, GLA_DV)), nh)
         for b in bs]

    s = [s_ref[b] for b in bs]
    eye = (lax.broadcasted_iota(jnp.int32, (kw, kw), 0) ==
           lax.broadcasted_iota(jnp.int32, (kw, kw), 1))
    bd = (lax.broadcasted_iota(jnp.int32, (kw, vw), 0) // GLA_DK ==
          lax.broadcasted_iota(jnp.int32, (kw, vw), 1) // GLA_DV)
    o_inter = [[] for _ in bs]
    for i in range(c // sc):
        lo, hi = i * sc, (i + 1) * sc
        for b in bs:
            o_inter[b].append(_mm(qp[b][lo:hi], s[b]))
            glast = gcum[b][hi - 1:hi]
            kpp = k[b][lo:hi] * jnp.exp(glast - gcum[b][lo:hi])
            u = _mm_tn(kpp, v[b][lo:hi])
            dcol = jnp.sum(jnp.where(eye, jnp.exp(glast), 0.0), axis=1, keepdims=True)
            s[b] = dcol * s[b] + jnp.where(bd, u, 0.0)
    for b in bs:
        s_ref[b] = s[b]
        for h in range(nh):
            sout_ref[b, h] = s[b][h * GLA_DK:(h + 1) * GLA_DK, h * GLA_DV:(h + 1) * GLA_DV]
        ob = o[b] + jnp.concatenate(o_inter[b], axis=0)
        gate = z_ref[b, :, 2 * kw + vw:2 * kw + 2 * vw]
        y_ref[b] = _head_rms(ob, gn_ref[...], nh, GLA_DV) * _silu(gate)


def _gla_mixer(z, s0, layer, w2, gb, gn, batch, seq):
    c = min(seq, MIX_CHUNK)
    sc = min(c, GLA_SUB)
    nt = seq // c
    g = _mixer_group(batch, c)
    kw, vw = GLA_HEADS * GLA_DK, GLA_HEADS * GLA_DV
    zw = z.shape[2]
    full2 = lambda i, j: (0, 0)
    sblk = (g, GLA_HEADS, GLA_DK, GLA_DV)
    return pl.pallas_call(
        functools.partial(_gla_kernel, c=c, sc=sc, g=g),
        grid=(batch // g, nt),
        in_specs=[pl.BlockSpec((g, c, zw), lambda i, j: (i, j, 0)),
                  pl.BlockSpec((None,) + sblk, lambda i, j: (layer, i, 0, 0, 0)),
                  pl.BlockSpec((LANES, kw), full2),
                  pl.BlockSpec((1, kw), full2),
                  pl.BlockSpec((1, vw), full2)],
        out_specs=[pl.BlockSpec((g, c, vw), lambda i, j: (i, j, 0)),
                   pl.BlockSpec(sblk, lambda i, j: (i, 0, 0, 0))],
        out_shape=[jax.ShapeDtypeStruct((batch, seq, vw), F32),
                   jax.ShapeDtypeStruct((batch, GLA_HEADS, GLA_DK, GLA_DV), F32)],
        scratch_shapes=[pltpu.VMEM((g, kw, vw), F32)],
        compiler_params=_cparams(2),
        name="gla_mixer",
    )(z, s0, w2, gb.reshape(1, kw), gn.reshape(1, vw))


def _gdn_prep(z_ref, cw_ref, alog_ref, dtb_ref, bufout_ref, xs_ref, *, c):
    nh = GDN_HEADS
    hw = nh * GDN_DK
    cw3 = 3 * hw

    xs_ref[8:, :] = z_ref[:, 0:cw3]
    cw = cw_ref[...]
    qkv = xs_ref[5:5 + c, :] * cw[0:1]
    for kk in range(1, CONV_K):
        qkv = qkv + xs_ref[5 + kk:5 + kk + c, :] * cw[kk:kk + 1]
    tail = xs_ref[5 + c:8 + c, :]
    xs_ref[5:8, :] = tail
    bufout_ref[...] = tail
    qkv = _silu(qkv)
    zg = z_ref[:, cw3:cw3 + hw]
    sm = z_ref[:, cw3 + hw:]

    lane_head = lax.broadcasted_iota(jnp.int32, (1, hw), 1) // GDN_DK

    def l2n(x):
        sq = x * x
        inv = jnp.zeros_like(x)
        for h in range(nh):
            m = lane_head == h
            ss = jnp.sum(jnp.where(m, sq, 0.0), axis=-1, keepdims=True)
            inv = jnp.where(m, lax.rsqrt(ss + EPS), inv)
        return x * inv

    q = l2n(qkv[:, 0:hw]) * (GDN_DK ** -0.5)
    k = l2n(qkv[:, hw:2 * hw])
    v = qkv[:, 2 * hw:3 * hw]
    beta = _sigmoid(sm)
    glog = -jnp.exp(alog_ref[...]) * _softplus(sm + dtb_ref[...])
    gcum = _seg_cumsum_rows(glog, c)

    n = nh * c
    bcol = jnp.concatenate([beta[:, h:h + 1] for h in range(nh)], axis=0)
    gcol = jnp.concatenate([gcum[:, nh + h:nh + h + 1] for h in range(nh)], axis=0)
    glast = jnp.concatenate(
        [jnp.broadcast_to(gcum[c - 1:c, nh + h:nh + h + 1], (c, 1)) for h in range(nh)], axis=0)
    ri = lax.broadcasted_iota(jnp.int32, (n, n), 0)
    ci = lax.broadcasted_iota(jnp.int32, (n, n), 1)
    grow = jnp.sum(jnp.where(ri == ci, gcol, 0.0), axis=0, keepdims=True)
    same = ri // c == ci // c
    incl = same & (ci <= ri)
    strict = same & (ci < ri)
    dec = jnp.where(incl, jnp.exp(jnp.where(incl, gcol - grow, 0.0)), 0.0)

    sdec = jnp.concatenate(
        [jnp.broadcast_to(jnp.exp(gcum[c - 1:c, nh + h:nh + h + 1]), (GDN_DK, 1)) for h in range(nh)],
        axis=0)
    return dict(ks=_stack_heads(k, nh, GDN_DK), qs=_stack_heads(q, nh, GDN_DK),
                vs=_stack_heads(v, nh, GDN_DV), bcol=bcol, egc=jnp.exp(gcol), dec=dec,
                kdec=jnp.exp(glast - gcol), sdec=sdec, zg=zg)


def _gdn_kernel(z_ref, s0_ref, buf0_ref, cw_ref, alog_ref, dtb_ref, gn_ref,
                y_ref, sout_ref, bufout_ref, xs_ref, s_ref, *, c, g):
    nh = GDN_HEADS
    hw = nh * GDN_DK
    n = nh * c

    @pl.when(pl.program_id(1) == 0)
    def _():
        xs_ref[:, 5:8, :] = buf0_ref[...]
        s_ref[...] = jnp.zeros_like(s_ref)
        for b in range(g):
            for h in range(nh):
                s_ref[b, h * GDN_DK:(h + 1) * GDN_DK, h * GDN_DV:(h + 1) * GDN_DV] = s0_ref[b, h]

    bs = range(g)
    pr = [_gdn_prep(z_ref.at[b], cw_ref, alog_ref, dtb_ref, bufout_ref.at[b], xs_ref.at[b], c=c)
          for b in bs]
    ri = lax.broadcasted_iota(jnp.int32, (n, n), 0)
    ci = lax.broadcasted_iota(jnp.int32, (n, n), 1)
    same = ri // c == ci // c
    incl = same & (ci <= ri)
    strict = same & (ci < ri)
    kq = [_mm_nt(jnp.concatenate([pr[b]["ks"], pr[b]["qs"]], axis=0), pr[b]["ks"]) for b in bs]

    p = [jnp.where(strict, -(pr[b]["bcol"] * kq[b][0:n]) * pr[b]["dec"], 0.0) for b in bs]
    tinv = [jnp.where(ri == ci, 1.0, 0.0) + p[b] for b in bs]
    span = 2
    while span < c:
        p = [_mm(p[b], p[b]) for b in bs]
        tinv = [tinv[b] + _mm(tinv[b], p[b]) for b in bs]
        span *= 2

    s = [s_ref[b] for b in bs]
    uw = [_mm(tinv[b], jnp.concatenate([pr[b]["vs"] * pr[b]["bcol"],
                                        pr[b]["ks"] * (pr[b]["bcol"] * pr[b]["egc"])], axis=1))
          for b in bs]
    qw = [_mm(jnp.concatenate([pr[b]["qs"] * pr[b]["egc"], uw[b][:, hw:]], axis=0), s[b])
          for b in bs]
    vnew = [uw[b][:, 0:hw] - qw[b][n:] for b in bs]
    av = [_mm(jnp.where(incl, kq[b][n:] * pr[b]["dec"], 0.0), vnew[b]) for b in bs]
    kv = [_mm_tn(pr[b]["ks"] * pr[b]["kdec"], vnew[b]) for b in bs]
    for b in bs:
        o = _unstack_heads(qw[b][0:n] + av[b], nh)
        sn = pr[b]["sdec"] * s[b] + kv[b]
        s_ref[b] = sn
        for h in range(nh):
            sout_ref[b, h] = sn[h * GDN_DK:(h + 1) * GDN_DK, h * GDN_DV:(h + 1) * GDN_DV]
        y_ref[b] = _head_rms(o, gn_ref[...], nh, GDN_DV) * _silu(pr[b]["zg"])


def _mixer_group(batch, c):
    return min(batch, max(4, 128 // c))


def _gdn_mixer(z, s0, buf0, layer, cw, alog, dtb, gn, batch, seq):
    c = min(seq, MIX_CHUNK)
    nt = seq // c
    g = _mixer_group(batch, c)
    hw = GDN_HEADS * GDN_DK
    zw = z.shape[2]
    full2 = lambda i, j: (0, 0)
    sblk = (g, GDN_HEADS, GDN_DK, GDN_DV)
    return pl.pallas_call(
        functools.partial(_gdn_kernel, c=c, g=g),
        grid=(batch // g, nt),
        in_specs=[pl.BlockSpec((g, c, zw), lambda i, j: (i, j, 0)),
                  pl.BlockSpec((None,) + sblk, lambda i, j: (layer, i, 0, 0, 0)),
                  pl.BlockSpec((None, g, CONV_K - 1, 3 * hw), lambda i, j: (layer, i, 0, 0)),
                  pl.BlockSpec((CONV_K, 3 * hw), full2),
                  pl.BlockSpec((1, LANES), full2),
                  pl.BlockSpec((1, LANES), full2),
                  pl.BlockSpec((1, hw), full2)],
        out_specs=[pl.BlockSpec((g, c, hw), lambda i, j: (i, j, 0)),
                   pl.BlockSpec(sblk, lambda i, j: (i, 0, 0, 0)),
                   pl.BlockSpec((g, CONV_K - 1, 3 * hw), lambda i, j: (i, 0, 0))],
        out_shape=[jax.ShapeDtypeStruct((batch, seq, hw), F32),
                   jax.ShapeDtypeStruct((batch, GDN_HEADS, GDN_DK, GDN_DV), F32),
                   jax.ShapeDtypeStruct((batch, CONV_K - 1, 3 * hw), F32)],
        scratch_shapes=[pltpu.VMEM((g, 8 + c, 3 * hw), F32), pltpu.VMEM((g, hw, hw), F32)],
        compiler_params=_cparams(2),
        name="gdn_mixer",
    )(z, s0, buf0, cw, alog, dtb, gn.reshape(1, hw))


def _outproj_kernel(x_ref, yl_ref, yg_ref, yd_ref, wo_ref, gq_ref, wq_ref, xn_ref, q_ref):
    lw = LRU_W
    gw = GLA_HEADS * GLA_DV
    y = _mm(yl_ref[...], wo_ref[0:lw, :])
    y = y + _mm(yg_ref[...], wo_ref[lw:lw + gw, :])
    y = y + _mm(yd_ref[...], wo_ref[lw + gw:, :])
    xn = x_ref[...] + y
    xn_ref[...] = xn
    q_ref[...] = _mm(_rms_rows(xn, gq_ref[...]), wq_ref[...])


def _outproj(x, yl, yg, yd, wo, gq, wq, tm=512, row_off=0):
    d = x.shape[1]
    t = yl.shape[0]
    tm = _row_tile(t, tm)
    off = row_off // tm
    row = lambda i: (i, 0)
    full2 = lambda i: (0, 0)
    return pl.pallas_call(
        _outproj_kernel,
        grid=(t // tm,),
        in_specs=[pl.BlockSpec((tm, d), lambda i: (i + off, 0)),
                  pl.BlockSpec((tm, yl.shape[1]), row),
                  pl.BlockSpec((tm, yg.shape[1]), row),
                  pl.BlockSpec((tm, yd.shape[1]), row),
                  pl.BlockSpec((d, d), full2),
                  pl.BlockSpec((1, d), full2),
                  pl.BlockSpec((d, d), full2)],
        out_specs=[pl.BlockSpec((tm, d), row), pl.BlockSpec((tm, d), row)],
        out_shape=[jax.ShapeDtypeStruct((t, d), F32), jax.ShapeDtypeStruct((t, d), F32)],
        compiler_params=_cparams(1),
        name="outproj_qproj",
    )(x, yl, yg, yd, wo, gq.reshape(1, d), wq)


def _attn_kernel(x_ref, q_ref, k_ref, v_ref, wo_ref, o_ref):
    hd = MEM_HD
    acc = x_ref[...]
    for h in range(MEM_HEADS):
        sl = slice(h * hd, (h + 1) * hd)
        s = _mm_nt(q_ref[:, sl], k_ref[:, sl]) * (hd ** -0.5)
        m = jnp.max(s, axis=-1, keepdims=True)
        p = jnp.exp(s - m)
        l = jnp.sum(p, axis=-1, keepdims=True)
        oh = _mm(p, v_ref[:, sl]) / l
        acc = acc + _mm(oh, wo_ref[sl, :])
    o_ref[...] = acc


def _attn_heads_kernel(q_ref, k_ref, v_ref, o_ref, *, gb, tl):
    nh, hd = MEM_HEADS, MEM_HD
    m = k_ref.shape[1]
    row_head = lax.broadcasted_iota(jnp.int32, (nh * tl, 1), 0) // tl
    col_head = lax.broadcasted_iota(jnp.int32, (1, m * nh), 1) % nh
    for b in range(gb):
        kf = k_ref[b].reshape(m * nh, hd)
        vf = v_ref[b].reshape(m * nh, hd)
        rows = slice(b * tl, (b + 1) * tl)
        qs = jnp.concatenate([q_ref[rows, h * hd:(h + 1) * hd] for h in range(nh)], axis=0)
        s = _mm_nt(qs, kf) * (hd ** -0.5)
        s = jnp.where(row_head == col_head, s, -jnp.inf)
        mx = jnp.max(s, axis=-1, keepdims=True)
        p = jnp.exp(s - mx)
        l = jnp.sum(p, axis=-1, keepdims=True)
        o = _mm(p, vf) / l
        for h in range(nh):
            o_ref[rows, h * hd:(h + 1) * hd] = o[h * tl:(h + 1) * tl]


def _oproj_kernel(x_ref, a_ref, wo_ref, *rest):
    o_ref = rest[-1]
    o_ref[...] = x_ref[...] + _mm(a_ref[...], wo_ref[...])


def _attn_first_kernel(x_ref, q_ref, k_ref, v_ref, wo_ref, o_ref, *, n_real):
    i = pl.program_id(0)

    @pl.when(i < n_real)
    def _():
        _attn_kernel(x_ref, q_ref, k_ref, v_ref, wo_ref, o_ref)

    @pl.when(i >= n_real)
    def _():
        o_ref[...] = jnp.zeros_like(o_ref)


def _attn_into_kernel(x_ref, q_ref, k_ref, v_ref, wo_ref, joint_ref, o_ref):
    _attn_kernel(x_ref, q_ref, k_ref, v_ref, wo_ref, o_ref)


def _attention(x, q, k, v, layer, wo, batch, seq, joint, row_off, total_rows):
    t, d = x.shape
    out_shape = jax.ShapeDtypeStruct((total_rows, d), F32)
    extra_specs = [] if joint is None else [pl.BlockSpec(memory_space=pl.ANY)]
    extra_args = () if joint is None else (joint,)
    if k.ndim == 4:
        tl = min(seq, 512)
        nl = seq // tl
        off = row_off // tl
        kblk = (None, None) + k.shape[2:]
        n_real = batch * nl
        if joint is None:
            assert row_off == 0
            n_steps = pl.cdiv(total_rows, tl)
            body = functools.partial(_attn_first_kernel, n_real=n_real)
        else:
            n_steps = n_real
            body = _attn_into_kernel
        kidx = lambda i: (layer, jnp.minimum(i // nl, batch - 1), 0, 0)
        row = lambda i: (jnp.minimum(i, n_real - 1), 0)
        return pl.pallas_call(
            body,
            grid=(n_steps,),
            in_specs=[pl.BlockSpec((tl, d), row),
                      pl.BlockSpec((tl, d), row),
                      pl.BlockSpec(kblk, kidx),
                      pl.BlockSpec(kblk, kidx),
                      pl.BlockSpec((d, d), lambda i: (0, 0))] + extra_specs,
            out_specs=pl.BlockSpec((tl, d), lambda i: (i + off, 0)),
            out_shape=out_shape,
            input_output_aliases={} if joint is None else {5: 0},
            compiler_params=_cparams(1),
            name="mem_attention",
        )(x, q, k, v, wo, *extra_args)
    gb = 2
    kblk = (None, gb) + k.shape[2:]
    kidx = lambda i: (layer, i, 0, 0, 0)
    att = pl.pallas_call(
        functools.partial(_attn_heads_kernel, gb=gb, tl=seq),
        grid=(batch // gb,),
        in_specs=[pl.BlockSpec((gb * seq, d), lambda i: (i, 0)),
                  pl.BlockSpec(kblk, kidx),
                  pl.BlockSpec(kblk, kidx)],
        out_specs=pl.BlockSpec((gb * seq, d), lambda i: (i, 0)),
        out_shape=jax.ShapeDtypeStruct((t, d), F32),
        compiler_params=_cparams(1),
        name="mem_attention_heads",
    )(q, k, v)
    tm = _row_tile(t, 512)
    off = row_off // tm
    return pl.pallas_call(
        _oproj_kernel,
        grid=(t // tm,),
        in_specs=[pl.BlockSpec((tm, d), lambda i: (i, 0)),
                  pl.BlockSpec((tm, d), lambda i: (i, 0)),
                  pl.BlockSpec((d, d), lambda i: (0, 0))] + extra_specs,
        out_specs=pl.BlockSpec((tm, d), lambda i: (i + off, 0)),
        out_shape=out_shape,
        input_output_aliases={} if joint is None else {3: 0},
        compiler_params=_cparams(1),
        name="mem_oproj",
    )(x, att, wo, *extra_args)


def _ffn_kernel(x_ref, g_ref, wg_ref, wu_ref, wd_ref, o_ref, h_ref, acc_ref):
    f = pl.program_id(1)

    @pl.when(f == 0)
    def _():
        h_ref[...] = _rms_rows(x_ref[...], g_ref[...]).astype(BF16)
        acc_ref[...] = jnp.zeros_like(acc_ref)

    h = h_ref[...]
    a = _mm(h, wg_ref[...])
    u = _mm(h, wu_ref[...])
    acc_ref[...] += _mm(_silu(a) * u, wd_ref[...])

    @pl.when(f == pl.num_programs(1) - 1)
    def _():
        o_ref[...] = x_ref[...] + acc_ref[...]


def _ffn_dense(x, g, wg, wu, wd, tm=1024, tf=512):
    t, d = x.shape
    dff = wg.shape[1]
    tm = _row_tile(t, tm)
    return pl.pallas_call(
        _ffn_kernel,
        grid=(t // tm, dff // tf),
        in_specs=[pl.BlockSpec((tm, d), lambda i, f: (i, 0)),
                  pl.BlockSpec((1, d), lambda i, f: (0, 0)),
                  pl.BlockSpec((d, tf), lambda i, f: (0, f)),
                  pl.BlockSpec((d, tf), lambda i, f: (0, f)),
                  pl.BlockSpec((tf, d), lambda i, f: (f, 0))],
        out_specs=pl.BlockSpec((tm, d), lambda i, f: (i, 0)),
        out_shape=jax.ShapeDtypeStruct((t, d), F32),
        scratch_shapes=[pltpu.VMEM((tm, d), BF16), pltpu.VMEM((tm, d), F32)],
        compiler_params=_cparams(2),
        name="ffn_dense",
    )(x, g.reshape(1, d), wg, wu, wd)


def _router_kernel(x_ref, g_ref, whi_ref, wlo_ref, info_ref, cnt_ref, carry_ref, *, tm):
    i = pl.program_id(0)

    @pl.when(i == 0)
    def _():
        carry_ref[...] = jnp.zeros_like(carry_ref)

    h = _rms_rows(x_ref[...], g_ref[...])
    hhi = h.astype(BF16)
    hlo = (h - hhi.astype(F32)).astype(BF16)
    whi = whi_ref[...]
    logits = (jnp.dot(hhi, whi, preferred_element_type=F32)
              + jnp.dot(hlo, whi, preferred_element_type=F32)
              + jnp.dot(hhi, wlo_ref[...], preferred_element_type=F32))
    lane = lax.broadcasted_iota(jnp.int32, (tm, LANES), 1)
    neg = jnp.float32(-jnp.inf)
    logits = jnp.where(lane < N_EXPERTS, logits, neg)
    m1 = jnp.max(logits, axis=-1, keepdims=True)
    i1 = jnp.min(jnp.where(logits == m1, lane, LANES), axis=-1, keepdims=True)
    rest = jnp.where(lane == i1, neg, logits)
    m2 = jnp.max(rest, axis=-1, keepdims=True)
    i2 = jnp.min(jnp.where(rest == m2, lane, LANES), axis=-1, keepdims=True)
    e = jnp.exp(m2 - m1)
    g1 = 1.0 / (1.0 + e)
    g2 = e / (1.0 + e)
    oh1 = jnp.where(lane == i1, 1.0, 0.0)
    oh2 = jnp.where(lane == i2, 1.0, 0.0)
    oh = oh1 + oh2
    ri = lax.broadcasted_iota(jnp.int32, (tm, tm), 0)
    ci = lax.broadcasted_iota(jnp.int32, (tm, tm), 1)
    tri = jnp.where(ci < ri, 1.0, 0.0)
    before = _mm(tri, oh) + carry_ref[0:1, :]
    r1 = jnp.sum(oh1 * before, axis=-1, keepdims=True)
    r2 = jnp.sum(oh2 * before, axis=-1, keepdims=True)
    carry = carry_ref[0:1, :] + jnp.sum(oh, axis=0, keepdims=True)
    carry_ref[...] = jnp.broadcast_to(carry, carry_ref.shape)
    cnt_ref[...] = jnp.broadcast_to(carry, cnt_ref.shape)
    info = jnp.where(lane == 0, i1.astype(F32), 0.0)
    info = jnp.where(lane == 1, i2.astype(F32), info)
    info = jnp.where(lane == 2, r1, info)
    info = jnp.where(lane == 3, r2, info)
    info = jnp.where(lane == 4, g1, info)
    info = jnp.where(lane == 5, g2, info)
    info_ref[...] = info


def _router(x, g, whi, wlo, tm):
    t, d = x.shape
    return pl.pallas_call(
        functools.partial(_router_kernel, tm=tm),
        grid=(t // tm,),
        in_specs=[pl.BlockSpec((tm, d), lambda i: (i, 0)),
                  pl.BlockSpec((1, d), lambda i: (0, 0)),
                  pl.BlockSpec((d, LANES), lambda i: (0, 0)),
                  pl.BlockSpec((d, LANES), lambda i: (0, 0))],
        out_specs=[pl.BlockSpec((tm, LANES), lambda i: (i, 0)),
                   pl.BlockSpec((8, LANES), lambda i: (i, 0))],
        out_shape=[jax.ShapeDtypeStruct((t, LANES), F32),
                   jax.ShapeDtypeStruct((t // tm * 8, LANES), F32)],
        scratch_shapes=[pltpu.VMEM((8, LANES), F32)],
        compiler_params=_cparams(1),
        name="moe_router",
    )(x, g.reshape(1, d), whi, wlo)


def _dispatch_kernel(s1_ref, s2_ref, zs_ref, zf_ref, x_ref, g_ref, xs_hbm, h_ref, zbuf_ref, zsem, sem,
                     *, rows, sub, n_blk):
    i = pl.program_id(0)

    @pl.when(i == 0)
    def _():
        zbuf_ref[...] = jnp.zeros_like(zbuf_ref)
        for e in range(N_EXPERTS):
            dst = xs_hbm.at[pl.ds(pl.multiple_of(zs_ref[e], 8), sub)]
            pltpu.make_async_copy(zbuf_ref, dst, zsem).start()
            pltpu.make_async_copy(zbuf_ref, dst, zsem).wait()

        def fill(j, carry):
            @pl.when(zf_ref[j] == 1)
            def _():
                dst = xs_hbm.at[pl.ds(pl.multiple_of(j * sub, sub), sub)]
                pltpu.make_async_copy(zbuf_ref, dst, zsem).start()
            return carry

        def drain(j, carry):
            @pl.when(zf_ref[j] == 1)
            def _():
                dst = xs_hbm.at[pl.ds(pl.multiple_of(j * sub, sub), sub)]
                pltpu.make_async_copy(zbuf_ref, dst, zsem).wait()
            return carry

        lax.fori_loop(0, n_blk, fill, 0)
        lax.fori_loop(0, n_blk, drain, 0)

    base = i * rows
    h_ref[...] = _rms_rows(x_ref[...], g_ref[...])

    def issue(r, carry):
        src = h_ref.at[pl.ds(r, 1)]
        pltpu.make_async_copy(src, xs_hbm.at[pl.ds(s1_ref[base + r], 1)], sem).start(priority=0)
        pltpu.make_async_copy(src, xs_hbm.at[pl.ds(s2_ref[base + r], 1)], sem).start(priority=1)
        return carry

    lax.fori_loop(0, rows, issue, 0, unroll=8)
    for _ in range(2):
        pltpu.make_async_copy(h_ref, xs_hbm.at[pl.ds(0, rows)], sem).wait()


def _dispatch(slot1, slot2, zero_start, zero_blk, x, g, sub, rows=256):
    t, d = x.shape
    rows = _row_tile(t, rows)
    n_blk = zero_blk.shape[0]
    n_rows = n_blk * sub
    return pl.pallas_call(
        functools.partial(_dispatch_kernel, rows=rows, sub=sub, n_blk=n_blk),
        grid_spec=pltpu.PrefetchScalarGridSpec(
            num_scalar_prefetch=4,
            grid=(t // rows,),
            in_specs=[pl.BlockSpec((rows, d), lambda i, a, b, c, e: (i, 0)),
                      pl.BlockSpec((1, d), lambda i, a, b, c, e: (0, 0))],
            out_specs=pl.BlockSpec(memory_space=pl.ANY),
            scratch_shapes=[pltpu.VMEM((rows, d), F32), pltpu.VMEM((sub, d), F32),
                            pltpu.SemaphoreType.DMA(()), pltpu.SemaphoreType.DMA(())]),
        out_shape=jax.ShapeDtypeStruct((n_rows, d), F32),
        compiler_params=_cparams(1),
        name="moe_dispatch",
    )(slot1, slot2, zero_start, zero_blk, x, g.reshape(1, d))


def _expert_kernel(be_ref, ns_ref, xi_ref, x_ref, wg_ref, wu_ref, wd_ref, o_ref, *, sub, n_sub):
    i = pl.program_id(0)
    f = pl.program_id(1)
    ns = ns_ref[i]

    @pl.when(f == 0)
    def _():
        o_ref[...] = jnp.zeros_like(o_ref)

    def swiglu(rows):
        h = x_ref[rows, :]
        a = _mm(h, wg_ref[...])
        u = _mm(h, wu_ref[...])
        o_ref[rows, :] += _mm(_silu(a) * u, wd_ref[...])

    for s in range(0, n_sub, 2):
        if s + 2 <= n_sub:
            @pl.when(s + 2 <= ns)
            def _():
                swiglu(slice(s * sub, (s + 2) * sub))

        @pl.when(s + 1 == ns)
        def _():
            swiglu(slice(s * sub, (s + 1) * sub))


def _expert_ffn(blk_exp, n_valid_sub, x_blk, xs, wg, wu, wd, sb, sub, tf=512):
    p, d = xs.shape
    dff = wg.shape[2]
    n_super = n_valid_sub.shape[0]
    nf = dff // tf

    def fidx(i, f, ns):
        used = jnp.minimum(ns[i], 1)
        return f * used + (nf - 1) * (1 - used)

    return pl.pallas_call(
        functools.partial(_expert_kernel, sub=sub, n_sub=sb // sub),
        grid_spec=pltpu.PrefetchScalarGridSpec(
            num_scalar_prefetch=3,
            grid=(n_super, nf),
            in_specs=[pl.BlockSpec((sb, d), lambda i, f, be, ns, xi: (xi[i], 0)),
                      pl.BlockSpec((None, d, tf), lambda i, f, be, ns, xi: (be[i], 0, fidx(i, f, ns))),
                      pl.BlockSpec((None, d, tf), lambda i, f, be, ns, xi: (be[i], 0, fidx(i, f, ns))),
                      pl.BlockSpec((None, tf, d), lambda i, f, be, ns, xi: (be[i], fidx(i, f, ns), 0))],
            out_specs=pl.BlockSpec((sb, d), lambda i, f, be, ns, xi: (i, 0))),
        out_shape=jax.ShapeDtypeStruct((n_super * sb, d), F32),
        compiler_params=pltpu.CompilerParams(dimension_semantics=("arbitrary", "arbitrary"),
                                             vmem_limit_bytes=EXPERT_VMEM_LIMIT_BYTES),
        name="moe_experts",
    )(blk_exp, n_valid_sub, x_blk, xs, wg, wu, wd)


COMBINE_CHUNK = 64
COMBINE_TYPICAL_CHUNKS = 3


def _combine_kernel(st_ref, nc_ref, cm_ref, x_ref, info_ref, ys_hbm, gf_ref, o_ref, buf_ref, sem,
                    *, rows, final_norm, blk_off, max_chunks):
    ck = COMBINE_CHUNK
    b = (pl.program_id(0) + blk_off) * N_EXPERTS

    @pl.when(pl.program_id(0) == 0)
    def _():
        buf_ref[...] = jnp.zeros_like(buf_ref)

    def chunk_copy(e, c):
        src = ys_hbm.at[pl.ds(pl.multiple_of(st_ref[b + e] + c * ck, 8), ck)]
        return pltpu.make_async_copy(src, buf_ref.at[e, pl.ds(pl.multiple_of(c * ck, ck), ck)], sem)

    def start_chunks(e):
        def body(c, carry):
            chunk_copy(e, c).start()
            return carry
        lax.fori_loop(0, nc_ref[b + e], body, 0)

    def wait_chunks(e):
        def body(c, carry):
            chunk_copy(e, c).wait()
            return carry
        lax.fori_loop(0, nc_ref[b + e], body, 0)

    most = nc_ref[b]
    for e in range(N_EXPERTS):
        start_chunks(e)
        most = jnp.maximum(most, nc_ref[b + e])

    info = info_ref[...]
    e1, e2 = info[:, 0:1], info[:, 1:2]
    r1, r2 = info[:, 2:3], info[:, 3:4]
    g1, g2 = info[:, 4:5], info[:, 5:6]

    def finish(n_chunks):
        width = n_chunks * ck
        lane = lax.broadcasted_iota(jnp.int32, (rows, width), 1).astype(F32)
        ps = []
        for e in range(N_EXPERTS):
            origin = cm_ref[b + e].astype(F32)
            l1 = jnp.where(e1 == e, r1 - origin, -1.0)
            l2 = jnp.where(e2 == e, r2 - origin, -1.0)
            p = jnp.where(l1 == lane, g1, 0.0) + jnp.where(l2 == lane, g2, 0.0)
            ps.append(p.astype(BF16))
        for e in range(N_EXPERTS):
            wait_chunks(e)
        acc = x_ref[...]
        for e in range(N_EXPERTS):
            acc = acc + _mm(ps[e], buf_ref[e, 0:width, :])
        if final_norm:
            acc = _rms_rows(acc, gf_ref[...])
        o_ref[...] = acc

    @pl.when(most <= COMBINE_TYPICAL_CHUNKS)
    def _():
        finish(min(COMBINE_TYPICAL_CHUNKS, max_chunks))

    if max_chunks > COMBINE_TYPICAL_CHUNKS:
        @pl.when(most > COMBINE_TYPICAL_CHUNKS)
        def _():
            finish(max_chunks)


def _combine(first_row, n_chunks, origin, x, info, ys, gf, final_norm, row_off, n_rows, rows):
    d = x.shape[1]
    off = row_off // rows
    max_chunks = pl.cdiv(rows + 7, COMBINE_CHUNK)
    return pl.pallas_call(
        functools.partial(_combine_kernel, rows=rows, final_norm=final_norm, blk_off=off,
                          max_chunks=max_chunks),
        grid_spec=pltpu.PrefetchScalarGridSpec(
            num_scalar_prefetch=3,
            grid=(n_rows // rows,),
            in_specs=[pl.BlockSpec((rows, d), lambda i, a, b, c: (i + off, 0)),
                      pl.BlockSpec((rows, LANES), lambda i, a, b, c: (i + off, 0)),
                      pl.BlockSpec(memory_space=pl.ANY),
                      pl.BlockSpec((1, d), lambda i, a, b, c: (0, 0))],
            out_specs=pl.BlockSpec((rows, d), lambda i, a, b, c: (i, 0)),
            scratch_shapes=[pltpu.VMEM((N_EXPERTS, max_chunks * COMBINE_CHUNK, d), F32),
                            pltpu.SemaphoreType.DMA(())]),
        out_shape=jax.ShapeDtypeStruct((n_rows, d), F32),
        compiler_params=_cparams(1),
        name="moe_combine",
    )(first_row, n_chunks, origin, x, info, ys, gf.reshape(1, d))


def _moe_ffn(x, g, whi, wlo, wg, wu, wd, gf, final_norm, groups, sub=512):
    t, d = x.shape
    sb = 2048 if 2 * t >= 16 * 1024 else sub
    tb = t
    for off, n in groups:
        tb = math.gcd(tb, math.gcd(off, n))
    tb = _row_tile(tb, 512)
    info, cnt = _router(x, g, whi, wlo, tb)
    e = info[:, 0:2].astype(jnp.int32)
    rank = info[:, 2:4].astype(jnp.int32)
    after = cnt[::8, :N_EXPERTS].astype(jnp.int32)
    before = jnp.concatenate([jnp.zeros((1, N_EXPERTS), jnp.int32), after[:-1]], axis=0)
    counts = after[-1]
    n_sb = (counts + sb - 1) // sb
    sb_end = jnp.cumsum(n_sb)
    sb_start = sb_end - n_sb
    row_start = sb_start * sb
    eid = jnp.arange(N_EXPERTS, dtype=jnp.int32)
    slot = jnp.sum(jnp.where(e[:, :, None] == eid, row_start, 0), axis=-1) + rank
    n_super = (2 * t) // sb + N_EXPERTS + 1
    blk = jnp.arange(n_super, dtype=jnp.int32)
    n_used = sb_end[-1]
    used = blk < n_used
    blk_c = jnp.minimum(blk, n_used - 1)
    be = jnp.minimum(jnp.sum((blk_c[:, None] >= sb_end[None, :]).astype(jnp.int32), axis=-1),
                     N_EXPERTS - 1)
    valid = jnp.clip(counts[be] - (blk_c - sb_start[be]) * sb, 0, sb)
    n_valid_sub = jnp.where(used, (valid + sub - 1) // sub, 0).astype(jnp.int32)
    zero_start = ((row_start + counts) // 8 * 8).astype(jnp.int32)
    per = sb // sub
    sub_in_blk = jnp.arange(per, dtype=jnp.int32)
    zero_blk = (sub_in_blk[None, :] >= n_valid_sub[:, None]).astype(jnp.int32).reshape(-1)
    zero_blk = jnp.concatenate([zero_blk, jnp.ones((1,), jnp.int32)])
    xs = _dispatch(slot[:, 0], slot[:, 1], zero_start, zero_blk, x, g, sub)
    ys = _expert_ffn(be.astype(jnp.int32), n_valid_sub, blk_c.astype(jnp.int32), xs, wg, wu, wd, sb, sub)
    first = row_start[None, :] + before
    first8 = first // 8 * 8
    n_in_blk = after - before
    n_chunks = jnp.where(n_in_blk > 0, (n_in_blk + first - first8 + COMBINE_CHUNK - 1) // COMBINE_CHUNK, 0)
    origin = before - (first - first8)
    tables = [a.reshape(-1).astype(jnp.int32) for a in (first8, n_chunks, origin)]
    return [_combine(*tables, x, info, ys, gf, final_norm, off, n, tb) for off, n in groups]


def _final_norm_kernel(x_ref, g_ref, o_ref):
    o_ref[...] = _rms_rows(x_ref[...], g_ref[...])


def _final_norm(x, g, row_off, n_rows, tm=512):
    d = x.shape[1]
    t = n_rows
    tm = _row_tile(t, tm)
    off = row_off // tm
    return pl.pallas_call(
        _final_norm_kernel,
        grid=(t // tm,),
        in_specs=[pl.BlockSpec((tm, d), lambda i: (i + off, 0)), pl.BlockSpec((1, d), lambda i: (0, 0))],
        out_specs=pl.BlockSpec((tm, d), lambda i: (i, 0)),
        out_shape=jax.ShapeDtypeStruct((t, d), F32),
        compiler_params=_cparams(1),
        name="final_norm",
    )(x, g.reshape(1, d))


def _block_diag(w):
    n, c, d = w.shape
    eye = jnp.eye(n, dtype=w.dtype)
    return jnp.einsum("ncd,nm->ncmd", w, eye).reshape(n * c, n * d)


def _pad_cols(w, n):
    return jnp.pad(w, ((0, 0), (0, n - w.shape[1])))


def _layer_params(l, p):
    (norm_mix, w_in, lru_conv_w, lru_conv_b, lru_a_w, lru_a_b, lru_x_w, lru_x_b, lru_lam, gla_gk_w2,
     gla_gk_b, gla_norm, gdn_conv_w, gdn_a_log, gdn_dt_bias, gdn_norm, w_out, norm_xq, norm_mem,
     w_mq, w_mk, w_mv, w_mo, norm_ffn) = [a[l] for a in p]
    lw = LRU_W
    gk, gv = GLA_HEADS * GLA_DK, GLA_HEADS * GLA_DV
    dh = GDN_HEADS * GDN_DK
    offs = [0]
    for s in (lw, lw, gk, gk, gv, GLA_RANK, gv, dh, dh, dh, GDN_HEADS, GDN_HEADS, dh):
        offs.append(offs[-1] + s)
    col = lambda i: w_in[:, offs[i]:offs[i + 1]]
    w_lru = jnp.concatenate([col(0), col(1)], axis=1)
    w_gla = jnp.concatenate([col(2), col(3), col(4), col(6), _pad_cols(col(5), LANES)], axis=1)
    w_gdn = jnp.concatenate(
        [col(7), col(8), col(9), col(12), _pad_cols(jnp.concatenate([col(10), col(11)], axis=1), LANES)],
        axis=1)
    w_cat = jnp.concatenate([w_lru, w_gla, w_gdn], axis=1).astype(BF16)
    widths = (w_lru.shape[1], w_gla.shape[1], w_gdn.shape[1])
    alog = jnp.zeros((1, LANES), F32).at[0, GDN_HEADS:2 * GDN_HEADS].set(gdn_a_log)
    dtb = jnp.zeros((1, LANES), F32).at[0, GDN_HEADS:2 * GDN_HEADS].set(gdn_dt_bias)
    return dict(
        norm_mix=norm_mix, w_cat=w_cat, widths=widths,
        lru_conv_w=lru_conv_w, lru_conv_b=lru_conv_b,
        lru_a=_block_diag(lru_a_w).astype(BF16), lru_a_b=lru_a_b,
        lru_x=_block_diag(lru_x_w).astype(BF16), lru_x_b=lru_x_b, lru_lam=lru_lam,
        gla_w2=jnp.pad(gla_gk_w2, ((0, LANES - GLA_RANK), (0, 0))).astype(BF16), gla_gk_b=gla_gk_b,
        gla_norm=jnp.tile(gla_norm, GLA_HEADS),
        gdn_conv_w=gdn_conv_w, gdn_alog=alog, gdn_dtb=dtb, gdn_norm=jnp.tile(gdn_norm, GDN_HEADS),
        w_out=w_out.astype(BF16), norm_xq=norm_xq, norm_mem=norm_mem, w_mq=w_mq.astype(BF16),
        w_mkv=jnp.concatenate([w_mk, w_mv], axis=1).astype(BF16), w_mo=w_mo.astype(BF16),
        norm_ffn=norm_ffn)


def _mix_and_attend(x, x_off, grp, l, lp, joint, total_rows):
    batch, seq, n = grp["batch"], grp["seq"], grp["batch"] * grp["seq"]
    lru_h0, lru_buf0, gla_s0, gdn_s0, gdn_buf0 = grp["states"]
    sl = grp["state_layer"](l)
    z_lru, z_gla, z_gdn = _rms_matmul(x, lp["norm_mix"], lp["w_cat"], lp["widths"], "in_proj",
                                      row_off=x_off, n_rows=n)
    y_lru, lru_h, lru_buf = _lru_mixer(
        z_lru, lru_h0, lru_buf0, sl, lp["lru_conv_w"], lp["lru_conv_b"], lp["lru_a"], lp["lru_a_b"],
        lp["lru_x"], lp["lru_x_b"], lp["lru_lam"], batch, seq)
    y_gla, gla_s = _gla_mixer(z_gla.reshape(batch, seq, -1), gla_s0, sl, lp["gla_w2"],
                              lp["gla_gk_b"], lp["gla_norm"], batch, seq)
    y_gdn, gdn_s, gdn_buf = _gdn_mixer(z_gdn.reshape(batch, seq, -1), gdn_s0, gdn_buf0, sl,
                                       lp["gdn_conv_w"], lp["gdn_alog"], lp["gdn_dtb"],
                                       lp["gdn_norm"], batch, seq)
    xn, q = _outproj(x, y_lru, y_gla.reshape(n, -1), y_gdn.reshape(n, -1),
                     lp["w_out"], lp["norm_xq"], lp["w_mq"], row_off=x_off)
    mem_k, mem_v, mem_layer = grp["mem"][l]
    joint = _attention(xn, q, mem_k, mem_v, mem_layer, lp["w_mo"], batch, seq, joint, grp["row_off"],
                       total_rows)
    return joint, (lru_h.reshape(batch, LRU_W), lru_buf, gla_s, gdn_s, gdn_buf)


def _run_layers(groups, layers, ffn, norm_final):
    total_rows = sum(g["batch"] * g["seq"] for g in groups)
    spans = [(g["row_off"], g["batch"] * g["seq"]) for g in groups]
    xs = [(g["x"], 0) for g in groups]
    new_states = [[] for _ in groups]
    outs = None
    for l, lp in enumerate(layers):
        joint = None
        for gi, grp in enumerate(groups):
            joint, st = _mix_and_attend(xs[gi][0], xs[gi][1], grp, l, lp, joint, total_rows)
            new_states[gi].append(st)
        last = l == len(layers) - 1
        kind, fp = ffn[l]
        if kind == "dense":
            joint = _ffn_dense(joint, lp["norm_ffn"], *fp)
            if last:
                outs = [_final_norm(joint, norm_final, off, n) for off, n in spans]
        elif last:
            outs = _moe_ffn(joint, lp["norm_ffn"], *fp, norm_final, True, spans)
        else:
            joint = _moe_ffn(joint, lp["norm_ffn"], *fp, norm_final, False, [(0, total_rows)])[0]
        xs = [(joint, g["row_off"]) for g in groups]
    states = [[jnp.stack(s) for s in zip(*ns)] for ns in new_states]
    return outs, states


def kernel(x_prompt, x_sample, mem_prompt, state_lru_h, state_lru_conv, state_gla, state_gdn, state_gdn_conv, cache_mem_k, cache_mem_v, norm_mix, w_in, lru_conv_w, lru_conv_b, lru_a_w, lru_a_b, lru_x_w, lru_x_b, lru_lam, gla_gk_w2, gla_gk_b, gla_norm, gdn_conv_w, gdn_a_log, gdn_dt_bias, gdn_norm, w_out, norm_xq, norm_mem, w_mq, w_mk, w_mv, w_mo, norm_ffn, w_ff_gate, w_ff_up, w_ff_down, w_router, w_e_gate, w_e_up, w_e_down, norm_final):
    depth = norm_mix.shape[0]
    per_layer = (norm_mix, w_in, lru_conv_w, lru_conv_b, lru_a_w, lru_a_b, lru_x_w, lru_x_b, lru_lam,
                 gla_gk_w2, gla_gk_b, gla_norm, gdn_conv_w, gdn_a_log, gdn_dt_bias, gdn_norm, w_out,
                 norm_xq, norm_mem, w_mq, w_mk, w_mv, w_mo, norm_ffn)
    layers = [_layer_params(l, per_layer) for l in range(depth)]
    ffn = []
    for l in range(depth):
        j = l // 2
        if l % 2 == 0:
            ffn.append(("dense", (w_ff_gate[j], w_ff_up[j], w_ff_down[j])))
        else:
            wr = _pad_cols(w_router[j], LANES)
            whi = wr.astype(BF16)
            wlo = (wr - whi.astype(F32)).astype(BF16)
            ffn.append(("moe", (whi, wlo, w_e_gate[j], w_e_up[j], w_e_down[j])))

    bp, mlen, d = mem_prompt.shape
    mem2 = mem_prompt.reshape(bp * mlen, d)
    pk, pv = [], []
    for lp in layers:
        k2, v2 = _rms_matmul(mem2, lp["norm_mem"], lp["w_mkv"], (d, d), "mem_kv")
        pk.append(k2.reshape(1, bp, mlen, d))
        pv.append(v2.reshape(1, bp, mlen, d))
    zero_state = (jnp.zeros((1, bp, LRU_W), F32), jnp.zeros((1, bp, CONV_K - 1, LRU_W), F32),
                  jnp.zeros((1, bp, GLA_HEADS, GLA_DK, GLA_DV), F32),
                  jnp.zeros((1, bp, GDN_HEADS, GDN_DK, GDN_DV), F32),
                  jnp.zeros((1, bp, CONV_K - 1, 3 * GDN_HEADS * GDN_DK), F32))
    sp = x_prompt.shape[1]
    bs, ss = x_sample.shape[0], x_sample.shape[1]
    groups = [
        dict(x=x_prompt.reshape(bp * sp, d), batch=bp, seq=sp, row_off=0, states=zero_state,
             state_layer=lambda l: 0, mem=[(pk[l], pv[l], 0) for l in range(depth)]),
        dict(x=x_sample.reshape(bs * ss, d), batch=bs, seq=ss, row_off=bp * sp,
             states=(state_lru_h, state_lru_conv, state_gla, state_gdn, state_gdn_conv),
             state_layer=lambda l: l, mem=[(cache_mem_k, cache_mem_v, l) for l in range(depth)]),
    ]
    (y_p, y_s), (p_st, s_st) = _run_layers(groups, layers, ffn, norm_final)
    p_mem_k = jnp.concatenate(pk, axis=0).reshape(depth, bp, mlen, MEM_HEADS, MEM_HD)
    p_mem_v = jnp.concatenate(pv, axis=0).reshape(depth, bp, mlen, MEM_HEADS, MEM_HD)

    return (y_p.reshape(bp, sp, d), y_s.reshape(bs, ss, d), p_st[0], p_st[1], p_st[2], p_st[3], p_st[4],
            p_mem_k, p_mem_v, s_st[0], s_st[1], s_st[2], s_st[3], s_st[4])
```

```python
import functools
import math

import jax
import jax.numpy as jnp
from jax import lax
from jax.experimental import pallas as pl
from jax.experimental.pallas import tpu as pltpu

F32 = jnp.float32
BF16 = jnp.bfloat16
EPS = 1e-6

D_MODEL = 1024
LRU_W = 512
LRU_BLOCKS = 8
LRU_C = 8.0
CONV_K = 4
GLA_HEADS = 4
GLA_DK = 32
GLA_DV = 64
GLA_RANK = 16
GLA_TAU = 16.0
GLA_SUB = 16
GDN_HEADS = 4
GDN_DK = 64
GDN_DV = 64
MIX_CHUNK = 64
MEM_HEADS = 4
MEM_HD = 256
N_EXPERTS = 8
LANES = 128
VMEM_LIMIT_BYTES = 48 * 1024 * 1024
EXPERT_VMEM_LIMIT_BYTES = 56 * 1024 * 1024


def _cparams(n_axes):
    return pltpu.CompilerParams(dimension_semantics=("arbitrary",) * n_axes,
                                vmem_limit_bytes=VMEM_LIMIT_BYTES)


def _mm(a, b):
    return jnp.dot(a.astype(BF16), b.astype(BF16), preferred_element_type=F32)


def _mm_nt(a, b):
    return lax.dot_general(a.astype(BF16), b.astype(BF16), (((1,), (1,)), ((), ())),
                           preferred_element_type=F32)


def _mm_tn(a, b):
    return lax.dot_general(a.astype(BF16), b.astype(BF16), (((0,), (0,)), ((), ())),
                           preferred_element_type=F32)


def _rms_rows(x, g):
    ms = jnp.mean(x * x, axis=-1, keepdims=True)
    return (x * lax.rsqrt(ms + EPS)) * g


def _softplus(x):
    return jnp.maximum(x, 0.0) + jnp.log1p(jnp.exp(-jnp.abs(x)))


def _sigmoid(x):
    return 1.0 / (1.0 + jnp.exp(-x))


def _silu(x):
    return x * _sigmoid(x)


def _gelu_tanh(x):
    c = 0.7978845608028654
    return x * (0.5 * (1.0 + jnp.tanh(c * (x + 0.044715 * (x * x * x)))))


def _seg_cumsum_rows(x, seg):
    rows = x.shape[0]
    tpos = lax.broadcasted_iota(jnp.int32, (rows, 1), 0) & (seg - 1)
    d = 1
    while d < seg:
        x = x + jnp.where(tpos >= d, pltpu.roll(x, d, axis=0), 0.0)
        d *= 2
    return x


def _head_rms(o, gain, n_heads, width):
    lane_head = lax.broadcasted_iota(jnp.int32, (1, n_heads * width), 1) // width
    sq = o * o
    inv = jnp.zeros_like(o)
    for h in range(n_heads):
        m = lane_head == h
        ms = jnp.sum(jnp.where(m, sq, 0.0), axis=-1, keepdims=True) * (1.0 / width)
        inv = jnp.where(m, lax.rsqrt(ms + EPS), inv)
    return (o * inv) * gain


def _stack_heads(x, n_heads, width):
    c = x.shape[0]
    t = jnp.concatenate([x] * n_heads, axis=0)
    row_head = lax.broadcasted_iota(jnp.int32, (n_heads * c, 1), 0) // c
    lane_head = lax.broadcasted_iota(jnp.int32, (1, n_heads * width), 1) // width
    return jnp.where(row_head == lane_head, t, 0.0)


def _unstack_heads(x, n_heads):
    c = x.shape[0] // n_heads
    o = x[0:c]
    for h in range(1, n_heads):
        o = o + x[h * c:(h + 1) * c]
    return o


def _rms_matmul_kernel(x_ref, g_ref, w_ref, *o_refs):
    h = _rms_rows(x_ref[...], g_ref[...]).astype(BF16)
    start = 0
    for o_ref in o_refs:
        n = o_ref.shape[1]
        o_ref[...] = jnp.dot(h, w_ref[:, start:start + n], preferred_element_type=F32)
        start += n


def _row_tile(t, pref):
    tile = min(pref, t)
    while t % tile or tile % 8:
        tile -= 8
    return tile


def _rms_matmul(x, g, w, widths, name, tm=512, row_off=0, n_rows=None):
    d = x.shape[1]
    t = x.shape[0] if n_rows is None else n_rows
    n = w.shape[1]
    tm = _row_tile(t, tm)
    off = row_off // tm
    return pl.pallas_call(
        _rms_matmul_kernel,
        grid=(t // tm,),
        in_specs=[pl.BlockSpec((tm, d), lambda i: (i + off, 0)),
                  pl.BlockSpec((1, d), lambda i: (0, 0)),
                  pl.BlockSpec((d, n), lambda i: (0, 0))],
        out_specs=[pl.BlockSpec((tm, wd), lambda i: (i, 0)) for wd in widths],
        out_shape=[jax.ShapeDtypeStruct((t, wd), F32) for wd in widths],
        compiler_params=_cparams(1),
        name=name,
    )(x, g.reshape(1, d), w)


def _lru_kernel(z_ref, h0_ref, buf0_ref, cw_ref, cb_ref, aw_ref, ab_ref, xw_ref, xb_ref, lam_ref,
                y_ref, hout_ref, bufout_ref, xs_ref, hc_ref, *, bt, tl, nt):
    w = LRU_W
    rows = bt * tl
    j = pl.program_id(0) % nt

    @pl.when(j == 0)
    def _():
        xs_ref[:, 5:8, :] = buf0_ref[...]
        hc_ref[...] = h0_ref[...]

    xs_ref[:, 8:, :] = z_ref[:, :w].reshape(bt, tl, w)
    gate = z_ref[:, w:]
    cw = cw_ref[...]
    xc = cb_ref[...] + xs_ref[:, 5:5 + tl, :] * cw[0:1]
    for k in range(1, CONV_K):
        xc = xc + xs_ref[:, 5 + k:5 + k + tl, :] * cw[k:k + 1]
    tail = xs_ref[:, 5 + tl:8 + tl, :]
    xs_ref[:, 5:8, :] = tail
    bufout_ref[...] = tail

    xc = xc.reshape(rows, w)
    r = _sigmoid(_mm(xc, aw_ref[...]) + ab_ref[...])
    ig = _sigmoid(_mm(xc, xw_ref[...]) + xb_ref[...])
    log_a = (-LRU_C * r) * _softplus(-lam_ref[...])
    a = jnp.exp(log_a)
    th = jnp.tanh(log_a)
    b = jnp.sqrt((-2.0 * th) / (1.0 - th)) * (ig * xc)

    sub = 8
    gps = tl // sub
    a3 = a.reshape(rows // sub, sub, w)
    b3 = b.reshape(rows // sub, sub, w)
    spos = lax.broadcasted_iota(jnp.int32, (1, sub, 1), 1)
    d = 1
    while d < sub:
        m = spos >= d
        b3 = jnp.where(m, a3 * pltpu.roll(b3, d, axis=1) + b3, b3)
        a3 = jnp.where(m, a3 * pltpu.roll(a3, d, axis=1), a3)
        d *= 2
    a4 = a3.reshape(bt, gps, sub, w)
    b4 = b3.reshape(bt, gps, sub, w)
    carry = hc_ref[...]
    hs = []
    for r in range(gps):
        hr = b4[:, r] + a4[:, r] * carry
        hs.append(hr)
        carry = hr[:, sub - 1:sub, :]
    h = jnp.stack(hs, axis=1).reshape(rows, w)
    hlast = carry
    hc_ref[...] = hlast
    hout_ref[...] = hlast
    y_ref[...] = h * _gelu_tanh(gate)


def _lru_mixer(z, h0, buf0, layer, cw, cb, aw, ab, xw, xb, lam, batch, seq):
    w = LRU_W
    tl = min(seq, 256)
    bt = min(batch, max(1, 256 // seq))
    nt = seq // tl
    rows = bt * tl
    grid = (batch * seq // rows,)
    if nt > 1:
        sidx = lambda i: (i // nt, 0, 0)
        lidx = lambda i: (layer, i // nt, 0, 0)
    else:
        sidx = lambda i: (i, 0, 0)
        lidx = lambda i: (layer, i, 0, 0)
    full2 = lambda i: (0, 0)
    return pl.pallas_call(
        functools.partial(_lru_kernel, bt=bt, tl=tl, nt=nt),
        grid=grid,
        in_specs=[pl.BlockSpec((rows, 2 * w), lambda i: (i, 0)),
                  pl.BlockSpec((None, bt, 1, w), lidx),
                  pl.BlockSpec((None, bt, CONV_K - 1, w), lidx),
                  pl.BlockSpec((CONV_K, w), full2),
                  pl.BlockSpec((1, w), full2),
                  pl.BlockSpec((w, w), full2),
                  pl.BlockSpec((1, w), full2),
                  pl.BlockSpec((w, w), full2),
                  pl.BlockSpec((1, w), full2),
                  pl.BlockSpec((1, w), full2)],
        out_specs=[pl.BlockSpec((rows, w), lambda i: (i, 0)),
                   pl.BlockSpec((bt, 1, w), sidx),
                   pl.BlockSpec((bt, CONV_K - 1, w), sidx)],
        out_shape=[jax.ShapeDtypeStruct((batch * seq, w), F32),
                   jax.ShapeDtypeStruct((batch, 1, w), F32),
                   jax.ShapeDtypeStruct((batch, CONV_K - 1, w), F32)],
        scratch_shapes=[pltpu.VMEM((bt, 8 + tl, w), F32), pltpu.VMEM((bt, 1, w), F32)],
        compiler_params=_cparams(1),
        name="lru_mixer",
    )(z, h0.reshape(h0.shape[0], batch, 1, w), buf0, cw, cb.reshape(1, w), aw, ab.reshape(1, w), xw,
      xb.reshape(1, w), lam.reshape(1, w))


def _gla_kernel(z_ref, s0_ref, w2_ref, gb_ref, gn_ref, y_ref, sout_ref, s_ref, *, c, sc, g):
    nh = GLA_HEADS
    kw = nh * GLA_DK
    vw = nh * GLA_DV

    @pl.when(pl.program_id(1) == 0)
    def _():
        s_ref[...] = jnp.zeros_like(s_ref)
        for b in range(g):
            for h in range(nh):
                s_ref[b, h * GLA_DK:(h + 1) * GLA_DK, h * GLA_DV:(h + 1) * GLA_DV] = s0_ref[b, h]

    bs = range(g)
    q = [z_ref[b, :, 0:kw] * (GLA_DK ** -0.5) for b in bs]
    k = [z_ref[b, :, kw:2 * kw] for b in bs]
    v = [z_ref[b, :, 2 * kw:2 * kw + vw] for b in bs]
    gk = [-_softplus(-(_mm(z_ref[b, :, 2 * kw + 2 * vw:], w2_ref[...]) + gb_ref[...])) / GLA_TAU
          for b in bs]
    gcum = [_seg_cumsum_rows(gk[b], sc) for b in bs]
    qp = [q[b] * jnp.exp(gcum[b]) for b in bs]

    ri = lax.broadcasted_iota(jnp.int32, (nh * c, nh * c), 0)
    ci = lax.broadcasted_iota(jnp.int32, (nh * c, nh * c), 1)
    keep = (ri // sc == ci // sc) & (ci <= ri)
    a = [_mm_nt(_stack_heads(qp[b], nh, GLA_DK), _stack_heads(k[b] * jnp.exp(-gcum[b]), nh, GLA_DK))
         for b in bs]
    o = [_unstack_heads(_mm(jnp.where(keep, a[b], 0.0), _stack_heads(v[b], nh, GLA_DV)), nh)
         for b in bs]

    s = [s_ref[b] for b in bs]
    eye = (lax.broadcasted_iota(jnp.int32, (kw, kw), 0) ==
           lax.broadcasted_iota(jnp.int32, (kw, kw), 1))
    bd = (lax.broadcasted_iota(jnp.int32, (kw, vw), 0) // GLA_DK ==
          lax.broadcasted_iota(jnp.int32, (kw, vw), 1) // GLA_DV)
    o_inter = [[] for _ in bs]
    for i in range(c // sc):
        lo, hi = i * sc, (i + 1) * sc
        for b in bs:
            o_inter[b].append(_mm(qp[b][lo:hi], s[b]))
            glast = gcum[b][hi - 1:hi]
            kpp = k[b][lo:hi] * jnp.exp(glast - gcum[b][lo:hi])
            u = _mm_tn(kpp, v[b][lo:hi])
            dcol = jnp.sum(jnp.where(eye, jnp.exp(glast), 0.0), axis=1, keepdims=True)
            s[b] = dcol * s[b] + jnp.where(bd, u, 0.0)
    for b in bs:
        s_ref[b] = s[b]
        for h in range(nh):
            sout_ref[b, h] = s[b][h * GLA_DK:(h + 1) * GLA_DK, h * GLA_DV:(h + 1) * GLA_DV]
        ob = o[b] + jnp.concatenate(o_inter[b], axis=0)
        gate = z_ref[b, :, 2 * kw + vw:2 * kw + 2 * vw]
        y_ref[b] = _head_rms(ob, gn_ref[...], nh, GLA_DV) * _silu(gate)


def _gla_mixer(z, s0, layer, w2, gb, gn, batch, seq):
    c = min(seq, MIX_CHUNK)
    sc = min(c, GLA_SUB)
    nt = seq // c
    g = _mixer_group(batch, c)
    kw, vw = GLA_HEADS * GLA_DK, GLA_HEADS * GLA_DV
    zw = z.shape[2]
    full2 = lambda i, j: (0, 0)
    sblk = (g, GLA_HEADS, GLA_DK, GLA_DV)
    return pl.pallas_call(
        functools.partial(_gla_kernel, c=c, sc=sc, g=g),
        grid=(batch // g, nt),
        in_specs=[pl.BlockSpec((g, c, zw), lambda i, j: (i, j, 0)),
                  pl.BlockSpec((None,) + sblk, lambda i, j: (layer, i, 0, 0, 0)),
                  pl.BlockSpec((LANES, kw), full2),
                  pl.BlockSpec((1, kw), full2),
                  pl.BlockSpec((1, vw), full2)],
        out_specs=[pl.BlockSpec((g, c, vw), lambda i, j: (i, j, 0)),
                   pl.BlockSpec(sblk, lambda i, j: (i, 0, 0, 0))],
        out_shape=[jax.ShapeDtypeStruct((batch, seq, vw), F32),
                   jax.ShapeDtypeStruct((batch, GLA_HEADS, GLA_DK, GLA_DV), F32)],
        scratch_shapes=[pltpu.VMEM((g, kw, vw), F32)],
        compiler_params=_cparams(2),
        name="gla_mixer",
    )(z, s0, w2, gb.reshape(1, kw), gn.reshape(1, vw))


def _gdn_prep(z_ref, cw_ref, alog_ref, dtb_ref, bufout_ref, xs_ref, *, c):
    nh = GDN_HEADS
    hw = nh * GDN_DK
    cw3 = 3 * hw

    xs_ref[8:, :] = z_ref[:, 0:cw3]
    cw = cw_ref[...]
    qkv = xs_ref[5:5 + c, :] * cw[0:1]
    for kk in range(1, CONV_K):
        qkv = qkv + xs_ref[5 + kk:5 + kk + c, :] * cw[kk:kk + 1]
    tail = xs_ref[5 + c:8 + c, :]
    xs_ref[5:8, :] = tail
    bufout_ref[...] = tail
    qkv = _silu(qkv)
    zg = z_ref[:, cw3:cw3 + hw]
    sm = z_ref[:, cw3 + hw:]

    lane_head = lax.broadcasted_iota(jnp.int32, (1, hw), 1) // GDN_DK

    def l2n(x):
        sq = x * x
        inv = jnp.zeros_like(x)
        for h in range(nh):
            m = lane_head == h
            ss = jnp.sum(jnp.where(m, sq, 0.0), axis=-1, keepdims=True)
            inv = jnp.where(m, lax.rsqrt(ss + EPS), inv)
        return x * inv

    q = l2n(qkv[:, 0:hw]) * (GDN_DK ** -0.5)
    k = l2n(qkv[:, hw:2 * hw])
    v = qkv[:, 2 * hw:3 * hw]
    beta = _sigmoid(sm)
    glog = -jnp.exp(alog_ref[...]) * _softplus(sm + dtb_ref[...])
    gcum = _seg_cumsum_rows(glog, c)

    n = nh * c
    bcol = jnp.concatenate([beta[:, h:h + 1] for h in range(nh)], axis=0)
    gcol = jnp.concatenate([gcum[:, nh + h:nh + h + 1] for h in range(nh)], axis=0)
    glast = jnp.concatenate(
        [jnp.broadcast_to(gcum[c - 1:c, nh + h:nh + h + 1], (c, 1)) for h in range(nh)], axis=0)
    ri = lax.broadcasted_iota(jnp.int32, (n, n), 0)
    ci = lax.broadcasted_iota(jnp.int32, (n, n), 1)
    grow = jnp.sum(jnp.where(ri == ci, gcol, 0.0), axis=0, keepdims=True)
    same = ri // c == ci // c
    incl = same & (ci <= ri)
    strict = same & (ci < ri)
    dec = jnp.where(incl, jnp.exp(jnp.where(incl, gcol - grow, 0.0)), 0.0)

    sdec = jnp.concatenate(
        [jnp.broadcast_to(jnp.exp(gcum[c - 1:c, nh + h:nh + h + 1]), (GDN_DK, 1)) for h in range(nh)],
        axis=0)
    return dict(ks=_stack_heads(k, nh, GDN_DK), qs=_stack_heads(q, nh, GDN_DK),
                vs=_stack_heads(v, nh, GDN_DV), bcol=bcol, egc=jnp.exp(gcol), dec=dec,
                kdec=jnp.exp(glast - gcol), sdec=sdec, zg=zg)


def _gdn_kernel(z_ref, s0_ref, buf0_ref, cw_ref, alog_ref, dtb_ref, gn_ref,
                y_ref, sout_ref, bufout_ref, xs_ref, s_ref, *, c, g):
    nh = GDN_HEADS
    hw = nh * GDN_DK
    n = nh * c

    @pl.when(pl.program_id(1) == 0)
    def _():
        xs_ref[:, 5:8, :] = buf0_ref[...]
        s_ref[...] = jnp.zeros_like(s_ref)
        for b in range(g):
            for h in range(nh):
                s_ref[b, h * GDN_DK:(h + 1) * GDN_DK, h * GDN_DV:(h + 1) * GDN_DV] = s0_ref[b, h]

    bs = range(g)
    pr = [_gdn_prep(z_ref.at[b], cw_ref, alog_ref, dtb_ref, bufout_ref.at[b], xs_ref.at[b], c=c)
          for b in bs]
    ri = lax.broadcasted_iota(jnp.int32, (n, n), 0)
    ci = lax.broadcasted_iota(jnp.int32, (n, n), 1)
    same = ri // c == ci // c
    incl = same & (ci <= ri)
    strict = same & (ci < ri)
    kq = [_mm_nt(jnp.concatenate([pr[b]["ks"], pr[b]["qs"]], axis=0), pr[b]["ks"]) for b in bs]

    p = [jnp.where(strict, -(pr[b]["bcol"] * kq[b][0:n]) * pr[b]["dec"], 0.0) for b in bs]
    tinv = [jnp.where(ri == ci, 1.0, 0.0) + p[b] for b in bs]
    span = 2
    while span < c:
        p = [_mm(p[b], p[b]) for b in bs]
        tinv = [tinv[b] + _mm(tinv[b], p[b]) for b in bs]
        span *= 2

    s = [s_ref[b] for b in bs]
    uw = [_mm(tinv[b], jnp.concatenate([pr[b]["vs"] * pr[b]["bcol"],
                                        pr[b]["ks"] * (pr[b]["bcol"] * pr[b]["egc"])], axis=1))
          for b in bs]
    qw = [_mm(jnp.concatenate([pr[b]["qs"] * pr[b]["egc"], uw[b][:, hw:]], axis=0), s[b])
          for b in bs]
    vnew = [uw[b][:, 0:hw] - qw[b][n:] for b in bs]
    av = [_mm(jnp.where(incl, kq[b][n:] * pr[b]["dec"], 0.0), vnew[b]) for b in bs]
    kv = [_mm_tn(pr[b]["ks"] * pr[b]["kdec"], vnew[b]) for b in bs]
    for b in bs:
        o = _unstack_heads(qw[b][0:n] + av[b], nh)
        sn = pr[b]["sdec"] * s[b] + kv[b]
        s_ref[b] = sn
        for h in range(nh):
            sout_ref[b, h] = sn[h * GDN_DK:(h + 1) * GDN_DK, h * GDN_DV:(h + 1) * GDN_DV]
        y_ref[b] = _head_rms(o, gn_ref[...], nh, GDN_DV) * _silu(pr[b]["zg"])


def _mixer_group(batch, c):
    return min(batch, max(4, 128 // c))


def _gdn_mixer(z, s0, buf0, layer, cw, alog, dtb, gn, batch, seq):
    c = min(seq, MIX_CHUNK)
    nt = seq // c
    g = _mixer_group(batch, c)
    hw = GDN_HEADS * GDN_DK
    zw = z.shape[2]
    full2 = lambda i, j: (0, 0)
    sblk = (g, GDN_HEADS, GDN_DK, GDN_DV)
    return pl.pallas_call(
        functools.partial(_gdn_kernel, c=c, g=g),
        grid=(batch // g, nt),
        in_specs=[pl.BlockSpec((g, c, zw), lambda i, j: (i, j, 0)),
                  pl.BlockSpec((None,) + sblk, lambda i, j: (layer, i, 0, 0, 0)),
                  pl.BlockSpec((None, g, CONV_K - 1, 3 * hw), lambda i, j: (layer, i, 0, 0)),
                  pl.BlockSpec((CONV_K, 3 * hw), full2),
                  pl.BlockSpec((1, LANES), full2),
                  pl.BlockSpec((1, LANES), full2),
                  pl.BlockSpec((1, hw), full2)],
        out_specs=[pl.BlockSpec((g, c, hw), lambda i, j: (i, j, 0)),
                   pl.BlockSpec(sblk, lambda i, j: (i, 0, 0, 0)),
                   pl.BlockSpec((g, CONV_K - 1, 3 * hw), lambda i, j: (i, 0, 0))],
        out_shape=[jax.ShapeDtypeStruct((batch, seq, hw), F32),
                   jax.ShapeDtypeStruct((batch, GDN_HEADS, GDN_DK, GDN_DV), F32),
                   jax.ShapeDtypeStruct((batch, CONV_K - 1, 3 * hw), F32)],
        scratch_shapes=[pltpu.VMEM((g, 8 + c, 3 * hw), F32), pltpu.VMEM((g, hw, hw), F32)],
        compiler_params=_cparams(2),
        name="gdn_mixer",
    )(z, s0, buf0, cw, alog, dtb, gn.reshape(1, hw))


def _outproj_kernel(x_ref, yl_ref, yg_ref, yd_ref, wo_ref, gq_ref, wq_ref, xn_ref, q_ref):
    lw = LRU_W
    gw = GLA_HEADS * GLA_DV
    y = _mm(yl_ref[...], wo_ref[0:lw, :])
    y = y + _mm(yg_ref[...], wo_ref[lw:lw + gw, :])
    y = y + _mm(yd_ref[...], wo_ref[lw + gw:, :])
    xn = x_ref[...] + y
    xn_ref[...] = xn
    q_ref[...] = _mm(_rms_rows(xn, gq_ref[...]), wq_ref[...])


def _outproj(x, yl, yg, yd, wo, gq, wq, tm=512, row_off=0):
    d = x.shape[1]
    t = yl.shape[0]
    tm = _row_tile(t, tm)
    off = row_off // tm
    row = lambda i: (i, 0)
    full2 = lambda i: (0, 0)
    return pl.pallas_call(
        _outproj_kernel,
        grid=(t // tm,),
        in_specs=[pl.BlockSpec((tm, d), lambda i: (i + off, 0)),
                  pl.BlockSpec((tm, yl.shape[1]), row),
                  pl.BlockSpec((tm, yg.shape[1]), row),
                  pl.BlockSpec((tm, yd.shape[1]), row),
                  pl.BlockSpec((d, d), full2),
                  pl.BlockSpec((1, d), full2),
                  pl.BlockSpec((d, d), full2)],
        out_specs=[pl.BlockSpec((tm, d), row), pl.BlockSpec((tm, d), row)],
        out_shape=[jax.ShapeDtypeStruct((t, d), F32), jax.ShapeDtypeStruct((t, d), F32)],
        compiler_params=_cparams(1),
        name="outproj_qproj",
    )(x, yl, yg, yd, wo, gq.reshape(1, d), wq)


def _attn_kernel(x_ref, q_ref, k_ref, v_ref, wo_ref, o_ref):
    hd = MEM_HD
    acc = x_ref[...]
    for h in range(MEM_HEADS):
        sl = slice(h * hd, (h + 1) * hd)
        s = _mm_nt(q_ref[:, sl], k_ref[:, sl]) * (hd ** -0.5)
        m = jnp.max(s, axis=-1, keepdims=True)
        p = jnp.exp(s - m)
        l = jnp.sum(p, axis=-1, keepdims=True)
        oh = _mm(p, v_ref[:, sl]) / l
        acc = acc + _mm(oh, wo_ref[sl, :])
    o_ref[...] = acc


def _attn_heads_kernel(q_ref, k_ref, v_ref, o_ref, *, gb, tl):
    nh, hd = MEM_HEADS, MEM_HD
    m = k_ref.shape[1]
    row_head = lax.broadcasted_iota(jnp.int32, (nh * tl, 1), 0) // tl
    col_head = lax.broadcasted_iota(jnp.int32, (1, m * nh), 1) % nh
    for b in range(gb):
        kf = k_ref[b].reshape(m * nh, hd)
        vf = v_ref[b].reshape(m * nh, hd)
        rows = slice(b * tl, (b + 1) * tl)
        qs = jnp.concatenate([q_ref[rows, h * hd:(h + 1) * hd] for h in range(nh)], axis=0)
        s = _mm_nt(qs, kf) * (hd ** -0.5)
        s = jnp.where(row_head == col_head, s, -jnp.inf)
        mx = jnp.max(s, axis=-1, keepdims=True)
        p = jnp.exp(s - mx)
        l = jnp.sum(p, axis=-1, keepdims=True)
        o = _mm(p, vf) / l
        for h in range(nh):
            o_ref[rows, h * hd:(h + 1) * hd] = o[h * tl:(h + 1) * tl]


def _oproj_kernel(x_ref, a_ref, wo_ref, *rest):
    o_ref = rest[-1]
    o_ref[...] = x_ref[...] + _mm(a_ref[...], wo_ref[...])


def _attn_first_kernel(x_ref, q_ref, k_ref, v_ref, wo_ref, o_ref, *, n_real):
    i = pl.program_id(0)

    @pl.when(i < n_real)
    def _():
        _attn_kernel(x_ref, q_ref, k_ref, v_ref, wo_ref, o_ref)

    @pl.when(i >= n_real)
    def _():
        o_ref[...] = jnp.zeros_like(o_ref)


def _attn_into_kernel(x_ref, q_ref, k_ref, v_ref, wo_ref, joint_ref, o_ref):
    _attn_kernel(x_ref, q_ref, k_ref, v_ref, wo_ref, o_ref)


def _attention(x, q, k, v, layer, wo, batch, seq, joint, row_off, total_rows):
    t, d = x.shape
    out_shape = jax.ShapeDtypeStruct((total_rows, d), F32)
    extra_specs = [] if joint is None else [pl.BlockSpec(memory_space=pl.ANY)]
    extra_args = () if joint is None else (joint,)
    if k.ndim == 4:
        tl = min(seq, 512)
        nl = seq // tl
        off = row_off // tl
        kblk = (None, None) + k.shape[2:]
        n_real = batch * nl
        if joint is None:
            assert row_off == 0
            n_steps = pl.cdiv(total_rows, tl)
            body = functools.partial(_attn_first_kernel, n_real=n_real)
        else:
            n_steps = n_real
            body = _attn_into_kernel
        kidx = lambda i: (layer, jnp.minimum(i // nl, batch - 1), 0, 0)
        row = lambda i: (jnp.minimum(i, n_real - 1), 0)
        return pl.pallas_call(
            body,
            grid=(n_steps,),
            in_specs=[pl.BlockSpec((tl, d), row),
                      pl.BlockSpec((tl, d), row),
                      pl.BlockSpec(kblk, kidx),
                      pl.BlockSpec(kblk, kidx),
                      pl.BlockSpec((d, d), lambda i: (0, 0))] + extra_specs,
            out_specs=pl.BlockSpec((tl, d), lambda i: (i + off, 0)),
            out_shape=out_shape,
            input_output_aliases={} if joint is None else {5: 0},
            compiler_params=_cparams(1),
            name="mem_attention",
        )(x, q, k, v, wo, *extra_args)
    gb = 2
    kblk = (None, gb) + k.shape[2:]
    kidx = lambda i: (layer, i, 0, 0, 0)
    att = pl.pallas_call(
        functools.partial(_attn_heads_kernel, gb=gb, tl=seq),
        grid=(batch // gb,),
        in_specs=[pl.BlockSpec((gb * seq, d), lambda i: (i, 0)),
                  pl.BlockSpec(kblk, kidx),
                  pl.BlockSpec(kblk, kidx)],
        out_specs=pl.BlockSpec((gb * seq, d), lambda i: (i, 0)),
        out_shape=jax.ShapeDtypeStruct((t, d), F32),
        compiler_params=_cparams(1),
        name="mem_attention_heads",
    )(q, k, v)
    tm = _row_tile(t, 512)
    off = row_off // tm
    return pl.pallas_call(
        _oproj_kernel,
        grid=(t // tm,),
        in_specs=[pl.BlockSpec((tm, d), lambda i: (i, 0)),
                  pl.BlockSpec((tm, d), lambda i: (i, 0)),
                  pl.BlockSpec((d, d), lambda i: (0, 0))] + extra_specs,
        out_specs=pl.BlockSpec((tm, d), lambda i: (i + off, 0)),
        out_shape=out_shape,
        input_output_aliases={} if joint is None else {3: 0},
        compiler_params=_cparams(1),
        name="mem_oproj",
    )(x, att, wo, *extra_args)


def _ffn_kernel(x_ref, g_ref, wg_ref, wu_ref, wd_ref, o_ref, h_ref, acc_ref):
    f = pl.program_id(1)

    @pl.when(f == 0)
    def _():
        h_ref[...] = _rms_rows(x_ref[...], g_ref[...]).astype(BF16)
        acc_ref[...] = jnp.zeros_like(acc_ref)

    h = h_ref[...]
    a = _mm(h, wg_ref[...])
    u = _mm(h, wu_ref[...])
    acc_ref[...] += _mm(_silu(a) * u, wd_ref[...])

    @pl.when(f == pl.num_programs(1) - 1)
    def _():
        o_ref[...] = x_ref[...] + acc_ref[...]


def _ffn_dense(x, g, wg, wu, wd, tm=1024, tf=512):
    t, d = x.shape
    dff = wg.shape[1]
    tm = _row_tile(t, tm)
    return pl.pallas_call(
        _ffn_kernel,
        grid=(t // tm, dff // tf),
        in_specs=[pl.BlockSpec((tm, d), lambda i, f: (i, 0)),
                  pl.BlockSpec((1, d), lambda i, f: (0, 0)),
                  pl.BlockSpec((d, tf), lambda i, f: (0, f)),
                  pl.BlockSpec((d, tf), lambda i, f: (0, f)),
                  pl.BlockSpec((tf, d), lambda i, f: (f, 0))],
        out_specs=pl.BlockSpec((tm, d), lambda i, f: (i, 0)),
        out_shape=jax.ShapeDtypeStruct((t, d), F32),
        scratch_shapes=[pltpu.VMEM((tm, d), BF16), pltpu.VMEM((tm, d), F32)],
        compiler_params=_cparams(2),
        name="ffn_dense",
    )(x, g.reshape(1, d), wg, wu, wd)


def _router_kernel(x_ref, g_ref, whi_ref, wlo_ref, info_ref, cnt_ref, carry_ref, *, tm):
    i = pl.program_id(0)

    @pl.when(i == 0)
    def _():
        carry_ref[...] = jnp.zeros_like(carry_ref)

    h = _rms_rows(x_ref[...], g_ref[...])
    hhi = h.astype(BF16)
    hlo = (h - hhi.astype(F32)).astype(BF16)
    whi = whi_ref[...]
    logits = (jnp.dot(hhi, whi, preferred_element_type=F32)
              + jnp.dot(hlo, whi, preferred_element_type=F32)
              + jnp.dot(hhi, wlo_ref[...], preferred_element_type=F32))
    lane = lax.broadcasted_iota(jnp.int32, (tm, LANES), 1)
    neg = jnp.float32(-jnp.inf)
    logits = jnp.where(lane < N_EXPERTS, logits, neg)
    m1 = jnp.max(logits, axis=-1, keepdims=True)
    i1 = jnp.min(jnp.where(logits == m1, lane, LANES), axis=-1, keepdims=True)
    rest = jnp.where(lane == i1, neg, logits)
    m2 = jnp.max(rest, axis=-1, keepdims=True)
    i2 = jnp.min(jnp.where(rest == m2, lane, LANES), axis=-1, keepdims=True)
    e = jnp.exp(m2 - m1)
    g1 = 1.0 / (1.0 + e)
    g2 = e / (1.0 + e)
    oh1 = jnp.where(lane == i1, 1.0, 0.0)
    oh2 = jnp.where(lane == i2, 1.0, 0.0)
    oh = oh1 + oh2
    ri = lax.broadcasted_iota(jnp.int32, (tm, tm), 0)
    ci = lax.broadcasted_iota(jnp.int32, (tm, tm), 1)
    tri = jnp.where(ci < ri, 1.0, 0.0)
    before = _mm(tri, oh) + carry_ref[0:1, :]
    r1 = jnp.sum(oh1 * before, axis=-1, keepdims=True)
    r2 = jnp.sum(oh2 * before, axis=-1, keepdims=True)
    carry = carry_ref[0:1, :] + jnp.sum(oh, axis=0, keepdims=True)
    carry_ref[...] = jnp.broadcast_to(carry, carry_ref.shape)
    cnt_ref[...] = jnp.broadcast_to(carry, cnt_ref.shape)
    info = jnp.where(lane == 0, i1.astype(F32), 0.0)
    info = jnp.where(lane == 1, i2.astype(F32), info)
    info = jnp.where(lane == 2, r1, info)
    info = jnp.where(lane == 3, r2, info)
    info = jnp.where(lane == 4, g1, info)
    info = jnp.where(lane == 5, g2, info)
    info_ref[...] = info


def _router(x, g, whi, wlo, tm):
    t, d = x.shape
    return pl.pallas_call(
        functools.partial(_router_kernel, tm=tm),
        grid=(t // tm,),
        in_specs=[pl.BlockSpec((tm, d), lambda i: (i, 0)),
                  pl.BlockSpec((1, d), lambda i: (0, 0)),
                  pl.BlockSpec((d, LANES), lambda i: (0, 0)),
                  pl.BlockSpec((d, LANES), lambda i: (0, 0))],
        out_specs=[pl.BlockSpec((tm, LANES), lambda i: (i, 0)),
                   pl.BlockSpec((8, LANES), lambda i: (i, 0))],
        out_shape=[jax.ShapeDtypeStruct((t, LANES), F32),
                   jax.ShapeDtypeStruct((t // tm * 8, LANES), F32)],
        scratch_shapes=[pltpu.VMEM((8, LANES), F32)],
        compiler_params=_cparams(1),
        name="moe_router",
    )(x, g.reshape(1, d), whi, wlo)


MOE_CHUNK = 32


def _dispatch_kernel(st_ref, nc_ref, pos_ref, zs_ref, zf_ref, x_ref, g_ref, ct_ref, xs_hbm, w_ref, zbuf_ref,
                     zsem, sem, *, rows, sub, n_blk, win):
    i = pl.program_id(0)

    @pl.when(i == 0)
    def _():
        zbuf_ref[...] = jnp.zeros_like(zbuf_ref)
        for e in range(N_EXPERTS):
            dst = xs_hbm.at[pl.ds(pl.multiple_of(zs_ref[e], 8), sub)]
            pltpu.make_async_copy(zbuf_ref, dst, zsem).start()
            pltpu.make_async_copy(zbuf_ref, dst, zsem).wait()

        def fill(j, carry):
            @pl.when(zf_ref[j] == 1)
            def _():
                dst = xs_hbm.at[pl.ds(pl.multiple_of(j * sub, sub), sub)]
                pltpu.make_async_copy(zbuf_ref, dst, zsem).start()
            return carry

        def drain(j, carry):
            @pl.when(zf_ref[j] == 1)
            def _():
                dst = xs_hbm.at[pl.ds(pl.multiple_of(j * sub, sub), sub)]
                pltpu.make_async_copy(zbuf_ref, dst, zsem).wait()
            return carry

        lax.fori_loop(0, n_blk, fill, 0)
        lax.fori_loop(0, n_blk, drain, 0)

    ck = MOE_CHUNK
    n_steps = pl.num_programs(0)

    def chunk_copy(step, e, c):
        t = step * N_EXPERTS + e
        src = w_ref.at[step % 2, pl.ds(pl.multiple_of(pos_ref[t] + c * ck, ck), ck)]
        dst = xs_hbm.at[pl.ds(pl.multiple_of(st_ref[t] + c * ck, 8), ck)]
        return pltpu.make_async_copy(src, dst, sem)

    def all_chunks(step, fn):
        for e in range(N_EXPERTS):
            def body(c, carry, e=e):
                fn(chunk_copy(step, e, c))
                return carry
            lax.fori_loop(0, nc_ref[step * N_EXPERTS + e], body, 0)

    h = _rms_rows(x_ref[...], g_ref[...])
    wrow = lax.broadcasted_iota(jnp.int32, (win, rows), 0)
    cols = ct_ref[...]
    onehot = jnp.where(wrow == cols[0:1, :], 1.0, 0.0) + jnp.where(wrow == cols[1:2, :], 1.0, 0.0)
    w_ref[i % 2] = _mm(onehot, h)

    @pl.when(i >= 1)
    def _():
        all_chunks(i - 1, lambda cp: cp.wait())

    all_chunks(i, lambda cp: cp.start())

    @pl.when(i == n_steps - 1)
    def _():
        all_chunks(i, lambda cp: cp.wait())


def _dispatch(seg_start, n_chunks, win_pos, zero_start, zero_blk, cols_t, x, g, sub, rows, win):
    t, d = x.shape
    n_blk = zero_blk.shape[0]
    n_rows = n_blk * sub
    return pl.pallas_call(
        functools.partial(_dispatch_kernel, rows=rows, sub=sub, n_blk=n_blk, win=win),
        grid_spec=pltpu.PrefetchScalarGridSpec(
            num_scalar_prefetch=5,
            grid=(t // rows,),
            in_specs=[pl.BlockSpec((rows, d), lambda i, *_: (i, 0)),
                      pl.BlockSpec((1, d), lambda i, *_: (0, 0)),
                      pl.BlockSpec((2, rows), lambda i, *_: (0, i))],
            out_specs=pl.BlockSpec(memory_space=pl.ANY),
            scratch_shapes=[pltpu.VMEM((2, win, d), F32), pltpu.VMEM((sub, d), F32),
                            pltpu.SemaphoreType.DMA(()), pltpu.SemaphoreType.DMA(())]),
        out_shape=jax.ShapeDtypeStruct((n_rows, d), F32),
        compiler_params=_cparams(1),
        name="moe_dispatch",
    )(seg_start, n_chunks, win_pos, zero_start, zero_blk, x, g.reshape(1, d), cols_t)


def _expert_kernel(be_ref, ns_ref, xi_ref, x_ref, wg_ref, wu_ref, wd_ref, o_ref, *, sub, n_sub):
    i = pl.program_id(0)
    f = pl.program_id(1)
    ns = ns_ref[i]

    @pl.when(f == 0)
    def _():
        o_ref[...] = jnp.zeros_like(o_ref)

    def swiglu(rows):
        h = x_ref[rows, :]
        a = _mm(h, wg_ref[...])
        u = _mm(h, wu_ref[...])
        o_ref[rows, :] += _mm(_silu(a) * u, wd_ref[...])

    for s in range(0, n_sub, 2):
        if s + 2 <= n_sub:
            @pl.when(s + 2 <= ns)
            def _():
                swiglu(slice(s * sub, (s + 2) * sub))

        @pl.when(s + 1 == ns)
        def _():
            swiglu(slice(s * sub, (s + 1) * sub))


def _expert_ffn(blk_exp, n_valid_sub, x_blk, xs, wg, wu, wd, sb, sub, tf=512):
    p, d = xs.shape
    dff = wg.shape[2]
    n_super = n_valid_sub.shape[0]
    nf = dff // tf

    def fidx(i, f, ns):
        used = jnp.minimum(ns[i], 1)
        return f * used + (nf - 1) * (1 - used)

    return pl.pallas_call(
        functools.partial(_expert_kernel, sub=sub, n_sub=sb // sub),
        grid_spec=pltpu.PrefetchScalarGridSpec(
            num_scalar_prefetch=3,
            grid=(n_super, nf),
            in_specs=[pl.BlockSpec((sb, d), lambda i, f, be, ns, xi: (xi[i], 0)),
                      pl.BlockSpec((None, d, tf), lambda i, f, be, ns, xi: (be[i], 0, fidx(i, f, ns))),
                      pl.BlockSpec((None, d, tf), lambda i, f, be, ns, xi: (be[i], 0, fidx(i, f, ns))),
                      pl.BlockSpec((None, tf, d), lambda i, f, be, ns, xi: (be[i], fidx(i, f, ns), 0))],
            out_specs=pl.BlockSpec((sb, d), lambda i, f, be, ns, xi: (i, 0))),
        out_shape=jax.ShapeDtypeStruct((n_super * sb, d), F32),
        compiler_params=pltpu.CompilerParams(dimension_semantics=("arbitrary", "arbitrary"),
                                             vmem_limit_bytes=EXPERT_VMEM_LIMIT_BYTES),
        name="moe_experts",
    )(blk_exp, n_valid_sub, x_blk, xs, wg, wu, wd)


def _combine_kernel(st_ref, nc_ref, pos_ref, x_ref, info_ref, col_ref, ys_hbm, gf_ref, o_ref, buf_ref, sem,
                    *, rows, final_norm, blk_off, win):
    ck = MOE_CHUNK
    i = pl.program_id(0)

    def chunk_copy(step, e, c):
        t = (step + blk_off) * N_EXPERTS + e
        src = ys_hbm.at[pl.ds(pl.multiple_of(st_ref[t] + c * ck, 8), ck)]
        dst = buf_ref.at[step % 2, pl.ds(pl.multiple_of(pos_ref[t] + c * ck, ck), ck)]
        return pltpu.make_async_copy(src, dst, sem.at[step % 2])

    def all_chunks(step, fn):
        for e in range(N_EXPERTS):
            def body(c, carry, e=e):
                fn(chunk_copy(step, e, c))
                return carry
            lax.fori_loop(0, nc_ref[(step + blk_off) * N_EXPERTS + e], body, 0)

    @pl.when(i == 0)
    def _():
        buf_ref[...] = jnp.zeros_like(buf_ref)
        all_chunks(i, lambda cp: cp.start())

    @pl.when(i + 1 < pl.num_programs(0))
    def _():
        all_chunks(i + 1, lambda cp: cp.start())

    info = info_ref[...]
    cols = col_ref[...]
    lane = lax.broadcasted_iota(jnp.int32, (rows, win), 1)
    p = (jnp.where(cols[:, 0:1] == lane, info[:, 4:5], 0.0)
         + jnp.where(cols[:, 1:2] == lane, info[:, 5:6], 0.0)).astype(BF16)
    all_chunks(i, lambda cp: cp.wait())
    acc = x_ref[...] + _mm(p, buf_ref[i % 2])
    if final_norm:
        acc = _rms_rows(acc, gf_ref[...])
    o_ref[...] = acc


def _combine(seg_start, n_chunks, win_pos, cols, x, info, ys, gf, final_norm, row_off, n_rows, rows, win):
    d = x.shape[1]
    off = row_off // rows
    return pl.pallas_call(
        functools.partial(_combine_kernel, rows=rows, final_norm=final_norm, blk_off=off, win=win),
        grid_spec=pltpu.PrefetchScalarGridSpec(
            num_scalar_prefetch=3,
            grid=(n_rows // rows,),
            in_specs=[pl.BlockSpec((rows, d), lambda i, *_: (i + off, 0)),
                      pl.BlockSpec((rows, LANES), lambda i, *_: (i + off, 0)),
                      pl.BlockSpec((rows, 2), lambda i, *_: (i + off, 0)),
                      pl.BlockSpec(memory_space=pl.ANY),
                      pl.BlockSpec((1, d), lambda i, *_: (0, 0))],
            out_specs=pl.BlockSpec((rows, d), lambda i, *_: (i, 0)),
            scratch_shapes=[pltpu.VMEM((2, win, d), F32), pltpu.SemaphoreType.DMA((2,))]),
        out_shape=jax.ShapeDtypeStruct((n_rows, d), F32),
        compiler_params=_cparams(1),
        name="moe_combine",
    )(seg_start, n_chunks, win_pos, x, info, cols, ys, gf.reshape(1, d))


def _moe_ffn(x, g, whi, wlo, wg, wu, wd, gf, final_norm, groups, sub=512):
    t, d = x.shape
    sb = 2048 if 2 * t >= 16 * 1024 else sub
    tb = t
    for off, n in groups:
        tb = math.gcd(tb, math.gcd(off, n))
    tb = _row_tile(tb, 512)
    info, cnt = _router(x, g, whi, wlo, tb)
    e = info[:, 0:2].astype(jnp.int32)
    rank = info[:, 2:4].astype(jnp.int32)
    after = cnt[::8, :N_EXPERTS].astype(jnp.int32)
    before = jnp.concatenate([jnp.zeros((1, N_EXPERTS), jnp.int32), after[:-1]], axis=0)
    n_blocks = t // tb
    ck = MOE_CHUNK
    n_seg = after - before
    seg_len = (n_seg + 7) // 8 * 8
    seg_rel = jnp.cumsum(seg_len, axis=0) - seg_len
    counts = jnp.sum(seg_len, axis=0)
    n_sb = (counts + ck + sb - 1) // sb
    sb_end = jnp.cumsum(n_sb)
    sb_start = sb_end - n_sb
    row_start = sb_start * sb
    seg_start = row_start[None, :] + seg_rel
    n_chunks = (n_seg + ck - 1) // ck
    win_pos = (jnp.cumsum(n_chunks, axis=1) - n_chunks) * ck
    win = (2 * tb + N_EXPERTS * (ck - 1) + LANES - 1) // LANES * LANES
    eid = jnp.arange(N_EXPERTS, dtype=jnp.int32)
    tok_origin = jnp.repeat(win_pos - before, tb, axis=0)
    cols = rank + jnp.sum(jnp.where(e[:, :, None] == eid, tok_origin[:, None, :], 0), axis=-1)
    cols = cols.astype(jnp.int32)
    n_super = (2 * t + n_blocks * N_EXPERTS * 7) // sb + N_EXPERTS + 2
    blk = jnp.arange(n_super, dtype=jnp.int32)
    n_used = sb_end[-1]
    used = blk < n_used
    blk_c = jnp.minimum(blk, n_used - 1)
    be = jnp.minimum(jnp.sum((blk_c[:, None] >= sb_end[None, :]).astype(jnp.int32), axis=-1),
                     N_EXPERTS - 1)
    valid = jnp.clip(counts[be] - (blk_c - sb_start[be]) * sb, 0, sb)
    n_valid_sub = jnp.where(used, (valid + sub - 1) // sub, 0).astype(jnp.int32)
    zero_start = (row_start + counts).astype(jnp.int32)
    per = sb // sub
    sub_in_blk = jnp.arange(per, dtype=jnp.int32)
    zero_blk = (sub_in_blk[None, :] >= n_valid_sub[:, None]).astype(jnp.int32).reshape(-1)
    zero_blk = jnp.concatenate([zero_blk, jnp.ones((1,), jnp.int32)])
    tables = [a.reshape(-1).astype(jnp.int32) for a in (seg_start, n_chunks, win_pos)]
    xs = _dispatch(*tables, zero_start, zero_blk, cols.T, x, g, sub, tb, win)
    ys = _expert_ffn(be.astype(jnp.int32), n_valid_sub, blk_c.astype(jnp.int32), xs, wg, wu, wd, sb, sub)
    return [_combine(*tables, cols, x, info, ys, gf, final_norm, off, n, tb, win) for off, n in groups]


def _final_norm_kernel(x_ref, g_ref, o_ref):
    o_ref[...] = _rms_rows(x_ref[...], g_ref[...])


def _final_norm(x, g, row_off, n_rows, tm=512):
    d = x.shape[1]
    t = n_rows
    tm = _row_tile(t, tm)
    off = row_off // tm
    return pl.pallas_call(
        _final_norm_kernel,
        grid=(t // tm,),
        in_specs=[pl.BlockSpec((tm, d), lambda i: (i + off, 0)), pl.BlockSpec((1, d), lambda i: (0, 0))],
        out_specs=pl.BlockSpec((tm, d), lambda i: (i, 0)),
        out_shape=jax.ShapeDtypeStruct((t, d), F32),
        compiler_params=_cparams(1),
        name="final_norm",
    )(x, g.reshape(1, d))


def _block_diag(w):
    n, c, d = w.shape
    eye = jnp.eye(n, dtype=w.dtype)
    return jnp.einsum("ncd,nm->ncmd", w, eye).reshape(n * c, n * d)


def _pad_cols(w, n):
    return jnp.pad(w, ((0, 0), (0, n - w.shape[1])))


def _layer_params(l, p):
    (norm_mix, w_in, lru_conv_w, lru_conv_b, lru_a_w, lru_a_b, lru_x_w, lru_x_b, lru_lam, gla_gk_w2,
     gla_gk_b, gla_norm, gdn_conv_w, gdn_a_log, gdn_dt_bias, gdn_norm, w_out, norm_xq, norm_mem,
     w_mq, w_mk, w_mv, w_mo, norm_ffn) = [a[l] for a in p]
    lw = LRU_W
    gk, gv = GLA_HEADS * GLA_DK, GLA_HEADS * GLA_DV
    dh = GDN_HEADS * GDN_DK
    offs = [0]
    for s in (lw, lw, gk, gk, gv, GLA_RANK, gv, dh, dh, dh, GDN_HEADS, GDN_HEADS, dh):
        offs.append(offs[-1] + s)
    col = lambda i: w_in[:, offs[i]:offs[i + 1]]
    w_lru = jnp.concatenate([col(0), col(1)], axis=1)
    w_gla = jnp.concatenate([col(2), col(3), col(4), col(6), _pad_cols(col(5), LANES)], axis=1)
    w_gdn = jnp.concatenate(
        [col(7), col(8), col(9), col(12), _pad_cols(jnp.concatenate([col(10), col(11)], axis=1), LANES)],
        axis=1)
    w_cat = jnp.concatenate([w_lru, w_gla, w_gdn], axis=1).astype(BF16)
    widths = (w_lru.shape[1], w_gla.shape[1], w_gdn.shape[1])
    alog = jnp.zeros((1, LANES), F32).at[0, GDN_HEADS:2 * GDN_HEADS].set(gdn_a_log)
    dtb = jnp.zeros((1, LANES), F32).at[0, GDN_HEADS:2 * GDN_HEADS].set(gdn_dt_bias)
    return dict(
        norm_mix=norm_mix, w_cat=w_cat, widths=widths,
        lru_conv_w=lru_conv_w, lru_conv_b=lru_conv_b,
        lru_a=_block_diag(lru_a_w).astype(BF16), lru_a_b=lru_a_b,
        lru_x=_block_diag(lru_x_w).astype(BF16), lru_x_b=lru_x_b, lru_lam=lru_lam,
        gla_w2=jnp.pad(gla_gk_w2, ((0, LANES - GLA_RANK), (0, 0))).astype(BF16), gla_gk_b=gla_gk_b,
        gla_norm=jnp.tile(gla_norm, GLA_HEADS),
        gdn_conv_w=gdn_conv_w, gdn_alog=alog, gdn_dtb=dtb, gdn_norm=jnp.tile(gdn_norm, GDN_HEADS),
        w_out=w_out.astype(BF16), norm_xq=norm_xq, norm_mem=norm_mem, w_mq=w_mq.astype(BF16),
        w_mkv=jnp.concatenate([w_mk, w_mv], axis=1).astype(BF16), w_mo=w_mo.astype(BF16),
        norm_ffn=norm_ffn)


def _mix_and_attend(x, x_off, grp, l, lp, joint, total_rows):
    batch, seq, n = grp["batch"], grp["seq"], grp["batch"] * grp["seq"]
    lru_h0, lru_buf0, gla_s0, gdn_s0, gdn_buf0 = grp["states"]
    sl = grp["state_layer"](l)
    z_lru, z_gla, z_gdn = _rms_matmul(x, lp["norm_mix"], lp["w_cat"], lp["widths"], "in_proj",
                                      row_off=x_off, n_rows=n)
    y_lru, lru_h, lru_buf = _lru_mixer(
        z_lru, lru_h0, lru_buf0, sl, lp["lru_conv_w"], lp["lru_conv_b"], lp["lru_a"], lp["lru_a_b"],
        lp["lru_x"], lp["lru_x_b"], lp["lru_lam"], batch, seq)
    y_gla, gla_s = _gla_mixer(z_gla.reshape(batch, seq, -1), gla_s0, sl, lp["gla_w2"],
                              lp["gla_gk_b"], lp["gla_norm"], batch, seq)
    y_gdn, gdn_s, gdn_buf = _gdn_mixer(z_gdn.reshape(batch, seq, -1), gdn_s0, gdn_buf0, sl,
                                       lp["gdn_conv_w"], lp["gdn_alog"], lp["gdn_dtb"],
                                       lp["gdn_norm"], batch, seq)
    xn, q = _outproj(x, y_lru, y_gla.reshape(n, -1), y_gdn.reshape(n, -1),
                     lp["w_out"], lp["norm_xq"], lp["w_mq"], row_off=x_off)
    mem_k, mem_v, mem_layer = grp["mem"][l]
    joint = _attention(xn, q, mem_k, mem_v, mem_layer, lp["w_mo"], batch, seq, joint, grp["row_off"],
                       total_rows)
    return joint, (lru_h.reshape(batch, LRU_W), lru_buf, gla_s, gdn_s, gdn_buf)


def _run_layers(groups, layers, ffn, norm_final):
    total_rows = sum(g["batch"] * g["seq"] for g in groups)
    spans = [(g["row_off"], g["batch"] * g["seq"]) for g in groups]
    xs = [(g["x"], 0) for g in groups]
    new_states = [[] for _ in groups]
    outs = None
    for l, lp in enumerate(layers):
        joint = None
        for gi, grp in enumerate(groups):
            joint, st = _mix_and_attend(xs[gi][0], xs[gi][1], grp, l, lp, joint, total_rows)
            new_states[gi].append(st)
        last = l == len(layers) - 1
        kind, fp = ffn[l]
        if kind == "dense":
            joint = _ffn_dense(joint, lp["norm_ffn"], *fp)
            if last:
                outs = [_final_norm(joint, norm_final, off, n) for off, n in spans]
        elif last:
            outs = _moe_ffn(joint, lp["norm_ffn"], *fp, norm_final, True, spans)
        else:
            joint = _moe_ffn(joint, lp["norm_ffn"], *fp, norm_final, False, [(0, total_rows)])[0]
        xs = [(joint, g["row_off"]) for g in groups]
    states = [[jnp.stack(s) for s in zip(*ns)] for ns in new_states]
    return outs, states


def kernel(x_prompt, x_sample, mem_prompt, state_lru_h, state_lru_conv, state_gla, state_gdn, state_gdn_conv, cache_mem_k, cache_mem_v, norm_mix, w_in, lru_conv_w, lru_conv_b, lru_a_w, lru_a_b, lru_x_w, lru_x_b, lru_lam, gla_gk_w2, gla_gk_b, gla_norm, gdn_conv_w, gdn_a_log, gdn_dt_bias, gdn_norm, w_out, norm_xq, norm_mem, w_mq, w_mk, w_mv, w_mo, norm_ffn, w_ff_gate, w_ff_up, w_ff_down, w_router, w_e_gate, w_e_up, w_e_down, norm_final):
    depth = norm_mix.shape[0]
    per_layer = (norm_mix, w_in, lru_conv_w, lru_conv_b, lru_a_w, lru_a_b, lru_x_w, lru_x_b, lru_lam,
                 gla_gk_w2, gla_gk_b, gla_norm, gdn_conv_w, gdn_a_log, gdn_dt_bias, gdn_norm, w_out,
                 norm_xq, norm_mem, w_mq, w_mk, w_mv, w_mo, norm_ffn)
    layers = [_layer_params(l, per_layer) for l in range(depth)]
    ffn = []
    for l in range(depth):
        j = l // 2
        if l % 2 == 0:
            ffn.append(("dense", (w_ff_gate[j], w_ff_up[j], w_ff_down[j])))
        else:
            wr = _pad_cols(w_router[j], LANES)
            whi = wr.astype(BF16)
            wlo = (wr - whi.astype(F32)).astype(BF16)
            ffn.append(("moe", (whi, wlo, w_e_gate[j], w_e_up[j], w_e_down[j])))

    bp, mlen, d = mem_prompt.shape
    mem2 = mem_prompt.reshape(bp * mlen, d)
    pk, pv = [], []
    for lp in layers:
        k2, v2 = _rms_matmul(mem2, lp["norm_mem"], lp["w_mkv"], (d, d), "mem_kv")
        pk.append(k2.reshape(1, bp, mlen, d))
        pv.append(v2.reshape(1, bp, mlen, d))
    zero_state = (jnp.zeros((1, bp, LRU_W), F32), jnp.zeros((1, bp, CONV_K - 1, LRU_W), F32),
                  jnp.zeros((1, bp, GLA_HEADS, GLA_DK, GLA_DV), F32),
                  jnp.zeros((1, bp, GDN_HEADS, GDN_DK, GDN_DV), F32),
                  jnp.zeros((1, bp, CONV_K - 1, 3 * GDN_HEADS * GDN_DK), F32))
    sp = x_prompt.shape[1]
    bs, ss = x_sample.shape[0], x_sample.shape[1]
    groups = [
        dict(x=x_prompt.reshape(bp * sp, d), batch=bp, seq=sp, row_off=0, states=zero_state,
             state_layer=lambda l: 0, mem=[(pk[l], pv[l], 0) for l in range(depth)]),
        dict(x=x_sample.reshape(bs * ss, d), batch=bs, seq=ss, row_off=bp * sp,
             states=(state_lru_h, state_lru_conv, state_gla, state_gdn, state_gdn_conv),
             state_layer=lambda l: l, mem=[(cache_mem_k, cache_mem_v, l) for l in range(depth)]),
    ]
    (y_p, y_s), (p_st, s_st) = _run_layers(groups, layers, ffn, norm_final)
    p_mem_k = jnp.concatenate(pk, axis=0).reshape(depth, bp, mlen, MEM_HEADS, MEM_HD)
    p_mem_v = jnp.concatenate(pv, axis=0).reshape(depth, bp, mlen, MEM_HEADS, MEM_HD)

    return (y_p.reshape(bp, sp, d), y_s.reshape(bs, ss, d), p_st[0], p_st[1], p_st[2], p_st[3], p_st[4],
            p_mem_k, p_mem_v, s_st[0], s_st[1], s_st[2], s_st[3], s_st[4])
```

```python
import functools
import math

import jax
import jax.numpy as jnp
from jax import lax
from jax.experimental import pallas as pl
from jax.experimental.pallas import tpu as pltpu

F32 = jnp.float32
BF16 = jnp.bfloat16
EPS = 1e-6

D_MODEL = 1024
LRU_W = 512
LRU_BLOCKS = 8
LRU_C = 8.0
CONV_K = 4
GLA_HEADS = 4
GLA_DK = 32
GLA_DV = 64
GLA_RANK = 16
GLA_TAU = 16.0
GLA_SUB = 16
GDN_HEADS = 4
GDN_DK = 64
GDN_DV = 64
MIX_CHUNK = 64
MEM_HEADS = 4
MEM_HD = 256
N_EXPERTS = 8
LANES = 128
VMEM_LIMIT_BYTES = 48 * 1024 * 1024
EXPERT_VMEM_LIMIT_BYTES = 56 * 1024 * 1024


def _cparams(n_axes):
    return pltpu.CompilerParams(dimension_semantics=("arbitrary",) * n_axes,
                                vmem_limit_bytes=VMEM_LIMIT_BYTES)


def _mm(a, b):
    return jnp.dot(a.astype(BF16), b.astype(BF16), preferred_element_type=F32)


def _mm_nt(a, b):
    return lax.dot_general(a.astype(BF16), b.astype(BF16), (((1,), (1,)), ((), ())),
                           preferred_element_type=F32)


def _mm_tn(a, b):
    return lax.dot_general(a.astype(BF16), b.astype(BF16), (((0,), (0,)), ((), ())),
                           preferred_element_type=F32)


def _rms_rows(x, g):
    ms = jnp.mean(x * x, axis=-1, keepdims=True)
    return (x * lax.rsqrt(ms + EPS)) * g


def _softplus(x):
    return jnp.maximum(x, 0.0) + jnp.log1p(jnp.exp(-jnp.abs(x)))


def _sigmoid(x):
    return 1.0 / (1.0 + jnp.exp(-x))


def _silu(x):
    return x * _sigmoid(x)


def _gelu_tanh(x):
    c = 0.7978845608028654
    return x * (0.5 * (1.0 + jnp.tanh(c * (x + 0.044715 * (x * x * x)))))


def _seg_cumsum_rows(x, seg):
    rows = x.shape[0]
    tpos = lax.broadcasted_iota(jnp.int32, (rows, 1), 0) & (seg - 1)
    d = 1
    while d < seg:
        x = x + jnp.where(tpos >= d, pltpu.roll(x, d, axis=0), 0.0)
        d *= 2
    return x


def _head_rms(o, gain, n_heads, width):
    lane_head = lax.broadcasted_iota(jnp.int32, (1, n_heads * width), 1) // width
    sq = o * o
    inv = jnp.zeros_like(o)
    for h in range(n_heads):
        m = lane_head == h
        ms = jnp.sum(jnp.where(m, sq, 0.0), axis=-1, keepdims=True) * (1.0 / width)
        inv = jnp.where(m, lax.rsqrt(ms + EPS), inv)
    return (o * inv) * gain


def _stack_heads(x, n_heads, width):
    c = x.shape[0]
    t = jnp.concatenate([x] * n_heads, axis=0)
    row_head = lax.broadcasted_iota(jnp.int32, (n_heads * c, 1), 0) // c
    lane_head = lax.broadcasted_iota(jnp.int32, (1, n_heads * width), 1) // width
    return jnp.where(row_head == lane_head, t, 0.0)


def _unstack_heads(x, n_heads):
    c = x.shape[0] // n_heads
    o = x[0:c]
    for h in range(1, n_heads):
        o = o + x[h * c:(h + 1) * c]
    return o


def _rms_matmul_kernel(x_ref, g_ref, w_ref, *o_refs):
    h = _rms_rows(x_ref[...], g_ref[...]).astype(BF16)
    start = 0
    for o_ref in o_refs:
        n = o_ref.shape[1]
        o_ref[...] = jnp.dot(h, w_ref[:, start:start + n], preferred_element_type=F32)
        start += n


def _row_tile(t, pref):
    tile = min(pref, t)
    while t % tile or tile % 8:
        tile -= 8
    return tile


def _rms_matmul(x, g, w, widths, name, tm=512, row_off=0, n_rows=None):
    d = x.shape[1]
    t = x.shape[0] if n_rows is None else n_rows
    n = w.shape[1]
    tm = _row_tile(t, tm)
    off = row_off // tm
    return pl.pallas_call(
        _rms_matmul_kernel,
        grid=(t // tm,),
        in_specs=[pl.BlockSpec((tm, d), lambda i: (i + off, 0)),
                  pl.BlockSpec((1, d), lambda i: (0, 0)),
                  pl.BlockSpec((d, n), lambda i: (0, 0))],
        out_specs=[pl.BlockSpec((tm, wd), lambda i: (i, 0)) for wd in widths],
        out_shape=[jax.ShapeDtypeStruct((t, wd), F32) for wd in widths],
        compiler_params=_cparams(1),
        name=name,
    )(x, g.reshape(1, d), w)


def _mem_kv_kernel(m_ref, g_ref, w_ref, k_ref, v_ref, k5_ref, v5_ref):
    d = m_ref.shape[1]
    h = _rms_rows(m_ref[...], g_ref[...])
    kv = _mm(h, w_ref[...])
    k_ref[...] = kv[:, :d]
    v_ref[...] = kv[:, d:]
    for hd in range(MEM_HEADS):
        sl = slice(hd * MEM_HD, (hd + 1) * MEM_HD)
        k5_ref[:, hd, :] = kv[:, sl]
        v5_ref[:, hd, :] = kv[:, d + hd * MEM_HD:d + (hd + 1) * MEM_HD]


def _mem_kv(mem, g, w):
    bp, mlen, d = mem.shape
    depth = w.shape[0]
    flat = jax.ShapeDtypeStruct((depth, bp, mlen, d), F32)
    split = jax.ShapeDtypeStruct((depth, bp, mlen, MEM_HEADS, MEM_HD), F32)
    fidx = lambda l, b: (l, b, 0, 0)
    sidx = lambda l, b: (l, b, 0, 0, 0)
    return pl.pallas_call(
        _mem_kv_kernel,
        grid=(depth, bp),
        in_specs=[pl.BlockSpec((None, mlen, d), lambda l, b: (b, 0, 0)),
                  pl.BlockSpec((None, 1, d), lambda l, b: (l, 0, 0)),
                  pl.BlockSpec((None, d, 2 * d), lambda l, b: (l, 0, 0))],
        out_specs=[pl.BlockSpec((None, None, mlen, d), fidx),
                   pl.BlockSpec((None, None, mlen, d), fidx),
                   pl.BlockSpec((None, None, mlen, MEM_HEADS, MEM_HD), sidx),
                   pl.BlockSpec((None, None, mlen, MEM_HEADS, MEM_HD), sidx)],
        out_shape=[flat, flat, split, split],
        compiler_params=_cparams(2),
        name="mem_kv",
    )(mem, g.reshape(depth, 1, d), w)


def _lru_kernel(z_ref, h0_ref, buf0_ref, cw_ref, cb_ref, aw_ref, ab_ref, xw_ref, xb_ref, lam_ref,
                y_ref, hout_ref, bufout_ref, xs_ref, hc_ref, *, bt, tl, nt):
    w = LRU_W
    rows = bt * tl
    j = pl.program_id(0) % nt

    @pl.when(j == 0)
    def _():
        xs_ref[:, 5:8, :] = buf0_ref[...]
        hc_ref[...] = h0_ref[...]

    xs_ref[:, 8:, :] = z_ref[:, :w].reshape(bt, tl, w)
    gate = z_ref[:, w:]
    cw = cw_ref[...]
    xc = cb_ref[...] + xs_ref[:, 5:5 + tl, :] * cw[0:1]
    for k in range(1, CONV_K):
        xc = xc + xs_ref[:, 5 + k:5 + k + tl, :] * cw[k:k + 1]
    tail = xs_ref[:, 5 + tl:8 + tl, :]
    xs_ref[:, 5:8, :] = tail
    bufout_ref[...] = tail

    xc = xc.reshape(rows, w)
    r = _sigmoid(_mm(xc, aw_ref[...]) + ab_ref[...])
    ig = _sigmoid(_mm(xc, xw_ref[...]) + xb_ref[...])
    log_a = (-LRU_C * r) * _softplus(-lam_ref[...])
    a = jnp.exp(log_a)
    th = jnp.tanh(log_a)
    b = jnp.sqrt((-2.0 * th) / (1.0 - th)) * (ig * xc)

    sub = 8
    gps = tl // sub
    a3 = a.reshape(rows // sub, sub, w)
    b3 = b.reshape(rows // sub, sub, w)
    spos = lax.broadcasted_iota(jnp.int32, (1, sub, 1), 1)
    d = 1
    while d < sub:
        m = spos >= d
        b3 = jnp.where(m, a3 * pltpu.roll(b3, d, axis=1) + b3, b3)
        a3 = jnp.where(m, a3 * pltpu.roll(a3, d, axis=1), a3)
        d *= 2
    a4 = a3.reshape(bt, gps, sub, w)
    b4 = b3.reshape(bt, gps, sub, w)
    carry = hc_ref[...]
    hs = []
    for r in range(gps):
        hr = b4[:, r] + a4[:, r] * carry
        hs.append(hr)
        carry = hr[:, sub - 1:sub, :]
    h = jnp.stack(hs, axis=1).reshape(rows, w)
    hlast = carry
    hc_ref[...] = hlast
    hout_ref[...] = hlast
    y_ref[...] = h * _gelu_tanh(gate)


def _lru_mixer(z, h0, buf0, layer, cw, cb, aw, ab, xw, xb, lam, batch, seq):
    w = LRU_W
    tl = min(seq, 256)
    bt = min(batch, max(1, 256 // seq))
    nt = seq // tl
    rows = bt * tl
    grid = (batch * seq // rows,)
    if nt > 1:
        sidx = lambda i: (i // nt, 0, 0)
        lidx = lambda i: (layer, i // nt, 0, 0)
    else:
        sidx = lambda i: (i, 0, 0)
        lidx = lambda i: (layer, i, 0, 0)
    full2 = lambda i: (0, 0)
    return pl.pallas_call(
        functools.partial(_lru_kernel, bt=bt, tl=tl, nt=nt),
        grid=grid,
        in_specs=[pl.BlockSpec((rows, 2 * w), lambda i: (i, 0)),
                  pl.BlockSpec((None, bt, 1, w), lidx),
                  pl.BlockSpec((None, bt, CONV_K - 1, w), lidx),
                  pl.BlockSpec((CONV_K, w), full2),
                  pl.BlockSpec((1, w), full2),
                  pl.BlockSpec((w, w), full2),
                  pl.BlockSpec((1, w), full2),
                  pl.BlockSpec((w, w), full2),
                  pl.BlockSpec((1, w), full2),
                  pl.BlockSpec((1, w), full2)],
        out_specs=[pl.BlockSpec((rows, w), lambda i: (i, 0)),
                   pl.BlockSpec((bt, 1, w), sidx),
                   pl.BlockSpec((bt, CONV_K - 1, w), sidx)],
        out_shape=[jax.ShapeDtypeStruct((batch * seq, w), F32),
                   jax.ShapeDtypeStruct((batch, 1, w), F32),
                   jax.ShapeDtypeStruct((batch, CONV_K - 1, w), F32)],
        scratch_shapes=[pltpu.VMEM((bt, 8 + tl, w), F32), pltpu.VMEM((bt, 1, w), F32)],
        compiler_params=_cparams(1),
        name="lru_mixer",
    )(z, h0.reshape(h0.shape[0], batch, 1, w), buf0, cw, cb.reshape(1, w), aw, ab.reshape(1, w), xw,
      xb.reshape(1, w), lam.reshape(1, w))


def _gla_kernel(z_ref, s0_ref, w2_ref, gb_ref, gn_ref, y_ref, sout_ref, s_ref, *, c, sc, g):
    nh = GLA_HEADS
    kw = nh * GLA_DK
    vw = nh * GLA_DV

    @pl.when(pl.program_id(1) == 0)
    def _():
        s_ref[...] = jnp.zeros_like(s_ref)
        for b in range(g):
            for h in range(nh):
                s_ref[b, h * GLA_DK:(h + 1) * GLA_DK, h * GLA_DV:(h + 1) * GLA_DV] = s0_ref[b, h]

    bs = range(g)
    q = [z_ref[b, :, 0:kw] * (GLA_DK ** -0.5) for b in bs]
    k = [z_ref[b, :, kw:2 * kw] for b in bs]
    v = [z_ref[b, :, 2 * kw:2 * kw + vw] for b in bs]
    gk = [-_softplus(-(_mm(z_ref[b, :, 2 * kw + 2 * vw:], w2_ref[...]) + gb_ref[...])) / GLA_TAU
          for b in bs]
    gcum = [_seg_cumsum_rows(gk[b], sc) for b in bs]
    qp = [q[b] * jnp.exp(gcum[b]) for b in bs]

    ri = lax.broadcasted_iota(jnp.int32, (nh * c, nh * c), 0)
    ci = lax.broadcasted_iota(jnp.int32, (nh * c, nh * c), 1)
    keep = (ri // sc == ci // sc) & (ci <= ri)
    a = [_mm_nt(_stack_heads(qp[b], nh, GLA_DK), _stack_heads(k[b] * jnp.exp(-gcum[b]), nh, GLA_DK))
         for b in bs]
    o = [_unstack_heads(_mm(jnp.where(keep, a[b], 0.0), _stack_heads(v[b], nh, GLA_DV)), nh)
         for b in bs]

    s = [s_ref[b] for b in bs]
    eye = (lax.broadcasted_iota(jnp.int32, (kw, kw), 0) ==
           lax.broadcasted_iota(jnp.int32, (kw, kw), 1))
    bd = (lax.broadcasted_iota(jnp.int32, (kw, vw), 0) // GLA_DK ==
          lax.broadcasted_iota(jnp.int32, (kw, vw), 1) // GLA_DV)
    o_inter = [[] for _ in bs]
    for i in range(c // sc):
        lo, hi = i * sc, (i + 1) * sc
        for b in bs:
            o_inter[b].append(_mm(qp[b][lo:hi], s[b]))
            glast = gcum[b][hi - 1:hi]
            kpp = k[b][lo:hi] * jnp.exp(glast - gcum[b][lo:hi])
            u = _mm_tn(kpp, v[b][lo:hi])
            dcol = jnp.sum(jnp.where(eye, jnp.exp(glast), 0.0), axis=1, keepdims=True)
            s[b] = dcol * s[b] + jnp.where(bd, u, 0.0)
    for b in bs:
        s_ref[b] = s[b]
        for h in range(nh):
            sout_ref[b, h] = s[b][h * GLA_DK:(h + 1) * GLA_DK, h * GLA_DV:(h + 1) * GLA_DV]
        ob = o[b] + jnp.concatenate(o_inter[b], axis=0)
        gate = z_ref[b, :, 2 * kw + vw:2 * kw + 2 * vw]
        y_ref[b] = _head_rms(ob, gn_ref[...], nh, GLA_DV) * _silu(gate)


def _gla_mixer(z, s0, layer, w2, gb, gn, batch, seq):
    c = min(seq, MIX_CHUNK)
    sc = min(c, GLA_SUB)
    nt = seq // c
    g = _mixer_group(batch, c)
    kw, vw = GLA_HEADS * GLA_DK, GLA_HEADS * GLA_DV
    zw = z.shape[2]
    full2 = lambda i, j: (0, 0)
    sblk = (g, GLA_HEADS, GLA_DK, GLA_DV)
    return pl.pallas_call(
        functools.partial(_gla_kernel, c=c, sc=sc, g=g),
        grid=(batch // g, nt),
        in_specs=[pl.BlockSpec((g, c, zw), lambda i, j: (i, j, 0)),
                  pl.BlockSpec((None,) + sblk, lambda i, j: (layer, i, 0, 0, 0)),
                  pl.BlockSpec((LANES, kw), full2),
                  pl.BlockSpec((1, kw), full2),
                  pl.BlockSpec((1, vw), full2)],
        out_specs=[pl.BlockSpec((g, c, vw), lambda i, j: (i, j, 0)),
                   pl.BlockSpec(sblk, lambda i, j: (i, 0, 0, 0))],
        out_shape=[jax.ShapeDtypeStruct((batch, seq, vw), F32),
                   jax.ShapeDtypeStruct((batch, GLA_HEADS, GLA_DK, GLA_DV), F32)],
        scratch_shapes=[pltpu.VMEM((g, kw, vw), F32)],
        compiler_params=_cparams(2),
        name="gla_mixer",
    )(z, s0, w2, gb.reshape(1, kw), gn.reshape(1, vw))


def _gdn_prep(z_ref, cw_ref, alog_ref, dtb_ref, bufout_ref, xs_ref, *, c):
    nh = GDN_HEADS
    hw = nh * GDN_DK
    cw3 = 3 * hw

    xs_ref[8:, :] = z_ref[:, 0:cw3]
    cw = cw_ref[...]
    qkv = xs_ref[5:5 + c, :] * cw[0:1]
    for kk in range(1, CONV_K):
        qkv = qkv + xs_ref[5 + kk:5 + kk + c, :] * cw[kk:kk + 1]
    tail = xs_ref[5 + c:8 + c, :]
    xs_ref[5:8, :] = tail
    bufout_ref[...] = tail
    qkv = _silu(qkv)
    zg = z_ref[:, cw3:cw3 + hw]
    sm = z_ref[:, cw3 + hw:]

    lane_head = lax.broadcasted_iota(jnp.int32, (1, hw), 1) // GDN_DK

    def l2n(x):
        sq = x * x
        inv = jnp.zeros_like(x)
        for h in range(nh):
            m = lane_head == h
            ss = jnp.sum(jnp.where(m, sq, 0.0), axis=-1, keepdims=True)
            inv = jnp.where(m, lax.rsqrt(ss + EPS), inv)
        return x * inv

    q = l2n(qkv[:, 0:hw]) * (GDN_DK ** -0.5)
    k = l2n(qkv[:, hw:2 * hw])
    v = qkv[:, 2 * hw:3 * hw]
    beta = _sigmoid(sm)
    glog = -jnp.exp(alog_ref[...]) * _softplus(sm + dtb_ref[...])
    gcum = _seg_cumsum_rows(glog, c)

    n = nh * c
    bcol = jnp.concatenate([beta[:, h:h + 1] for h in range(nh)], axis=0)
    gcol = jnp.concatenate([gcum[:, nh + h:nh + h + 1] for h in range(nh)], axis=0)
    glast = jnp.concatenate(
        [jnp.broadcast_to(gcum[c - 1:c, nh + h:nh + h + 1], (c, 1)) for h in range(nh)], axis=0)
    ri = lax.broadcasted_iota(jnp.int32, (n, n), 0)
    ci = lax.broadcasted_iota(jnp.int32, (n, n), 1)
    grow = jnp.sum(jnp.where(ri == ci, gcol, 0.0), axis=0, keepdims=True)
    same = ri // c == ci // c
    incl = same & (ci <= ri)
    strict = same & (ci < ri)
    dec = jnp.where(incl, jnp.exp(jnp.where(incl, gcol - grow, 0.0)), 0.0)

    sdec = jnp.concatenate(
        [jnp.broadcast_to(jnp.exp(gcum[c - 1:c, nh + h:nh + h + 1]), (GDN_DK, 1)) for h in range(nh)],
        axis=0)
    return dict(ks=_stack_heads(k, nh, GDN_DK), qs=_stack_heads(q, nh, GDN_DK),
                vs=_stack_heads(v, nh, GDN_DV), bcol=bcol, egc=jnp.exp(gcol), dec=dec,
                kdec=jnp.exp(glast - gcol), sdec=sdec, zg=zg)


def _gdn_kernel(z_ref, s0_ref, buf0_ref, cw_ref, alog_ref, dtb_ref, gn_ref,
                y_ref, sout_ref, bufout_ref, xs_ref, s_ref, *, c, g):
    nh = GDN_HEADS
    hw = nh * GDN_DK
    n = nh * c

    @pl.when(pl.program_id(1) == 0)
    def _():
        xs_ref[:, 5:8, :] = buf0_ref[...]
        s_ref[...] = jnp.zeros_like(s_ref)
        for b in range(g):
            for h in range(nh):
                s_ref[b, h * GDN_DK:(h + 1) * GDN_DK, h * GDN_DV:(h + 1) * GDN_DV] = s0_ref[b, h]

    bs = range(g)
    pr = [_gdn_prep(z_ref.at[b], cw_ref, alog_ref, dtb_ref, bufout_ref.at[b], xs_ref.at[b], c=c)
          for b in bs]
    ri = lax.broadcasted_iota(jnp.int32, (n, n), 0)
    ci = lax.broadcasted_iota(jnp.int32, (n, n), 1)
    same = ri // c == ci // c
    incl = same & (ci <= ri)
    strict = same & (ci < ri)
    kq = [_mm_nt(jnp.concatenate([pr[b]["ks"], pr[b]["qs"]], axis=0), pr[b]["ks"]) for b in bs]

    p = [jnp.where(strict, -(pr[b]["bcol"] * kq[b][0:n]) * pr[b]["dec"], 0.0) for b in bs]
    tinv = [jnp.where(ri == ci, 1.0, 0.0) + p[b] for b in bs]
    span = 2
    while span < c:
        p = [_mm(p[b], p[b]) for b in bs]
        tinv = [tinv[b] + _mm(tinv[b], p[b]) for b in bs]
        span *= 2

    s = [s_ref[b] for b in bs]
    uw = [_mm(tinv[b], jnp.concatenate([pr[b]["vs"] * pr[b]["bcol"],
                                        pr[b]["ks"] * (pr[b]["bcol"] * pr[b]["egc"])], axis=1))
          for b in bs]
    qw = [_mm(jnp.concatenate([pr[b]["qs"] * pr[b]["egc"], uw[b][:, hw:]], axis=0), s[b])
          for b in bs]
    vnew = [uw[b][:, 0:hw] - qw[b][n:] for b in bs]
    av = [_mm(jnp.where(incl, kq[b][n:] * pr[b]["dec"], 0.0), vnew[b]) for b in bs]
    kv = [_mm_tn(pr[b]["ks"] * pr[b]["kdec"], vnew[b]) for b in bs]
    for b in bs:
        o = _unstack_heads(qw[b][0:n] + av[b], nh)
        sn = pr[b]["sdec"] * s[b] + kv[b]
        s_ref[b] = sn
        for h in range(nh):
            sout_ref[b, h] = sn[h * GDN_DK:(h + 1) * GDN_DK, h * GDN_DV:(h + 1) * GDN_DV]
        y_ref[b] = _head_rms(o, gn_ref[...], nh, GDN_DV) * _silu(pr[b]["zg"])


def _mixer_group(batch, c):
    return min(batch, max(8, 128 // c))


def _gdn_mixer(z, s0, buf0, layer, cw, alog, dtb, gn, batch, seq):
    c = min(seq, MIX_CHUNK)
    nt = seq // c
    g = _mixer_group(batch, c)
    hw = GDN_HEADS * GDN_DK
    zw = z.shape[2]
    full2 = lambda i, j: (0, 0)
    sblk = (g, GDN_HEADS, GDN_DK, GDN_DV)
    return pl.pallas_call(
        functools.partial(_gdn_kernel, c=c, g=g),
        grid=(batch // g, nt),
        in_specs=[pl.BlockSpec((g, c, zw), lambda i, j: (i, j, 0)),
                  pl.BlockSpec((None,) + sblk, lambda i, j: (layer, i, 0, 0, 0)),
                  pl.BlockSpec((None, g, CONV_K - 1, 3 * hw), lambda i, j: (layer, i, 0, 0)),
                  pl.BlockSpec((CONV_K, 3 * hw), full2),
                  pl.BlockSpec((1, LANES), full2),
                  pl.BlockSpec((1, LANES), full2),
                  pl.BlockSpec((1, hw), full2)],
        out_specs=[pl.BlockSpec((g, c, hw), lambda i, j: (i, j, 0)),
                   pl.BlockSpec(sblk, lambda i, j: (i, 0, 0, 0)),
                   pl.BlockSpec((g, CONV_K - 1, 3 * hw), lambda i, j: (i, 0, 0))],
        out_shape=[jax.ShapeDtypeStruct((batch, seq, hw), F32),
                   jax.ShapeDtypeStruct((batch, GDN_HEADS, GDN_DK, GDN_DV), F32),
                   jax.ShapeDtypeStruct((batch, CONV_K - 1, 3 * hw), F32)],
        scratch_shapes=[pltpu.VMEM((g, 8 + c, 3 * hw), F32), pltpu.VMEM((g, hw, hw), F32)],
        compiler_params=_cparams(2),
        name="gdn_mixer",
    )(z, s0, buf0, cw, alog, dtb, gn.reshape(1, hw))


def _outproj_kernel(x_ref, yl_ref, yg_ref, yd_ref, wo_ref, gq_ref, wq_ref, xn_ref, q_ref):
    lw = LRU_W
    gw = GLA_HEADS * GLA_DV
    y = _mm(yl_ref[...], wo_ref[0:lw, :])
    y = y + _mm(yg_ref[...], wo_ref[lw:lw + gw, :])
    y = y + _mm(yd_ref[...], wo_ref[lw + gw:, :])
    xn = x_ref[...] + y
    xn_ref[...] = xn
    q_ref[...] = _mm(_rms_rows(xn, gq_ref[...]), wq_ref[...])


def _outproj(x, yl, yg, yd, wo, gq, wq, tm=512, row_off=0):
    d = x.shape[1]
    t = yl.shape[0]
    tm = _row_tile(t, tm)
    off = row_off // tm
    row = lambda i: (i, 0)
    full2 = lambda i: (0, 0)
    return pl.pallas_call(
        _outproj_kernel,
        grid=(t // tm,),
        in_specs=[pl.BlockSpec((tm, d), lambda i: (i + off, 0)),
                  pl.BlockSpec((tm, yl.shape[1]), row),
                  pl.BlockSpec((tm, yg.shape[1]), row),
                  pl.BlockSpec((tm, yd.shape[1]), row),
                  pl.BlockSpec((d, d), full2),
                  pl.BlockSpec((1, d), full2),
                  pl.BlockSpec((d, d), full2)],
        out_specs=[pl.BlockSpec((tm, d), row), pl.BlockSpec((tm, d), row)],
        out_shape=[jax.ShapeDtypeStruct((t, d), F32), jax.ShapeDtypeStruct((t, d), F32)],
        compiler_params=_cparams(1),
        name="outproj_qproj",
    )(x, yl, yg, yd, wo, gq.reshape(1, d), wq)


def _attn_kernel(x_ref, q_ref, k_ref, v_ref, wo_ref, o_ref):
    hd = MEM_HD
    acc = x_ref[...]
    for h in range(MEM_HEADS):
        sl = slice(h * hd, (h + 1) * hd)
        s = _mm_nt(q_ref[:, sl], k_ref[:, sl]) * (hd ** -0.5)
        m = jnp.max(s, axis=-1, keepdims=True)
        p = jnp.exp(s - m)
        l = jnp.sum(p, axis=-1, keepdims=True)
        oh = _mm(p, v_ref[:, sl]) / l
        acc = acc + _mm(oh, wo_ref[sl, :])
    o_ref[...] = acc


def _attn_heads_kernel(q_ref, k_ref, v_ref, o_ref, *, gb, tl):
    nh, hd = MEM_HEADS, MEM_HD
    m = k_ref.shape[1]
    row_head = lax.broadcasted_iota(jnp.int32, (nh * tl, 1), 0) // tl
    col_head = lax.broadcasted_iota(jnp.int32, (1, m * nh), 1) % nh
    for b in range(gb):
        kf = k_ref[b].reshape(m * nh, hd)
        vf = v_ref[b].reshape(m * nh, hd)
        rows = slice(b * tl, (b + 1) * tl)
        qs = jnp.concatenate([q_ref[rows, h * hd:(h + 1) * hd] for h in range(nh)], axis=0)
        s = _mm_nt(qs, kf) * (hd ** -0.5)
        s = jnp.where(row_head == col_head, s, -jnp.inf)
        mx = jnp.max(s, axis=-1, keepdims=True)
        p = jnp.exp(s - mx)
        l = jnp.sum(p, axis=-1, keepdims=True)
        o = _mm(p, vf) / l
        for h in range(nh):
            o_ref[rows, h * hd:(h + 1) * hd] = o[h * tl:(h + 1) * tl]


def _oproj_kernel(x_ref, a_ref, wo_ref, *rest):
    o_ref = rest[-1]
    o_ref[...] = x_ref[...] + _mm(a_ref[...], wo_ref[...])


def _attn_first_kernel(x_ref, q_ref, k_ref, v_ref, wo_ref, o_ref, *, n_real):
    i = pl.program_id(0)

    @pl.when(i < n_real)
    def _():
        _attn_kernel(x_ref, q_ref, k_ref, v_ref, wo_ref, o_ref)

    @pl.when(i >= n_real)
    def _():
        o_ref[...] = jnp.zeros_like(o_ref)


def _attn_into_kernel(x_ref, q_ref, k_ref, v_ref, wo_ref, joint_ref, o_ref):
    _attn_kernel(x_ref, q_ref, k_ref, v_ref, wo_ref, o_ref)


def _attention(x, q, k, v, layer, wo, batch, seq, joint, row_off, total_rows):
    t, d = x.shape
    out_shape = jax.ShapeDtypeStruct((total_rows, d), F32)
    extra_specs = [] if joint is None else [pl.BlockSpec(memory_space=pl.ANY)]
    extra_args = () if joint is None else (joint,)
    if k.ndim == 4:
        tl = min(seq, 512)
        nl = seq // tl
        off = row_off // tl
        kblk = (None, None) + k.shape[2:]
        n_real = batch * nl
        if joint is None:
            assert row_off == 0
            n_steps = pl.cdiv(total_rows, tl)
            body = functools.partial(_attn_first_kernel, n_real=n_real)
        else:
            n_steps = n_real
            body = _attn_into_kernel
        kidx = lambda i: (layer, jnp.minimum(i // nl, batch - 1), 0, 0)
        row = lambda i: (jnp.minimum(i, n_real - 1), 0)
        return pl.pallas_call(
            body,
            grid=(n_steps,),
            in_specs=[pl.BlockSpec((tl, d), row),
                      pl.BlockSpec((tl, d), row),
                      pl.BlockSpec(kblk, kidx),
                      pl.BlockSpec(kblk, kidx),
                      pl.BlockSpec((d, d), lambda i: (0, 0))] + extra_specs,
            out_specs=pl.BlockSpec((tl, d), lambda i: (i + off, 0)),
            out_shape=out_shape,
            input_output_aliases={} if joint is None else {5: 0},
            compiler_params=_cparams(1),
            name="mem_attention",
        )(x, q, k, v, wo, *extra_args)
    gb = 4 if batch % 4 == 0 else 2
    kblk = (None, gb) + k.shape[2:]
    kidx = lambda i: (layer, i, 0, 0, 0)
    att = pl.pallas_call(
        functools.partial(_attn_heads_kernel, gb=gb, tl=seq),
        grid=(batch // gb,),
        in_specs=[pl.BlockSpec((gb * seq, d), lambda i: (i, 0)),
                  pl.BlockSpec(kblk, kidx),
                  pl.BlockSpec(kblk, kidx)],
        out_specs=pl.BlockSpec((gb * seq, d), lambda i: (i, 0)),
        out_shape=jax.ShapeDtypeStruct((t, d), F32),
        compiler_params=_cparams(1),
        name="mem_attention_heads",
    )(q, k, v)
    tm = _row_tile(t, 512)
    off = row_off // tm
    return pl.pallas_call(
        _oproj_kernel,
        grid=(t // tm,),
        in_specs=[pl.BlockSpec((tm, d), lambda i: (i, 0)),
                  pl.BlockSpec((tm, d), lambda i: (i, 0)),
                  pl.BlockSpec((d, d), lambda i: (0, 0))] + extra_specs,
        out_specs=pl.BlockSpec((tm, d), lambda i: (i + off, 0)),
        out_shape=out_shape,
        input_output_aliases={} if joint is None else {3: 0},
        compiler_params=_cparams(1),
        name="mem_oproj",
    )(x, att, wo, *extra_args)


def _ffn_kernel(x_ref, g_ref, wg_ref, wu_ref, wd_ref, o_ref, h_ref, acc_ref):
    f = pl.program_id(1)

    @pl.when(f == 0)
    def _():
        h_ref[...] = _rms_rows(x_ref[...], g_ref[...]).astype(BF16)
        acc_ref[...] = jnp.zeros_like(acc_ref)

    h = h_ref[...]
    a = _mm(h, wg_ref[...])
    u = _mm(h, wu_ref[...])
    acc_ref[...] += _mm(_silu(a) * u, wd_ref[...])

    @pl.when(f == pl.num_programs(1) - 1)
    def _():
        o_ref[...] = x_ref[...] + acc_ref[...]


def _ffn_dense(x, g, wg, wu, wd, tm=1024, tf=512):
    t, d = x.shape
    dff = wg.shape[1]
    tm = _row_tile(t, tm)
    return pl.pallas_call(
        _ffn_kernel,
        grid=(t // tm, dff // tf),
        in_specs=[pl.BlockSpec((tm, d), lambda i, f: (i, 0)),
                  pl.BlockSpec((1, d), lambda i, f: (0, 0)),
                  pl.BlockSpec((d, tf), lambda i, f: (0, f)),
                  pl.BlockSpec((d, tf), lambda i, f: (0, f)),
                  pl.BlockSpec((tf, d), lambda i, f: (f, 0))],
        out_specs=pl.BlockSpec((tm, d), lambda i, f: (i, 0)),
        out_shape=jax.ShapeDtypeStruct((t, d), F32),
        scratch_shapes=[pltpu.VMEM((tm, d), BF16), pltpu.VMEM((tm, d), F32)],
        compiler_params=_cparams(2),
        name="ffn_dense",
    )(x, g.reshape(1, d), wg, wu, wd)


def _router_kernel(x_ref, g_ref, whi_ref, wlo_ref, info_ref, cnt_ref, carry_ref, *, tm):
    i = pl.program_id(0)

    @pl.when(i == 0)
    def _():
        carry_ref[...] = jnp.zeros_like(carry_ref)

    h = _rms_rows(x_ref[...], g_ref[...])
    hhi = h.astype(BF16)
    hlo = (h - hhi.astype(F32)).astype(BF16)
    whi = whi_ref[...]
    logits = (jnp.dot(hhi, whi, preferred_element_type=F32)
              + jnp.dot(hlo, whi, preferred_element_type=F32)
              + jnp.dot(hhi, wlo_ref[...], preferred_element_type=F32))
    lane = lax.broadcasted_iota(jnp.int32, (tm, LANES), 1)
    neg = jnp.float32(-jnp.inf)
    logits = jnp.where(lane < N_EXPERTS, logits, neg)
    m1 = jnp.max(logits, axis=-1, keepdims=True)
    i1 = jnp.min(jnp.where(logits == m1, lane, LANES), axis=-1, keepdims=True)
    rest = jnp.where(lane == i1, neg, logits)
    m2 = jnp.max(rest, axis=-1, keepdims=True)
    i2 = jnp.min(jnp.where(rest == m2, lane, LANES), axis=-1, keepdims=True)
    e = jnp.exp(m2 - m1)
    g1 = 1.0 / (1.0 + e)
    g2 = e / (1.0 + e)
    oh1 = jnp.where(lane == i1, 1.0, 0.0)
    oh2 = jnp.where(lane == i2, 1.0, 0.0)
    oh = oh1 + oh2
    ri = lax.broadcasted_iota(jnp.int32, (tm, tm), 0)
    ci = lax.broadcasted_iota(jnp.int32, (tm, tm), 1)
    tri = jnp.where(ci < ri, 1.0, 0.0)
    before = _mm(tri, oh) + carry_ref[0:1, :]
    r1 = jnp.sum(oh1 * before, axis=-1, keepdims=True)
    r2 = jnp.sum(oh2 * before, axis=-1, keepdims=True)
    carry = carry_ref[0:1, :] + jnp.sum(oh, axis=0, keepdims=True)
    carry_ref[...] = jnp.broadcast_to(carry, carry_ref.shape)
    cnt_ref[...] = jnp.broadcast_to(carry, cnt_ref.shape)
    info = jnp.where(lane == 0, i1.astype(F32), 0.0)
    info = jnp.where(lane == 1, i2.astype(F32), info)
    info = jnp.where(lane == 2, r1, info)
    info = jnp.where(lane == 3, r2, info)
    info = jnp.where(lane == 4, g1, info)
    info = jnp.where(lane == 5, g2, info)
    info_ref[...] = info


def _router(x, g, whi, wlo, tm):
    t, d = x.shape
    return pl.pallas_call(
        functools.partial(_router_kernel, tm=tm),
        grid=(t // tm,),
        in_specs=[pl.BlockSpec((tm, d), lambda i: (i, 0)),
                  pl.BlockSpec((1, d), lambda i: (0, 0)),
                  pl.BlockSpec((d, LANES), lambda i: (0, 0)),
                  pl.BlockSpec((d, LANES), lambda i: (0, 0))],
        out_specs=[pl.BlockSpec((tm, LANES), lambda i: (i, 0)),
                   pl.BlockSpec((8, LANES), lambda i: (i, 0))],
        out_shape=[jax.ShapeDtypeStruct((t, LANES), F32),
                   jax.ShapeDtypeStruct((t // tm * 8, LANES), F32)],
        scratch_shapes=[pltpu.VMEM((8, LANES), F32)],
        compiler_params=_cparams(1),
        name="moe_router",
    )(x, g.reshape(1, d), whi, wlo)


MOE_CHUNK = 32


def _dispatch_kernel(st_ref, nc_ref, pos_ref, zs_ref, zf_ref, x_ref, g_ref, ct_ref, xs_hbm, w_ref, zbuf_ref,
                     zsem, sem, *, rows, sub, n_blk, win):
    i = pl.program_id(0)

    @pl.when(i == 0)
    def _():
        zbuf_ref[...] = jnp.zeros_like(zbuf_ref)
        for e in range(N_EXPERTS):
            dst = xs_hbm.at[pl.ds(pl.multiple_of(zs_ref[e], 8), sub)]
            pltpu.make_async_copy(zbuf_ref, dst, zsem).start()
            pltpu.make_async_copy(zbuf_ref, dst, zsem).wait()

        def fill(j, carry):
            @pl.when(zf_ref[j] == 1)
            def _():
                dst = xs_hbm.at[pl.ds(pl.multiple_of(j * sub, sub), sub)]
                pltpu.make_async_copy(zbuf_ref, dst, zsem).start()
            return carry

        def drain(j, carry):
            @pl.when(zf_ref[j] == 1)
            def _():
                dst = xs_hbm.at[pl.ds(pl.multiple_of(j * sub, sub), sub)]
                pltpu.make_async_copy(zbuf_ref, dst, zsem).wait()
            return carry

        lax.fori_loop(0, n_blk, fill, 0)
        lax.fori_loop(0, n_blk, drain, 0)

    ck = MOE_CHUNK
    n_steps = pl.num_programs(0)

    def chunk_copy(step, e, c):
        t = step * N_EXPERTS + e
        src = w_ref.at[step % 2, pl.ds(pl.multiple_of(pos_ref[t] + c * ck, ck), ck)]
        dst = xs_hbm.at[pl.ds(pl.multiple_of(st_ref[t] + c * ck, 8), ck)]
        return pltpu.make_async_copy(src, dst, sem)

    def all_chunks(step, fn):
        for e in range(N_EXPERTS):
            def body(c, carry, e=e):
                fn(chunk_copy(step, e, c))
                return carry
            lax.fori_loop(0, nc_ref[step * N_EXPERTS + e], body, 0)

    h = _rms_rows(x_ref[...], g_ref[...])
    wrow = lax.broadcasted_iota(jnp.int32, (win, rows), 0)
    cols = ct_ref[...]
    onehot = jnp.where(wrow == cols[0:1, :], 1.0, 0.0) + jnp.where(wrow == cols[1:2, :], 1.0, 0.0)
    w_ref[i % 2] = _mm(onehot, h)

    @pl.when(i >= 1)
    def _():
        all_chunks(i - 1, lambda cp: cp.wait())

    all_chunks(i, lambda cp: cp.start())

    @pl.when(i == n_steps - 1)
    def _():
        all_chunks(i, lambda cp: cp.wait())


def _dispatch(seg_start, n_chunks, win_pos, zero_start, zero_blk, cols_t, x, g, sub, rows, win):
    t, d = x.shape
    n_blk = zero_blk.shape[0]
    n_rows = n_blk * sub
    return pl.pallas_call(
        functools.partial(_dispatch_kernel, rows=rows, sub=sub, n_blk=n_blk, win=win),
        grid_spec=pltpu.PrefetchScalarGridSpec(
            num_scalar_prefetch=5,
            grid=(t // rows,),
            in_specs=[pl.BlockSpec((rows, d), lambda i, *_: (i, 0)),
                      pl.BlockSpec((1, d), lambda i, *_: (0, 0)),
                      pl.BlockSpec((2, rows), lambda i, *_: (0, i))],
            out_specs=pl.BlockSpec(memory_space=pl.ANY),
            scratch_shapes=[pltpu.VMEM((2, win, d), F32), pltpu.VMEM((sub, d), F32),
                            pltpu.SemaphoreType.DMA(()), pltpu.SemaphoreType.DMA(())]),
        out_shape=jax.ShapeDtypeStruct((n_rows, d), F32),
        compiler_params=_cparams(1),
        name="moe_dispatch",
    )(seg_start, n_chunks, win_pos, zero_start, zero_blk, x, g.reshape(1, d), cols_t)


def _expert_kernel(be_ref, ns_ref, xi_ref, x_ref, wg_ref, wu_ref, wd_ref, o_ref, *, sub, n_sub):
    i = pl.program_id(0)
    f = pl.program_id(1)
    ns = ns_ref[i]

    @pl.when(f == 0)
    def _():
        o_ref[...] = jnp.zeros_like(o_ref)

    def swiglu(rows):
        h = x_ref[rows, :]
        a = _mm(h, wg_ref[...])
        u = _mm(h, wu_ref[...])
        o_ref[rows, :] += _mm(_silu(a) * u, wd_ref[...])

    for s in range(0, n_sub, 2):
        if s + 2 <= n_sub:
            @pl.when(s + 2 <= ns)
            def _():
                swiglu(slice(s * sub, (s + 2) * sub))

        @pl.when(s + 1 == ns)
        def _():
            swiglu(slice(s * sub, (s + 1) * sub))


def _expert_ffn(blk_exp, n_valid_sub, x_blk, xs, wg, wu, wd, sb, sub, tf=512):
    p, d = xs.shape
    dff = wg.shape[2]
    n_super = n_valid_sub.shape[0]
    nf = dff // tf

    def fidx(i, f, ns):
        used = jnp.minimum(ns[i], 1)
        return f * used + (nf - 1) * (1 - used)

    return pl.pallas_call(
        functools.partial(_expert_kernel, sub=sub, n_sub=sb // sub),
        grid_spec=pltpu.PrefetchScalarGridSpec(
            num_scalar_prefetch=3,
            grid=(n_super, nf),
            in_specs=[pl.BlockSpec((sb, d), lambda i, f, be, ns, xi: (xi[i], 0)),
                      pl.BlockSpec((None, d, tf), lambda i, f, be, ns, xi: (be[i], 0, fidx(i, f, ns))),
                      pl.BlockSpec((None, d, tf), lambda i, f, be, ns, xi: (be[i], 0, fidx(i, f, ns))),
                      pl.BlockSpec((None, tf, d), lambda i, f, be, ns, xi: (be[i], fidx(i, f, ns), 0))],
            out_specs=pl.BlockSpec((sb, d), lambda i, f, be, ns, xi: (i, 0))),
        out_shape=jax.ShapeDtypeStruct((n_super * sb, d), F32),
        compiler_params=pltpu.CompilerParams(dimension_semantics=("arbitrary", "arbitrary"),
                                             vmem_limit_bytes=EXPERT_VMEM_LIMIT_BYTES),
        name="moe_experts",
    )(blk_exp, n_valid_sub, x_blk, xs, wg, wu, wd)


def _combine_kernel(st_ref, nc_ref, pos_ref, x_ref, info_ref, col_ref, ys_hbm, gf_ref, o_ref, buf_ref, sem,
                    *, rows, final_norm, blk_off, win):
    ck = MOE_CHUNK
    i = pl.program_id(0)

    def chunk_copy(step, e, c):
        t = (step + blk_off) * N_EXPERTS + e
        src = ys_hbm.at[pl.ds(pl.multiple_of(st_ref[t] + c * ck, 8), ck)]
        dst = buf_ref.at[step % 2, pl.ds(pl.multiple_of(pos_ref[t] + c * ck, ck), ck)]
        return pltpu.make_async_copy(src, dst, sem.at[step % 2])

    def all_chunks(step, fn):
        for e in range(N_EXPERTS):
            def body(c, carry, e=e):
                fn(chunk_copy(step, e, c))
                return carry
            lax.fori_loop(0, nc_ref[(step + blk_off) * N_EXPERTS + e], body, 0)

    @pl.when(i == 0)
    def _():
        buf_ref[...] = jnp.zeros_like(buf_ref)
        all_chunks(i, lambda cp: cp.start())

    @pl.when(i + 1 < pl.num_programs(0))
    def _():
        all_chunks(i + 1, lambda cp: cp.start())

    info = info_ref[...]
    cols = col_ref[...]
    lane = lax.broadcasted_iota(jnp.int32, (rows, win), 1)
    p = (jnp.where(cols[:, 0:1] == lane, info[:, 4:5], 0.0)
         + jnp.where(cols[:, 1:2] == lane, info[:, 5:6], 0.0)).astype(BF16)
    all_chunks(i, lambda cp: cp.wait())
    acc = x_ref[...] + _mm(p, buf_ref[i % 2])
    if final_norm:
        acc = _rms_rows(acc, gf_ref[...])
    o_ref[...] = acc


def _combine(seg_start, n_chunks, win_pos, cols, x, info, ys, gf, final_norm, row_off, n_rows, rows, win):
    d = x.shape[1]
    off = row_off // rows
    return pl.pallas_call(
        functools.partial(_combine_kernel, rows=rows, final_norm=final_norm, blk_off=off, win=win),
        grid_spec=pltpu.PrefetchScalarGridSpec(
            num_scalar_prefetch=3,
            grid=(n_rows // rows,),
            in_specs=[pl.BlockSpec((rows, d), lambda i, *_: (i + off, 0)),
                      pl.BlockSpec((rows, LANES), lambda i, *_: (i + off, 0)),
                      pl.BlockSpec((rows, 2), lambda i, *_: (i + off, 0)),
                      pl.BlockSpec(memory_space=pl.ANY),
                      pl.BlockSpec((1, d), lambda i, *_: (0, 0))],
            out_specs=pl.BlockSpec((rows, d), lambda i, *_: (i, 0)),
            scratch_shapes=[pltpu.VMEM((2, win, d), F32), pltpu.SemaphoreType.DMA((2,))]),
        out_shape=jax.ShapeDtypeStruct((n_rows, d), F32),
        compiler_params=_cparams(1),
        name="moe_combine",
    )(seg_start, n_chunks, win_pos, x, info, cols, ys, gf.reshape(1, d))


def _moe_ffn(x, g, whi, wlo, wg, wu, wd, gf, final_norm, groups, sub=512):
    t, d = x.shape
    sb = 2048 if 2 * t >= 16 * 1024 else sub
    tb = t
    for off, n in groups:
        tb = math.gcd(tb, math.gcd(off, n))
    tb = _row_tile(tb, 512)
    info, cnt = _router(x, g, whi, wlo, tb)
    e = info[:, 0:2].astype(jnp.int32)
    rank = info[:, 2:4].astype(jnp.int32)
    after = cnt[::8, :N_EXPERTS].astype(jnp.int32)
    before = jnp.concatenate([jnp.zeros((1, N_EXPERTS), jnp.int32), after[:-1]], axis=0)
    n_blocks = t // tb
    ck = MOE_CHUNK
    n_seg = after - before
    seg_len = (n_seg + 7) // 8 * 8
    seg_rel = jnp.cumsum(seg_len, axis=0) - seg_len
    counts = jnp.sum(seg_len, axis=0)
    n_sb = (counts + ck + sb - 1) // sb
    sb_end = jnp.cumsum(n_sb)
    sb_start = sb_end - n_sb
    row_start = sb_start * sb
    seg_start = row_start[None, :] + seg_rel
    n_chunks = (n_seg + ck - 1) // ck
    win_pos = (jnp.cumsum(n_chunks, axis=1) - n_chunks) * ck
    win = (2 * tb + N_EXPERTS * (ck - 1) + LANES - 1) // LANES * LANES
    eid = jnp.arange(N_EXPERTS, dtype=jnp.int32)
    tok_origin = jnp.repeat(win_pos - before, tb, axis=0)
    cols = rank + jnp.sum(jnp.where(e[:, :, None] == eid, tok_origin[:, None, :], 0), axis=-1)
    cols = cols.astype(jnp.int32)
    n_super = (2 * t + n_blocks * N_EXPERTS * 7) // sb + N_EXPERTS + 2
    blk = jnp.arange(n_super, dtype=jnp.int32)
    n_used = sb_end[-1]
    used = blk < n_used
    blk_c = jnp.minimum(blk, n_used - 1)
    be = jnp.minimum(jnp.sum((blk_c[:, None] >= sb_end[None, :]).astype(jnp.int32), axis=-1),
                     N_EXPERTS - 1)
    valid = jnp.clip(counts[be] - (blk_c - sb_start[be]) * sb, 0, sb)
    n_valid_sub = jnp.where(used, (valid + sub - 1) // sub, 0).astype(jnp.int32)
    zero_start = (row_start + counts).astype(jnp.int32)
    per = sb // sub
    sub_in_blk = jnp.arange(per, dtype=jnp.int32)
    zero_blk = (sub_in_blk[None, :] >= n_valid_sub[:, None]).astype(jnp.int32).reshape(-1)
    zero_blk = jnp.concatenate([zero_blk, jnp.ones((1,), jnp.int32)])
    tables = [a.reshape(-1).astype(jnp.int32) for a in (seg_start, n_chunks, win_pos)]
    xs = _dispatch(*tables, zero_start, zero_blk, cols.T, x, g, sub, tb, win)
    ys = _expert_ffn(be.astype(jnp.int32), n_valid_sub, blk_c.astype(jnp.int32), xs, wg, wu, wd, sb, sub)
    return [_combine(*tables, cols, x, info, ys, gf, final_norm, off, n, tb, win) for off, n in groups]


def _final_norm_kernel(x_ref, g_ref, o_ref):
    o_ref[...] = _rms_rows(x_ref[...], g_ref[...])


def _final_norm(x, g, row_off, n_rows, tm=512):
    d = x.shape[1]
    t = n_rows
    tm = _row_tile(t, tm)
    off = row_off // tm
    return pl.pallas_call(
        _final_norm_kernel,
        grid=(t // tm,),
        in_specs=[pl.BlockSpec((tm, d), lambda i: (i + off, 0)), pl.BlockSpec((1, d), lambda i: (0, 0))],
        out_specs=pl.BlockSpec((tm, d), lambda i: (i, 0)),
        out_shape=jax.ShapeDtypeStruct((t, d), F32),
        compiler_params=_cparams(1),
        name="final_norm",
    )(x, g.reshape(1, d))


def _block_diag(w):
    n, c, d = w.shape
    eye = jnp.eye(n, dtype=w.dtype)
    return jnp.einsum("ncd,nm->ncmd", w, eye).reshape(n * c, n * d)


def _pad_cols(w, n):
    return jnp.pad(w, ((0, 0), (0, n - w.shape[1])))


def _layer_params(l, p):
    (norm_mix, w_in, lru_conv_w, lru_conv_b, lru_a_w, lru_a_b, lru_x_w, lru_x_b, lru_lam, gla_gk_w2,
     gla_gk_b, gla_norm, gdn_conv_w, gdn_a_log, gdn_dt_bias, gdn_norm, w_out, norm_xq, norm_mem,
     w_mq, w_mk, w_mv, w_mo, norm_ffn) = [a[l] for a in p]
    lw = LRU_W
    gk, gv = GLA_HEADS * GLA_DK, GLA_HEADS * GLA_DV
    dh = GDN_HEADS * GDN_DK
    offs = [0]
    for s in (lw, lw, gk, gk, gv, GLA_RANK, gv, dh, dh, dh, GDN_HEADS, GDN_HEADS, dh):
        offs.append(offs[-1] + s)
    col = lambda i: w_in[:, offs[i]:offs[i + 1]]
    w_lru = jnp.concatenate([col(0), col(1)], axis=1)
    w_gla = jnp.concatenate([col(2), col(3), col(4), col(6), _pad_cols(col(5), LANES)], axis=1)
    w_gdn = jnp.concatenate(
        [col(7), col(8), col(9), col(12), _pad_cols(jnp.concatenate([col(10), col(11)], axis=1), LANES)],
        axis=1)
    w_cat = jnp.concatenate([w_lru, w_gla, w_gdn], axis=1).astype(BF16)
    widths = (w_lru.shape[1], w_gla.shape[1], w_gdn.shape[1])
    alog = jnp.zeros((1, LANES), F32).at[0, GDN_HEADS:2 * GDN_HEADS].set(gdn_a_log)
    dtb = jnp.zeros((1, LANES), F32).at[0, GDN_HEADS:2 * GDN_HEADS].set(gdn_dt_bias)
    return dict(
        norm_mix=norm_mix, w_cat=w_cat, widths=widths,
        lru_conv_w=lru_conv_w, lru_conv_b=lru_conv_b,
        lru_a=_block_diag(lru_a_w).astype(BF16), lru_a_b=lru_a_b,
        lru_x=_block_diag(lru_x_w).astype(BF16), lru_x_b=lru_x_b, lru_lam=lru_lam,
        gla_w2=jnp.pad(gla_gk_w2, ((0, LANES - GLA_RANK), (0, 0))).astype(BF16), gla_gk_b=gla_gk_b,
        gla_norm=jnp.tile(gla_norm, GLA_HEADS),
        gdn_conv_w=gdn_conv_w, gdn_alog=alog, gdn_dtb=dtb, gdn_norm=jnp.tile(gdn_norm, GDN_HEADS),
        w_out=w_out.astype(BF16), norm_xq=norm_xq, w_mq=w_mq.astype(BF16), w_mo=w_mo.astype(BF16),
        norm_ffn=norm_ffn)


def _mix_and_attend(x, x_off, grp, l, lp, joint, total_rows):
    batch, seq, n = grp["batch"], grp["seq"], grp["batch"] * grp["seq"]
    lru_h0, lru_buf0, gla_s0, gdn_s0, gdn_buf0 = grp["states"]
    sl = grp["state_layer"](l)
    z_lru, z_gla, z_gdn = _rms_matmul(x, lp["norm_mix"], lp["w_cat"], lp["widths"], "in_proj",
                                      row_off=x_off, n_rows=n)
    y_lru, lru_h, lru_buf = _lru_mixer(
        z_lru, lru_h0, lru_buf0, sl, lp["lru_conv_w"], lp["lru_conv_b"], lp["lru_a"], lp["lru_a_b"],
        lp["lru_x"], lp["lru_x_b"], lp["lru_lam"], batch, seq)
    y_gla, gla_s = _gla_mixer(z_gla.reshape(batch, seq, -1), gla_s0, sl, lp["gla_w2"],
                              lp["gla_gk_b"], lp["gla_norm"], batch, seq)
    y_gdn, gdn_s, gdn_buf = _gdn_mixer(z_gdn.reshape(batch, seq, -1), gdn_s0, gdn_buf0, sl,
                                       lp["gdn_conv_w"], lp["gdn_alog"], lp["gdn_dtb"],
                                       lp["gdn_norm"], batch, seq)
    xn, q = _outproj(x, y_lru, y_gla.reshape(n, -1), y_gdn.reshape(n, -1),
                     lp["w_out"], lp["norm_xq"], lp["w_mq"], row_off=x_off)
    mem_k, mem_v, mem_layer = grp["mem"][l]
    joint = _attention(xn, q, mem_k, mem_v, mem_layer, lp["w_mo"], batch, seq, joint, grp["row_off"],
                       total_rows)
    return joint, (lru_h.reshape(batch, LRU_W), lru_buf, gla_s, gdn_s, gdn_buf)


def _run_layers(groups, layers, ffn, norm_final):
    total_rows = sum(g["batch"] * g["seq"] for g in groups)
    spans = [(g["row_off"], g["batch"] * g["seq"]) for g in groups]
    xs = [(g["x"], 0) for g in groups]
    new_states = [[] for _ in groups]
    outs = None
    for l, lp in enumerate(layers):
        joint = None
        for gi, grp in enumerate(groups):
            joint, st = _mix_and_attend(xs[gi][0], xs[gi][1], grp, l, lp, joint, total_rows)
            new_states[gi].append(st)
        last = l == len(layers) - 1
        kind, fp = ffn[l]
        if kind == "dense":
            joint = _ffn_dense(joint, lp["norm_ffn"], *fp)
            if last:
                outs = [_final_norm(joint, norm_final, off, n) for off, n in spans]
        elif last:
            outs = _moe_ffn(joint, lp["norm_ffn"], *fp, norm_final, True, spans)
        else:
            joint = _moe_ffn(joint, lp["norm_ffn"], *fp, norm_final, False, [(0, total_rows)])[0]
        xs = [(joint, g["row_off"]) for g in groups]
    states = [[jnp.stack(s) for s in zip(*ns)] for ns in new_states]
    return outs, states


def kernel(x_prompt, x_sample, mem_prompt, state_lru_h, state_lru_conv, state_gla, state_gdn, state_gdn_conv, cache_mem_k, cache_mem_v, norm_mix, w_in, lru_conv_w, lru_conv_b, lru_a_w, lru_a_b, lru_x_w, lru_x_b, lru_lam, gla_gk_w2, gla_gk_b, gla_norm, gdn_conv_w, gdn_a_log, gdn_dt_bias, gdn_norm, w_out, norm_xq, norm_mem, w_mq, w_mk, w_mv, w_mo, norm_ffn, w_ff_gate, w_ff_up, w_ff_down, w_router, w_e_gate, w_e_up, w_e_down, norm_final):
    depth = norm_mix.shape[0]
    per_layer = (norm_mix, w_in, lru_conv_w, lru_conv_b, lru_a_w, lru_a_b, lru_x_w, lru_x_b, lru_lam,
                 gla_gk_w2, gla_gk_b, gla_norm, gdn_conv_w, gdn_a_log, gdn_dt_bias, gdn_norm, w_out,
                 norm_xq, norm_mem, w_mq, w_mk, w_mv, w_mo, norm_ffn)
    layers = [_layer_params(l, per_layer) for l in range(depth)]
    ffn = []
    for l in range(depth):
        j = l // 2
        if l % 2 == 0:
            ffn.append(("dense", (w_ff_gate[j], w_ff_up[j], w_ff_down[j])))
        else:
            wr = _pad_cols(w_router[j], LANES)
            whi = wr.astype(BF16)
            wlo = (wr - whi.astype(F32)).astype(BF16)
            ffn.append(("moe", (whi, wlo, w_e_gate[j], w_e_up[j], w_e_down[j])))

    bp, mlen, d = mem_prompt.shape
    w_mkv = jnp.concatenate([w_mk, w_mv], axis=2).astype(BF16)
    pk, pv, p_mem_k, p_mem_v = _mem_kv(mem_prompt, norm_mem, w_mkv)
    zero_state = (jnp.zeros((1, bp, LRU_W), F32), jnp.zeros((1, bp, CONV_K - 1, LRU_W), F32),
                  jnp.zeros((1, bp, GLA_HEADS, GLA_DK, GLA_DV), F32),
                  jnp.zeros((1, bp, GDN_HEADS, GDN_DK, GDN_DV), F32),
                  jnp.zeros((1, bp, CONV_K - 1, 3 * GDN_HEADS * GDN_DK), F32))
    sp = x_prompt.shape[1]
    bs, ss = x_sample.shape[0], x_sample.shape[1]
    groups = [
        dict(x=x_prompt.reshape(bp * sp, d), batch=bp, seq=sp, row_off=0, states=zero_state,
             state_layer=lambda l: 0, mem=[(pk, pv, l) for l in range(depth)]),
        dict(x=x_sample.reshape(bs * ss, d), batch=bs, seq=ss, row_off=bp * sp,
             states=(state_lru_h, state_lru_conv, state_gla, state_gdn, state_gdn_conv),
             state_layer=lambda l: l, mem=[(cache_mem_k, cache_mem_v, l) for l in range(depth)]),
    ]
    (y_p, y_s), (p_st, s_st) = _run_layers(groups, layers, ffn, norm_final)

    return (y_p.reshape(bp, sp, d), y_s.reshape(bs, ss, d), p_st[0], p_st[1], p_st[2], p_st[3], p_st[4],
            p_mem_k, p_mem_v, s_st[0], s_st[1], s_st[2], s_st[3], s_st[4])
```

```python
import functools
import math

import jax
import jax.numpy as jnp
from jax import lax
from jax.experimental import pallas as pl
from jax.experimental.pallas import tpu as pltpu

F32 = jnp.float32
BF16 = jnp.bfloat16
EPS = 1e-6

D_MODEL = 1024
LRU_W = 512
LRU_BLOCKS = 8
LRU_C = 8.0
CONV_K = 4
GLA_HEADS = 4
GLA_DK = 32
GLA_DV = 64
GLA_RANK = 16
GLA_TAU = 16.0
GLA_SUB = 16
GDN_HEADS = 4
GDN_DK = 64
GDN_DV = 64
MIX_CHUNK = 64
MEM_HEADS = 4
MEM_HD = 256
N_EXPERTS = 8
LANES = 128
VMEM_LIMIT_BYTES = 48 * 1024 * 1024
EXPERT_VMEM_LIMIT_BYTES = 56 * 1024 * 1024


def _cparams(n_axes):
    return pltpu.CompilerParams(dimension_semantics=("arbitrary",) * n_axes,
                                vmem_limit_bytes=VMEM_LIMIT_BYTES)


def _mm(a, b):
    return jnp.dot(a.astype(BF16), b.astype(BF16), preferred_element_type=F32)


def _mm_nt(a, b):
    return lax.dot_general(a.astype(BF16), b.astype(BF16), (((1,), (1,)), ((), ())),
                           preferred_element_type=F32)


def _mm_tn(a, b):
    return lax.dot_general(a.astype(BF16), b.astype(BF16), (((0,), (0,)), ((), ())),
                           preferred_element_type=F32)


def _rms_rows(x, g):
    ms = jnp.mean(x * x, axis=-1, keepdims=True)
    return (x * lax.rsqrt(ms + EPS)) * g


def _softplus(x):
    return jnp.maximum(x, 0.0) + jnp.log1p(jnp.exp(-jnp.abs(x)))


def _sigmoid(x):
    return 1.0 / (1.0 + jnp.exp(-x))


def _silu(x):
    return x * _sigmoid(x)


def _gelu_tanh(x):
    c = 0.7978845608028654
    return x * (0.5 * (1.0 + jnp.tanh(c * (x + 0.044715 * (x * x * x)))))


def _seg_cumsum_rows(x, seg):
    rows = x.shape[0]
    tpos = lax.broadcasted_iota(jnp.int32, (rows, 1), 0) & (seg - 1)
    d = 1
    while d < seg:
        x = x + jnp.where(tpos >= d, pltpu.roll(x, d, axis=0), 0.0)
        d *= 2
    return x


def _head_rms(o, gain, n_heads, width):
    lane_head = lax.broadcasted_iota(jnp.int32, (1, n_heads * width), 1) // width
    sq = o * o
    inv = jnp.zeros_like(o)
    for h in range(n_heads):
        m = lane_head == h
        ms = jnp.sum(jnp.where(m, sq, 0.0), axis=-1, keepdims=True) * (1.0 / width)
        inv = jnp.where(m, lax.rsqrt(ms + EPS), inv)
    return (o * inv) * gain


def _stack_heads(x, n_heads, width):
    c = x.shape[0]
    t = jnp.concatenate([x] * n_heads, axis=0)
    row_head = lax.broadcasted_iota(jnp.int32, (n_heads * c, 1), 0) // c
    lane_head = lax.broadcasted_iota(jnp.int32, (1, n_heads * width), 1) // width
    return jnp.where(row_head == lane_head, t, 0.0)


def _unstack_heads(x, n_heads):
    c = x.shape[0] // n_heads
    o = x[0:c]
    for h in range(1, n_heads):
        o = o + x[h * c:(h + 1) * c]
    return o


def _rms_matmul_kernel(x_ref, g_ref, w_ref, *o_refs):
    h = _rms_rows(x_ref[...], g_ref[...]).astype(BF16)
    start = 0
    for o_ref in o_refs:
        n = o_ref.shape[1]
        o_ref[...] = jnp.dot(h, w_ref[:, start:start + n], preferred_element_type=F32)
        start += n


def _row_tile(t, pref):
    tile = min(pref, t)
    while t % tile or tile % 8:
        tile -= 8
    return tile


def _time_major_spec(tm, width, batch, seq):
    nl = seq // tm
    return (pl.BlockSpec((tm, width), lambda i: (i % nl, i // nl)),
            jax.ShapeDtypeStruct((seq, batch * width), F32))


def _rms_matmul(x, g, w, widths, name, tm=512, row_off=0, n_rows=None, time_major=None):
    d = x.shape[1]
    t = x.shape[0] if n_rows is None else n_rows
    n = w.shape[1]
    tm = _row_tile(t, tm)
    off = row_off // tm
    out_specs = [pl.BlockSpec((tm, wd), lambda i: (i, 0)) for wd in widths]
    out_shape = [jax.ShapeDtypeStruct((t, wd), F32) for wd in widths]
    if time_major is not None:
        out_specs[0], out_shape[0] = _time_major_spec(tm, widths[0], *time_major)
    return pl.pallas_call(
        _rms_matmul_kernel,
        grid=(t // tm,),
        in_specs=[pl.BlockSpec((tm, d), lambda i: (i + off, 0)),
                  pl.BlockSpec((1, d), lambda i: (0, 0)),
                  pl.BlockSpec((d, n), lambda i: (0, 0))],
        out_specs=out_specs,
        out_shape=out_shape,
        compiler_params=_cparams(1),
        name=name,
    )(x, g.reshape(1, d), w)


def _mem_kv_kernel(m_ref, g_ref, w_ref, k_ref, v_ref, k5_ref, v5_ref):
    d = m_ref.shape[1]
    h = _rms_rows(m_ref[...], g_ref[...])
    kv = _mm(h, w_ref[...])
    k_ref[...] = kv[:, :d].astype(k_ref.dtype)
    v_ref[...] = kv[:, d:].astype(v_ref.dtype)
    for hd in range(MEM_HEADS):
        sl = slice(hd * MEM_HD, (hd + 1) * MEM_HD)
        k5_ref[:, hd, :] = kv[:, sl]
        v5_ref[:, hd, :] = kv[:, d + hd * MEM_HD:d + (hd + 1) * MEM_HD]


def _mem_kv(mem, g, w):
    bp, mlen, d = mem.shape
    depth = w.shape[0]
    flat = jax.ShapeDtypeStruct((depth, bp, mlen, d), BF16)
    split = jax.ShapeDtypeStruct((depth, bp, mlen, MEM_HEADS, MEM_HD), F32)
    fidx = lambda l, b: (l, b, 0, 0)
    sidx = lambda l, b: (l, b, 0, 0, 0)
    return pl.pallas_call(
        _mem_kv_kernel,
        grid=(depth, bp),
        in_specs=[pl.BlockSpec((None, mlen, d), lambda l, b: (b, 0, 0)),
                  pl.BlockSpec((None, 1, d), lambda l, b: (l, 0, 0)),
                  pl.BlockSpec((None, d, 2 * d), lambda l, b: (l, 0, 0))],
        out_specs=[pl.BlockSpec((None, None, mlen, d), fidx),
                   pl.BlockSpec((None, None, mlen, d), fidx),
                   pl.BlockSpec((None, None, mlen, MEM_HEADS, MEM_HD), sidx),
                   pl.BlockSpec((None, None, mlen, MEM_HEADS, MEM_HD), sidx)],
        out_shape=[flat, flat, split, split],
        compiler_params=_cparams(2),
        name="mem_kv",
    )(mem, g.reshape(depth, 1, d), w)


def _lru_coeffs(xc, aw_ref, ab_ref, xw_ref, xb_ref, lam_ref):
    r = _sigmoid(_mm(xc, aw_ref[...]) + ab_ref[...])
    ig = _sigmoid(_mm(xc, xw_ref[...]) + xb_ref[...])
    log_a = (-LRU_C * r) * _softplus(-lam_ref[...])
    a = jnp.exp(log_a)
    th = jnp.tanh(log_a)
    return a, jnp.sqrt((-2.0 * th) / (1.0 - th)) * (ig * xc)


def _lru_kernel(z_ref, h0_ref, buf0_ref, cw_ref, cb_ref, aw_ref, ab_ref, xw_ref, xb_ref, lam_ref,
                y_ref, hout_ref, bufout_ref, xs_ref, hc_ref, *, bt, tl, nt):
    w = LRU_W
    rows = bt * tl
    j = pl.program_id(0) % nt

    @pl.when(j == 0)
    def _():
        xs_ref[:, 5:8, :] = buf0_ref[...]
        hc_ref[...] = h0_ref[...]

    xs_ref[:, 8:, :] = z_ref[:, :w].reshape(bt, tl, w)
    gate = z_ref[:, w:]
    cw = cw_ref[...]
    xc = cb_ref[...] + xs_ref[:, 5:5 + tl, :] * cw[0:1]
    for k in range(1, CONV_K):
        xc = xc + xs_ref[:, 5 + k:5 + k + tl, :] * cw[k:k + 1]
    tail = xs_ref[:, 5 + tl:8 + tl, :]
    xs_ref[:, 5:8, :] = tail
    bufout_ref[...] = tail

    a, b = _lru_coeffs(xc.reshape(rows, w), aw_ref, ab_ref, xw_ref, xb_ref, lam_ref)

    sub = 8
    gps = tl // sub
    a3 = a.reshape(rows // sub, sub, w)
    b3 = b.reshape(rows // sub, sub, w)
    spos = lax.broadcasted_iota(jnp.int32, (1, sub, 1), 1)
    d = 1
    while d < sub:
        m = spos >= d
        b3 = jnp.where(m, a3 * pltpu.roll(b3, d, axis=1) + b3, b3)
        a3 = jnp.where(m, a3 * pltpu.roll(a3, d, axis=1), a3)
        d *= 2
    a4 = a3.reshape(bt, gps, sub, w)
    b4 = b3.reshape(bt, gps, sub, w)
    carry = hc_ref[...]
    hs = []
    for r in range(gps):
        hr = b4[:, r] + a4[:, r] * carry
        hs.append(hr)
        carry = hr[:, sub - 1:sub, :]
    h = jnp.stack(hs, axis=1).reshape(rows, w)
    hlast = carry
    hc_ref[...] = hlast
    hout_ref[...] = hlast
    y_ref[...] = h * _gelu_tanh(gate)


def _lru_tm_kernel(z_ref, h0_ref, buf0_ref, cw_ref, cb_ref, aw_ref, ab_ref, xw_ref, xb_ref, lam_ref,
                   y_ref, hout_ref, bufout_ref, xs_ref, hc_ref, *, bt, tl):
    w = LRU_W

    @pl.when(pl.program_id(1) == 0)
    def _():
        for k in range(CONV_K - 1):
            xs_ref[k] = buf0_ref[:, k, :]
        hc_ref[...] = h0_ref[...]

    xs_ref[CONV_K - 1:] = z_ref[:, :, :w]
    cw = cw_ref[...]
    xc = cb_ref[...] + xs_ref[0:tl] * cw[0:1]
    for k in range(1, CONV_K):
        xc = xc + xs_ref[k:k + tl] * cw[k:k + 1]
    for k in range(CONV_K - 1):
        tail = xs_ref[tl + k]
        xs_ref[k] = tail
        bufout_ref[:, k, :] = tail

    a, b = _lru_coeffs(xc.reshape(tl * bt, w), aw_ref, ab_ref, xw_ref, xb_ref, lam_ref)
    a = a.reshape(tl, bt, w)
    b = b.reshape(tl, bt, w)
    h = hc_ref[...]
    hs = []
    for t in range(tl):
        h = a[t] * h + b[t]
        hs.append(h)
    hc_ref[...] = h
    hout_ref[...] = h
    y_ref[...] = jnp.stack(hs, axis=0) * _gelu_tanh(z_ref[:, :, w:])


def _lru_time_major(batch, seq):
    return batch % 8 == 0 and seq % 512 == 0


def _lru_mixer_tm(z, h0, buf0, layer, cw, cb, aw, ab, xw, xb, lam, batch, seq):
    w = LRU_W
    bt, tl = 8, 64
    full2 = lambda i, j: (0, 0)
    y, hout, bufout = pl.pallas_call(
        functools.partial(_lru_tm_kernel, bt=bt, tl=tl),
        grid=(batch // bt, seq // tl),
        in_specs=[pl.BlockSpec((tl, bt, 2 * w), lambda i, j: (j, i, 0)),
                  pl.BlockSpec((None, bt, w), lambda i, j: (layer, i, 0)),
                  pl.BlockSpec((None, bt, CONV_K - 1, w), lambda i, j: (layer, i, 0, 0)),
                  pl.BlockSpec((CONV_K, w), full2),
                  pl.BlockSpec((1, w), full2),
                  pl.BlockSpec((w, w), full2),
                  pl.BlockSpec((1, w), full2),
                  pl.BlockSpec((w, w), full2),
                  pl.BlockSpec((1, w), full2),
                  pl.BlockSpec((1, w), full2)],
        out_specs=[pl.BlockSpec((tl, bt, w), lambda i, j: (j, i, 0)),
                   pl.BlockSpec((bt, w), lambda i, j: (i, 0)),
                   pl.BlockSpec((bt, CONV_K - 1, w), lambda i, j: (i, 0, 0))],
        out_shape=[jax.ShapeDtypeStruct((seq, batch, w), F32),
                   jax.ShapeDtypeStruct((batch, w), F32),
                   jax.ShapeDtypeStruct((batch, CONV_K - 1, w), F32)],
        scratch_shapes=[pltpu.VMEM((CONV_K - 1 + tl, bt, w), F32), pltpu.VMEM((bt, w), F32)],
        compiler_params=_cparams(2),
        name="lru_mixer_tm",
    )(z.reshape(seq, batch, 2 * w), h0, buf0, cw, cb.reshape(1, w), aw, ab.reshape(1, w), xw,
      xb.reshape(1, w), lam.reshape(1, w))
    return y.reshape(seq, batch * w), hout, bufout


def _lru_mixer(z, h0, buf0, layer, cw, cb, aw, ab, xw, xb, lam, batch, seq):
    w = LRU_W
    tl = min(seq, 256)
    bt = min(batch, max(1, 256 // seq))
    nt = seq // tl
    rows = bt * tl
    grid = (batch * seq // rows,)
    if nt > 1:
        sidx = lambda i: (i // nt, 0, 0)
        lidx = lambda i: (layer, i // nt, 0, 0)
    else:
        sidx = lambda i: (i, 0, 0)
        lidx = lambda i: (layer, i, 0, 0)
    full2 = lambda i: (0, 0)
    return pl.pallas_call(
        functools.partial(_lru_kernel, bt=bt, tl=tl, nt=nt),
        grid=grid,
        in_specs=[pl.BlockSpec((rows, 2 * w), lambda i: (i, 0)),
                  pl.BlockSpec((None, bt, 1, w), lidx),
                  pl.BlockSpec((None, bt, CONV_K - 1, w), lidx),
                  pl.BlockSpec((CONV_K, w), full2),
                  pl.BlockSpec((1, w), full2),
                  pl.BlockSpec((w, w), full2),
                  pl.BlockSpec((1, w), full2),
                  pl.BlockSpec((w, w), full2),
                  pl.BlockSpec((1, w), full2),
                  pl.BlockSpec((1, w), full2)],
        out_specs=[pl.BlockSpec((rows, w), lambda i: (i, 0)),
                   pl.BlockSpec((bt, 1, w), sidx),
                   pl.BlockSpec((bt, CONV_K - 1, w), sidx)],
        out_shape=[jax.ShapeDtypeStruct((batch * seq, w), F32),
                   jax.ShapeDtypeStruct((batch, 1, w), F32),
                   jax.ShapeDtypeStruct((batch, CONV_K - 1, w), F32)],
        scratch_shapes=[pltpu.VMEM((bt, 8 + tl, w), F32), pltpu.VMEM((bt, 1, w), F32)],
        compiler_params=_cparams(1),
        name="lru_mixer",
    )(z, h0.reshape(h0.shape[0], batch, 1, w), buf0, cw, cb.reshape(1, w), aw, ab.reshape(1, w), xw,
      xb.reshape(1, w), lam.reshape(1, w))


def _gla_kernel(z_ref, s0_ref, w2_ref, gb_ref, gn_ref, y_ref, sout_ref, s_ref, *, c, sc, g):
    nh = GLA_HEADS
    kw = nh * GLA_DK
    vw = nh * GLA_DV

    @pl.when(pl.program_id(1) == 0)
    def _():
        s_ref[...] = jnp.zeros_like(s_ref)
        for b in range(g):
            for h in range(nh):
                s_ref[b, h * GLA_DK:(h + 1) * GLA_DK, h * GLA_DV:(h + 1) * GLA_DV] = s0_ref[b, h]

    bs = range(g)
    q = [z_ref[b, :, 0:kw] * (GLA_DK ** -0.5) for b in bs]
    k = [z_ref[b, :, kw:2 * kw] for b in bs]
    v = [z_ref[b, :, 2 * kw:2 * kw + vw] for b in bs]
    gk = [-_softplus(-(_mm(z_ref[b, :, 2 * kw + 2 * vw:], w2_ref[...]) + gb_ref[...])) / GLA_TAU
          for b in bs]
    gcum = [_seg_cumsum_rows(gk[b], sc) for b in bs]
    qp = [q[b] * jnp.exp(gcum[b]) for b in bs]

    ri = lax.broadcasted_iota(jnp.int32, (nh * c, nh * c), 0)
    ci = lax.broadcasted_iota(jnp.int32, (nh * c, nh * c), 1)
    keep = (ri // sc == ci // sc) & (ci <= ri)
    a = [_mm_nt(_stack_heads(qp[b], nh, GLA_DK), _stack_heads(k[b] * jnp.exp(-gcum[b]), nh, GLA_DK))
         for b in bs]
    o = [_unstack_heads(_mm(jnp.where(keep, a[b], 0.0), _stack_heads(v[b], nh, GLA_DV)), nh)
         for b in bs]

    s = [s_ref[b] for b in bs]
    eye = (lax.broadcasted_iota(jnp.int32, (kw, kw), 0) ==
           lax.broadcasted_iota(jnp.int32, (kw, kw), 1))
    bd = (lax.broadcasted_iota(jnp.int32, (kw, vw), 0) // GLA_DK ==
          lax.broadcasted_iota(jnp.int32, (kw, vw), 1) // GLA_DV)
    o_inter = [[] for _ in bs]
    for i in range(c // sc):
        lo, hi = i * sc, (i + 1) * sc
        for b in bs:
            o_inter[b].append(_mm(qp[b][lo:hi], s[b]))
            glast = gcum[b][hi - 1:hi]
            kpp = k[b][lo:hi] * jnp.exp(glast - gcum[b][lo:hi])
            u = _mm_tn(kpp, v[b][lo:hi])
            dcol = jnp.sum(jnp.where(eye, jnp.exp(glast), 0.0), axis=1, keepdims=True)
            s[b] = dcol * s[b] + jnp.where(bd, u, 0.0)
    for b in bs:
        s_ref[b] = s[b]
        for h in range(nh):
            sout_ref[b, h] = s[b][h * GLA_DK:(h + 1) * GLA_DK, h * GLA_DV:(h + 1) * GLA_DV]
        ob = o[b] + jnp.concatenate(o_inter[b], axis=0)
        gate = z_ref[b, :, 2 * kw + vw:2 * kw + 2 * vw]
        y_ref[b] = _head_rms(ob, gn_ref[...], nh, GLA_DV) * _silu(gate)


def _gla_mixer(z, s0, layer, w2, gb, gn, batch, seq):
    c = min(seq, MIX_CHUNK)
    sc = min(c, GLA_SUB)
    nt = seq // c
    g = _mixer_group(batch, c)
    kw, vw = GLA_HEADS * GLA_DK, GLA_HEADS * GLA_DV
    zw = z.shape[2]
    full2 = lambda i, j: (0, 0)
    sblk = (g, GLA_HEADS, GLA_DK, GLA_DV)
    return pl.pallas_call(
        functools.partial(_gla_kernel, c=c, sc=sc, g=g),
        grid=(batch // g, nt),
        in_specs=[pl.BlockSpec((g, c, zw), lambda i, j: (i, j, 0)),
                  pl.BlockSpec((None,) + sblk, lambda i, j: (layer, i, 0, 0, 0)),
                  pl.BlockSpec((LANES, kw), full2),
                  pl.BlockSpec((1, kw), full2),
                  pl.BlockSpec((1, vw), full2)],
        out_specs=[pl.BlockSpec((g, c, vw), lambda i, j: (i, j, 0)),
                   pl.BlockSpec(sblk, lambda i, j: (i, 0, 0, 0))],
        out_shape=[jax.ShapeDtypeStruct((batch, seq, vw), F32),
                   jax.ShapeDtypeStruct((batch, GLA_HEADS, GLA_DK, GLA_DV), F32)],
        scratch_shapes=[pltpu.VMEM((g, kw, vw), F32)],
        compiler_params=_cparams(2),
        name="gla_mixer",
    )(z, s0, w2, gb.reshape(1, kw), gn.reshape(1, vw))


def _gdn_prep(z_ref, cw_ref, alog_ref, dtb_ref, bufout_ref, xs_ref, *, c):
    nh = GDN_HEADS
    hw = nh * GDN_DK
    cw3 = 3 * hw

    xs_ref[8:, :] = z_ref[:, 0:cw3]
    cw = cw_ref[...]
    qkv = xs_ref[5:5 + c, :] * cw[0:1]
    for kk in range(1, CONV_K):
        qkv = qkv + xs_ref[5 + kk:5 + kk + c, :] * cw[kk:kk + 1]
    tail = xs_ref[5 + c:8 + c, :]
    xs_ref[5:8, :] = tail
    bufout_ref[...] = tail
    qkv = _silu(qkv)
    zg = z_ref[:, cw3:cw3 + hw]
    sm = z_ref[:, cw3 + hw:]

    lane_head = lax.broadcasted_iota(jnp.int32, (1, hw), 1) // GDN_DK

    def l2n(x):
        sq = x * x
        inv = jnp.zeros_like(x)
        for h in range(nh):
            m = lane_head == h
            ss = jnp.sum(jnp.where(m, sq, 0.0), axis=-1, keepdims=True)
            inv = jnp.where(m, lax.rsqrt(ss + EPS), inv)
        return x * inv

    q = l2n(qkv[:, 0:hw]) * (GDN_DK ** -0.5)
    k = l2n(qkv[:, hw:2 * hw])
    v = qkv[:, 2 * hw:3 * hw]
    beta = _sigmoid(sm)
    glog = -jnp.exp(alog_ref[...]) * _softplus(sm + dtb_ref[...])
    gcum = _seg_cumsum_rows(glog, c)

    n = nh * c
    bcol = jnp.concatenate([beta[:, h:h + 1] for h in range(nh)], axis=0)
    gcol = jnp.concatenate([gcum[:, nh + h:nh + h + 1] for h in range(nh)], axis=0)
    glast = jnp.concatenate(
        [jnp.broadcast_to(gcum[c - 1:c, nh + h:nh + h + 1], (c, 1)) for h in range(nh)], axis=0)
    ri = lax.broadcasted_iota(jnp.int32, (n, n), 0)
    ci = lax.broadcasted_iota(jnp.int32, (n, n), 1)
    grow = jnp.sum(jnp.where(ri == ci, gcol, 0.0), axis=0, keepdims=True)
    same = ri // c == ci // c
    incl = same & (ci <= ri)
    strict = same & (ci < ri)
    dec = jnp.where(incl, jnp.exp(jnp.where(incl, gcol - grow, 0.0)), 0.0)

    sdec = jnp.concatenate(
        [jnp.broadcast_to(jnp.exp(gcum[c - 1:c, nh + h:nh + h + 1]), (GDN_DK, 1)) for h in range(nh)],
        axis=0)
    return dict(ks=_stack_heads(k, nh, GDN_DK), qs=_stack_heads(q, nh, GDN_DK),
                vs=_stack_heads(v, nh, GDN_DV), bcol=bcol, egc=jnp.exp(gcol), dec=dec,
                kdec=jnp.exp(glast - gcol), sdec=sdec, zg=zg)


def _gdn_kernel(z_ref, s0_ref, buf0_ref, cw_ref, alog_ref, dtb_ref, gn_ref,
                y_ref, sout_ref, bufout_ref, xs_ref, s_ref, *, c, g):
    nh = GDN_HEADS
    hw = nh * GDN_DK
    n = nh * c

    @pl.when(pl.program_id(1) == 0)
    def _():
        xs_ref[:, 5:8, :] = buf0_ref[...]
        s_ref[...] = jnp.zeros_like(s_ref)
        for b in range(g):
            for h in range(nh):
                s_ref[b, h * GDN_DK:(h + 1) * GDN_DK, h * GDN_DV:(h + 1) * GDN_DV] = s0_ref[b, h]

    bs = range(g)
    pr = [_gdn_prep(z_ref.at[b], cw_ref, alog_ref, dtb_ref, bufout_ref.at[b], xs_ref.at[b], c=c)
          for b in bs]
    ri = lax.broadcasted_iota(jnp.int32, (n, n), 0)
    ci = lax.broadcasted_iota(jnp.int32, (n, n), 1)
    same = ri // c == ci // c
    incl = same & (ci <= ri)
    strict = same & (ci < ri)
    kq = [_mm_nt(jnp.concatenate([pr[b]["ks"], pr[b]["qs"]], axis=0), pr[b]["ks"]) for b in bs]

    p = [jnp.where(strict, -(pr[b]["bcol"] * kq[b][0:n]) * pr[b]["dec"], 0.0) for b in bs]
    tinv = [jnp.where(ri == ci, 1.0, 0.0) + p[b] for b in bs]
    span = 2
    while span < c:
        p = [_mm(p[b], p[b]) for b in bs]
        tinv = [tinv[b] + _mm(tinv[b], p[b]) for b in bs]
        span *= 2

    s = [s_ref[b] for b in bs]
    uw = [_mm(tinv[b], jnp.concatenate([pr[b]["vs"] * pr[b]["bcol"],
                                        pr[b]["ks"] * (pr[b]["bcol"] * pr[b]["egc"])], axis=1))
          for b in bs]
    qw = [_mm(jnp.concatenate([pr[b]["qs"] * pr[b]["egc"], uw[b][:, hw:]], axis=0), s[b])
          for b in bs]
    vnew = [uw[b][:, 0:hw] - qw[b][n:] for b in bs]
    av = [_mm(jnp.where(incl, kq[b][n:] * pr[b]["dec"], 0.0), vnew[b]) for b in bs]
    kv = [_mm_tn(pr[b]["ks"] * pr[b]["kdec"], vnew[b]) for b in bs]
    for b in bs:
        o = _unstack_heads(qw[b][0:n] + av[b], nh)
        sn = pr[b]["sdec"] * s[b] + kv[b]
        s_ref[b] = sn
        for h in range(nh):
            sout_ref[b, h] = sn[h * GDN_DK:(h + 1) * GDN_DK, h * GDN_DV:(h + 1) * GDN_DV]
        y_ref[b] = _head_rms(o, gn_ref[...], nh, GDN_DV) * _silu(pr[b]["zg"])


def _mixer_group(batch, c):
    return min(batch, max(8, 128 // c))


def _gdn_mixer(z, s0, buf0, layer, cw, alog, dtb, gn, batch, seq):
    c = min(seq, MIX_CHUNK)
    nt = seq // c
    g = _mixer_group(batch, c)
    hw = GDN_HEADS * GDN_DK
    zw = z.shape[2]
    full2 = lambda i, j: (0, 0)
    sblk = (g, GDN_HEADS, GDN_DK, GDN_DV)
    return pl.pallas_call(
        functools.partial(_gdn_kernel, c=c, g=g),
        grid=(batch // g, nt),
        in_specs=[pl.BlockSpec((g, c, zw), lambda i, j: (i, j, 0)),
                  pl.BlockSpec((None,) + sblk, lambda i, j: (layer, i, 0, 0, 0)),
                  pl.BlockSpec((None, g, CONV_K - 1, 3 * hw), lambda i, j: (layer, i, 0, 0)),
                  pl.BlockSpec((CONV_K, 3 * hw), full2),
                  pl.BlockSpec((1, LANES), full2),
                  pl.BlockSpec((1, LANES), full2),
                  pl.BlockSpec((1, hw), full2)],
        out_specs=[pl.BlockSpec((g, c, hw), lambda i, j: (i, j, 0)),
                   pl.BlockSpec(sblk, lambda i, j: (i, 0, 0, 0)),
                   pl.BlockSpec((g, CONV_K - 1, 3 * hw), lambda i, j: (i, 0, 0))],
        out_shape=[jax.ShapeDtypeStruct((batch, seq, hw), F32),
                   jax.ShapeDtypeStruct((batch, GDN_HEADS, GDN_DK, GDN_DV), F32),
                   jax.ShapeDtypeStruct((batch, CONV_K - 1, 3 * hw), F32)],
        scratch_shapes=[pltpu.VMEM((g, 8 + c, 3 * hw), F32), pltpu.VMEM((g, hw, hw), F32)],
        compiler_params=_cparams(2),
        name="gdn_mixer",
    )(z, s0, buf0, cw, alog, dtb, gn.reshape(1, hw))


def _outproj_kernel(x_ref, yl_ref, yg_ref, yd_ref, wo_ref, gq_ref, wq_ref, xn_ref, q_ref):
    lw = LRU_W
    gw = GLA_HEADS * GLA_DV
    y = _mm(yl_ref[...], wo_ref[0:lw, :])
    y = y + _mm(yg_ref[...], wo_ref[lw:lw + gw, :])
    y = y + _mm(yd_ref[...], wo_ref[lw + gw:, :])
    xn = x_ref[...] + y
    xn_ref[...] = xn
    q_ref[...] = _mm(_rms_rows(xn, gq_ref[...]), wq_ref[...])


def _outproj(x, yl, yg, yd, wo, gq, wq, tm=512, row_off=0, time_major=None):
    d = x.shape[1]
    t = yg.shape[0]
    tm = _row_tile(t, tm)
    off = row_off // tm
    row = lambda i: (i, 0)
    full2 = lambda i: (0, 0)
    if time_major is None:
        yl_spec = pl.BlockSpec((tm, yl.shape[1]), row)
    else:
        yl_spec = _time_major_spec(tm, LRU_W, *time_major)[0]
    return pl.pallas_call(
        _outproj_kernel,
        grid=(t // tm,),
        in_specs=[pl.BlockSpec((tm, d), lambda i: (i + off, 0)),
                  yl_spec,
                  pl.BlockSpec((tm, yg.shape[1]), row),
                  pl.BlockSpec((tm, yd.shape[1]), row),
                  pl.BlockSpec((d, d), full2),
                  pl.BlockSpec((1, d), full2),
                  pl.BlockSpec((d, d), full2)],
        out_specs=[pl.BlockSpec((tm, d), row), pl.BlockSpec((tm, d), row)],
        out_shape=[jax.ShapeDtypeStruct((t, d), F32), jax.ShapeDtypeStruct((t, d), F32)],
        compiler_params=_cparams(1),
        name="outproj_qproj",
    )(x, yl, yg, yd, wo, gq.reshape(1, d), wq)


def _attn_kernel(x_ref, q_ref, k_ref, v_ref, wo_ref, o_ref):
    hd = MEM_HD
    acc = x_ref[...]
    for h in range(MEM_HEADS):
        sl = slice(h * hd, (h + 1) * hd)
        s = _mm_nt(q_ref[:, sl], k_ref[:, sl]) * (hd ** -0.5)
        m = jnp.max(s, axis=-1, keepdims=True)
        p = jnp.exp(s - m)
        l = jnp.sum(p, axis=-1, keepdims=True)
        oh = _mm(p, v_ref[:, sl]) / l
        acc = acc + _mm(oh, wo_ref[sl, :])
    o_ref[...] = acc


def _attn_heads_kernel(q_ref, k_ref, v_ref, o_ref, *, gb, tl):
    nh, hd = MEM_HEADS, MEM_HD
    m = k_ref.shape[1]
    row_head = lax.broadcasted_iota(jnp.int32, (nh * tl, 1), 0) // tl
    col_head = lax.broadcasted_iota(jnp.int32, (1, m * nh), 1) % nh
    for b in range(gb):
        kf = k_ref[b].reshape(m * nh, hd)
        vf = v_ref[b].reshape(m * nh, hd)
        rows = slice(b * tl, (b + 1) * tl)
        qs = jnp.concatenate([q_ref[rows, h * hd:(h + 1) * hd] for h in range(nh)], axis=0)
        s = _mm_nt(qs, kf) * (hd ** -0.5)
        s = jnp.where(row_head == col_head, s, -jnp.inf)
        mx = jnp.max(s, axis=-1, keepdims=True)
        p = jnp.exp(s - mx)
        l = jnp.sum(p, axis=-1, keepdims=True)
        o = _mm(p, vf) / l
        for h in range(nh):
            o_ref[rows, h * hd:(h + 1) * hd] = o[h * tl:(h + 1) * tl]


def _oproj_kernel(x_ref, a_ref, wo_ref, *rest):
    o_ref = rest[-1]
    o_ref[...] = x_ref[...] + _mm(a_ref[...], wo_ref[...])


def _attn_first_kernel(x_ref, q_ref, k_ref, v_ref, wo_ref, o_ref, *, n_real):
    i = pl.program_id(0)

    @pl.when(i < n_real)
    def _():
        _attn_kernel(x_ref, q_ref, k_ref, v_ref, wo_ref, o_ref)

    @pl.when(i >= n_real)
    def _():
        o_ref[...] = jnp.zeros_like(o_ref)


def _attn_into_kernel(x_ref, q_ref, k_ref, v_ref, wo_ref, joint_ref, o_ref):
    _attn_kernel(x_ref, q_ref, k_ref, v_ref, wo_ref, o_ref)


def _attention(x, q, k, v, layer, wo, batch, seq, joint, row_off, total_rows):
    t, d = x.shape
    out_shape = jax.ShapeDtypeStruct((total_rows, d), F32)
    extra_specs = [] if joint is None else [pl.BlockSpec(memory_space=pl.ANY)]
    extra_args = () if joint is None else (joint,)
    if k.ndim == 4:
        tl = min(seq, 512)
        nl = seq // tl
        off = row_off // tl
        kblk = (None, None) + k.shape[2:]
        n_real = batch * nl
        if joint is None:
            assert row_off == 0
            n_steps = pl.cdiv(total_rows, tl)
            body = functools.partial(_attn_first_kernel, n_real=n_real)
        else:
            n_steps = n_real
            body = _attn_into_kernel
        kidx = lambda i: (layer, jnp.minimum(i // nl, batch - 1), 0, 0)
        row = lambda i: (jnp.minimum(i, n_real - 1), 0)
        return pl.pallas_call(
            body,
            grid=(n_steps,),
            in_specs=[pl.BlockSpec((tl, d), row),
                      pl.BlockSpec((tl, d), row),
                      pl.BlockSpec(kblk, kidx),
                      pl.BlockSpec(kblk, kidx),
                      pl.BlockSpec((d, d), lambda i: (0, 0))] + extra_specs,
            out_specs=pl.BlockSpec((tl, d), lambda i: (i + off, 0)),
            out_shape=out_shape,
            input_output_aliases={} if joint is None else {5: 0},
            compiler_params=_cparams(1),
            name="mem_attention",
        )(x, q, k, v, wo, *extra_args)
    gb = 4 if batch % 4 == 0 else 2
    kblk = (None, gb) + k.shape[2:]
    kidx = lambda i: (layer, i, 0, 0, 0)
    att = pl.pallas_call(
        functools.partial(_attn_heads_kernel, gb=gb, tl=seq),
        grid=(batch // gb,),
        in_specs=[pl.BlockSpec((gb * seq, d), lambda i: (i, 0)),
                  pl.BlockSpec(kblk, kidx),
                  pl.BlockSpec(kblk, kidx)],
        out_specs=pl.BlockSpec((gb * seq, d), lambda i: (i, 0)),
        out_shape=jax.ShapeDtypeStruct((t, d), F32),
        compiler_params=_cparams(1),
        name="mem_attention_heads",
    )(q, k, v)
    tm = _row_tile(t, 512)
    off = row_off // tm
    return pl.pallas_call(
        _oproj_kernel,
        grid=(t // tm,),
        in_specs=[pl.BlockSpec((tm, d), lambda i: (i, 0)),
                  pl.BlockSpec((tm, d), lambda i: (i, 0)),
                  pl.BlockSpec((d, d), lambda i: (0, 0))] + extra_specs,
        out_specs=pl.BlockSpec((tm, d), lambda i: (i + off, 0)),
        out_shape=out_shape,
        input_output_aliases={} if joint is None else {3: 0},
        compiler_params=_cparams(1),
        name="mem_oproj",
    )(x, att, wo, *extra_args)


def _ffn_kernel(x_ref, g_ref, wg_ref, wu_ref, wd_ref, o_ref, h_ref, acc_ref):
    f = pl.program_id(1)

    @pl.when(f == 0)
    def _():
        h_ref[...] = _rms_rows(x_ref[...], g_ref[...]).astype(BF16)
        acc_ref[...] = jnp.zeros_like(acc_ref)

    h = h_ref[...]
    a = _mm(h, wg_ref[...])
    u = _mm(h, wu_ref[...])
    acc_ref[...] += _mm(_silu(a) * u, wd_ref[...])

    @pl.when(f == pl.num_programs(1) - 1)
    def _():
        o_ref[...] = x_ref[...] + acc_ref[...]


def _ffn_dense(x, g, wg, wu, wd, tm=1024, tf=512):
    t, d = x.shape
    dff = wg.shape[1]
    tm = _row_tile(t, tm)
    return pl.pallas_call(
        _ffn_kernel,
        grid=(t // tm, dff // tf),
        in_specs=[pl.BlockSpec((tm, d), lambda i, f: (i, 0)),
                  pl.BlockSpec((1, d), lambda i, f: (0, 0)),
                  pl.BlockSpec((d, tf), lambda i, f: (0, f)),
                  pl.BlockSpec((d, tf), lambda i, f: (0, f)),
                  pl.BlockSpec((tf, d), lambda i, f: (f, 0))],
        out_specs=pl.BlockSpec((tm, d), lambda i, f: (i, 0)),
        out_shape=jax.ShapeDtypeStruct((t, d), F32),
        scratch_shapes=[pltpu.VMEM((tm, d), BF16), pltpu.VMEM((tm, d), F32)],
        compiler_params=_cparams(2),
        name="ffn_dense",
    )(x, g.reshape(1, d), wg, wu, wd)


def _router_kernel(x_ref, g_ref, whi_ref, wlo_ref, info_ref, cnt_ref, carry_ref, *, tm):
    i = pl.program_id(0)

    @pl.when(i == 0)
    def _():
        carry_ref[...] = jnp.zeros_like(carry_ref)

    h = _rms_rows(x_ref[...], g_ref[...])
    hhi = h.astype(BF16)
    hlo = (h - hhi.astype(F32)).astype(BF16)
    whi = whi_ref[...]
    logits = (jnp.dot(hhi, whi, preferred_element_type=F32)
              + jnp.dot(hlo, whi, preferred_element_type=F32)
              + jnp.dot(hhi, wlo_ref[...], preferred_element_type=F32))
    lane = lax.broadcasted_iota(jnp.int32, (tm, LANES), 1)
    neg = jnp.float32(-jnp.inf)
    logits = jnp.where(lane < N_EXPERTS, logits, neg)
    m1 = jnp.max(logits, axis=-1, keepdims=True)
    i1 = jnp.min(jnp.where(logits == m1, lane, LANES), axis=-1, keepdims=True)
    rest = jnp.where(lane == i1, neg, logits)
    m2 = jnp.max(rest, axis=-1, keepdims=True)
    i2 = jnp.min(jnp.where(rest == m2, lane, LANES), axis=-1, keepdims=True)
    e = jnp.exp(m2 - m1)
    g1 = 1.0 / (1.0 + e)
    g2 = e / (1.0 + e)
    oh1 = jnp.where(lane == i1, 1.0, 0.0)
    oh2 = jnp.where(lane == i2, 1.0, 0.0)
    oh = oh1 + oh2
    ri = lax.broadcasted_iota(jnp.int32, (tm, tm), 0)
    ci = lax.broadcasted_iota(jnp.int32, (tm, tm), 1)
    tri = jnp.where(ci < ri, 1.0, 0.0)
    before = _mm(tri, oh) + carry_ref[0:1, :]
    r1 = jnp.sum(oh1 * before, axis=-1, keepdims=True)
    r2 = jnp.sum(oh2 * before, axis=-1, keepdims=True)
    carry = carry_ref[0:1, :] + jnp.sum(oh, axis=0, keepdims=True)
    carry_ref[...] = jnp.broadcast_to(carry, carry_ref.shape)
    cnt_ref[...] = jnp.broadcast_to(carry, cnt_ref.shape)
    info = jnp.where(lane == 0, i1.astype(F32), 0.0)
    info = jnp.where(lane == 1, i2.astype(F32), info)
    info = jnp.where(lane == 2, r1, info)
    info = jnp.where(lane == 3, r2, info)
    info = jnp.where(lane == 4, g1, info)
    info = jnp.where(lane == 5, g2, info)
    info_ref[...] = info


def _router(x, g, whi, wlo, tm):
    t, d = x.shape
    return pl.pallas_call(
        functools.partial(_router_kernel, tm=tm),
        grid=(t // tm,),
        in_specs=[pl.BlockSpec((tm, d), lambda i: (i, 0)),
                  pl.BlockSpec((1, d), lambda i: (0, 0)),
                  pl.BlockSpec((d, LANES), lambda i: (0, 0)),
                  pl.BlockSpec((d, LANES), lambda i: (0, 0))],
        out_specs=[pl.BlockSpec((tm, LANES), lambda i: (i, 0)),
                   pl.BlockSpec((8, LANES), lambda i: (i, 0))],
        out_shape=[jax.ShapeDtypeStruct((t, LANES), F32),
                   jax.ShapeDtypeStruct((t // tm * 8, LANES), F32)],
        scratch_shapes=[pltpu.VMEM((8, LANES), F32)],
        compiler_params=_cparams(1),
        name="moe_router",
    )(x, g.reshape(1, d), whi, wlo)


MOE_CHUNK = 32


def _dispatch_kernel(st_ref, nc_ref, pos_ref, zs_ref, zf_ref, x_ref, g_ref, ct_ref, xs_hbm, w_ref, zbuf_ref,
                     zsem, sem, *, rows, sub, n_blk, win):
    i = pl.program_id(0)

    @pl.when(i == 0)
    def _():
        zbuf_ref[...] = jnp.zeros_like(zbuf_ref)
        for e in range(N_EXPERTS):
            dst = xs_hbm.at[pl.ds(pl.multiple_of(zs_ref[e], 8), sub)]
            pltpu.make_async_copy(zbuf_ref, dst, zsem).start()
            pltpu.make_async_copy(zbuf_ref, dst, zsem).wait()

        def fill(j, carry):
            @pl.when(zf_ref[j] == 1)
            def _():
                dst = xs_hbm.at[pl.ds(pl.multiple_of(j * sub, sub), sub)]
                pltpu.make_async_copy(zbuf_ref, dst, zsem).start()
            return carry

        def drain(j, carry):
            @pl.when(zf_ref[j] == 1)
            def _():
                dst = xs_hbm.at[pl.ds(pl.multiple_of(j * sub, sub), sub)]
                pltpu.make_async_copy(zbuf_ref, dst, zsem).wait()
            return carry

        lax.fori_loop(0, n_blk, fill, 0)
        lax.fori_loop(0, n_blk, drain, 0)

    ck = MOE_CHUNK
    n_steps = pl.num_programs(0)

    def chunk_copy(step, e, c):
        t = step * N_EXPERTS + e
        src = w_ref.at[step % 2, pl.ds(pl.multiple_of(pos_ref[t] + c * ck, ck), ck)]
        dst = xs_hbm.at[pl.ds(pl.multiple_of(st_ref[t] + c * ck, 8), ck)]
        return pltpu.make_async_copy(src, dst, sem)

    def all_chunks(step, fn):
        for e in range(N_EXPERTS):
            def body(c, carry, e=e):
                fn(chunk_copy(step, e, c))
                return carry
            lax.fori_loop(0, nc_ref[step * N_EXPERTS + e], body, 0)

    h = _rms_rows(x_ref[...], g_ref[...])
    wrow = lax.broadcasted_iota(jnp.int32, (win, rows), 0)
    cols = ct_ref[...]
    onehot = jnp.where(wrow == cols[0:1, :], 1.0, 0.0) + jnp.where(wrow == cols[1:2, :], 1.0, 0.0)
    w_ref[i % 2] = _mm(onehot, h)

    @pl.when(i >= 1)
    def _():
        all_chunks(i - 1, lambda cp: cp.wait())

    all_chunks(i, lambda cp: cp.start())

    @pl.when(i == n_steps - 1)
    def _():
        all_chunks(i, lambda cp: cp.wait())


def _dispatch(seg_start, n_chunks, win_pos, zero_start, zero_blk, cols_t, x, g, sub, rows, win):
    t, d = x.shape
    n_blk = zero_blk.shape[0]
    n_rows = n_blk * sub
    return pl.pallas_call(
        functools.partial(_dispatch_kernel, rows=rows, sub=sub, n_blk=n_blk, win=win),
        grid_spec=pltpu.PrefetchScalarGridSpec(
            num_scalar_prefetch=5,
            grid=(t // rows,),
            in_specs=[pl.BlockSpec((rows, d), lambda i, *_: (i, 0)),
                      pl.BlockSpec((1, d), lambda i, *_: (0, 0)),
                      pl.BlockSpec((2, rows), lambda i, *_: (0, i))],
            out_specs=pl.BlockSpec(memory_space=pl.ANY),
            scratch_shapes=[pltpu.VMEM((2, win, d), F32), pltpu.VMEM((sub, d), F32),
                            pltpu.SemaphoreType.DMA(()), pltpu.SemaphoreType.DMA(())]),
        out_shape=jax.ShapeDtypeStruct((n_rows, d), F32),
        compiler_params=_cparams(1),
        name="moe_dispatch",
    )(seg_start, n_chunks, win_pos, zero_start, zero_blk, x, g.reshape(1, d), cols_t)


def _expert_kernel(be_ref, ns_ref, xi_ref, x_ref, wg_ref, wu_ref, wd_ref, o_ref, *, sub, n_sub):
    i = pl.program_id(0)
    f = pl.program_id(1)
    ns = ns_ref[i]

    @pl.when(f == 0)
    def _():
        o_ref[...] = jnp.zeros_like(o_ref)

    def swiglu(rows):
        h = x_ref[rows, :]
        a = _mm(h, wg_ref[...])
        u = _mm(h, wu_ref[...])
        o_ref[rows, :] += _mm(_silu(a) * u, wd_ref[...])

    for s in range(0, n_sub, 2):
        if s + 2 <= n_sub:
            @pl.when(s + 2 <= ns)
            def _():
                swiglu(slice(s * sub, (s + 2) * sub))

        @pl.when(s + 1 == ns)
        def _():
            swiglu(slice(s * sub, (s + 1) * sub))


def _expert_ffn(blk_exp, n_valid_sub, x_blk, xs, wg, wu, wd, sb, sub, tf=512):
    p, d = xs.shape
    dff = wg.shape[2]
    n_super = n_valid_sub.shape[0]
    nf = dff // tf

    def fidx(i, f, ns):
        used = jnp.minimum(ns[i], 1)
        return f * used + (nf - 1) * (1 - used)

    return pl.pallas_call(
        functools.partial(_expert_kernel, sub=sub, n_sub=sb // sub),
        grid_spec=pltpu.PrefetchScalarGridSpec(
            num_scalar_prefetch=3,
            grid=(n_super, nf),
            in_specs=[pl.BlockSpec((sb, d), lambda i, f, be, ns, xi: (xi[i], 0)),
                      pl.BlockSpec((None, d, tf), lambda i, f, be, ns, xi: (be[i], 0, fidx(i, f, ns))),
                      pl.BlockSpec((None, d, tf), lambda i, f, be, ns, xi: (be[i], 0, fidx(i, f, ns))),
                      pl.BlockSpec((None, tf, d), lambda i, f, be, ns, xi: (be[i], fidx(i, f, ns), 0))],
            out_specs=pl.BlockSpec((sb, d), lambda i, f, be, ns, xi: (i, 0))),
        out_shape=jax.ShapeDtypeStruct((n_super * sb, d), F32),
        compiler_params=pltpu.CompilerParams(dimension_semantics=("arbitrary", "arbitrary"),
                                             vmem_limit_bytes=EXPERT_VMEM_LIMIT_BYTES),
        name="moe_experts",
    )(blk_exp, n_valid_sub, x_blk, xs, wg, wu, wd)


def _combine_kernel(st_ref, nc_ref, pos_ref, x_ref, info_ref, col_ref, ys_hbm, gf_ref, o_ref, buf_ref, sem,
                    *, rows, final_norm, blk_off, win):
    ck = MOE_CHUNK
    i = pl.program_id(0)

    def chunk_copy(step, e, c):
        t = (step + blk_off) * N_EXPERTS + e
        src = ys_hbm.at[pl.ds(pl.multiple_of(st_ref[t] + c * ck, 8), ck)]
        dst = buf_ref.at[step % 2, pl.ds(pl.multiple_of(pos_ref[t] + c * ck, ck), ck)]
        return pltpu.make_async_copy(src, dst, sem.at[step % 2])

    def all_chunks(step, fn):
        for e in range(N_EXPERTS):
            def body(c, carry, e=e):
                fn(chunk_copy(step, e, c))
                return carry
            lax.fori_loop(0, nc_ref[(step + blk_off) * N_EXPERTS + e], body, 0)

    @pl.when(i == 0)
    def _():
        buf_ref[...] = jnp.zeros_like(buf_ref)
        all_chunks(i, lambda cp: cp.start())

    @pl.when(i + 1 < pl.num_programs(0))
    def _():
        all_chunks(i + 1, lambda cp: cp.start())

    info = info_ref[...]
    cols = col_ref[...]
    lane = lax.broadcasted_iota(jnp.int32, (rows, win), 1)
    p = (jnp.where(cols[:, 0:1] == lane, info[:, 4:5], 0.0)
         + jnp.where(cols[:, 1:2] == lane, info[:, 5:6], 0.0)).astype(BF16)
    all_chunks(i, lambda cp: cp.wait())
    acc = x_ref[...] + _mm(p, buf_ref[i % 2])
    if final_norm:
        acc = _rms_rows(acc, gf_ref[...])
    o_ref[...] = acc


def _combine(seg_start, n_chunks, win_pos, cols, x, info, ys, gf, final_norm, row_off, n_rows, rows, win):
    d = x.shape[1]
    off = row_off // rows
    return pl.pallas_call(
        functools.partial(_combine_kernel, rows=rows, final_norm=final_norm, blk_off=off, win=win),
        grid_spec=pltpu.PrefetchScalarGridSpec(
            num_scalar_prefetch=3,
            grid=(n_rows // rows,),
            in_specs=[pl.BlockSpec((rows, d), lambda i, *_: (i + off, 0)),
                      pl.BlockSpec((rows, LANES), lambda i, *_: (i + off, 0)),
                      pl.BlockSpec((rows, 2), lambda i, *_: (i + off, 0)),
                      pl.BlockSpec(memory_space=pl.ANY),
                      pl.BlockSpec((1, d), lambda i, *_: (0, 0))],
            out_specs=pl.BlockSpec((rows, d), lambda i, *_: (i, 0)),
            scratch_shapes=[pltpu.VMEM((2, win, d), F32), pltpu.SemaphoreType.DMA((2,))]),
        out_shape=jax.ShapeDtypeStruct((n_rows, d), F32),
        compiler_params=_cparams(1),
        name="moe_combine",
    )(seg_start, n_chunks, win_pos, x, info, cols, ys, gf.reshape(1, d))


def _moe_ffn(x, g, whi, wlo, wg, wu, wd, gf, final_norm, groups, sub=512):
    t, d = x.shape
    sb = 2048 if 2 * t >= 16 * 1024 else sub
    tb = t
    for off, n in groups:
        tb = math.gcd(tb, math.gcd(off, n))
    tb = _row_tile(tb, 512)
    info, cnt = _router(x, g, whi, wlo, tb)
    e = info[:, 0:2].astype(jnp.int32)
    rank = info[:, 2:4].astype(jnp.int32)
    after = cnt[::8, :N_EXPERTS].astype(jnp.int32)
    before = jnp.concatenate([jnp.zeros((1, N_EXPERTS), jnp.int32), after[:-1]], axis=0)
    n_blocks = t // tb
    ck = MOE_CHUNK
    n_seg = after - before
    seg_len = (n_seg + 7) // 8 * 8
    seg_rel = jnp.cumsum(seg_len, axis=0) - seg_len
    counts = jnp.sum(seg_len, axis=0)
    n_sb = (counts + ck + sb - 1) // sb
    sb_end = jnp.cumsum(n_sb)
    sb_start = sb_end - n_sb
    row_start = sb_start * sb
    seg_start = row_start[None, :] + seg_rel
    n_chunks = (n_seg + ck - 1) // ck
    win_pos = (jnp.cumsum(n_chunks, axis=1) - n_chunks) * ck
    win = (2 * tb + N_EXPERTS * (ck - 1) + LANES - 1) // LANES * LANES
    eid = jnp.arange(N_EXPERTS, dtype=jnp.int32)
    tok_origin = jnp.repeat(win_pos - before, tb, axis=0)
    cols = rank + jnp.sum(jnp.where(e[:, :, None] == eid, tok_origin[:, None, :], 0), axis=-1)
    cols = cols.astype(jnp.int32)
    n_super = (2 * t + n_blocks * N_EXPERTS * 7) // sb + N_EXPERTS + 2
    blk = jnp.arange(n_super, dtype=jnp.int32)
    n_used = sb_end[-1]
    used = blk < n_used
    blk_c = jnp.minimum(blk, n_used - 1)
    be = jnp.minimum(jnp.sum((blk_c[:, None] >= sb_end[None, :]).astype(jnp.int32), axis=-1),
                     N_EXPERTS - 1)
    valid = jnp.clip(counts[be] - (blk_c - sb_start[be]) * sb, 0, sb)
    n_valid_sub = jnp.where(used, (valid + sub - 1) // sub, 0).astype(jnp.int32)
    zero_start = (row_start + counts).astype(jnp.int32)
    per = sb // sub
    sub_in_blk = jnp.arange(per, dtype=jnp.int32)
    zero_blk = (sub_in_blk[None, :] >= n_valid_sub[:, None]).astype(jnp.int32).reshape(-1)
    zero_blk = jnp.concatenate([zero_blk, jnp.ones((1,), jnp.int32)])
    tables = [a.reshape(-1).astype(jnp.int32) for a in (seg_start, n_chunks, win_pos)]
    xs = _dispatch(*tables, zero_start, zero_blk, cols.T, x, g, sub, tb, win)
    ys = _expert_ffn(be.astype(jnp.int32), n_valid_sub, blk_c.astype(jnp.int32), xs, wg, wu, wd, sb, sub)
    return [_combine(*tables, cols, x, info, ys, gf, final_norm, off, n, tb, win) for off, n in groups]


def _final_norm_kernel(x_ref, g_ref, o_ref):
    o_ref[...] = _rms_rows(x_ref[...], g_ref[...])


def _final_norm(x, g, row_off, n_rows, tm=512):
    d = x.shape[1]
    t = n_rows
    tm = _row_tile(t, tm)
    off = row_off // tm
    return pl.pallas_call(
        _final_norm_kernel,
        grid=(t // tm,),
        in_specs=[pl.BlockSpec((tm, d), lambda i: (i + off, 0)), pl.BlockSpec((1, d), lambda i: (0, 0))],
        out_specs=pl.BlockSpec((tm, d), lambda i: (i, 0)),
        out_shape=jax.ShapeDtypeStruct((t, d), F32),
        compiler_params=_cparams(1),
        name="final_norm",
    )(x, g.reshape(1, d))


def _block_diag(w):
    n, c, d = w.shape
    eye = jnp.eye(n, dtype=w.dtype)
    return jnp.einsum("ncd,nm->ncmd", w, eye).reshape(n * c, n * d)


def _pad_cols(w, n):
    return jnp.pad(w, ((0, 0), (0, n - w.shape[1])))


def _layer_params(l, p):
    (norm_mix, w_in, lru_conv_w, lru_conv_b, lru_a_w, lru_a_b, lru_x_w, lru_x_b, lru_lam, gla_gk_w2,
     gla_gk_b, gla_norm, gdn_conv_w, gdn_a_log, gdn_dt_bias, gdn_norm, w_out, norm_xq, norm_mem,
     w_mq, w_mk, w_mv, w_mo, norm_ffn) = [a[l] for a in p]
    lw = LRU_W
    gk, gv = GLA_HEADS * GLA_DK, GLA_HEADS * GLA_DV
    dh = GDN_HEADS * GDN_DK
    offs = [0]
    for s in (lw, lw, gk, gk, gv, GLA_RANK, gv, dh, dh, dh, GDN_HEADS, GDN_HEADS, dh):
        offs.append(offs[-1] + s)
    col = lambda i: w_in[:, offs[i]:offs[i + 1]]
    w_lru = jnp.concatenate([col(0), col(1)], axis=1)
    w_gla = jnp.concatenate([col(2), col(3), col(4), col(6), _pad_cols(col(5), LANES)], axis=1)
    w_gdn = jnp.concatenate(
        [col(7), col(8), col(9), col(12), _pad_cols(jnp.concatenate([col(10), col(11)], axis=1), LANES)],
        axis=1)
    w_cat = jnp.concatenate([w_lru, w_gla, w_gdn], axis=1).astype(BF16)
    widths = (w_lru.shape[1], w_gla.shape[1], w_gdn.shape[1])
    alog = jnp.zeros((1, LANES), F32).at[0, GDN_HEADS:2 * GDN_HEADS].set(gdn_a_log)
    dtb = jnp.zeros((1, LANES), F32).at[0, GDN_HEADS:2 * GDN_HEADS].set(gdn_dt_bias)
    return dict(
        norm_mix=norm_mix, w_cat=w_cat, widths=widths,
        lru_conv_w=lru_conv_w, lru_conv_b=lru_conv_b,
        lru_a=_block_diag(lru_a_w).astype(BF16), lru_a_b=lru_a_b,
        lru_x=_block_diag(lru_x_w).astype(BF16), lru_x_b=lru_x_b, lru_lam=lru_lam,
        gla_w2=jnp.pad(gla_gk_w2, ((0, LANES - GLA_RANK), (0, 0))).astype(BF16), gla_gk_b=gla_gk_b,
        gla_norm=jnp.tile(gla_norm, GLA_HEADS),
        gdn_conv_w=gdn_conv_w, gdn_alog=alog, gdn_dtb=dtb, gdn_norm=jnp.tile(gdn_norm, GDN_HEADS),
        w_out=w_out.astype(BF16), norm_xq=norm_xq, w_mq=w_mq.astype(BF16), w_mo=w_mo.astype(BF16),
        norm_ffn=norm_ffn)


def _mix_and_attend(x, x_off, grp, l, lp, joint, total_rows):
    batch, seq, n = grp["batch"], grp["seq"], grp["batch"] * grp["seq"]
    lru_h0, lru_buf0, gla_s0, gdn_s0, gdn_buf0 = grp["states"]
    sl = grp["state_layer"](l)
    tmaj = (batch, seq) if _lru_time_major(batch, seq) else None
    z_lru, z_gla, z_gdn = _rms_matmul(x, lp["norm_mix"], lp["w_cat"], lp["widths"], "in_proj",
                                      row_off=x_off, n_rows=n, time_major=tmaj)
    y_lru, lru_h, lru_buf = (_lru_mixer_tm if tmaj else _lru_mixer)(
        z_lru, lru_h0, lru_buf0, sl, lp["lru_conv_w"], lp["lru_conv_b"], lp["lru_a"], lp["lru_a_b"],
        lp["lru_x"], lp["lru_x_b"], lp["lru_lam"], batch, seq)
    y_gla, gla_s = _gla_mixer(z_gla.reshape(batch, seq, -1), gla_s0, sl, lp["gla_w2"],
                              lp["gla_gk_b"], lp["gla_norm"], batch, seq)
    y_gdn, gdn_s, gdn_buf = _gdn_mixer(z_gdn.reshape(batch, seq, -1), gdn_s0, gdn_buf0, sl,
                                       lp["gdn_conv_w"], lp["gdn_alog"], lp["gdn_dtb"],
                                       lp["gdn_norm"], batch, seq)
    xn, q = _outproj(x, y_lru, y_gla.reshape(n, -1), y_gdn.reshape(n, -1),
                     lp["w_out"], lp["norm_xq"], lp["w_mq"], row_off=x_off, time_major=tmaj)
    mem_k, mem_v, mem_layer = grp["mem"][l]
    joint = _attention(xn, q, mem_k, mem_v, mem_layer, lp["w_mo"], batch, seq, joint, grp["row_off"],
                       total_rows)
    return joint, (lru_h.reshape(batch, LRU_W), lru_buf, gla_s, gdn_s, gdn_buf)


def _run_layers(groups, layers, ffn, norm_final):
    total_rows = sum(g["batch"] * g["seq"] for g in groups)
    spans = [(g["row_off"], g["batch"] * g["seq"]) for g in groups]
    xs = [(g["x"], 0) for g in groups]
    new_states = [[] for _ in groups]
    outs = None
    for l, lp in enumerate(layers):
        joint = None
        for gi, grp in enumerate(groups):
            joint, st = _mix_and_attend(xs[gi][0], xs[gi][1], grp, l, lp, joint, total_rows)
            new_states[gi].append(st)
        last = l == len(layers) - 1
        kind, fp = ffn[l]
        if kind == "dense":
            joint = _ffn_dense(joint, lp["norm_ffn"], *fp)
            if last:
                outs = [_final_norm(joint, norm_final, off, n) for off, n in spans]
        elif last:
            outs = _moe_ffn(joint, lp["norm_ffn"], *fp, norm_final, True, spans)
        else:
            joint = _moe_ffn(joint, lp["norm_ffn"], *fp, norm_final, False, [(0, total_rows)])[0]
        xs = [(joint, g["row_off"]) for g in groups]
    states = [[jnp.stack(s) for s in zip(*ns)] for ns in new_states]
    return outs, states


def kernel(x_prompt, x_sample, mem_prompt, state_lru_h, state_lru_conv, state_gla, state_gdn, state_gdn_conv, cache_mem_k, cache_mem_v, norm_mix, w_in, lru_conv_w, lru_conv_b, lru_a_w, lru_a_b, lru_x_w, lru_x_b, lru_lam, gla_gk_w2, gla_gk_b, gla_norm, gdn_conv_w, gdn_a_log, gdn_dt_bias, gdn_norm, w_out, norm_xq, norm_mem, w_mq, w_mk, w_mv, w_mo, norm_ffn, w_ff_gate, w_ff_up, w_ff_down, w_router, w_e_gate, w_e_up, w_e_down, norm_final):
    depth = norm_mix.shape[0]
    per_layer = (norm_mix, w_in, lru_conv_w, lru_conv_b, lru_a_w, lru_a_b, lru_x_w, lru_x_b, lru_lam,
                 gla_gk_w2, gla_gk_b, gla_norm, gdn_conv_w, gdn_a_log, gdn_dt_bias, gdn_norm, w_out,
                 norm_xq, norm_mem, w_mq, w_mk, w_mv, w_mo, norm_ffn)
    layers = [_layer_params(l, per_layer) for l in range(depth)]
    ffn = []
    for l in range(depth):
        j = l // 2
        if l % 2 == 0:
            ffn.append(("dense", (w_ff_gate[j], w_ff_up[j], w_ff_down[j])))
        else:
            wr = _pad_cols(w_router[j], LANES)
            whi = wr.astype(BF16)
            wlo = (wr - whi.astype(F32)).astype(BF16)
            ffn.append(("moe", (whi, wlo, w_e_gate[j], w_e_up[j], w_e_down[j])))

    bp, mlen, d = mem_prompt.shape
    w_mkv = jnp.concatenate([w_mk, w_mv], axis=2).astype(BF16)
    pk, pv, p_mem_k, p_mem_v = _mem_kv(mem_prompt, norm_mem, w_mkv)
    zero_state = (jnp.zeros((1, bp, LRU_W), F32), jnp.zeros((1, bp, CONV_K - 1, LRU_W), F32),
                  jnp.zeros((1, bp, GLA_HEADS, GLA_DK, GLA_DV), F32),
                  jnp.zeros((1, bp, GDN_HEADS, GDN_DK, GDN_DV), F32),
                  jnp.zeros((1, bp, CONV_K - 1, 3 * GDN_HEADS * GDN_DK), F32))
    sp = x_prompt.shape[1]
    bs, ss = x_sample.shape[0], x_sample.shape[1]
    groups = [
        dict(x=x_prompt.reshape(bp * sp, d), batch=bp, seq=sp, row_off=0, states=zero_state,
             state_layer=lambda l: 0, mem=[(pk, pv, l) for l in range(depth)]),
        dict(x=x_sample.reshape(bs * ss, d), batch=bs, seq=ss, row_off=bp * sp,
             states=(state_lru_h, state_lru_conv, state_gla, state_gdn, state_gdn_conv),
             state_layer=lambda l: l, mem=[(cache_mem_k, cache_mem_v, l) for l in range(depth)]),
    ]
    (y_p, y_s), (p_st, s_st) = _run_layers(groups, layers, ffn, norm_final)

    return (y_p.reshape(bp, sp, d), y_s.reshape(bs, ss, d), p_st[0], p_st[1], p_st[2], p_st[3], p_st[4],
            p_mem_k, p_mem_v, s_st[0], s_st[1], s_st[2], s_st[3], s_st[4])
```

```python
import functools
import math

import jax
import jax.numpy as jnp
from jax import lax
from jax.experimental import pallas as pl
from jax.experimental.pallas import tpu as pltpu

F32 = jnp.float32
BF16 = jnp.bfloat16
EPS = 1e-6

D_MODEL = 1024
LRU_W = 512
LRU_BLOCKS = 8
LRU_C = 8.0
CONV_K = 4
GLA_HEADS = 4
GLA_DK = 32
GLA_DV = 64
GLA_RANK = 16
GLA_TAU = 16.0
GLA_SUB = 16
GDN_HEADS = 4
GDN_DK = 64
GDN_DV = 64
MIX_CHUNK = 64
MEM_HEADS = 4
MEM_HD = 256
N_EXPERTS = 8
LANES = 128
VMEM_LIMIT_BYTES = 48 * 1024 * 1024
EXPERT_VMEM_LIMIT_BYTES = 56 * 1024 * 1024


def _cparams(n_axes):
    return pltpu.CompilerParams(dimension_semantics=("arbitrary",) * n_axes,
                                vmem_limit_bytes=VMEM_LIMIT_BYTES)


def _mm(a, b):
    return jnp.dot(a.astype(BF16), b.astype(BF16), preferred_element_type=F32)


def _mm_nt(a, b):
    return lax.dot_general(a.astype(BF16), b.astype(BF16), (((1,), (1,)), ((), ())),
                           preferred_element_type=F32)


def _mm_tn(a, b):
    return lax.dot_general(a.astype(BF16), b.astype(BF16), (((0,), (0,)), ((), ())),
                           preferred_element_type=F32)


def _rms_rows(x, g):
    ms = jnp.mean(x * x, axis=-1, keepdims=True)
    return (x * lax.rsqrt(ms + EPS)) * g


def _softplus(x):
    return jnp.maximum(x, 0.0) + jnp.log1p(jnp.exp(-jnp.abs(x)))


def _sigmoid(x):
    return 1.0 / (1.0 + jnp.exp(-x))


def _silu(x):
    return x * _sigmoid(x)


def _gelu_tanh(x):
    c = 0.7978845608028654
    return x * (0.5 * (1.0 + jnp.tanh(c * (x + 0.044715 * (x * x * x)))))


def _seg_cumsum_rows(x, seg):
    rows = x.shape[0]
    tpos = lax.broadcasted_iota(jnp.int32, (rows, 1), 0) & (seg - 1)
    d = 1
    while d < seg:
        x = x + jnp.where(tpos >= d, pltpu.roll(x, d, axis=0), 0.0)
        d *= 2
    return x


def _head_rms(o, gain, n_heads, width):
    lane_head = lax.broadcasted_iota(jnp.int32, (1, n_heads * width), 1) // width
    sq = o * o
    inv = jnp.zeros_like(o)
    for h in range(n_heads):
        m = lane_head == h
        ms = jnp.sum(jnp.where(m, sq, 0.0), axis=-1, keepdims=True) * (1.0 / width)
        inv = jnp.where(m, lax.rsqrt(ms + EPS), inv)
    return (o * inv) * gain


def _stack_heads(x, n_heads, width):
    c = x.shape[0]
    t = jnp.concatenate([x] * n_heads, axis=0)
    row_head = lax.broadcasted_iota(jnp.int32, (n_heads * c, 1), 0) // c
    lane_head = lax.broadcasted_iota(jnp.int32, (1, n_heads * width), 1) // width
    return jnp.where(row_head == lane_head, t, 0.0)


def _unstack_heads(x, n_heads):
    c = x.shape[0] // n_heads
    o = x[0:c]
    for h in range(1, n_heads):
        o = o + x[h * c:(h + 1) * c]
    return o


def _rms_matmul_kernel(x_ref, g_ref, w_ref, *o_refs):
    h = _rms_rows(x_ref[...], g_ref[...]).astype(BF16)
    start = 0
    for o_ref in o_refs:
        n = o_ref.shape[1]
        o_ref[...] = jnp.dot(h, w_ref[:, start:start + n], preferred_element_type=F32)
        start += n


def _row_tile(t, pref):
    tile = min(pref, t)
    while t % tile or tile % 8:
        tile -= 8
    return tile


def _rms_matmul(x, g, w, widths, name, tm=512, row_off=0, n_rows=None):
    d = x.shape[1]
    t = x.shape[0] if n_rows is None else n_rows
    n = w.shape[1]
    tm = _row_tile(t, tm)
    off = row_off // tm
    out_specs = [pl.BlockSpec((tm, wd), lambda i: (i, 0)) for wd in widths]
    out_shape = [jax.ShapeDtypeStruct((t, wd), F32) for wd in widths]
    return pl.pallas_call(
        _rms_matmul_kernel,
        grid=(t // tm,),
        in_specs=[pl.BlockSpec((tm, d), lambda i: (i + off, 0)),
                  pl.BlockSpec((1, d), lambda i: (0, 0)),
                  pl.BlockSpec((d, n), lambda i: (0, 0))],
        out_specs=out_specs,
        out_shape=out_shape,
        compiler_params=_cparams(1),
        name=name,
    )(x, g.reshape(1, d), w)


def _mem_kv_kernel(m_ref, g_ref, w_ref, k_ref, v_ref, k5_ref, v5_ref):
    d = m_ref.shape[1]
    h = _rms_rows(m_ref[...], g_ref[...])
    kv = _mm(h, w_ref[...])
    k_ref[...] = kv[:, :d].astype(k_ref.dtype)
    v_ref[...] = kv[:, d:].astype(v_ref.dtype)
    for hd in range(MEM_HEADS):
        sl = slice(hd * MEM_HD, (hd + 1) * MEM_HD)
        k5_ref[:, hd, :] = kv[:, sl]
        v5_ref[:, hd, :] = kv[:, d + hd * MEM_HD:d + (hd + 1) * MEM_HD]


def _mem_kv(mem, g, w):
    bp, mlen, d = mem.shape
    depth = w.shape[0]
    flat = jax.ShapeDtypeStruct((depth, bp, mlen, d), BF16)
    split = jax.ShapeDtypeStruct((depth, bp, mlen, MEM_HEADS, MEM_HD), F32)
    fidx = lambda l, b: (l, b, 0, 0)
    sidx = lambda l, b: (l, b, 0, 0, 0)
    return pl.pallas_call(
        _mem_kv_kernel,
        grid=(depth, bp),
        in_specs=[pl.BlockSpec((None, mlen, d), lambda l, b: (b, 0, 0)),
                  pl.BlockSpec((None, 1, d), lambda l, b: (l, 0, 0)),
                  pl.BlockSpec((None, d, 2 * d), lambda l, b: (l, 0, 0))],
        out_specs=[pl.BlockSpec((None, None, mlen, d), fidx),
                   pl.BlockSpec((None, None, mlen, d), fidx),
                   pl.BlockSpec((None, None, mlen, MEM_HEADS, MEM_HD), sidx),
                   pl.BlockSpec((None, None, mlen, MEM_HEADS, MEM_HD), sidx)],
        out_shape=[flat, flat, split, split],
        compiler_params=_cparams(2),
        name="mem_kv",
    )(mem, g.reshape(depth, 1, d), w)


def _lru_coeffs(xc, aw_ref, ab_ref, xw_ref, xb_ref, lam_ref):
    r = _sigmoid(_mm(xc, aw_ref[...]) + ab_ref[...])
    ig = _sigmoid(_mm(xc, xw_ref[...]) + xb_ref[...])
    log_a = (-LRU_C * r) * _softplus(-lam_ref[...])
    a = jnp.exp(log_a)
    th = jnp.tanh(log_a)
    return a, jnp.sqrt((-2.0 * th) / (1.0 - th)) * (ig * xc)


def _lru_kernel(z_ref, h0_ref, buf0_ref, cw_ref, cb_ref, aw_ref, ab_ref, xw_ref, xb_ref, lam_ref,
                y_ref, hout_ref, bufout_ref, xs_ref, hc_ref, *, bt, tl, nt):
    w = LRU_W
    rows = bt * tl
    j = pl.program_id(0) % nt

    @pl.when(j == 0)
    def _():
        xs_ref[:, 5:8, :] = buf0_ref[...]
        hc_ref[...] = h0_ref[...]

    xs_ref[:, 8:, :] = z_ref[:, :w].reshape(bt, tl, w)
    gate = z_ref[:, w:]
    cw = cw_ref[...]
    xc = cb_ref[...] + xs_ref[:, 5:5 + tl, :] * cw[0:1]
    for k in range(1, CONV_K):
        xc = xc + xs_ref[:, 5 + k:5 + k + tl, :] * cw[k:k + 1]
    tail = xs_ref[:, 5 + tl:8 + tl, :]
    xs_ref[:, 5:8, :] = tail
    bufout_ref[...] = tail

    a, b = _lru_coeffs(xc.reshape(rows, w), aw_ref, ab_ref, xw_ref, xb_ref, lam_ref)

    sub = 8
    gps = tl // sub
    a3 = a.reshape(rows // sub, sub, w)
    b3 = b.reshape(rows // sub, sub, w)
    spos = lax.broadcasted_iota(jnp.int32, (1, sub, 1), 1)
    d = 1
    while d < sub:
        m = spos >= d
        b3 = jnp.where(m, a3 * pltpu.roll(b3, d, axis=1) + b3, b3)
        a3 = jnp.where(m, a3 * pltpu.roll(a3, d, axis=1), a3)
        d *= 2
    a4 = a3.reshape(bt, gps, sub, w)
    b4 = b3.reshape(bt, gps, sub, w)
    carry = hc_ref[...]
    hs = []
    for r in range(gps):
        hr = b4[:, r] + a4[:, r] * carry
        hs.append(hr)
        carry = hr[:, sub - 1:sub, :]
    h = jnp.stack(hs, axis=1).reshape(rows, w)
    hlast = carry
    hc_ref[...] = hlast
    hout_ref[...] = hlast
    y_ref[...] = h * _gelu_tanh(gate)


def _lru_mixer(z, h0, buf0, layer, cw, cb, aw, ab, xw, xb, lam, batch, seq):
    w = LRU_W
    tl = min(seq, 512)
    bt = min(batch, max(1, 512 // seq))
    nt = seq // tl
    rows = bt * tl
    grid = (batch * seq // rows,)
    if nt > 1:
        sidx = lambda i: (i // nt, 0, 0)
        lidx = lambda i: (layer, i // nt, 0, 0)
    else:
        sidx = lambda i: (i, 0, 0)
        lidx = lambda i: (layer, i, 0, 0)
    full2 = lambda i: (0, 0)
    return pl.pallas_call(
        functools.partial(_lru_kernel, bt=bt, tl=tl, nt=nt),
        grid=grid,
        in_specs=[pl.BlockSpec((rows, 2 * w), lambda i: (i, 0)),
                  pl.BlockSpec((None, bt, 1, w), lidx),
                  pl.BlockSpec((None, bt, CONV_K - 1, w), lidx),
                  pl.BlockSpec((CONV_K, w), full2),
                  pl.BlockSpec((1, w), full2),
                  pl.BlockSpec((w, w), full2),
                  pl.BlockSpec((1, w), full2),
                  pl.BlockSpec((w, w), full2),
                  pl.BlockSpec((1, w), full2),
                  pl.BlockSpec((1, w), full2)],
        out_specs=[pl.BlockSpec((rows, w), lambda i: (i, 0)),
                   pl.BlockSpec((bt, 1, w), sidx),
                   pl.BlockSpec((bt, CONV_K - 1, w), sidx)],
        out_shape=[jax.ShapeDtypeStruct((batch * seq, w), F32),
                   jax.ShapeDtypeStruct((batch, 1, w), F32),
                   jax.ShapeDtypeStruct((batch, CONV_K - 1, w), F32)],
        scratch_shapes=[pltpu.VMEM((bt, 8 + tl, w), F32), pltpu.VMEM((bt, 1, w), F32)],
        compiler_params=_cparams(1),
        name="lru_mixer",
    )(z, h0.reshape(h0.shape[0], batch, 1, w), buf0, cw, cb.reshape(1, w), aw, ab.reshape(1, w), xw,
      xb.reshape(1, w), lam.reshape(1, w))


def _gla_kernel(z_ref, s0_ref, w2_ref, gb_ref, gn_ref, y_ref, sout_ref, s_ref, *, c, sc, g):
    nh = GLA_HEADS
    kw = nh * GLA_DK
    vw = nh * GLA_DV

    @pl.when(pl.program_id(1) == 0)
    def _():
        s_ref[...] = jnp.zeros_like(s_ref)
        for b in range(g):
            for h in range(nh):
                s_ref[b, h * GLA_DK:(h + 1) * GLA_DK, h * GLA_DV:(h + 1) * GLA_DV] = s0_ref[b, h]

    bs = range(g)
    q = [z_ref[b, :, 0:kw] * (GLA_DK ** -0.5) for b in bs]
    k = [z_ref[b, :, kw:2 * kw] for b in bs]
    v = [z_ref[b, :, 2 * kw:2 * kw + vw] for b in bs]
    gk = [-_softplus(-(_mm(z_ref[b, :, 2 * kw + 2 * vw:], w2_ref[...]) + gb_ref[...])) / GLA_TAU
          for b in bs]
    gcum = [_seg_cumsum_rows(gk[b], sc) for b in bs]
    qp = [q[b] * jnp.exp(gcum[b]) for b in bs]

    ri = lax.broadcasted_iota(jnp.int32, (nh * c, nh * c), 0)
    ci = lax.broadcasted_iota(jnp.int32, (nh * c, nh * c), 1)
    keep = (ri // sc == ci // sc) & (ci <= ri)
    a = [_mm_nt(_stack_heads(qp[b], nh, GLA_DK), _stack_heads(k[b] * jnp.exp(-gcum[b]), nh, GLA_DK))
         for b in bs]
    o = [_unstack_heads(_mm(jnp.where(keep, a[b], 0.0), _stack_heads(v[b], nh, GLA_DV)), nh)
         for b in bs]

    s = [s_ref[b] for b in bs]
    eye = (lax.broadcasted_iota(jnp.int32, (kw, kw), 0) ==
           lax.broadcasted_iota(jnp.int32, (kw, kw), 1))
    bd = (lax.broadcasted_iota(jnp.int32, (kw, vw), 0) // GLA_DK ==
          lax.broadcasted_iota(jnp.int32, (kw, vw), 1) // GLA_DV)
    o_inter = [[] for _ in bs]
    for i in range(c // sc):
        lo, hi = i * sc, (i + 1) * sc
        for b in bs:
            o_inter[b].append(_mm(qp[b][lo:hi], s[b]))
            glast = gcum[b][hi - 1:hi]
            kpp = k[b][lo:hi] * jnp.exp(glast - gcum[b][lo:hi])
            u = _mm_tn(kpp, v[b][lo:hi])
            dcol = jnp.sum(jnp.where(eye, jnp.exp(glast), 0.0), axis=1, keepdims=True)
            s[b] = dcol * s[b] + jnp.where(bd, u, 0.0)
    for b in bs:
        s_ref[b] = s[b]
        for h in range(nh):
            sout_ref[b, h] = s[b][h * GLA_DK:(h + 1) * GLA_DK, h * GLA_DV:(h + 1) * GLA_DV]
        ob = o[b] + jnp.concatenate(o_inter[b], axis=0)
        gate = z_ref[b, :, 2 * kw + vw:2 * kw + 2 * vw]
        y_ref[b] = _head_rms(ob, gn_ref[...], nh, GLA_DV) * _silu(gate)


def _gla_mixer(z, s0, layer, w2, gb, gn, batch, seq):
    c = min(seq, MIX_CHUNK)
    sc = min(c, GLA_SUB)
    nt = seq // c
    g = _mixer_group(batch, c)
    kw, vw = GLA_HEADS * GLA_DK, GLA_HEADS * GLA_DV
    zw = z.shape[2]
    full2 = lambda i, j: (0, 0)
    sblk = (g, GLA_HEADS, GLA_DK, GLA_DV)
    return pl.pallas_call(
        functools.partial(_gla_kernel, c=c, sc=sc, g=g),
        grid=(batch // g, nt),
        in_specs=[pl.BlockSpec((g, c, zw), lambda i, j: (i, j, 0)),
                  pl.BlockSpec((None,) + sblk, lambda i, j: (layer, i, 0, 0, 0)),
                  pl.BlockSpec((LANES, kw), full2),
                  pl.BlockSpec((1, kw), full2),
                  pl.BlockSpec((1, vw), full2)],
        out_specs=[pl.BlockSpec((g, c, vw), lambda i, j: (i, j, 0)),
                   pl.BlockSpec(sblk, lambda i, j: (i, 0, 0, 0))],
        out_shape=[jax.ShapeDtypeStruct((batch, seq, vw), F32),
                   jax.ShapeDtypeStruct((batch, GLA_HEADS, GLA_DK, GLA_DV), F32)],
        scratch_shapes=[pltpu.VMEM((g, kw, vw), F32)],
        compiler_params=_cparams(2),
        name="gla_mixer",
    )(z, s0, w2, gb.reshape(1, kw), gn.reshape(1, vw))


def _gdn_prep(z_ref, cw_ref, alog_ref, dtb_ref, bufout_ref, xs_ref, *, c):
    nh = GDN_HEADS
    hw = nh * GDN_DK
    cw3 = 3 * hw

    xs_ref[8:, :] = z_ref[:, 0:cw3]
    cw = cw_ref[...]
    qkv = xs_ref[5:5 + c, :] * cw[0:1]
    for kk in range(1, CONV_K):
        qkv = qkv + xs_ref[5 + kk:5 + kk + c, :] * cw[kk:kk + 1]
    tail = xs_ref[5 + c:8 + c, :]
    xs_ref[5:8, :] = tail
    bufout_ref[...] = tail
    qkv = _silu(qkv)
    zg = z_ref[:, cw3:cw3 + hw]
    sm = z_ref[:, cw3 + hw:]

    lane_head = lax.broadcasted_iota(jnp.int32, (1, hw), 1) // GDN_DK

    def l2n(x):
        sq = x * x
        inv = jnp.zeros_like(x)
        for h in range(nh):
            m = lane_head == h
            ss = jnp.sum(jnp.where(m, sq, 0.0), axis=-1, keepdims=True)
            inv = jnp.where(m, lax.rsqrt(ss + EPS), inv)
        return x * inv

    q = l2n(qkv[:, 0:hw]) * (GDN_DK ** -0.5)
    k = l2n(qkv[:, hw:2 * hw])
    v = qkv[:, 2 * hw:3 * hw]
    beta = _sigmoid(sm)
    glog = -jnp.exp(alog_ref[...]) * _softplus(sm + dtb_ref[...])
    gcum = _seg_cumsum_rows(glog, c)

    n = nh * c
    bcol = jnp.concatenate([beta[:, h:h + 1] for h in range(nh)], axis=0)
    gcol = jnp.concatenate([gcum[:, nh + h:nh + h + 1] for h in range(nh)], axis=0)
    glast = jnp.concatenate(
        [jnp.broadcast_to(gcum[c - 1:c, nh + h:nh + h + 1], (c, 1)) for h in range(nh)], axis=0)
    ri = lax.broadcasted_iota(jnp.int32, (n, n), 0)
    ci = lax.broadcasted_iota(jnp.int32, (n, n), 1)
    grow = jnp.sum(jnp.where(ri == ci, gcol, 0.0), axis=0, keepdims=True)
    same = ri // c == ci // c
    incl = same & (ci <= ri)
    strict = same & (ci < ri)
    dec = jnp.where(incl, jnp.exp(jnp.where(incl, gcol - grow, 0.0)), 0.0)

    sdec = jnp.concatenate(
        [jnp.broadcast_to(jnp.exp(gcum[c - 1:c, nh + h:nh + h + 1]), (GDN_DK, 1)) for h in range(nh)],
        axis=0)
    return dict(ks=_stack_heads(k, nh, GDN_DK), qs=_stack_heads(q, nh, GDN_DK),
                vs=_stack_heads(v, nh, GDN_DV), bcol=bcol, egc=jnp.exp(gcol), dec=dec,
                kdec=jnp.exp(glast - gcol), sdec=sdec, zg=zg)


def _gdn_kernel(z_ref, s0_ref, buf0_ref, cw_ref, alog_ref, dtb_ref, gn_ref,
                y_ref, sout_ref, bufout_ref, xs_ref, s_ref, *, c, g):
    nh = GDN_HEADS
    hw = nh * GDN_DK
    n = nh * c

    @pl.when(pl.program_id(1) == 0)
    def _():
        xs_ref[:, 5:8, :] = buf0_ref[...]
        s_ref[...] = jnp.zeros_like(s_ref)
        for b in range(g):
            for h in range(nh):
                s_ref[b, h * GDN_DK:(h + 1) * GDN_DK, h * GDN_DV:(h + 1) * GDN_DV] = s0_ref[b, h]

    bs = range(g)
    pr = [_gdn_prep(z_ref.at[b], cw_ref, alog_ref, dtb_ref, bufout_ref.at[b], xs_ref.at[b], c=c)
          for b in bs]
    ri = lax.broadcasted_iota(jnp.int32, (n, n), 0)
    ci = lax.broadcasted_iota(jnp.int32, (n, n), 1)
    same = ri // c == ci // c
    incl = same & (ci <= ri)
    strict = same & (ci < ri)
    kq = [_mm_nt(jnp.concatenate([pr[b]["ks"], pr[b]["qs"]], axis=0), pr[b]["ks"]) for b in bs]

    p = [jnp.where(strict, -(pr[b]["bcol"] * kq[b][0:n]) * pr[b]["dec"], 0.0) for b in bs]
    tinv = [jnp.where(ri == ci, 1.0, 0.0) + p[b] for b in bs]
    span = 2
    while span < c:
        p = [_mm(p[b], p[b]) for b in bs]
        tinv = [tinv[b] + _mm(tinv[b], p[b]) for b in bs]
        span *= 2

    s = [s_ref[b] for b in bs]
    uw = [_mm(tinv[b], jnp.concatenate([pr[b]["vs"] * pr[b]["bcol"],
                                        pr[b]["ks"] * (pr[b]["bcol"] * pr[b]["egc"])], axis=1))
          for b in bs]
    qw = [_mm(jnp.concatenate([pr[b]["qs"] * pr[b]["egc"], uw[b][:, hw:]], axis=0), s[b])
          for b in bs]
    vnew = [uw[b][:, 0:hw] - qw[b][n:] for b in bs]
    av = [_mm(jnp.where(incl, kq[b][n:] * pr[b]["dec"], 0.0), vnew[b]) for b in bs]
    kv = [_mm_tn(pr[b]["ks"] * pr[b]["kdec"], vnew[b]) for b in bs]
    for b in bs:
        o = _unstack_heads(qw[b][0:n] + av[b], nh)
        sn = pr[b]["sdec"] * s[b] + kv[b]
        s_ref[b] = sn
        for h in range(nh):
            sout_ref[b, h] = sn[h * GDN_DK:(h + 1) * GDN_DK, h * GDN_DV:(h + 1) * GDN_DV]
        y_ref[b] = _head_rms(o, gn_ref[...], nh, GDN_DV) * _silu(pr[b]["zg"])


def _mixer_group(batch, c):
    return min(batch, max(8, 128 // c))


def _gdn_mixer(z, s0, buf0, layer, cw, alog, dtb, gn, batch, seq):
    c = min(seq, MIX_CHUNK)
    nt = seq // c
    g = _mixer_group(batch, c)
    hw = GDN_HEADS * GDN_DK
    zw = z.shape[2]
    full2 = lambda i, j: (0, 0)
    sblk = (g, GDN_HEADS, GDN_DK, GDN_DV)
    return pl.pallas_call(
        functools.partial(_gdn_kernel, c=c, g=g),
        grid=(batch // g, nt),
        in_specs=[pl.BlockSpec((g, c, zw), lambda i, j: (i, j, 0)),
                  pl.BlockSpec((None,) + sblk, lambda i, j: (layer, i, 0, 0, 0)),
                  pl.BlockSpec((None, g, CONV_K - 1, 3 * hw), lambda i, j: (layer, i, 0, 0)),
                  pl.BlockSpec((CONV_K, 3 * hw), full2),
                  pl.BlockSpec((1, LANES), full2),
                  pl.BlockSpec((1, LANES), full2),
                  pl.BlockSpec((1, hw), full2)],
        out_specs=[pl.BlockSpec((g, c, hw), lambda i, j: (i, j, 0)),
                   pl.BlockSpec(sblk, lambda i, j: (i, 0, 0, 0)),
                   pl.BlockSpec((g, CONV_K - 1, 3 * hw), lambda i, j: (i, 0, 0))],
        out_shape=[jax.ShapeDtypeStruct((batch, seq, hw), F32),
                   jax.ShapeDtypeStruct((batch, GDN_HEADS, GDN_DK, GDN_DV), F32),
                   jax.ShapeDtypeStruct((batch, CONV_K - 1, 3 * hw), F32)],
        scratch_shapes=[pltpu.VMEM((g, 8 + c, 3 * hw), F32), pltpu.VMEM((g, hw, hw), F32)],
        compiler_params=_cparams(2),
        name="gdn_mixer",
    )(z, s0, buf0, cw, alog, dtb, gn.reshape(1, hw))


def _outproj_kernel(x_ref, yl_ref, yg_ref, yd_ref, wo_ref, gq_ref, wq_ref, xn_ref, q_ref):
    lw = LRU_W
    gw = GLA_HEADS * GLA_DV
    y = _mm(yl_ref[...], wo_ref[0:lw, :])
    y = y + _mm(yg_ref[...], wo_ref[lw:lw + gw, :])
    y = y + _mm(yd_ref[...], wo_ref[lw + gw:, :])
    xn = x_ref[...] + y
    xn_ref[...] = xn
    q_ref[...] = _mm(_rms_rows(xn, gq_ref[...]), wq_ref[...])


def _outproj(x, yl, yg, yd, wo, gq, wq, tm=512, row_off=0):
    d = x.shape[1]
    t = yl.shape[0]
    tm = _row_tile(t, tm)
    off = row_off // tm
    row = lambda i: (i, 0)
    full2 = lambda i: (0, 0)
    return pl.pallas_call(
        _outproj_kernel,
        grid=(t // tm,),
        in_specs=[pl.BlockSpec((tm, d), lambda i: (i + off, 0)),
                  pl.BlockSpec((tm, yl.shape[1]), row),
                  pl.BlockSpec((tm, yg.shape[1]), row),
                  pl.BlockSpec((tm, yd.shape[1]), row),
                  pl.BlockSpec((d, d), full2),
                  pl.BlockSpec((1, d), full2),
                  pl.BlockSpec((d, d), full2)],
        out_specs=[pl.BlockSpec((tm, d), row), pl.BlockSpec((tm, d), row)],
        out_shape=[jax.ShapeDtypeStruct((t, d), F32), jax.ShapeDtypeStruct((t, d), F32)],
        compiler_params=_cparams(1),
        name="outproj_qproj",
    )(x, yl, yg, yd, wo, gq.reshape(1, d), wq)


def _attn_kernel(x_ref, q_ref, k_ref, v_ref, wo_ref, o_ref):
    hd = MEM_HD
    acc = x_ref[...]
    for h in range(MEM_HEADS):
        sl = slice(h * hd, (h + 1) * hd)
        s = _mm_nt(q_ref[:, sl], k_ref[:, sl]) * (hd ** -0.5)
        m = jnp.max(s, axis=-1, keepdims=True)
        p = jnp.exp(s - m)
        l = jnp.sum(p, axis=-1, keepdims=True)
        oh = _mm(p, v_ref[:, sl]) / l
        acc = acc + _mm(oh, wo_ref[sl, :])
    o_ref[...] = acc


def _attn_heads_kernel(q_ref, k_ref, v_ref, o_ref, *, gb, tl):
    nh, hd = MEM_HEADS, MEM_HD
    m = k_ref.shape[1]
    row_head = lax.broadcasted_iota(jnp.int32, (nh * tl, 1), 0) // tl
    col_head = lax.broadcasted_iota(jnp.int32, (1, m * nh), 1) % nh
    for b in range(gb):
        kf = k_ref[b].reshape(m * nh, hd)
        vf = v_ref[b].reshape(m * nh, hd)
        rows = slice(b * tl, (b + 1) * tl)
        qs = jnp.concatenate([q_ref[rows, h * hd:(h + 1) * hd] for h in range(nh)], axis=0)
        s = _mm_nt(qs, kf) * (hd ** -0.5)
        s = jnp.where(row_head == col_head, s, -jnp.inf)
        mx = jnp.max(s, axis=-1, keepdims=True)
        p = jnp.exp(s - mx)
        l = jnp.sum(p, axis=-1, keepdims=True)
        o = _mm(p, vf) / l
        for h in range(nh):
            o_ref[rows, h * hd:(h + 1) * hd] = o[h * tl:(h + 1) * tl]


def _oproj_kernel(x_ref, a_ref, wo_ref, *rest):
    o_ref = rest[-1]
    o_ref[...] = x_ref[...] + _mm(a_ref[...], wo_ref[...])


def _attn_first_kernel(x_ref, q_ref, k_ref, v_ref, wo_ref, o_ref, *, n_real):
    i = pl.program_id(0)

    @pl.when(i < n_real)
    def _():
        _attn_kernel(x_ref, q_ref, k_ref, v_ref, wo_ref, o_ref)

    @pl.when(i >= n_real)
    def _():
        o_ref[...] = jnp.zeros_like(o_ref)


def _attn_into_kernel(x_ref, q_ref, k_ref, v_ref, wo_ref, joint_ref, o_ref):
    _attn_kernel(x_ref, q_ref, k_ref, v_ref, wo_ref, o_ref)


def _attention(x, q, k, v, layer, wo, batch, seq, joint, row_off, total_rows):
    t, d = x.shape
    out_shape = jax.ShapeDtypeStruct((total_rows, d), F32)
    extra_specs = [] if joint is None else [pl.BlockSpec(memory_space=pl.ANY)]
    extra_args = () if joint is None else (joint,)
    if k.ndim == 4:
        tl = min(seq, 512)
        nl = seq // tl
        off = row_off // tl
        kblk = (None, None) + k.shape[2:]
        n_real = batch * nl
        if joint is None:
            assert row_off == 0
            n_steps = pl.cdiv(total_rows, tl)
            body = functools.partial(_attn_first_kernel, n_real=n_real)
        else:
            n_steps = n_real
            body = _attn_into_kernel
        kidx = lambda i: (layer, jnp.minimum(i // nl, batch - 1), 0, 0)
        row = lambda i: (jnp.minimum(i, n_real - 1), 0)
        return pl.pallas_call(
            body,
            grid=(n_steps,),
            in_specs=[pl.BlockSpec((tl, d), row),
                      pl.BlockSpec((tl, d), row),
                      pl.BlockSpec(kblk, kidx),
                      pl.BlockSpec(kblk, kidx),
                      pl.BlockSpec((d, d), lambda i: (0, 0))] + extra_specs,
            out_specs=pl.BlockSpec((tl, d), lambda i: (i + off, 0)),
            out_shape=out_shape,
            input_output_aliases={} if joint is None else {5: 0},
            compiler_params=_cparams(1),
            name="mem_attention",
        )(x, q, k, v, wo, *extra_args)
    gb = 4 if batch % 4 == 0 else 2
    kblk = (None, gb) + k.shape[2:]
    kidx = lambda i: (layer, i, 0, 0, 0)
    att = pl.pallas_call(
        functools.partial(_attn_heads_kernel, gb=gb, tl=seq),
        grid=(batch // gb,),
        in_specs=[pl.BlockSpec((gb * seq, d), lambda i: (i, 0)),
                  pl.BlockSpec(kblk, kidx),
                  pl.BlockSpec(kblk, kidx)],
        out_specs=pl.BlockSpec((gb * seq, d), lambda i: (i, 0)),
        out_shape=jax.ShapeDtypeStruct((t, d), F32),
        compiler_params=_cparams(1),
        name="mem_attention_heads",
    )(q, k, v)
    tm = _row_tile(t, 512)
    off = row_off // tm
    return pl.pallas_call(
        _oproj_kernel,
        grid=(t // tm,),
        in_specs=[pl.BlockSpec((tm, d), lambda i: (i, 0)),
                  pl.BlockSpec((tm, d), lambda i: (i, 0)),
                  pl.BlockSpec((d, d), lambda i: (0, 0))] + extra_specs,
        out_specs=pl.BlockSpec((tm, d), lambda i: (i + off, 0)),
        out_shape=out_shape,
        input_output_aliases={} if joint is None else {3: 0},
        compiler_params=_cparams(1),
        name="mem_oproj",
    )(x, att, wo, *extra_args)


def _ffn_kernel(x_ref, g_ref, wg_ref, wu_ref, wd_ref, o_ref, h_ref, acc_ref):
    f = pl.program_id(1)

    @pl.when(f == 0)
    def _():
        h_ref[...] = _rms_rows(x_ref[...], g_ref[...]).astype(BF16)
        acc_ref[...] = jnp.zeros_like(acc_ref)

    h = h_ref[...]
    a = _mm(h, wg_ref[...])
    u = _mm(h, wu_ref[...])
    acc_ref[...] += _mm(_silu(a) * u, wd_ref[...])

    @pl.when(f == pl.num_programs(1) - 1)
    def _():
        o_ref[...] = x_ref[...] + acc_ref[...]


def _ffn_dense(x, g, wg, wu, wd, tm=1024, tf=512):
    t, d = x.shape
    dff = wg.shape[1]
    tm = _row_tile(t, tm)
    return pl.pallas_call(
        _ffn_kernel,
        grid=(t // tm, dff // tf),
        in_specs=[pl.BlockSpec((tm, d), lambda i, f: (i, 0)),
                  pl.BlockSpec((1, d), lambda i, f: (0, 0)),
                  pl.BlockSpec((d, tf), lambda i, f: (0, f)),
                  pl.BlockSpec((d, tf), lambda i, f: (0, f)),
                  pl.BlockSpec((tf, d), lambda i, f: (f, 0))],
        out_specs=pl.BlockSpec((tm, d), lambda i, f: (i, 0)),
        out_shape=jax.ShapeDtypeStruct((t, d), F32),
        scratch_shapes=[pltpu.VMEM((tm, d), BF16), pltpu.VMEM((tm, d), F32)],
        compiler_params=_cparams(2),
        name="ffn_dense",
    )(x, g.reshape(1, d), wg, wu, wd)


def _router_kernel(x_ref, g_ref, whi_ref, wlo_ref, info_ref, cnt_ref, carry_ref, *, tm):
    i = pl.program_id(0)

    @pl.when(i == 0)
    def _():
        carry_ref[...] = jnp.zeros_like(carry_ref)

    h = _rms_rows(x_ref[...], g_ref[...])
    hhi = h.astype(BF16)
    hlo = (h - hhi.astype(F32)).astype(BF16)
    whi = whi_ref[...]
    logits = (jnp.dot(hhi, whi, preferred_element_type=F32)
              + jnp.dot(hlo, whi, preferred_element_type=F32)
              + jnp.dot(hhi, wlo_ref[...], preferred_element_type=F32))
    lane = lax.broadcasted_iota(jnp.int32, (tm, LANES), 1)
    neg = jnp.float32(-jnp.inf)
    logits = jnp.where(lane < N_EXPERTS, logits, neg)
    m1 = jnp.max(logits, axis=-1, keepdims=True)
    i1 = jnp.min(jnp.where(logits == m1, lane, LANES), axis=-1, keepdims=True)
    rest = jnp.where(lane == i1, neg, logits)
    m2 = jnp.max(rest, axis=-1, keepdims=True)
    i2 = jnp.min(jnp.where(rest == m2, lane, LANES), axis=-1, keepdims=True)
    e = jnp.exp(m2 - m1)
    g1 = 1.0 / (1.0 + e)
    g2 = e / (1.0 + e)
    oh1 = jnp.where(lane == i1, 1.0, 0.0)
    oh2 = jnp.where(lane == i2, 1.0, 0.0)
    oh = oh1 + oh2
    ri = lax.broadcasted_iota(jnp.int32, (tm, tm), 0)
    ci = lax.broadcasted_iota(jnp.int32, (tm, tm), 1)
    tri = jnp.where(ci < ri, 1.0, 0.0)
    before = _mm(tri, oh) + carry_ref[0:1, :]
    r1 = jnp.sum(oh1 * before, axis=-1, keepdims=True)
    r2 = jnp.sum(oh2 * before, axis=-1, keepdims=True)
    carry = carry_ref[0:1, :] + jnp.sum(oh, axis=0, keepdims=True)
    carry_ref[...] = jnp.broadcast_to(carry, carry_ref.shape)
    cnt_ref[...] = jnp.broadcast_to(carry, cnt_ref.shape)
    info = jnp.where(lane == 0, i1.astype(F32), 0.0)
    info = jnp.where(lane == 1, i2.astype(F32), info)
    info = jnp.where(lane == 2, r1, info)
    info = jnp.where(lane == 3, r2, info)
    info = jnp.where(lane == 4, g1, info)
    info = jnp.where(lane == 5, g2, info)
    info_ref[...] = info


def _router(x, g, whi, wlo, tm):
    t, d = x.shape
    return pl.pallas_call(
        functools.partial(_router_kernel, tm=tm),
        grid=(t // tm,),
        in_specs=[pl.BlockSpec((tm, d), lambda i: (i, 0)),
                  pl.BlockSpec((1, d), lambda i: (0, 0)),
                  pl.BlockSpec((d, LANES), lambda i: (0, 0)),
                  pl.BlockSpec((d, LANES), lambda i: (0, 0))],
        out_specs=[pl.BlockSpec((tm, LANES), lambda i: (i, 0)),
                   pl.BlockSpec((8, LANES), lambda i: (i, 0))],
        out_shape=[jax.ShapeDtypeStruct((t, LANES), F32),
                   jax.ShapeDtypeStruct((t // tm * 8, LANES), F32)],
        scratch_shapes=[pltpu.VMEM((8, LANES), F32)],
        compiler_params=_cparams(1),
        name="moe_router",
    )(x, g.reshape(1, d), whi, wlo)


MOE_CHUNK = 32


def _dispatch_kernel(st_ref, nc_ref, pos_ref, zs_ref, zf_ref, x_ref, g_ref, ct_ref, xs_hbm, w_ref, zbuf_ref,
                     zsem, sem, *, rows, sub, n_blk, win):
    i = pl.program_id(0)

    @pl.when(i == 0)
    def _():
        zbuf_ref[...] = jnp.zeros_like(zbuf_ref)
        for e in range(N_EXPERTS):
            dst = xs_hbm.at[pl.ds(pl.multiple_of(zs_ref[e], 8), sub)]
            pltpu.make_async_copy(zbuf_ref, dst, zsem).start()
            pltpu.make_async_copy(zbuf_ref, dst, zsem).wait()

        def fill(j, carry):
            @pl.when(zf_ref[j] == 1)
            def _():
                dst = xs_hbm.at[pl.ds(pl.multiple_of(j * sub, sub), sub)]
                pltpu.make_async_copy(zbuf_ref, dst, zsem).start()
            return carry

        def drain(j, carry):
            @pl.when(zf_ref[j] == 1)
            def _():
                dst = xs_hbm.at[pl.ds(pl.multiple_of(j * sub, sub), sub)]
                pltpu.make_async_copy(zbuf_ref, dst, zsem).wait()
            return carry

        lax.fori_loop(0, n_blk, fill, 0)
        lax.fori_loop(0, n_blk, drain, 0)

    ck = MOE_CHUNK
    n_steps = pl.num_programs(0)

    def chunk_copy(step, e, c):
        t = step * N_EXPERTS + e
        src = w_ref.at[step % 2, pl.ds(pl.multiple_of(pos_ref[t] + c * ck, ck), ck)]
        dst = xs_hbm.at[pl.ds(pl.multiple_of(st_ref[t] + c * ck, 8), ck)]
        return pltpu.make_async_copy(src, dst, sem)

    def all_chunks(step, fn):
        for e in range(N_EXPERTS):
            def body(c, carry, e=e):
                fn(chunk_copy(step, e, c))
                return carry
            lax.fori_loop(0, nc_ref[step * N_EXPERTS + e], body, 0)

    h = _rms_rows(x_ref[...], g_ref[...])
    wrow = lax.broadcasted_iota(jnp.int32, (win, rows), 0)
    cols = ct_ref[...]
    onehot = jnp.where(wrow == cols[0:1, :], 1.0, 0.0) + jnp.where(wrow == cols[1:2, :], 1.0, 0.0)
    w_ref[i % 2] = _mm(onehot, h)

    @pl.when(i >= 1)
    def _():
        all_chunks(i - 1, lambda cp: cp.wait())

    all_chunks(i, lambda cp: cp.start())

    @pl.when(i == n_steps - 1)
    def _():
        all_chunks(i, lambda cp: cp.wait())


def _dispatch(seg_start, n_chunks, win_pos, zero_start, zero_blk, cols_t, x, g, sub, rows, win):
    t, d = x.shape
    n_blk = zero_blk.shape[0]
    n_rows = n_blk * sub
    return pl.pallas_call(
        functools.partial(_dispatch_kernel, rows=rows, sub=sub, n_blk=n_blk, win=win),
        grid_spec=pltpu.PrefetchScalarGridSpec(
            num_scalar_prefetch=5,
            grid=(t // rows,),
            in_specs=[pl.BlockSpec((rows, d), lambda i, *_: (i, 0)),
                      pl.BlockSpec((1, d), lambda i, *_: (0, 0)),
                      pl.BlockSpec((2, rows), lambda i, *_: (0, i))],
            out_specs=pl.BlockSpec(memory_space=pl.ANY),
            scratch_shapes=[pltpu.VMEM((2, win, d), F32), pltpu.VMEM((sub, d), F32),
                            pltpu.SemaphoreType.DMA(()), pltpu.SemaphoreType.DMA(())]),
        out_shape=jax.ShapeDtypeStruct((n_rows, d), F32),
        compiler_params=_cparams(1),
        name="moe_dispatch",
    )(seg_start, n_chunks, win_pos, zero_start, zero_blk, x, g.reshape(1, d), cols_t)


def _expert_kernel(be_ref, ns_ref, xi_ref, x_ref, wg_ref, wu_ref, wd_ref, o_ref, *, sub, n_sub):
    i = pl.program_id(0)
    f = pl.program_id(1)
    ns = ns_ref[i]

    @pl.when(f == 0)
    def _():
        o_ref[...] = jnp.zeros_like(o_ref)

    def swiglu(rows):
        h = x_ref[rows, :]
        a = _mm(h, wg_ref[...])
        u = _mm(h, wu_ref[...])
        o_ref[rows, :] += _mm(_silu(a) * u, wd_ref[...])

    for s in range(0, n_sub, 2):
        if s + 2 <= n_sub:
            @pl.when(s + 2 <= ns)
            def _():
                swiglu(slice(s * sub, (s + 2) * sub))

        @pl.when(s + 1 == ns)
        def _():
            swiglu(slice(s * sub, (s + 1) * sub))


def _expert_ffn(blk_exp, n_valid_sub, x_blk, xs, wg, wu, wd, sb, sub, tf=512):
    p, d = xs.shape
    dff = wg.shape[2]
    n_super = n_valid_sub.shape[0]
    nf = dff // tf

    def fidx(i, f, ns):
        used = jnp.minimum(ns[i], 1)
        return f * used + (nf - 1) * (1 - used)

    return pl.pallas_call(
        functools.partial(_expert_kernel, sub=sub, n_sub=sb // sub),
        grid_spec=pltpu.PrefetchScalarGridSpec(
            num_scalar_prefetch=3,
            grid=(n_super, nf),
            in_specs=[pl.BlockSpec((sb, d), lambda i, f, be, ns, xi: (xi[i], 0)),
                      pl.BlockSpec((None, d, tf), lambda i, f, be, ns, xi: (be[i], 0, fidx(i, f, ns))),
                      pl.BlockSpec((None, d, tf), lambda i, f, be, ns, xi: (be[i], 0, fidx(i, f, ns))),
                      pl.BlockSpec((None, tf, d), lambda i, f, be, ns, xi: (be[i], fidx(i, f, ns), 0))],
            out_specs=pl.BlockSpec((sb, d), lambda i, f, be, ns, xi: (i, 0))),
        out_shape=jax.ShapeDtypeStruct((n_super * sb, d), F32),
        compiler_params=pltpu.CompilerParams(dimension_semantics=("arbitrary", "arbitrary"),
                                             vmem_limit_bytes=EXPERT_VMEM_LIMIT_BYTES),
        name="moe_experts",
    )(blk_exp, n_valid_sub, x_blk, xs, wg, wu, wd)


def _combine_kernel(st_ref, nc_ref, pos_ref, x_ref, info_ref, col_ref, ys_hbm, gf_ref, o_ref, buf_ref, sem,
                    *, rows, final_norm, blk_off, win):
    ck = MOE_CHUNK
    i = pl.program_id(0)

    def chunk_copy(step, e, c):
        t = (step + blk_off) * N_EXPERTS + e
        src = ys_hbm.at[pl.ds(pl.multiple_of(st_ref[t] + c * ck, 8), ck)]
        dst = buf_ref.at[step % 2, pl.ds(pl.multiple_of(pos_ref[t] + c * ck, ck), ck)]
        return pltpu.make_async_copy(src, dst, sem.at[step % 2])

    def all_chunks(step, fn):
        for e in range(N_EXPERTS):
            def body(c, carry, e=e):
                fn(chunk_copy(step, e, c))
                return carry
            lax.fori_loop(0, nc_ref[(step + blk_off) * N_EXPERTS + e], body, 0)

    @pl.when(i == 0)
    def _():
        buf_ref[...] = jnp.zeros_like(buf_ref)
        all_chunks(i, lambda cp: cp.start())

    @pl.when(i + 1 < pl.num_programs(0))
    def _():
        all_chunks(i + 1, lambda cp: cp.start())

    info = info_ref[...]
    cols = col_ref[...]
    lane = lax.broadcasted_iota(jnp.int32, (rows, win), 1)
    p = (jnp.where(cols[:, 0:1] == lane, info[:, 4:5], 0.0)
         + jnp.where(cols[:, 1:2] == lane, info[:, 5:6], 0.0)).astype(BF16)
    all_chunks(i, lambda cp: cp.wait())
    acc = x_ref[...] + _mm(p, buf_ref[i % 2])
    if final_norm:
        acc = _rms_rows(acc, gf_ref[...])
    o_ref[...] = acc


def _combine(seg_start, n_chunks, win_pos, cols, x, info, ys, gf, final_norm, row_off, n_rows, rows, win):
    d = x.shape[1]
    off = row_off // rows
    return pl.pallas_call(
        functools.partial(_combine_kernel, rows=rows, final_norm=final_norm, blk_off=off, win=win),
        grid_spec=pltpu.PrefetchScalarGridSpec(
            num_scalar_prefetch=3,
            grid=(n_rows // rows,),
            in_specs=[pl.BlockSpec((rows, d), lambda i, *_: (i + off, 0)),
                      pl.BlockSpec((rows, LANES), lambda i, *_: (i + off, 0)),
                      pl.BlockSpec((rows, 2), lambda i, *_: (i + off, 0)),
                      pl.BlockSpec(memory_space=pl.ANY),
                      pl.BlockSpec((1, d), lambda i, *_: (0, 0))],
            out_specs=pl.BlockSpec((rows, d), lambda i, *_: (i, 0)),
            scratch_shapes=[pltpu.VMEM((2, win, d), F32), pltpu.SemaphoreType.DMA((2,))]),
        out_shape=jax.ShapeDtypeStruct((n_rows, d), F32),
        compiler_params=_cparams(1),
        name="moe_combine",
    )(seg_start, n_chunks, win_pos, x, info, cols, ys, gf.reshape(1, d))


def _moe_ffn(x, g, whi, wlo, wg, wu, wd, gf, final_norm, groups, sub=512):
    t, d = x.shape
    sb = 2048 if 2 * t >= 16 * 1024 else sub
    tb = t
    for off, n in groups:
        tb = math.gcd(tb, math.gcd(off, n))
    tb = _row_tile(tb, 512)
    info, cnt = _router(x, g, whi, wlo, tb)
    e = info[:, 0:2].astype(jnp.int32)
    rank = info[:, 2:4].astype(jnp.int32)
    after = cnt[::8, :N_EXPERTS].astype(jnp.int32)
    before = jnp.concatenate([jnp.zeros((1, N_EXPERTS), jnp.int32), after[:-1]], axis=0)
    n_blocks = t // tb
    ck = MOE_CHUNK
    n_seg = after - before
    seg_len = (n_seg + 7) // 8 * 8
    seg_rel = jnp.cumsum(seg_len, axis=0) - seg_len
    counts = jnp.sum(seg_len, axis=0)
    n_sb = (counts + ck + sb - 1) // sb
    sb_end = jnp.cumsum(n_sb)
    sb_start = sb_end - n_sb
    row_start = sb_start * sb
    seg_start = row_start[None, :] + seg_rel
    n_chunks = (n_seg + ck - 1) // ck
    win_pos = (jnp.cumsum(n_chunks, axis=1) - n_chunks) * ck
    win = (2 * tb + N_EXPERTS * (ck - 1) + LANES - 1) // LANES * LANES
    eid = jnp.arange(N_EXPERTS, dtype=jnp.int32)
    tok_origin = jnp.repeat(win_pos - before, tb, axis=0)
    cols = rank + jnp.sum(jnp.where(e[:, :, None] == eid, tok_origin[:, None, :], 0), axis=-1)
    cols = cols.astype(jnp.int32)
    n_super = (2 * t + n_blocks * N_EXPERTS * 7) // sb + N_EXPERTS + 2
    blk = jnp.arange(n_super, dtype=jnp.int32)
    n_used = sb_end[-1]
    used = blk < n_used
    blk_c = jnp.minimum(blk, n_used - 1)
    be = jnp.minimum(jnp.sum((blk_c[:, None] >= sb_end[None, :]).astype(jnp.int32), axis=-1),
                     N_EXPERTS - 1)
    valid = jnp.clip(counts[be] - (blk_c - sb_start[be]) * sb, 0, sb)
    n_valid_sub = jnp.where(used, (valid + sub - 1) // sub, 0).astype(jnp.int32)
    zero_start = (row_start + counts).astype(jnp.int32)
    per = sb // sub
    sub_in_blk = jnp.arange(per, dtype=jnp.int32)
    zero_blk = (sub_in_blk[None, :] >= n_valid_sub[:, None]).astype(jnp.int32).reshape(-1)
    zero_blk = jnp.concatenate([zero_blk, jnp.ones((1,), jnp.int32)])
    tables = [a.reshape(-1).astype(jnp.int32) for a in (seg_start, n_chunks, win_pos)]
    xs = _dispatch(*tables, zero_start, zero_blk, cols.T, x, g, sub, tb, win)
    ys = _expert_ffn(be.astype(jnp.int32), n_valid_sub, blk_c.astype(jnp.int32), xs, wg, wu, wd, sb, sub)
    return [_combine(*tables, cols, x, info, ys, gf, final_norm, off, n, tb, win) for off, n in groups]


def _final_norm_kernel(x_ref, g_ref, o_ref):
    o_ref[...] = _rms_rows(x_ref[...], g_ref[...])


def _final_norm(x, g, row_off, n_rows, tm=512):
    d = x.shape[1]
    t = n_rows
    tm = _row_tile(t, tm)
    off = row_off // tm
    return pl.pallas_call(
        _final_norm_kernel,
        grid=(t // tm,),
        in_specs=[pl.BlockSpec((tm, d), lambda i: (i + off, 0)), pl.BlockSpec((1, d), lambda i: (0, 0))],
        out_specs=pl.BlockSpec((tm, d), lambda i: (i, 0)),
        out_shape=jax.ShapeDtypeStruct((t, d), F32),
        compiler_params=_cparams(1),
        name="final_norm",
    )(x, g.reshape(1, d))


def _block_diag(w):
    n, c, d = w.shape
    eye = jnp.eye(n, dtype=w.dtype)
    return jnp.einsum("ncd,nm->ncmd", w, eye).reshape(n * c, n * d)


def _pad_cols(w, n):
    return jnp.pad(w, ((0, 0), (0, n - w.shape[1])))


def _layer_params(l, p):
    (norm_mix, w_in, lru_conv_w, lru_conv_b, lru_a_w, lru_a_b, lru_x_w, lru_x_b, lru_lam, gla_gk_w2,
     gla_gk_b, gla_norm, gdn_conv_w, gdn_a_log, gdn_dt_bias, gdn_norm, w_out, norm_xq, norm_mem,
     w_mq, w_mk, w_mv, w_mo, norm_ffn) = [a[l] for a in p]
    lw = LRU_W
    gk, gv = GLA_HEADS * GLA_DK, GLA_HEADS * GLA_DV
    dh = GDN_HEADS * GDN_DK
    offs = [0]
    for s in (lw, lw, gk, gk, gv, GLA_RANK, gv, dh, dh, dh, GDN_HEADS, GDN_HEADS, dh):
        offs.append(offs[-1] + s)
    col = lambda i: w_in[:, offs[i]:offs[i + 1]]
    w_lru = jnp.concatenate([col(0), col(1)], axis=1)
    w_gla = jnp.concatenate([col(2), col(3), col(4), col(6), _pad_cols(col(5), LANES)], axis=1)
    w_gdn = jnp.concatenate(
        [col(7), col(8), col(9), col(12), _pad_cols(jnp.concatenate([col(10), col(11)], axis=1), LANES)],
        axis=1)
    w_cat = jnp.concatenate([w_lru, w_gla, w_gdn], axis=1).astype(BF16)
    widths = (w_lru.shape[1], w_gla.shape[1], w_gdn.shape[1])
    alog = jnp.zeros((1, LANES), F32).at[0, GDN_HEADS:2 * GDN_HEADS].set(gdn_a_log)
    dtb = jnp.zeros((1, LANES), F32).at[0, GDN_HEADS:2 * GDN_HEADS].set(gdn_dt_bias)
    return dict(
        norm_mix=norm_mix, w_cat=w_cat, widths=widths,
        lru_conv_w=lru_conv_w, lru_conv_b=lru_conv_b,
        lru_a=_block_diag(lru_a_w).astype(BF16), lru_a_b=lru_a_b,
        lru_x=_block_diag(lru_x_w).astype(BF16), lru_x_b=lru_x_b, lru_lam=lru_lam,
        gla_w2=jnp.pad(gla_gk_w2, ((0, LANES - GLA_RANK), (0, 0))).astype(BF16), gla_gk_b=gla_gk_b,
        gla_norm=jnp.tile(gla_norm, GLA_HEADS),
        gdn_conv_w=gdn_conv_w, gdn_alog=alog, gdn_dtb=dtb, gdn_norm=jnp.tile(gdn_norm, GDN_HEADS),
        w_out=w_out.astype(BF16), norm_xq=norm_xq, w_mq=w_mq.astype(BF16), w_mo=w_mo.astype(BF16),
        norm_ffn=norm_ffn)


def _mix_and_attend(x, x_off, grp, l, lp, joint, total_rows):
    batch, seq, n = grp["batch"], grp["seq"], grp["batch"] * grp["seq"]
    lru_h0, lru_buf0, gla_s0, gdn_s0, gdn_buf0 = grp["states"]
    sl = grp["state_layer"](l)
    z_lru, z_gla, z_gdn = _rms_matmul(x, lp["norm_mix"], lp["w_cat"], lp["widths"], "in_proj",
                                      row_off=x_off, n_rows=n)
    y_lru, lru_h, lru_buf = _lru_mixer(
        z_lru, lru_h0, lru_buf0, sl, lp["lru_conv_w"], lp["lru_conv_b"], lp["lru_a"], lp["lru_a_b"],
        lp["lru_x"], lp["lru_x_b"], lp["lru_lam"], batch, seq)
    y_gla, gla_s = _gla_mixer(z_gla.reshape(batch, seq, -1), gla_s0, sl, lp["gla_w2"],
                              lp["gla_gk_b"], lp["gla_norm"], batch, seq)
    y_gdn, gdn_s, gdn_buf = _gdn_mixer(z_gdn.reshape(batch, seq, -1), gdn_s0, gdn_buf0, sl,
                                       lp["gdn_conv_w"], lp["gdn_alog"], lp["gdn_dtb"],
                                       lp["gdn_norm"], batch, seq)
    xn, q = _outproj(x, y_lru, y_gla.reshape(n, -1), y_gdn.reshape(n, -1),
                     lp["w_out"], lp["norm_xq"], lp["w_mq"], row_off=x_off)
    mem_k, mem_v, mem_layer = grp["mem"][l]
    joint = _attention(xn, q, mem_k, mem_v, mem_layer, lp["w_mo"], batch, seq, joint, grp["row_off"],
                       total_rows)
    return joint, (lru_h.reshape(batch, LRU_W), lru_buf, gla_s, gdn_s, gdn_buf)


def _run_layers(groups, layers, ffn, norm_final):
    total_rows = sum(g["batch"] * g["seq"] for g in groups)
    spans = [(g["row_off"], g["batch"] * g["seq"]) for g in groups]
    xs = [(g["x"], 0) for g in groups]
    new_states = [[] for _ in groups]
    outs = None
    for l, lp in enumerate(layers):
        joint = None
        for gi, grp in enumerate(groups):
            joint, st = _mix_and_attend(xs[gi][0], xs[gi][1], grp, l, lp, joint, total_rows)
            new_states[gi].append(st)
        last = l == len(layers) - 1
        kind, fp = ffn[l]
        if kind == "dense":
            joint = _ffn_dense(joint, lp["norm_ffn"], *fp)
            if last:
                outs = [_final_norm(joint, norm_final, off, n) for off, n in spans]
        elif last:
            outs = _moe_ffn(joint, lp["norm_ffn"], *fp, norm_final, True, spans)
        else:
            joint = _moe_ffn(joint, lp["norm_ffn"], *fp, norm_final, False, [(0, total_rows)])[0]
        xs = [(joint, g["row_off"]) for g in groups]
    states = [[jnp.stack(s) for s in zip(*ns)] for ns in new_states]
    return outs, states


def kernel(x_prompt, x_sample, mem_prompt, state_lru_h, state_lru_conv, state_gla, state_gdn, state_gdn_conv, cache_mem_k, cache_mem_v, norm_mix, w_in, lru_conv_w, lru_conv_b, lru_a_w, lru_a_b, lru_x_w, lru_x_b, lru_lam, gla_gk_w2, gla_gk_b, gla_norm, gdn_conv_w, gdn_a_log, gdn_dt_bias, gdn_norm, w_out, norm_xq, norm_mem, w_mq, w_mk, w_mv, w_mo, norm_ffn, w_ff_gate, w_ff_up, w_ff_down, w_router, w_e_gate, w_e_up, w_e_down, norm_final):
    depth = norm_mix.shape[0]
    per_layer = (norm_mix, w_in, lru_conv_w, lru_conv_b, lru_a_w, lru_a_b, lru_x_w, lru_x_b, lru_lam,
                 gla_gk_w2, gla_gk_b, gla_norm, gdn_conv_w, gdn_a_log, gdn_dt_bias, gdn_norm, w_out,
                 norm_xq, norm_mem, w_mq, w_mk, w_mv, w_mo, norm_ffn)
    layers = [_layer_params(l, per_layer) for l in range(depth)]
    ffn = []
    for l in range(depth):
        j = l // 2
        if l % 2 == 0:
            ffn.append(("dense", (w_ff_gate[j], w_ff_up[j], w_ff_down[j])))
        else:
            wr = _pad_cols(w_router[j], LANES)
            whi = wr.astype(BF16)
            wlo = (wr - whi.astype(F32)).astype(BF16)
            ffn.append(("moe", (whi, wlo, w_e_gate[j], w_e_up[j], w_e_down[j])))

    bp, mlen, d = mem_prompt.shape
    w_mkv = jnp.concatenate([w_mk, w_mv], axis=2).astype(BF16)
    pk, pv, p_mem_k, p_mem_v = _mem_kv(mem_prompt, norm_mem, w_mkv)
    zero_state = (jnp.zeros((1, bp, LRU_W), F32), jnp.zeros((1, bp, CONV_K - 1, LRU_W), F32),
                  jnp.zeros((1, bp, GLA_HEADS, GLA_DK, GLA_DV), F32),
                  jnp.zeros((1, bp, GDN_HEADS, GDN_DK, GDN_DV), F32),
                  jnp.zeros((1, bp, CONV_K - 1, 3 * GDN_HEADS * GDN_DK), F32))
    sp = x_prompt.shape[1]
    bs, ss = x_sample.shape[0], x_sample.shape[1]
    groups = [
        dict(x=x_prompt.reshape(bp * sp, d), batch=bp, seq=sp, row_off=0, states=zero_state,
             state_layer=lambda l: 0, mem=[(pk, pv, l) for l in range(depth)]),
        dict(x=x_sample.reshape(bs * ss, d), batch=bs, seq=ss, row_off=bp * sp,
             states=(state_lru_h, state_lru_conv, state_gla, state_gdn, state_gdn_conv),
             state_layer=lambda l: l, mem=[(cache_mem_k, cache_mem_v, l) for l in range(depth)]),
    ]
    (y_p, y_s), (p_st, s_st) = _run_layers(groups, layers, ffn, norm_final)

    return (y_p.reshape(bp, sp, d), y_s.reshape(bs, ss, d), p_st[0], p_st[1], p_st[2], p_st[3], p_st[4],
            p_mem_k, p_mem_v, s_st[0], s_st[1], s_st[2], s_st[3], s_st[4])
```

```python
import functools
import math

import jax
import jax.numpy as jnp
from jax import lax
from jax.experimental import pallas as pl
from jax.experimental.pallas import tpu as pltpu

F32 = jnp.float32
BF16 = jnp.bfloat16
EPS = 1e-6

D_MODEL = 1024
LRU_W = 512
LRU_BLOCKS = 8
LRU_C = 8.0
CONV_K = 4
GLA_HEADS = 4
GLA_DK = 32
GLA_DV = 64
GLA_RANK = 16
GLA_TAU = 16.0
GLA_SUB = 16
GDN_HEADS = 4
GDN_DK = 64
GDN_DV = 64
MIX_CHUNK = 64
MEM_HEADS = 4
MEM_HD = 256
N_EXPERTS = 8
LANES = 128
VMEM_LIMIT_BYTES = 48 * 1024 * 1024
EXPERT_VMEM_LIMIT_BYTES = 56 * 1024 * 1024


def _cparams(n_axes):
    return pltpu.CompilerParams(dimension_semantics=("arbitrary",) * n_axes,
                                vmem_limit_bytes=VMEM_LIMIT_BYTES)


def _mm(a, b):
    return jnp.dot(a.astype(BF16), b.astype(BF16), preferred_element_type=F32)


def _mm_nt(a, b):
    return lax.dot_general(a.astype(BF16), b.astype(BF16), (((1,), (1,)), ((), ())),
                           preferred_element_type=F32)


def _mm_tn(a, b):
    return lax.dot_general(a.astype(BF16), b.astype(BF16), (((0,), (0,)), ((), ())),
                           preferred_element_type=F32)


def _rms_rows(x, g):
    ms = jnp.mean(x * x, axis=-1, keepdims=True)
    return (x * lax.rsqrt(ms + EPS)) * g


def _softplus(x):
    return jnp.maximum(x, 0.0) + jnp.log1p(jnp.exp(-jnp.abs(x)))


def _sigmoid(x):
    return 1.0 / (1.0 + jnp.exp(-x))


def _silu(x):
    return x * _sigmoid(x)


def _gelu_tanh(x):
    c = 0.7978845608028654
    return x * (0.5 * (1.0 + jnp.tanh(c * (x + 0.044715 * (x * x * x)))))


def _seg_cumsum_rows(x, seg):
    rows = x.shape[0]
    tpos = lax.broadcasted_iota(jnp.int32, (rows, 1), 0) & (seg - 1)
    d = 1
    while d < seg:
        x = x + jnp.where(tpos >= d, pltpu.roll(x, d, axis=0), 0.0)
        d *= 2
    return x


def _head_rms(o, gain, n_heads, width):
    lane_head = lax.broadcasted_iota(jnp.int32, (1, n_heads * width), 1) // width
    sq = o * o
    inv = jnp.zeros_like(o)
    for h in range(n_heads):
        m = lane_head == h
        ms = jnp.sum(jnp.where(m, sq, 0.0), axis=-1, keepdims=True) * (1.0 / width)
        inv = jnp.where(m, lax.rsqrt(ms + EPS), inv)
    return (o * inv) * gain


def _stack_heads(x, n_heads, width):
    c = x.shape[0]
    t = jnp.concatenate([x] * n_heads, axis=0)
    row_head = lax.broadcasted_iota(jnp.int32, (n_heads * c, 1), 0) // c
    lane_head = lax.broadcasted_iota(jnp.int32, (1, n_heads * width), 1) // width
    return jnp.where(row_head == lane_head, t, 0.0)


def _unstack_heads(x, n_heads):
    c = x.shape[0] // n_heads
    o = x[0:c]
    for h in range(1, n_heads):
        o = o + x[h * c:(h + 1) * c]
    return o


def _rms_matmul_kernel(x_ref, g_ref, w_ref, *o_refs):
    h = _rms_rows(x_ref[...], g_ref[...]).astype(BF16)
    start = 0
    for o_ref in o_refs:
        n = o_ref.shape[1]
        o_ref[...] = jnp.dot(h, w_ref[:, start:start + n], preferred_element_type=F32)
        start += n


def _row_tile(t, pref):
    tile = min(pref, t)
    while t % tile or tile % 8:
        tile -= 8
    return tile


def _rms_matmul(x, g, w, widths, name, tm=512, row_off=0, n_rows=None):
    d = x.shape[1]
    t = x.shape[0] if n_rows is None else n_rows
    n = w.shape[1]
    tm = _row_tile(t, tm)
    off = row_off // tm
    out_specs = [pl.BlockSpec((tm, wd), lambda i: (i, 0)) for wd in widths]
    out_shape = [jax.ShapeDtypeStruct((t, wd), F32) for wd in widths]
    return pl.pallas_call(
        _rms_matmul_kernel,
        grid=(t // tm,),
        in_specs=[pl.BlockSpec((tm, d), lambda i: (i + off, 0)),
                  pl.BlockSpec((1, d), lambda i: (0, 0)),
                  pl.BlockSpec((d, n), lambda i: (0, 0))],
        out_specs=out_specs,
        out_shape=out_shape,
        compiler_params=_cparams(1),
        name=name,
    )(x, g.reshape(1, d), w)


def _mem_kv_kernel(m_ref, g_ref, w_ref, k_ref, v_ref, k5_ref, v5_ref):
    d = m_ref.shape[1]
    h = _rms_rows(m_ref[...], g_ref[...])
    kv = _mm(h, w_ref[...])
    k_ref[...] = kv[:, :d].astype(k_ref.dtype)
    v_ref[...] = kv[:, d:].astype(v_ref.dtype)
    for hd in range(MEM_HEADS):
        sl = slice(hd * MEM_HD, (hd + 1) * MEM_HD)
        k5_ref[:, hd, :] = kv[:, sl]
        v5_ref[:, hd, :] = kv[:, d + hd * MEM_HD:d + (hd + 1) * MEM_HD]


def _mem_kv(mem, g, w):
    bp, mlen, d = mem.shape
    depth = w.shape[0]
    flat = jax.ShapeDtypeStruct((depth, bp, mlen, d), BF16)
    split = jax.ShapeDtypeStruct((depth, bp, mlen, MEM_HEADS, MEM_HD), F32)
    fidx = lambda l, b: (l, b, 0, 0)
    sidx = lambda l, b: (l, b, 0, 0, 0)
    return pl.pallas_call(
        _mem_kv_kernel,
        grid=(depth, bp),
        in_specs=[pl.BlockSpec((None, mlen, d), lambda l, b: (b, 0, 0)),
                  pl.BlockSpec((None, 1, d), lambda l, b: (l, 0, 0)),
                  pl.BlockSpec((None, d, 2 * d), lambda l, b: (l, 0, 0))],
        out_specs=[pl.BlockSpec((None, None, mlen, d), fidx),
                   pl.BlockSpec((None, None, mlen, d), fidx),
                   pl.BlockSpec((None, None, mlen, MEM_HEADS, MEM_HD), sidx),
                   pl.BlockSpec((None, None, mlen, MEM_HEADS, MEM_HD), sidx)],
        out_shape=[flat, flat, split, split],
        compiler_params=_cparams(2),
        name="mem_kv",
    )(mem, g.reshape(depth, 1, d), w)


def _lru_coeffs(xc, aw_ref, ab_ref, xw_ref, xb_ref, lam_ref):
    r = _sigmoid(_mm(xc, aw_ref[...]) + ab_ref[...])
    ig = _sigmoid(_mm(xc, xw_ref[...]) + xb_ref[...])
    log_a = (-LRU_C * r) * _softplus(-lam_ref[...])
    a = jnp.exp(log_a)
    th = jnp.tanh(log_a)
    return a, jnp.sqrt((-2.0 * th) / (1.0 - th)) * (ig * xc)


def _lru_kernel(z_ref, h0_ref, buf0_ref, cw_ref, cb_ref, aw_ref, ab_ref, xw_ref, xb_ref, lam_ref,
                y_ref, hout_ref, bufout_ref, xs_ref, hc_ref, *, bt, tl, nt):
    w = LRU_W
    rows = bt * tl
    j = pl.program_id(0) % nt

    @pl.when(j == 0)
    def _():
        xs_ref[:, 5:8, :] = buf0_ref[...]
        hc_ref[...] = h0_ref[...]

    xs_ref[:, 8:, :] = z_ref[:, :w].reshape(bt, tl, w)
    gate = z_ref[:, w:]
    cw = cw_ref[...]
    xc = cb_ref[...] + xs_ref[:, 5:5 + tl, :] * cw[0:1]
    for k in range(1, CONV_K):
        xc = xc + xs_ref[:, 5 + k:5 + k + tl, :] * cw[k:k + 1]
    tail = xs_ref[:, 5 + tl:8 + tl, :]
    xs_ref[:, 5:8, :] = tail
    bufout_ref[...] = tail

    a, b = _lru_coeffs(xc.reshape(rows, w), aw_ref, ab_ref, xw_ref, xb_ref, lam_ref)

    sub = 8
    gps = tl // sub
    a3 = a.reshape(rows // sub, sub, w)
    b3 = b.reshape(rows // sub, sub, w)
    spos = lax.broadcasted_iota(jnp.int32, (1, sub, 1), 1)
    d = 1
    while d < sub:
        m = spos >= d
        b3 = jnp.where(m, a3 * pltpu.roll(b3, d, axis=1) + b3, b3)
        a3 = jnp.where(m, a3 * pltpu.roll(a3, d, axis=1), a3)
        d *= 2
    a4 = a3.reshape(bt, gps, sub, w)
    b4 = b3.reshape(bt, gps, sub, w)
    carry = hc_ref[...]
    hs = []
    for r in range(gps):
        hr = b4[:, r] + a4[:, r] * carry
        hs.append(hr)
        carry = hr[:, sub - 1:sub, :]
    h = jnp.stack(hs, axis=1).reshape(rows, w)
    hlast = carry
    hc_ref[...] = hlast
    hout_ref[...] = hlast
    y_ref[...] = h * _gelu_tanh(gate)


def _lru_mixer(z, h0, buf0, layer, cw, cb, aw, ab, xw, xb, lam, batch, seq):
    w = LRU_W
    tl = min(seq, 512)
    bt = min(batch, max(1, 512 // seq))
    nt = seq // tl
    rows = bt * tl
    grid = (batch * seq // rows,)
    if nt > 1:
        sidx = lambda i: (i // nt, 0, 0)
        lidx = lambda i: (layer, i // nt, 0, 0)
    else:
        sidx = lambda i: (i, 0, 0)
        lidx = lambda i: (layer, i, 0, 0)
    full2 = lambda i: (0, 0)
    return pl.pallas_call(
        functools.partial(_lru_kernel, bt=bt, tl=tl, nt=nt),
        grid=grid,
        in_specs=[pl.BlockSpec((rows, 2 * w), lambda i: (i, 0)),
                  pl.BlockSpec((None, bt, 1, w), lidx),
                  pl.BlockSpec((None, bt, CONV_K - 1, w), lidx),
                  pl.BlockSpec((CONV_K, w), full2),
                  pl.BlockSpec((1, w), full2),
                  pl.BlockSpec((w, w), full2),
                  pl.BlockSpec((1, w), full2),
                  pl.BlockSpec((w, w), full2),
                  pl.BlockSpec((1, w), full2),
                  pl.BlockSpec((1, w), full2)],
        out_specs=[pl.BlockSpec((rows, w), lambda i: (i, 0)),
                   pl.BlockSpec((bt, 1, w), sidx),
                   pl.BlockSpec((bt, CONV_K - 1, w), sidx)],
        out_shape=[jax.ShapeDtypeStruct((batch * seq, w), F32),
                   jax.ShapeDtypeStruct((batch, 1, w), F32),
                   jax.ShapeDtypeStruct((batch, CONV_K - 1, w), F32)],
        scratch_shapes=[pltpu.VMEM((bt, 8 + tl, w), F32), pltpu.VMEM((bt, 1, w), F32)],
        compiler_params=_cparams(1),
        name="lru_mixer",
    )(z, h0.reshape(h0.shape[0], batch, 1, w), buf0, cw, cb.reshape(1, w), aw, ab.reshape(1, w), xw,
      xb.reshape(1, w), lam.reshape(1, w))


def _gla_kernel(z_ref, s0_ref, w2_ref, gb_ref, gn_ref, y_ref, sout_ref, s_ref, *, c, sc, g):
    nh = GLA_HEADS
    kw = nh * GLA_DK
    vw = nh * GLA_DV

    @pl.when(pl.program_id(1) == 0)
    def _():
        s_ref[...] = jnp.zeros_like(s_ref)
        for b in range(g):
            for h in range(nh):
                s_ref[b, h * GLA_DK:(h + 1) * GLA_DK, h * GLA_DV:(h + 1) * GLA_DV] = s0_ref[b, h]

    bs = range(g)
    q = [z_ref[b, :, 0:kw] * (GLA_DK ** -0.5) for b in bs]
    k = [z_ref[b, :, kw:2 * kw] for b in bs]
    v = [z_ref[b, :, 2 * kw:2 * kw + vw] for b in bs]
    gk = [-_softplus(-(_mm(z_ref[b, :, 2 * kw + 2 * vw:], w2_ref[...]) + gb_ref[...])) / GLA_TAU
          for b in bs]
    gcum = [_seg_cumsum_rows(gk[b], sc) for b in bs]
    qp = [q[b] * jnp.exp(gcum[b]) for b in bs]

    ri = lax.broadcasted_iota(jnp.int32, (nh * c, nh * c), 0)
    ci = lax.broadcasted_iota(jnp.int32, (nh * c, nh * c), 1)
    keep = (ri // sc == ci // sc) & (ci <= ri)
    a = [_mm_nt(_stack_heads(qp[b], nh, GLA_DK), _stack_heads(k[b] * jnp.exp(-gcum[b]), nh, GLA_DK))
         for b in bs]
    o = [_unstack_heads(_mm(jnp.where(keep, a[b], 0.0), _stack_heads(v[b], nh, GLA_DV)), nh)
         for b in bs]

    s = [s_ref[b] for b in bs]
    eye = (lax.broadcasted_iota(jnp.int32, (kw, kw), 0) ==
           lax.broadcasted_iota(jnp.int32, (kw, kw), 1))
    bd = (lax.broadcasted_iota(jnp.int32, (kw, vw), 0) // GLA_DK ==
          lax.broadcasted_iota(jnp.int32, (kw, vw), 1) // GLA_DV)
    o_inter = [[] for _ in bs]
    for i in range(c // sc):
        lo, hi = i * sc, (i + 1) * sc
        for b in bs:
            o_inter[b].append(_mm(qp[b][lo:hi], s[b]))
            glast = gcum[b][hi - 1:hi]
            kpp = k[b][lo:hi] * jnp.exp(glast - gcum[b][lo:hi])
            u = _mm_tn(kpp, v[b][lo:hi])
            dcol = jnp.sum(jnp.where(eye, jnp.exp(glast), 0.0), axis=1, keepdims=True)
            s[b] = dcol * s[b] + jnp.where(bd, u, 0.0)
    for b in bs:
        s_ref[b] = s[b]
        for h in range(nh):
            sout_ref[b, h] = s[b][h * GLA_DK:(h + 1) * GLA_DK, h * GLA_DV:(h + 1) * GLA_DV]
        ob = o[b] + jnp.concatenate(o_inter[b], axis=0)
        gate = z_ref[b, :, 2 * kw + vw:2 * kw + 2 * vw]
        y_ref[b] = _head_rms(ob, gn_ref[...], nh, GLA_DV) * _silu(gate)


def _gla_mixer(z, s0, layer, w2, gb, gn, batch, seq):
    c = min(seq, MIX_CHUNK)
    sc = min(c, GLA_SUB)
    nt = seq // c
    g = _mixer_group(batch, c)
    kw, vw = GLA_HEADS * GLA_DK, GLA_HEADS * GLA_DV
    zw = z.shape[2]
    full2 = lambda i, j: (0, 0)
    sblk = (g, GLA_HEADS, GLA_DK, GLA_DV)
    return pl.pallas_call(
        functools.partial(_gla_kernel, c=c, sc=sc, g=g),
        grid=(batch // g, nt),
        in_specs=[pl.BlockSpec((g, c, zw), lambda i, j: (i, j, 0)),
                  pl.BlockSpec((None,) + sblk, lambda i, j: (layer, i, 0, 0, 0)),
                  pl.BlockSpec((LANES, kw), full2),
                  pl.BlockSpec((1, kw), full2),
                  pl.BlockSpec((1, vw), full2)],
        out_specs=[pl.BlockSpec((g, c, vw), lambda i, j: (i, j, 0)),
                   pl.BlockSpec(sblk, lambda i, j: (i, 0, 0, 0))],
        out_shape=[jax.ShapeDtypeStruct((batch, seq, vw), F32),
                   jax.ShapeDtypeStruct((batch, GLA_HEADS, GLA_DK, GLA_DV), F32)],
        scratch_shapes=[pltpu.VMEM((g, kw, vw), F32)],
        compiler_params=_cparams(2),
        name="gla_mixer",
    )(z, s0, w2, gb.reshape(1, kw), gn.reshape(1, vw))


def _gdn_prep(z_ref, cw_ref, alog_ref, dtb_ref, bufout_ref, xs_ref, *, c):
    nh = GDN_HEADS
    hw = nh * GDN_DK
    cw3 = 3 * hw

    xs_ref[8:, :] = z_ref[:, 0:cw3]
    cw = cw_ref[...]
    qkv = xs_ref[5:5 + c, :] * cw[0:1]
    for kk in range(1, CONV_K):
        qkv = qkv + xs_ref[5 + kk:5 + kk + c, :] * cw[kk:kk + 1]
    tail = xs_ref[5 + c:8 + c, :]
    xs_ref[5:8, :] = tail
    bufout_ref[...] = tail
    qkv = _silu(qkv)
    zg = z_ref[:, cw3:cw3 + hw]
    sm = z_ref[:, cw3 + hw:]

    lane_head = lax.broadcasted_iota(jnp.int32, (1, hw), 1) // GDN_DK

    def l2n(x):
        sq = x * x
        inv = jnp.zeros_like(x)
        for h in range(nh):
            m = lane_head == h
            ss = jnp.sum(jnp.where(m, sq, 0.0), axis=-1, keepdims=True)
            inv = jnp.where(m, lax.rsqrt(ss + EPS), inv)
        return x * inv

    q = l2n(qkv[:, 0:hw]) * (GDN_DK ** -0.5)
    k = l2n(qkv[:, hw:2 * hw])
    v = qkv[:, 2 * hw:3 * hw]
    beta = _sigmoid(sm)
    glog = -jnp.exp(alog_ref[...]) * _softplus(sm + dtb_ref[...])
    gcum = _seg_cumsum_rows(glog, c)

    n = nh * c
    bcol = jnp.concatenate([beta[:, h:h + 1] for h in range(nh)], axis=0)
    gcol = jnp.concatenate([gcum[:, nh + h:nh + h + 1] for h in range(nh)], axis=0)
    glast = jnp.concatenate(
        [jnp.broadcast_to(gcum[c - 1:c, nh + h:nh + h + 1], (c, 1)) for h in range(nh)], axis=0)
    ri = lax.broadcasted_iota(jnp.int32, (n, n), 0)
    ci = lax.broadcasted_iota(jnp.int32, (n, n), 1)
    grow = jnp.sum(jnp.where(ri == ci, gcol, 0.0), axis=0, keepdims=True)
    same = ri // c == ci // c
    incl = same & (ci <= ri)
    strict = same & (ci < ri)
    dec = jnp.where(incl, jnp.exp(jnp.where(incl, gcol - grow, 0.0)), 0.0)

    sdec = jnp.concatenate(
        [jnp.broadcast_to(jnp.exp(gcum[c - 1:c, nh + h:nh + h + 1]), (GDN_DK, 1)) for h in range(nh)],
        axis=0)
    return dict(ks=_stack_heads(k, nh, GDN_DK), qs=_stack_heads(q, nh, GDN_DK),
                vs=_stack_heads(v, nh, GDN_DV), bcol=bcol, egc=jnp.exp(gcol), dec=dec,
                kdec=jnp.exp(glast - gcol), sdec=sdec, zg=zg)


def _gdn_kernel(z_ref, s0_ref, buf0_ref, cw_ref, alog_ref, dtb_ref, gn_ref,
                y_ref, sout_ref, bufout_ref, xs_ref, s_ref, *, c, g):
    nh = GDN_HEADS
    hw = nh * GDN_DK
    n = nh * c

    @pl.when(pl.program_id(1) == 0)
    def _():
        xs_ref[:, 5:8, :] = buf0_ref[...]
        s_ref[...] = jnp.zeros_like(s_ref)
        for b in range(g):
            for h in range(nh):
                s_ref[b, h * GDN_DK:(h + 1) * GDN_DK, h * GDN_DV:(h + 1) * GDN_DV] = s0_ref[b, h]

    bs = range(g)
    pr = [_gdn_prep(z_ref.at[b], cw_ref, alog_ref, dtb_ref, bufout_ref.at[b], xs_ref.at[b], c=c)
          for b in bs]
    ri = lax.broadcasted_iota(jnp.int32, (n, n), 0)
    ci = lax.broadcasted_iota(jnp.int32, (n, n), 1)
    same = ri // c == ci // c
    incl = same & (ci <= ri)
    strict = same & (ci < ri)
    kq = [_mm_nt(jnp.concatenate([pr[b]["ks"], pr[b]["qs"]], axis=0), pr[b]["ks"]) for b in bs]

    p = [jnp.where(strict, -(pr[b]["bcol"] * kq[b][0:n]) * pr[b]["dec"], 0.0) for b in bs]
    tinv = [jnp.where(ri == ci, 1.0, 0.0) + p[b] for b in bs]
    span = 2
    while span < c:
        p = [_mm(p[b], p[b]) for b in bs]
        tinv = [tinv[b] + _mm(tinv[b], p[b]) for b in bs]
        span *= 2

    s = [s_ref[b] for b in bs]
    uw = [_mm(tinv[b], jnp.concatenate([pr[b]["vs"] * pr[b]["bcol"],
                                        pr[b]["ks"] * (pr[b]["bcol"] * pr[b]["egc"])], axis=1))
          for b in bs]
    qw = [_mm(jnp.concatenate([pr[b]["qs"] * pr[b]["egc"], uw[b][:, hw:]], axis=0), s[b])
          for b in bs]
    vnew = [uw[b][:, 0:hw] - qw[b][n:] for b in bs]
    av = [_mm(jnp.where(incl, kq[b][n:] * pr[b]["dec"], 0.0), vnew[b]) for b in bs]
    kv = [_mm_tn(pr[b]["ks"] * pr[b]["kdec"], vnew[b]) for b in bs]
    for b in bs:
        o = _unstack_heads(qw[b][0:n] + av[b], nh)
        sn = pr[b]["sdec"] * s[b] + kv[b]
        s_ref[b] = sn
        for h in range(nh):
            sout_ref[b, h] = sn[h * GDN_DK:(h + 1) * GDN_DK, h * GDN_DV:(h + 1) * GDN_DV]
        y_ref[b] = _head_rms(o, gn_ref[...], nh, GDN_DV) * _silu(pr[b]["zg"])


def _mixer_group(batch, c):
    return min(batch, max(8, 128 // c))


def _gdn_mixer(z, s0, buf0, layer, cw, alog, dtb, gn, batch, seq):
    c = min(seq, MIX_CHUNK)
    nt = seq // c
    g = _mixer_group(batch, c)
    hw = GDN_HEADS * GDN_DK
    zw = z.shape[2]
    full2 = lambda i, j: (0, 0)
    sblk = (g, GDN_HEADS, GDN_DK, GDN_DV)
    return pl.pallas_call(
        functools.partial(_gdn_kernel, c=c, g=g),
        grid=(batch // g, nt),
        in_specs=[pl.BlockSpec((g, c, zw), lambda i, j: (i, j, 0)),
                  pl.BlockSpec((None,) + sblk, lambda i, j: (layer, i, 0, 0, 0)),
                  pl.BlockSpec((None, g, CONV_K - 1, 3 * hw), lambda i, j: (layer, i, 0, 0)),
                  pl.BlockSpec((CONV_K, 3 * hw), full2),
                  pl.BlockSpec((1, LANES), full2),
                  pl.BlockSpec((1, LANES), full2),
                  pl.BlockSpec((1, hw), full2)],
        out_specs=[pl.BlockSpec((g, c, hw), lambda i, j: (i, j, 0)),
                   pl.BlockSpec(sblk, lambda i, j: (i, 0, 0, 0)),
                   pl.BlockSpec((g, CONV_K - 1, 3 * hw), lambda i, j: (i, 0, 0))],
        out_shape=[jax.ShapeDtypeStruct((batch, seq, hw), F32),
                   jax.ShapeDtypeStruct((batch, GDN_HEADS, GDN_DK, GDN_DV), F32),
                   jax.ShapeDtypeStruct((batch, CONV_K - 1, 3 * hw), F32)],
        scratch_shapes=[pltpu.VMEM((g, 8 + c, 3 * hw), F32), pltpu.VMEM((g, hw, hw), F32)],
        compiler_params=_cparams(2),
        name="gdn_mixer",
    )(z, s0, buf0, cw, alog, dtb, gn.reshape(1, hw))


def _outproj_kernel(x_ref, yl_ref, yg_ref, yd_ref, wo_ref, gq_ref, wq_ref, xn_ref, q_ref):
    lw = LRU_W
    gw = GLA_HEADS * GLA_DV
    y = _mm(yl_ref[...], wo_ref[0:lw, :])
    y = y + _mm(yg_ref[...], wo_ref[lw:lw + gw, :])
    y = y + _mm(yd_ref[...], wo_ref[lw + gw:, :])
    xn = x_ref[...] + y
    xn_ref[...] = xn
    q_ref[...] = _mm(_rms_rows(xn, gq_ref[...]), wq_ref[...])


def _outproj(x, yl, yg, yd, wo, gq, wq, tm=512, row_off=0):
    d = x.shape[1]
    t = yl.shape[0]
    tm = _row_tile(t, tm)
    off = row_off // tm
    row = lambda i: (i, 0)
    full2 = lambda i: (0, 0)
    return pl.pallas_call(
        _outproj_kernel,
        grid=(t // tm,),
        in_specs=[pl.BlockSpec((tm, d), lambda i: (i + off, 0)),
                  pl.BlockSpec((tm, yl.shape[1]), row),
                  pl.BlockSpec((tm, yg.shape[1]), row),
                  pl.BlockSpec((tm, yd.shape[1]), row),
                  pl.BlockSpec((d, d), full2),
                  pl.BlockSpec((1, d), full2),
                  pl.BlockSpec((d, d), full2)],
        out_specs=[pl.BlockSpec((tm, d), row), pl.BlockSpec((tm, d), row)],
        out_shape=[jax.ShapeDtypeStruct((t, d), F32), jax.ShapeDtypeStruct((t, d), F32)],
        compiler_params=_cparams(1),
        name="outproj_qproj",
    )(x, yl, yg, yd, wo, gq.reshape(1, d), wq)


def _attn_kernel(x_ref, q_ref, k_ref, v_ref, wo_ref, o_ref):
    hd = MEM_HD
    acc = x_ref[...]
    for h in range(MEM_HEADS):
        sl = slice(h * hd, (h + 1) * hd)
        s = _mm_nt(q_ref[:, sl], k_ref[:, sl]) * (hd ** -0.5)
        m = jnp.max(s, axis=-1, keepdims=True)
        p = jnp.exp(s - m)
        l = jnp.sum(p, axis=-1, keepdims=True)
        oh = _mm(p, v_ref[:, sl]) / l
        acc = acc + _mm(oh, wo_ref[sl, :])
    o_ref[...] = acc


def _attn_heads_kernel(q_ref, k_ref, v_ref, o_ref, *, gb, tl):
    nh, hd = MEM_HEADS, MEM_HD
    m = k_ref.shape[1]
    row_head = lax.broadcasted_iota(jnp.int32, (nh * tl, 1), 0) // tl
    col_head = lax.broadcasted_iota(jnp.int32, (1, m * nh), 1) % nh
    for b in range(gb):
        kf = k_ref[b].reshape(m * nh, hd)
        vf = v_ref[b].reshape(m * nh, hd)
        rows = slice(b * tl, (b + 1) * tl)
        qs = jnp.concatenate([q_ref[rows, h * hd:(h + 1) * hd] for h in range(nh)], axis=0)
        s = _mm_nt(qs, kf) * (hd ** -0.5)
        s = jnp.where(row_head == col_head, s, -jnp.inf)
        mx = jnp.max(s, axis=-1, keepdims=True)
        p = jnp.exp(s - mx)
        l = jnp.sum(p, axis=-1, keepdims=True)
        o = _mm(p, vf) / l
        for h in range(nh):
            o_ref[rows, h * hd:(h + 1) * hd] = o[h * tl:(h + 1) * tl]


def _oproj_kernel(x_ref, a_ref, wo_ref, *rest):
    o_ref = rest[-1]
    o_ref[...] = x_ref[...] + _mm(a_ref[...], wo_ref[...])


def _attn_first_kernel(x_ref, q_ref, k_ref, v_ref, wo_ref, o_ref, *, n_real):
    i = pl.program_id(0)

    @pl.when(i < n_real)
    def _():
        _attn_kernel(x_ref, q_ref, k_ref, v_ref, wo_ref, o_ref)

    @pl.when(i >= n_real)
    def _():
        o_ref[...] = jnp.zeros_like(o_ref)


def _attn_into_kernel(x_ref, q_ref, k_ref, v_ref, wo_ref, joint_ref, o_ref):
    _attn_kernel(x_ref, q_ref, k_ref, v_ref, wo_ref, o_ref)


def _attention(x, q, k, v, layer, wo, batch, seq, joint, row_off, total_rows):
    t, d = x.shape
    out_shape = jax.ShapeDtypeStruct((total_rows, d), F32)
    extra_specs = [] if joint is None else [pl.BlockSpec(memory_space=pl.ANY)]
    extra_args = () if joint is None else (joint,)
    if k.ndim == 4:
        tl = min(seq, 512)
        nl = seq // tl
        off = row_off // tl
        kblk = (None, None) + k.shape[2:]
        n_real = batch * nl
        if joint is None:
            assert row_off == 0
            n_steps = pl.cdiv(total_rows, tl)
            body = functools.partial(_attn_first_kernel, n_real=n_real)
        else:
            n_steps = n_real
            body = _attn_into_kernel
        kidx = lambda i: (layer, jnp.minimum(i // nl, batch - 1), 0, 0)
        row = lambda i: (jnp.minimum(i, n_real - 1), 0)
        return pl.pallas_call(
            body,
            grid=(n_steps,),
            in_specs=[pl.BlockSpec((tl, d), row),
                      pl.BlockSpec((tl, d), row),
                      pl.BlockSpec(kblk, kidx),
                      pl.BlockSpec(kblk, kidx),
                      pl.BlockSpec((d, d), lambda i: (0, 0))] + extra_specs,
            out_specs=pl.BlockSpec((tl, d), lambda i: (i + off, 0)),
            out_shape=out_shape,
            input_output_aliases={} if joint is None else {5: 0},
            compiler_params=_cparams(1),
            name="mem_attention",
        )(x, q, k, v, wo, *extra_args)
    gb = 4 if batch % 4 == 0 else 2
    kblk = (None, gb) + k.shape[2:]
    kidx = lambda i: (layer, i, 0, 0, 0)
    att = pl.pallas_call(
        functools.partial(_attn_heads_kernel, gb=gb, tl=seq),
        grid=(batch // gb,),
        in_specs=[pl.BlockSpec((gb * seq, d), lambda i: (i, 0)),
                  pl.BlockSpec(kblk, kidx),
                  pl.BlockSpec(kblk, kidx)],
        out_specs=pl.BlockSpec((gb * seq, d), lambda i: (i, 0)),
        out_shape=jax.ShapeDtypeStruct((t, d), F32),
        compiler_params=_cparams(1),
        name="mem_attention_heads",
    )(q, k, v)
    tm = _row_tile(t, 512)
    off = row_off // tm
    return pl.pallas_call(
        _oproj_kernel,
        grid=(t // tm,),
        in_specs=[pl.BlockSpec((tm, d), lambda i: (i, 0)),
                  pl.BlockSpec((tm, d), lambda i: (i, 0)),
                  pl.BlockSpec((d, d), lambda i: (0, 0))] + extra_specs,
        out_specs=pl.BlockSpec((tm, d), lambda i: (i + off, 0)),
        out_shape=out_shape,
        input_output_aliases={} if joint is None else {3: 0},
        compiler_params=_cparams(1),
        name="mem_oproj",
    )(x, att, wo, *extra_args)


def _ffn_kernel(x_ref, g_ref, wg_ref, wu_ref, wd_ref, o_ref, h_ref):
    f = pl.program_id(1)

    @pl.when(f == 0)
    def _():
        h_ref[...] = _rms_rows(x_ref[...], g_ref[...]).astype(BF16)

    def chunk():
        h = h_ref[...]
        a = _mm(h, wg_ref[...])
        u = _mm(h, wu_ref[...])
        return _mm(_silu(a) * u, wd_ref[...])

    @pl.when(f == 0)
    def _():
        o_ref[...] = x_ref[...] + chunk()

    @pl.when(f > 0)
    def _():
        o_ref[...] += chunk()


def _ffn_dense(x, g, wg, wu, wd, tm=1024, tf=512):
    t, d = x.shape
    dff = wg.shape[1]
    tm = _row_tile(t, tm)
    return pl.pallas_call(
        _ffn_kernel,
        grid=(t // tm, dff // tf),
        in_specs=[pl.BlockSpec((tm, d), lambda i, f: (i, 0)),
                  pl.BlockSpec((1, d), lambda i, f: (0, 0)),
                  pl.BlockSpec((d, tf), lambda i, f: (0, f)),
                  pl.BlockSpec((d, tf), lambda i, f: (0, f)),
                  pl.BlockSpec((tf, d), lambda i, f: (f, 0))],
        out_specs=pl.BlockSpec((tm, d), lambda i, f: (i, 0)),
        out_shape=jax.ShapeDtypeStruct((t, d), F32),
        scratch_shapes=[pltpu.VMEM((tm, d), BF16)],
        compiler_params=_cparams(2),
        name="ffn_dense",
    )(x, g.reshape(1, d), wg, wu, wd)


def _router_kernel(x_ref, g_ref, whi_ref, wlo_ref, info_ref, cnt_ref, carry_ref, *, tm):
    i = pl.program_id(0)

    @pl.when(i == 0)
    def _():
        carry_ref[...] = jnp.zeros_like(carry_ref)

    h = _rms_rows(x_ref[...], g_ref[...])
    hhi = h.astype(BF16)
    hlo = (h - hhi.astype(F32)).astype(BF16)
    whi = whi_ref[...]
    logits = (jnp.dot(hhi, whi, preferred_element_type=F32)
              + jnp.dot(hlo, whi, preferred_element_type=F32)
              + jnp.dot(hhi, wlo_ref[...], preferred_element_type=F32))
    lane = lax.broadcasted_iota(jnp.int32, (tm, LANES), 1)
    neg = jnp.float32(-jnp.inf)
    logits = jnp.where(lane < N_EXPERTS, logits, neg)
    m1 = jnp.max(logits, axis=-1, keepdims=True)
    i1 = jnp.min(jnp.where(logits == m1, lane, LANES), axis=-1, keepdims=True)
    rest = jnp.where(lane == i1, neg, logits)
    m2 = jnp.max(rest, axis=-1, keepdims=True)
    i2 = jnp.min(jnp.where(rest == m2, lane, LANES), axis=-1, keepdims=True)
    e = jnp.exp(m2 - m1)
    g1 = 1.0 / (1.0 + e)
    g2 = e / (1.0 + e)
    oh1 = jnp.where(lane == i1, 1.0, 0.0)
    oh2 = jnp.where(lane == i2, 1.0, 0.0)
    oh = oh1 + oh2
    ri = lax.broadcasted_iota(jnp.int32, (tm, tm), 0)
    ci = lax.broadcasted_iota(jnp.int32, (tm, tm), 1)
    tri = jnp.where(ci < ri, 1.0, 0.0)
    before = _mm(tri, oh) + carry_ref[0:1, :]
    r1 = jnp.sum(oh1 * before, axis=-1, keepdims=True)
    r2 = jnp.sum(oh2 * before, axis=-1, keepdims=True)
    carry = carry_ref[0:1, :] + jnp.sum(oh, axis=0, keepdims=True)
    carry_ref[...] = jnp.broadcast_to(carry, carry_ref.shape)
    cnt_ref[...] = jnp.broadcast_to(carry, cnt_ref.shape)
    info = jnp.where(lane == 0, i1.astype(F32), 0.0)
    info = jnp.where(lane == 1, i2.astype(F32), info)
    info = jnp.where(lane == 2, r1, info)
    info = jnp.where(lane == 3, r2, info)
    info = jnp.where(lane == 4, g1, info)
    info = jnp.where(lane == 5, g2, info)
    info_ref[...] = info


def _router(x, g, whi, wlo, tm):
    t, d = x.shape
    return pl.pallas_call(
        functools.partial(_router_kernel, tm=tm),
        grid=(t // tm,),
        in_specs=[pl.BlockSpec((tm, d), lambda i: (i, 0)),
                  pl.BlockSpec((1, d), lambda i: (0, 0)),
                  pl.BlockSpec((d, LANES), lambda i: (0, 0)),
                  pl.BlockSpec((d, LANES), lambda i: (0, 0))],
        out_specs=[pl.BlockSpec((tm, LANES), lambda i: (i, 0)),
                   pl.BlockSpec((8, LANES), lambda i: (i, 0))],
        out_shape=[jax.ShapeDtypeStruct((t, LANES), F32),
                   jax.ShapeDtypeStruct((t // tm * 8, LANES), F32)],
        scratch_shapes=[pltpu.VMEM((8, LANES), F32)],
        compiler_params=_cparams(1),
        name="moe_router",
    )(x, g.reshape(1, d), whi, wlo)


MOE_CHUNK = 32


def _dispatch_kernel(st_ref, nc_ref, pos_ref, zs_ref, zf_ref, x_ref, g_ref, ct_ref, xs_hbm, w_ref, zbuf_ref,
                     zsem, sem, *, rows, sub, n_blk, win):
    i = pl.program_id(0)

    @pl.when(i == 0)
    def _():
        zbuf_ref[...] = jnp.zeros_like(zbuf_ref)
        for e in range(N_EXPERTS):
            dst = xs_hbm.at[pl.ds(pl.multiple_of(zs_ref[e], 8), sub)]
            pltpu.make_async_copy(zbuf_ref, dst, zsem).start()
            pltpu.make_async_copy(zbuf_ref, dst, zsem).wait()

        def fill(j, carry):
            @pl.when(zf_ref[j] == 1)
            def _():
                dst = xs_hbm.at[pl.ds(pl.multiple_of(j * sub, sub), sub)]
                pltpu.make_async_copy(zbuf_ref, dst, zsem).start()
            return carry

        def drain(j, carry):
            @pl.when(zf_ref[j] == 1)
            def _():
                dst = xs_hbm.at[pl.ds(pl.multiple_of(j * sub, sub), sub)]
                pltpu.make_async_copy(zbuf_ref, dst, zsem).wait()
            return carry

        lax.fori_loop(0, n_blk, fill, 0)
        lax.fori_loop(0, n_blk, drain, 0)

    ck = MOE_CHUNK
    n_steps = pl.num_programs(0)

    def chunk_copy(step, e, c):
        t = step * N_EXPERTS + e
        src = w_ref.at[step % 2, pl.ds(pl.multiple_of(pos_ref[t] + c * ck, ck), ck)]
        dst = xs_hbm.at[pl.ds(pl.multiple_of(st_ref[t] + c * ck, 8), ck)]
        return pltpu.make_async_copy(src, dst, sem)

    def all_chunks(step, fn):
        for e in range(N_EXPERTS):
            def body(c, carry, e=e):
                fn(chunk_copy(step, e, c))
                return carry
            lax.fori_loop(0, nc_ref[step * N_EXPERTS + e], body, 0)

    h = _rms_rows(x_ref[...], g_ref[...])
    wrow = lax.broadcasted_iota(jnp.int32, (win, rows), 0)
    cols = ct_ref[...]
    onehot = jnp.where(wrow == cols[0:1, :], 1.0, 0.0) + jnp.where(wrow == cols[1:2, :], 1.0, 0.0)
    w_ref[i % 2] = _mm(onehot, h)

    @pl.when(i >= 1)
    def _():
        all_chunks(i - 1, lambda cp: cp.wait())

    all_chunks(i, lambda cp: cp.start())

    @pl.when(i == n_steps - 1)
    def _():
        all_chunks(i, lambda cp: cp.wait())


def _dispatch(seg_start, n_chunks, win_pos, zero_start, zero_blk, cols_t, x, g, sub, rows, win):
    t, d = x.shape
    n_blk = zero_blk.shape[0]
    n_rows = n_blk * sub
    return pl.pallas_call(
        functools.partial(_dispatch_kernel, rows=rows, sub=sub, n_blk=n_blk, win=win),
        grid_spec=pltpu.PrefetchScalarGridSpec(
            num_scalar_prefetch=5,
            grid=(t // rows,),
            in_specs=[pl.BlockSpec((rows, d), lambda i, *_: (i, 0)),
                      pl.BlockSpec((1, d), lambda i, *_: (0, 0)),
                      pl.BlockSpec((2, rows), lambda i, *_: (0, i))],
            out_specs=pl.BlockSpec(memory_space=pl.ANY),
            scratch_shapes=[pltpu.VMEM((2, win, d), F32), pltpu.VMEM((sub, d), F32),
                            pltpu.SemaphoreType.DMA(()), pltpu.SemaphoreType.DMA(())]),
        out_shape=jax.ShapeDtypeStruct((n_rows, d), F32),
        compiler_params=_cparams(1),
        name="moe_dispatch",
    )(seg_start, n_chunks, win_pos, zero_start, zero_blk, x, g.reshape(1, d), cols_t)


def _expert_kernel(be_ref, ns_ref, xi_ref, x_ref, wg_ref, wu_ref, wd_ref, o_ref, *, sub, n_sub):
    i = pl.program_id(0)
    f = pl.program_id(1)
    ns = ns_ref[i]

    for s in range(n_sub):
        @pl.when((f == 0) & (s >= ns))
        def _():
            o_ref[s * sub:(s + 1) * sub, :] = jnp.zeros((sub, o_ref.shape[1]), o_ref.dtype)

    def swiglu(rows, first):
        h = x_ref[rows, :]
        a = _mm(h, wg_ref[...])
        u = _mm(h, wu_ref[...])
        y = _mm(_silu(a) * u, wd_ref[...])
        if first:
            o_ref[rows, :] = y
        else:
            o_ref[rows, :] += y

    for first in (True, False):
        is_first = (f == 0) if first else (f > 0)
        for s in range(0, n_sub, 2):
            if s + 2 <= n_sub:
                @pl.when((s + 2 <= ns) & is_first)
                def _():
                    swiglu(slice(s * sub, (s + 2) * sub), first)

            @pl.when((s + 1 == ns) & is_first)
            def _():
                swiglu(slice(s * sub, (s + 1) * sub), first)


def _expert_ffn(blk_exp, n_valid_sub, x_blk, xs, wg, wu, wd, sb, sub, tf=512):
    p, d = xs.shape
    dff = wg.shape[2]
    n_super = n_valid_sub.shape[0]
    nf = dff // tf

    def fidx(i, f, ns):
        used = jnp.minimum(ns[i], 1)
        return f * used + (nf - 1) * (1 - used)

    return pl.pallas_call(
        functools.partial(_expert_kernel, sub=sub, n_sub=sb // sub),
        grid_spec=pltpu.PrefetchScalarGridSpec(
            num_scalar_prefetch=3,
            grid=(n_super, nf),
            in_specs=[pl.BlockSpec((sb, d), lambda i, f, be, ns, xi: (xi[i], 0)),
                      pl.BlockSpec((None, d, tf), lambda i, f, be, ns, xi: (be[i], 0, fidx(i, f, ns))),
                      pl.BlockSpec((None, d, tf), lambda i, f, be, ns, xi: (be[i], 0, fidx(i, f, ns))),
                      pl.BlockSpec((None, tf, d), lambda i, f, be, ns, xi: (be[i], fidx(i, f, ns), 0))],
            out_specs=pl.BlockSpec((sb, d), lambda i, f, be, ns, xi: (i, 0))),
        out_shape=jax.ShapeDtypeStruct((n_super * sb, d), F32),
        compiler_params=pltpu.CompilerParams(dimension_semantics=("arbitrary", "arbitrary"),
                                             vmem_limit_bytes=EXPERT_VMEM_LIMIT_BYTES),
        name="moe_experts",
    )(blk_exp, n_valid_sub, x_blk, xs, wg, wu, wd)


def _combine_kernel(st_ref, nc_ref, pos_ref, x_ref, info_ref, col_ref, ys_hbm, gf_ref, o_ref, buf_ref, sem,
                    *, rows, final_norm, blk_off, win):
    ck = MOE_CHUNK
    i = pl.program_id(0)

    def chunk_copy(step, e, c):
        t = (step + blk_off) * N_EXPERTS + e
        src = ys_hbm.at[pl.ds(pl.multiple_of(st_ref[t] + c * ck, 8), ck)]
        dst = buf_ref.at[step % 2, pl.ds(pl.multiple_of(pos_ref[t] + c * ck, ck), ck)]
        return pltpu.make_async_copy(src, dst, sem.at[step % 2])

    def all_chunks(step, fn):
        for e in range(N_EXPERTS):
            def body(c, carry, e=e):
                fn(chunk_copy(step, e, c))
                return carry
            lax.fori_loop(0, nc_ref[(step + blk_off) * N_EXPERTS + e], body, 0)

    @pl.when(i == 0)
    def _():
        buf_ref[...] = jnp.zeros_like(buf_ref)
        all_chunks(i, lambda cp: cp.start())

    @pl.when(i + 1 < pl.num_programs(0))
    def _():
        all_chunks(i + 1, lambda cp: cp.start())

    info = info_ref[...]
    cols = col_ref[...]
    lane = lax.broadcasted_iota(jnp.int32, (rows, win), 1)
    p = (jnp.where(cols[:, 0:1] == lane, info[:, 4:5], 0.0)
         + jnp.where(cols[:, 1:2] == lane, info[:, 5:6], 0.0)).astype(BF16)
    all_chunks(i, lambda cp: cp.wait())
    acc = x_ref[...] + _mm(p, buf_ref[i % 2])
    if final_norm:
        acc = _rms_rows(acc, gf_ref[...])
    o_ref[...] = acc


def _combine(seg_start, n_chunks, win_pos, cols, x, info, ys, gf, final_norm, row_off, n_rows, rows, win):
    d = x.shape[1]
    off = row_off // rows
    return pl.pallas_call(
        functools.partial(_combine_kernel, rows=rows, final_norm=final_norm, blk_off=off, win=win),
        grid_spec=pltpu.PrefetchScalarGridSpec(
            num_scalar_prefetch=3,
            grid=(n_rows // rows,),
            in_specs=[pl.BlockSpec((rows, d), lambda i, *_: (i + off, 0)),
                      pl.BlockSpec((rows, LANES), lambda i, *_: (i + off, 0)),
                      pl.BlockSpec((rows, 2), lambda i, *_: (i + off, 0)),
                      pl.BlockSpec(memory_space=pl.ANY),
                      pl.BlockSpec((1, d), lambda i, *_: (0, 0))],
            out_specs=pl.BlockSpec((rows, d), lambda i, *_: (i, 0)),
            scratch_shapes=[pltpu.VMEM((2, win, d), F32), pltpu.SemaphoreType.DMA((2,))]),
        out_shape=jax.ShapeDtypeStruct((n_rows, d), F32),
        compiler_params=_cparams(1),
        name="moe_combine",
    )(seg_start, n_chunks, win_pos, x, info, cols, ys, gf.reshape(1, d))


def _moe_ffn(x, g, whi, wlo, wg, wu, wd, gf, final_norm, groups, sub=512):
    t, d = x.shape
    sb = 2048 if 2 * t >= 16 * 1024 else sub
    tb = t
    for off, n in groups:
        tb = math.gcd(tb, math.gcd(off, n))
    tb = _row_tile(tb, 512)
    info, cnt = _router(x, g, whi, wlo, tb)
    e = info[:, 0:2].astype(jnp.int32)
    rank = info[:, 2:4].astype(jnp.int32)
    after = cnt[::8, :N_EXPERTS].astype(jnp.int32)
    before = jnp.concatenate([jnp.zeros((1, N_EXPERTS), jnp.int32), after[:-1]], axis=0)
    n_blocks = t // tb
    ck = MOE_CHUNK
    n_seg = after - before
    seg_len = (n_seg + 7) // 8 * 8
    seg_rel = jnp.cumsum(seg_len, axis=0) - seg_len
    counts = jnp.sum(seg_len, axis=0)
    n_sb = (counts + ck + sb - 1) // sb
    sb_end = jnp.cumsum(n_sb)
    sb_start = sb_end - n_sb
    row_start = sb_start * sb
    seg_start = row_start[None, :] + seg_rel
    n_chunks = (n_seg + ck - 1) // ck
    win_pos = (jnp.cumsum(n_chunks, axis=1) - n_chunks) * ck
    win = (2 * tb + N_EXPERTS * (ck - 1) + LANES - 1) // LANES * LANES
    eid = jnp.arange(N_EXPERTS, dtype=jnp.int32)
    tok_origin = jnp.repeat(win_pos - before, tb, axis=0)
    cols = rank + jnp.sum(jnp.where(e[:, :, None] == eid, tok_origin[:, None, :], 0), axis=-1)
    cols = cols.astype(jnp.int32)
    n_super = (2 * t + n_blocks * N_EXPERTS * 7) // sb + N_EXPERTS + 2
    blk = jnp.arange(n_super, dtype=jnp.int32)
    n_used = sb_end[-1]
    used = blk < n_used
    blk_c = jnp.minimum(blk, n_used - 1)
    be = jnp.minimum(jnp.sum((blk_c[:, None] >= sb_end[None, :]).astype(jnp.int32), axis=-1),
                     N_EXPERTS - 1)
    valid = jnp.clip(counts[be] - (blk_c - sb_start[be]) * sb, 0, sb)
    n_valid_sub = jnp.where(used, (valid + sub - 1) // sub, 0).astype(jnp.int32)
    zero_start = (row_start + counts).astype(jnp.int32)
    per = sb // sub
    sub_in_blk = jnp.arange(per, dtype=jnp.int32)
    zero_blk = (sub_in_blk[None, :] >= n_valid_sub[:, None]).astype(jnp.int32).reshape(-1)
    zero_blk = jnp.concatenate([zero_blk, jnp.ones((1,), jnp.int32)])
    tables = [a.reshape(-1).astype(jnp.int32) for a in (seg_start, n_chunks, win_pos)]
    xs = _dispatch(*tables, zero_start, zero_blk, cols.T, x, g, sub, tb, win)
    ys = _expert_ffn(be.astype(jnp.int32), n_valid_sub, blk_c.astype(jnp.int32), xs, wg, wu, wd, sb, sub)
    return [_combine(*tables, cols, x, info, ys, gf, final_norm, off, n, tb, win) for off, n in groups]


def _final_norm_kernel(x_ref, g_ref, o_ref):
    o_ref[...] = _rms_rows(x_ref[...], g_ref[...])


def _final_norm(x, g, row_off, n_rows, tm=512):
    d = x.shape[1]
    t = n_rows
    tm = _row_tile(t, tm)
    off = row_off // tm
    return pl.pallas_call(
        _final_norm_kernel,
        grid=(t // tm,),
        in_specs=[pl.BlockSpec((tm, d), lambda i: (i + off, 0)), pl.BlockSpec((1, d), lambda i: (0, 0))],
        out_specs=pl.BlockSpec((tm, d), lambda i: (i, 0)),
        out_shape=jax.ShapeDtypeStruct((t, d), F32),
        compiler_params=_cparams(1),
        name="final_norm",
    )(x, g.reshape(1, d))


def _block_diag(w):
    n, c, d = w.shape
    eye = jnp.eye(n, dtype=w.dtype)
    return jnp.einsum("ncd,nm->ncmd", w, eye).reshape(n * c, n * d)


def _pad_cols(w, n):
    return jnp.pad(w, ((0, 0), (0, n - w.shape[1])))


def _layer_params(l, p):
    (norm_mix, w_in, lru_conv_w, lru_conv_b, lru_a_w, lru_a_b, lru_x_w, lru_x_b, lru_lam, gla_gk_w2,
     gla_gk_b, gla_norm, gdn_conv_w, gdn_a_log, gdn_dt_bias, gdn_norm, w_out, norm_xq, norm_mem,
     w_mq, w_mk, w_mv, w_mo, norm_ffn) = [a[l] for a in p]
    lw = LRU_W
    gk, gv = GLA_HEADS * GLA_DK, GLA_HEADS * GLA_DV
    dh = GDN_HEADS * GDN_DK
    offs = [0]
    for s in (lw, lw, gk, gk, gv, GLA_RANK, gv, dh, dh, dh, GDN_HEADS, GDN_HEADS, dh):
        offs.append(offs[-1] + s)
    col = lambda i: w_in[:, offs[i]:offs[i + 1]]
    w_lru = jnp.concatenate([col(0), col(1)], axis=1)
    w_gla = jnp.concatenate([col(2), col(3), col(4), col(6), _pad_cols(col(5), LANES)], axis=1)
    w_gdn = jnp.concatenate(
        [col(7), col(8), col(9), col(12), _pad_cols(jnp.concatenate([col(10), col(11)], axis=1), LANES)],
        axis=1)
    w_cat = jnp.concatenate([w_lru, w_gla, w_gdn], axis=1).astype(BF16)
    widths = (w_lru.shape[1], w_gla.shape[1], w_gdn.shape[1])
    alog = jnp.zeros((1, LANES), F32).at[0, GDN_HEADS:2 * GDN_HEADS].set(gdn_a_log)
    dtb = jnp.zeros((1, LANES), F32).at[0, GDN_HEADS:2 * GDN_HEADS].set(gdn_dt_bias)
    return dict(
        norm_mix=norm_mix, w_cat=w_cat, widths=widths,
        lru_conv_w=lru_conv_w, lru_conv_b=lru_conv_b,
        lru_a=_block_diag(lru_a_w).astype(BF16), lru_a_b=lru_a_b,
        lru_x=_block_diag(lru_x_w).astype(BF16), lru_x_b=lru_x_b, lru_lam=lru_lam,
        gla_w2=jnp.pad(gla_gk_w2, ((0, LANES - GLA_RANK), (0, 0))).astype(BF16), gla_gk_b=gla_gk_b,
        gla_norm=jnp.tile(gla_norm, GLA_HEADS),
        gdn_conv_w=gdn_conv_w, gdn_alog=alog, gdn_dtb=dtb, gdn_norm=jnp.tile(gdn_norm, GDN_HEADS),
        w_out=w_out.astype(BF16), norm_xq=norm_xq, w_mq=w_mq.astype(BF16), w_mo=w_mo.astype(BF16),
        norm_ffn=norm_ffn)


def _mix_and_attend(x, x_off, grp, l, lp, joint, total_rows):
    batch, seq, n = grp["batch"], grp["seq"], grp["batch"] * grp["seq"]
    lru_h0, lru_buf0, gla_s0, gdn_s0, gdn_buf0 = grp["states"]
    sl = grp["state_layer"](l)
    z_lru, z_gla, z_gdn = _rms_matmul(x, lp["norm_mix"], lp["w_cat"], lp["widths"], "in_proj",
                                      row_off=x_off, n_rows=n)
    y_lru, lru_h, lru_buf = _lru_mixer(
        z_lru, lru_h0, lru_buf0, sl, lp["lru_conv_w"], lp["lru_conv_b"], lp["lru_a"], lp["lru_a_b"],
        lp["lru_x"], lp["lru_x_b"], lp["lru_lam"], batch, seq)
    y_gla, gla_s = _gla_mixer(z_gla.reshape(batch, seq, -1), gla_s0, sl, lp["gla_w2"],
                              lp["gla_gk_b"], lp["gla_norm"], batch, seq)
    y_gdn, gdn_s, gdn_buf = _gdn_mixer(z_gdn.reshape(batch, seq, -1), gdn_s0, gdn_buf0, sl,
                                       lp["gdn_conv_w"], lp["gdn_alog"], lp["gdn_dtb"],
                                       lp["gdn_norm"], batch, seq)
    xn, q = _outproj(x, y_lru, y_gla.reshape(n, -1), y_gdn.reshape(n, -1),
                     lp["w_out"], lp["norm_xq"], lp["w_mq"], row_off=x_off)
    mem_k, mem_v, mem_layer = grp["mem"][l]
    joint = _attention(xn, q, mem_k, mem_v, mem_layer, lp["w_mo"], batch, seq, joint, grp["row_off"],
                       total_rows)
    return joint, (lru_h.reshape(batch, LRU_W), lru_buf, gla_s, gdn_s, gdn_buf)


def _run_layers(groups, layers, ffn, norm_final):
    total_rows = sum(g["batch"] * g["seq"] for g in groups)
    spans = [(g["row_off"], g["batch"] * g["seq"]) for g in groups]
    xs = [(g["x"], 0) for g in groups]
    new_states = [[] for _ in groups]
    outs = None
    for l, lp in enumerate(layers):
        joint = None
        for gi, grp in enumerate(groups):
            joint, st = _mix_and_attend(xs[gi][0], xs[gi][1], grp, l, lp, joint, total_rows)
            new_states[gi].append(st)
        last = l == len(layers) - 1
        kind, fp = ffn[l]
        if kind == "dense":
            joint = _ffn_dense(joint, lp["norm_ffn"], *fp)
            if last:
                outs = [_final_norm(joint, norm_final, off, n) for off, n in spans]
        elif last:
            outs = _moe_ffn(joint, lp["norm_ffn"], *fp, norm_final, True, spans)
        else:
            joint = _moe_ffn(joint, lp["norm_ffn"], *fp, norm_final, False, [(0, total_rows)])[0]
        xs = [(joint, g["row_off"]) for g in groups]
    states = [[jnp.stack(s) for s in zip(*ns)] for ns in new_states]
    return outs, states


def kernel(x_prompt, x_sample, mem_prompt, state_lru_h, state_lru_conv, state_gla, state_gdn, state_gdn_conv, cache_mem_k, cache_mem_v, norm_mix, w_in, lru_conv_w, lru_conv_b, lru_a_w, lru_a_b, lru_x_w, lru_x_b, lru_lam, gla_gk_w2, gla_gk_b, gla_norm, gdn_conv_w, gdn_a_log, gdn_dt_bias, gdn_norm, w_out, norm_xq, norm_mem, w_mq, w_mk, w_mv, w_mo, norm_ffn, w_ff_gate, w_ff_up, w_ff_down, w_router, w_e_gate, w_e_up, w_e_down, norm_final):
    depth = norm_mix.shape[0]
    per_layer = (norm_mix, w_in, lru_conv_w, lru_conv_b, lru_a_w, lru_a_b, lru_x_w, lru_x_b, lru_lam,
                 gla_gk_w2, gla_gk_b, gla_norm, gdn_conv_w, gdn_a_log, gdn_dt_bias, gdn_norm, w_out,
                 norm_xq, norm_mem, w_mq, w_mk, w_mv, w_mo, norm_ffn)
    layers = [_layer_params(l, per_layer) for l in range(depth)]
    ffn = []
    for l in range(depth):
        j = l // 2
        if l % 2 == 0:
            ffn.append(("dense", (w_ff_gate[j], w_ff_up[j], w_ff_down[j])))
        else:
            wr = _pad_cols(w_router[j], LANES)
            whi = wr.astype(BF16)
            wlo = (wr - whi.astype(F32)).astype(BF16)
            ffn.append(("moe", (whi, wlo, w_e_gate[j], w_e_up[j], w_e_down[j])))

    bp, mlen, d = mem_prompt.shape
    w_mkv = jnp.concatenate([w_mk, w_mv], axis=2).astype(BF16)
    pk, pv, p_mem_k, p_mem_v = _mem_kv(mem_prompt, norm_mem, w_mkv)
    zero_state = (jnp.zeros((1, bp, LRU_W), F32), jnp.zeros((1, bp, CONV_K - 1, LRU_W), F32),
                  jnp.zeros((1, bp, GLA_HEADS, GLA_DK, GLA_DV), F32),
                  jnp.zeros((1, bp, GDN_HEADS, GDN_DK, GDN_DV), F32),
                  jnp.zeros((1, bp, CONV_K - 1, 3 * GDN_HEADS * GDN_DK), F32))
    sp = x_prompt.shape[1]
    bs, ss = x_sample.shape[0], x_sample.shape[1]
    groups = [
        dict(x=x_prompt.reshape(bp * sp, d), batch=bp, seq=sp, row_off=0, states=zero_state,
             state_layer=lambda l: 0, mem=[(pk, pv, l) for l in range(depth)]),
        dict(x=x_sample.reshape(bs * ss, d), batch=bs, seq=ss, row_off=bp * sp,
             states=(state_lru_h, state_lru_conv, state_gla, state_gdn, state_gdn_conv),
             state_layer=lambda l: l, mem=[(cache_mem_k, cache_mem_v, l) for l in range(depth)]),
    ]
    (y_p, y_s), (p_st, s_st) = _run_layers(groups, layers, ffn, norm_final)

    return (y_p.reshape(bp, sp, d), y_s.reshape(bs, ss, d), p_st[0], p_st[1], p_st[2], p_st[3], p_st[4],
            p_mem_k, p_mem_v, s_st[0], s_st[1], s_st[2], s_st[3], s_st[4])
```

```python
import functools
import math

import jax
import jax.numpy as jnp
from jax import lax
from jax.experimental import pallas as pl
from jax.experimental.pallas import tpu as pltpu

F32 = jnp.float32
BF16 = jnp.bfloat16
EPS = 1e-6

D_MODEL = 1024
LRU_W = 512
LRU_BLOCKS = 8
LRU_C = 8.0
CONV_K = 4
GLA_HEADS = 4
GLA_DK = 32
GLA_DV = 64
GLA_RANK = 16
GLA_TAU = 16.0
GLA_SUB = 16
GDN_HEADS = 4
GDN_DK = 64
GDN_DV = 64
MIX_CHUNK = 64
MEM_HEADS = 4
MEM_HD = 256
N_EXPERTS = 8
LANES = 128
VMEM_LIMIT_BYTES = 48 * 1024 * 1024
EXPERT_VMEM_LIMIT_BYTES = 56 * 1024 * 1024


def _cparams(n_axes):
    return pltpu.CompilerParams(dimension_semantics=("arbitrary",) * n_axes,
                                vmem_limit_bytes=VMEM_LIMIT_BYTES)


def _mm(a, b):
    return jnp.dot(a.astype(BF16), b.astype(BF16), preferred_element_type=F32)


def _mm_nt(a, b):
    return lax.dot_general(a.astype(BF16), b.astype(BF16), (((1,), (1,)), ((), ())),
                           preferred_element_type=F32)


def _mm_tn(a, b):
    return lax.dot_general(a.astype(BF16), b.astype(BF16), (((0,), (0,)), ((), ())),
                           preferred_element_type=F32)


def _rms_rows(x, g):
    ms = jnp.mean(x * x, axis=-1, keepdims=True)
    return (x * lax.rsqrt(ms + EPS)) * g


def _softplus(x):
    return jnp.maximum(x, 0.0) + jnp.log1p(jnp.exp(-jnp.abs(x)))


def _sigmoid(x):
    return 1.0 / (1.0 + jnp.exp(-x))


def _silu(x):
    return x * _sigmoid(x)


def _gelu_tanh(x):
    c = 0.7978845608028654
    return x * (0.5 * (1.0 + jnp.tanh(c * (x + 0.044715 * (x * x * x)))))


def _seg_cumsum_rows(x, seg):
    rows = x.shape[0]
    tpos = lax.broadcasted_iota(jnp.int32, (rows, 1), 0) & (seg - 1)
    d = 1
    while d < seg:
        x = x + jnp.where(tpos >= d, pltpu.roll(x, d, axis=0), 0.0)
        d *= 2
    return x


def _head_rms(o, gain, n_heads, width):
    lane_head = lax.broadcasted_iota(jnp.int32, (1, n_heads * width), 1) // width
    sq = o * o
    inv = jnp.zeros_like(o)
    for h in range(n_heads):
        m = lane_head == h
        ms = jnp.sum(jnp.where(m, sq, 0.0), axis=-1, keepdims=True) * (1.0 / width)
        inv = jnp.where(m, lax.rsqrt(ms + EPS), inv)
    return (o * inv) * gain


def _stack_heads(x, n_heads, width):
    c = x.shape[0]
    t = jnp.concatenate([x] * n_heads, axis=0)
    row_head = lax.broadcasted_iota(jnp.int32, (n_heads * c, 1), 0) // c
    lane_head = lax.broadcasted_iota(jnp.int32, (1, n_heads * width), 1) // width
    return jnp.where(row_head == lane_head, t, 0.0)


def _unstack_heads(x, n_heads):
    c = x.shape[0] // n_heads
    o = x[0:c]
    for h in range(1, n_heads):
        o = o + x[h * c:(h + 1) * c]
    return o


def _rms_matmul_kernel(x_ref, g_ref, w_ref, *o_refs):
    h = _rms_rows(x_ref[...], g_ref[...]).astype(BF16)
    start = 0
    for o_ref in o_refs:
        n = o_ref.shape[1]
        o_ref[...] = jnp.dot(h, w_ref[:, start:start + n], preferred_element_type=F32)
        start += n


def _row_tile(t, pref):
    tile = min(pref, t)
    while t % tile or tile % 8:
        tile -= 8
    return tile


def _rms_matmul(x, g, w, widths, name, tm=512, row_off=0, n_rows=None):
    d = x.shape[1]
    t = x.shape[0] if n_rows is None else n_rows
    n = w.shape[1]
    tm = _row_tile(t, tm)
    off = row_off // tm
    out_specs = [pl.BlockSpec((tm, wd), lambda i: (i, 0)) for wd in widths]
    out_shape = [jax.ShapeDtypeStruct((t, wd), F32) for wd in widths]
    return pl.pallas_call(
        _rms_matmul_kernel,
        grid=(t // tm,),
        in_specs=[pl.BlockSpec((tm, d), lambda i: (i + off, 0)),
                  pl.BlockSpec((1, d), lambda i: (0, 0)),
                  pl.BlockSpec((d, n), lambda i: (0, 0))],
        out_specs=out_specs,
        out_shape=out_shape,
        compiler_params=_cparams(1),
        name=name,
    )(x, g.reshape(1, d), w)


def _mem_kv_kernel(m_ref, g_ref, w_ref, k_ref, v_ref, k5_ref, v5_ref):
    d = m_ref.shape[1]
    h = _rms_rows(m_ref[...], g_ref[...])
    kv = _mm(h, w_ref[...])
    k_ref[...] = kv[:, :d].astype(k_ref.dtype)
    v_ref[...] = kv[:, d:].astype(v_ref.dtype)
    for hd in range(MEM_HEADS):
        sl = slice(hd * MEM_HD, (hd + 1) * MEM_HD)
        k5_ref[:, hd, :] = kv[:, sl]
        v5_ref[:, hd, :] = kv[:, d + hd * MEM_HD:d + (hd + 1) * MEM_HD]


def _mem_kv(mem, g, w):
    bp, mlen, d = mem.shape
    depth = w.shape[0]
    flat = jax.ShapeDtypeStruct((depth, bp, mlen, d), BF16)
    split = jax.ShapeDtypeStruct((depth, bp, mlen, MEM_HEADS, MEM_HD), F32)
    fidx = lambda l, b: (l, b, 0, 0)
    sidx = lambda l, b: (l, b, 0, 0, 0)
    return pl.pallas_call(
        _mem_kv_kernel,
        grid=(depth, bp),
        in_specs=[pl.BlockSpec((None, mlen, d), lambda l, b: (b, 0, 0)),
                  pl.BlockSpec((None, 1, d), lambda l, b: (l, 0, 0)),
                  pl.BlockSpec((None, d, 2 * d), lambda l, b: (l, 0, 0))],
        out_specs=[pl.BlockSpec((None, None, mlen, d), fidx),
                   pl.BlockSpec((None, None, mlen, d), fidx),
                   pl.BlockSpec((None, None, mlen, MEM_HEADS, MEM_HD), sidx),
                   pl.BlockSpec((None, None, mlen, MEM_HEADS, MEM_HD), sidx)],
        out_shape=[flat, flat, split, split],
        compiler_params=_cparams(2),
        name="mem_kv",
    )(mem, g.reshape(depth, 1, d), w)


def _lru_coeffs(xc, aw_ref, ab_ref, xw_ref, xb_ref, lam_ref):
    r = _sigmoid(_mm(xc, aw_ref[...]) + ab_ref[...])
    ig = _sigmoid(_mm(xc, xw_ref[...]) + xb_ref[...])
    log_a = (-LRU_C * r) * _softplus(-lam_ref[...])
    a = jnp.exp(log_a)
    th = jnp.tanh(log_a)
    return a, jnp.sqrt((-2.0 * th) / (1.0 - th)) * (ig * xc)


def _lru_kernel(z_ref, h0_ref, buf0_ref, cw_ref, cb_ref, aw_ref, ab_ref, xw_ref, xb_ref, lam_ref,
                y_ref, hout_ref, bufout_ref, xs_ref, hc_ref, *, bt, tl, nt):
    w = LRU_W
    rows = bt * tl
    j = pl.program_id(0) % nt

    @pl.when(j == 0)
    def _():
        xs_ref[:, 5:8, :] = buf0_ref[...]
        hc_ref[...] = h0_ref[...]

    xs_ref[:, 8:, :] = z_ref[:, :w].reshape(bt, tl, w)
    gate = z_ref[:, w:]
    cw = cw_ref[...]
    xc = cb_ref[...] + xs_ref[:, 5:5 + tl, :] * cw[0:1]
    for k in range(1, CONV_K):
        xc = xc + xs_ref[:, 5 + k:5 + k + tl, :] * cw[k:k + 1]
    tail = xs_ref[:, 5 + tl:8 + tl, :]
    xs_ref[:, 5:8, :] = tail
    bufout_ref[...] = tail

    a, b = _lru_coeffs(xc.reshape(rows, w), aw_ref, ab_ref, xw_ref, xb_ref, lam_ref)

    sub = 8
    gps = tl // sub
    a3 = a.reshape(rows // sub, sub, w)
    b3 = b.reshape(rows // sub, sub, w)
    spos = lax.broadcasted_iota(jnp.int32, (1, sub, 1), 1)
    d = 1
    while d < sub:
        m = spos >= d
        b3 = jnp.where(m, a3 * pltpu.roll(b3, d, axis=1) + b3, b3)
        a3 = jnp.where(m, a3 * pltpu.roll(a3, d, axis=1), a3)
        d *= 2
    a4 = a3.reshape(bt, gps, sub, w)
    b4 = b3.reshape(bt, gps, sub, w)
    carry = hc_ref[...]
    hs = []
    for r in range(gps):
        hr = b4[:, r] + a4[:, r] * carry
        hs.append(hr)
        carry = hr[:, sub - 1:sub, :]
    h = jnp.stack(hs, axis=1).reshape(rows, w)
    hlast = carry
    hc_ref[...] = hlast
    hout_ref[...] = hlast
    y_ref[...] = h * _gelu_tanh(gate)


def _lru_mixer(z, h0, buf0, layer, cw, cb, aw, ab, xw, xb, lam, batch, seq):
    w = LRU_W
    tl = min(seq, 512)
    bt = min(batch, max(1, 512 // seq))
    nt = seq // tl
    rows = bt * tl
    grid = (batch * seq // rows,)
    if nt > 1:
        sidx = lambda i: (i // nt, 0, 0)
        lidx = lambda i: (layer, i // nt, 0, 0)
    else:
        sidx = lambda i: (i, 0, 0)
        lidx = lambda i: (layer, i, 0, 0)
    full2 = lambda i: (0, 0)
    return pl.pallas_call(
        functools.partial(_lru_kernel, bt=bt, tl=tl, nt=nt),
        grid=grid,
        in_specs=[pl.BlockSpec((rows, 2 * w), lambda i: (i, 0)),
                  pl.BlockSpec((None, bt, 1, w), lidx),
                  pl.BlockSpec((None, bt, CONV_K - 1, w), lidx),
                  pl.BlockSpec((CONV_K, w), full2),
                  pl.BlockSpec((1, w), full2),
                  pl.BlockSpec((w, w), full2),
                  pl.BlockSpec((1, w), full2),
                  pl.BlockSpec((w, w), full2),
                  pl.BlockSpec((1, w), full2),
                  pl.BlockSpec((1, w), full2)],
        out_specs=[pl.BlockSpec((rows, w), lambda i: (i, 0)),
                   pl.BlockSpec((bt, 1, w), sidx),
                   pl.BlockSpec((bt, CONV_K - 1, w), sidx)],
        out_shape=[jax.ShapeDtypeStruct((batch * seq, w), F32),
                   jax.ShapeDtypeStruct((batch, 1, w), F32),
                   jax.ShapeDtypeStruct((batch, CONV_K - 1, w), F32)],
        scratch_shapes=[pltpu.VMEM((bt, 8 + tl, w), F32), pltpu.VMEM((bt, 1, w), F32)],
        compiler_params=_cparams(1),
        name="lru_mixer",
    )(z, h0.reshape(h0.shape[0], batch, 1, w), buf0, cw, cb.reshape(1, w), aw, ab.reshape(1, w), xw,
      xb.reshape(1, w), lam.reshape(1, w))


def _gla_kernel(z_ref, s0_ref, w2_ref, gb_ref, gn_ref, y_ref, sout_ref, s_ref, *, c, sc, g):
    nh = GLA_HEADS
    kw = nh * GLA_DK
    vw = nh * GLA_DV

    @pl.when(pl.program_id(1) == 0)
    def _():
        s_ref[...] = jnp.zeros_like(s_ref)
        for b in range(g):
            for h in range(nh):
                s_ref[b, h * GLA_DK:(h + 1) * GLA_DK, h * GLA_DV:(h + 1) * GLA_DV] = s0_ref[b, h]

    bs = range(g)
    q = [z_ref[b, :, 0:kw] * (GLA_DK ** -0.5) for b in bs]
    k = [z_ref[b, :, kw:2 * kw] for b in bs]
    v = [z_ref[b, :, 2 * kw:2 * kw + vw] for b in bs]
    gk = [-_softplus(-(_mm(z_ref[b, :, 2 * kw + 2 * vw:], w2_ref[...]) + gb_ref[...])) / GLA_TAU
          for b in bs]
    gcum = [_seg_cumsum_rows(gk[b], sc) for b in bs]
    qp = [q[b] * jnp.exp(gcum[b]) for b in bs]

    ri = lax.broadcasted_iota(jnp.int32, (nh * c, nh * c), 0)
    ci = lax.broadcasted_iota(jnp.int32, (nh * c, nh * c), 1)
    keep = (ri // sc == ci // sc) & (ci <= ri)
    a = [_mm_nt(_stack_heads(qp[b], nh, GLA_DK), _stack_heads(k[b] * jnp.exp(-gcum[b]), nh, GLA_DK))
         for b in bs]
    o = [_unstack_heads(_mm(jnp.where(keep, a[b], 0.0), _stack_heads(v[b], nh, GLA_DV)), nh)
         for b in bs]

    s = [s_ref[b] for b in bs]
    eye = (lax.broadcasted_iota(jnp.int32, (kw, kw), 0) ==
           lax.broadcasted_iota(jnp.int32, (kw, kw), 1))
    bd = (lax.broadcasted_iota(jnp.int32, (kw, vw), 0) // GLA_DK ==
          lax.broadcasted_iota(jnp.int32, (kw, vw), 1) // GLA_DV)
    o_inter = [[] for _ in bs]
    for i in range(c // sc):
        lo, hi = i * sc, (i + 1) * sc
        for b in bs:
            o_inter[b].append(_mm(qp[b][lo:hi], s[b]))
            glast = gcum[b][hi - 1:hi]
            kpp = k[b][lo:hi] * jnp.exp(glast - gcum[b][lo:hi])
            u = _mm_tn(kpp, v[b][lo:hi])
            dcol = jnp.sum(jnp.where(eye, jnp.exp(glast), 0.0), axis=1, keepdims=True)
            s[b] = dcol * s[b] + jnp.where(bd, u, 0.0)
    for b in bs:
        s_ref[b] = s[b]
        for h in range(nh):
            sout_ref[b, h] = s[b][h * GLA_DK:(h + 1) * GLA_DK, h * GLA_DV:(h + 1) * GLA_DV]
        ob = o[b] + jnp.concatenate(o_inter[b], axis=0)
        gate = z_ref[b, :, 2 * kw + vw:2 * kw + 2 * vw]
        y_ref[b] = _head_rms(ob, gn_ref[...], nh, GLA_DV) * _silu(gate)


def _gla_mixer(z, s0, layer, w2, gb, gn, batch, seq):
    c = min(seq, MIX_CHUNK)
    sc = min(c, GLA_SUB)
    nt = seq // c
    g = _mixer_group(batch, c)
    kw, vw = GLA_HEADS * GLA_DK, GLA_HEADS * GLA_DV
    zw = z.shape[2]
    full2 = lambda i, j: (0, 0)
    sblk = (g, GLA_HEADS, GLA_DK, GLA_DV)
    return pl.pallas_call(
        functools.partial(_gla_kernel, c=c, sc=sc, g=g),
        grid=(batch // g, nt),
        in_specs=[pl.BlockSpec((g, c, zw), lambda i, j: (i, j, 0)),
                  pl.BlockSpec((None,) + sblk, lambda i, j: (layer, i, 0, 0, 0)),
                  pl.BlockSpec((LANES, kw), full2),
                  pl.BlockSpec((1, kw), full2),
                  pl.BlockSpec((1, vw), full2)],
        out_specs=[pl.BlockSpec((g, c, vw), lambda i, j: (i, j, 0)),
                   pl.BlockSpec(sblk, lambda i, j: (i, 0, 0, 0))],
        out_shape=[jax.ShapeDtypeStruct((batch, seq, vw), F32),
                   jax.ShapeDtypeStruct((batch, GLA_HEADS, GLA_DK, GLA_DV), F32)],
        scratch_shapes=[pltpu.VMEM((g, kw, vw), F32)],
        compiler_params=_cparams(2),
        name="gla_mixer",
    )(z, s0, w2, gb.reshape(1, kw), gn.reshape(1, vw))


def _gdn_prep(z_ref, cw_ref, alog_ref, dtb_ref, bufout_ref, xs_ref, *, c):
    nh = GDN_HEADS
    hw = nh * GDN_DK
    cw3 = 3 * hw

    xs_ref[8:, :] = z_ref[:, 0:cw3]
    cw = cw_ref[...]
    qkv = xs_ref[5:5 + c, :] * cw[0:1]
    for kk in range(1, CONV_K):
        qkv = qkv + xs_ref[5 + kk:5 + kk + c, :] * cw[kk:kk + 1]
    tail = xs_ref[5 + c:8 + c, :]
    xs_ref[5:8, :] = tail
    bufout_ref[...] = tail
    qkv = _silu(qkv)
    zg = z_ref[:, cw3:cw3 + hw]
    sm = z_ref[:, cw3 + hw:]

    lane_head = lax.broadcasted_iota(jnp.int32, (1, hw), 1) // GDN_DK

    def l2n(x):
        sq = x * x
        inv = jnp.zeros_like(x)
        for h in range(nh):
            m = lane_head == h
            ss = jnp.sum(jnp.where(m, sq, 0.0), axis=-1, keepdims=True)
            inv = jnp.where(m, lax.rsqrt(ss + EPS), inv)
        return x * inv

    q = l2n(qkv[:, 0:hw]) * (GDN_DK ** -0.5)
    k = l2n(qkv[:, hw:2 * hw])
    v = qkv[:, 2 * hw:3 * hw]
    beta = _sigmoid(sm)
    glog = -jnp.exp(alog_ref[...]) * _softplus(sm + dtb_ref[...])
    gcum = _seg_cumsum_rows(glog, c)

    n = nh * c
    bcol = jnp.concatenate([beta[:, h:h + 1] for h in range(nh)], axis=0)
    gcol = jnp.concatenate([gcum[:, nh + h:nh + h + 1] for h in range(nh)], axis=0)
    glast = jnp.concatenate(
        [jnp.broadcast_to(gcum[c - 1:c, nh + h:nh + h + 1], (c, 1)) for h in range(nh)], axis=0)
    ri = lax.broadcasted_iota(jnp.int32, (n, n), 0)
    ci = lax.broadcasted_iota(jnp.int32, (n, n), 1)
    grow = jnp.sum(jnp.where(ri == ci, gcol, 0.0), axis=0, keepdims=True)
    incl = (ri // c == ci // c) & (ci <= ri)
    dec = jnp.where(incl, jnp.exp(jnp.where(incl, gcol - grow, 0.0)), 0.0)

    sdec = jnp.concatenate(
        [jnp.broadcast_to(jnp.exp(gcum[c - 1:c, nh + h:nh + h + 1]), (GDN_DK, 1)) for h in range(nh)],
        axis=0)
    return dict(ks=_stack_heads(k, nh, GDN_DK), qs=_stack_heads(q, nh, GDN_DK),
                vs=_stack_heads(v, nh, GDN_DV), bcol=bcol, egc=jnp.exp(gcol), dec=dec,
                kdec=jnp.exp(glast - gcol), sdec=sdec, zg=zg)


def _gdn_kernel(z_ref, s0_ref, buf0_ref, cw_ref, alog_ref, dtb_ref, gn_ref,
                y_ref, sout_ref, bufout_ref, xs_ref, s_ref, *, c, g):
    nh = GDN_HEADS
    hw = nh * GDN_DK
    n = nh * c

    @pl.when(pl.program_id(1) == 0)
    def _():
        xs_ref[:, 5:8, :] = buf0_ref[...]
        s_ref[...] = jnp.zeros_like(s_ref)
        for b in range(g):
            for h in range(nh):
                s_ref[b, h * GDN_DK:(h + 1) * GDN_DK, h * GDN_DV:(h + 1) * GDN_DV] = s0_ref[b, h]

    bs = range(g)
    pr = [_gdn_prep(z_ref.at[b], cw_ref, alog_ref, dtb_ref, bufout_ref.at[b], xs_ref.at[b], c=c)
          for b in bs]
    ri = lax.broadcasted_iota(jnp.int32, (n, n), 0)
    ci = lax.broadcasted_iota(jnp.int32, (n, n), 1)
    same = ri // c == ci // c
    incl = same & (ci <= ri)
    strict = same & (ci < ri)
    kq = [_mm_nt(jnp.concatenate([pr[b]["ks"], pr[b]["qs"]], axis=0), pr[b]["ks"]) for b in bs]

    p = [jnp.where(strict, -(pr[b]["bcol"] * kq[b][0:n]) * pr[b]["dec"], 0.0) for b in bs]
    tinv = [jnp.where(ri == ci, 1.0, 0.0) + p[b] for b in bs]
    span = 2
    while span < c:
        p = [_mm(p[b], p[b]) for b in bs]
        tinv = [tinv[b] + _mm(tinv[b], p[b]) for b in bs]
        span *= 2

    s = [s_ref[b] for b in bs]
    uw = [_mm(tinv[b], jnp.concatenate([pr[b]["vs"] * pr[b]["bcol"],
                                        pr[b]["ks"] * (pr[b]["bcol"] * pr[b]["egc"])], axis=1))
          for b in bs]
    qw = [_mm(jnp.concatenate([pr[b]["qs"] * pr[b]["egc"], uw[b][:, hw:]], axis=0), s[b])
          for b in bs]
    vnew = [uw[b][:, 0:hw] - qw[b][n:] for b in bs]
    av = [_mm(jnp.where(incl, kq[b][n:] * pr[b]["dec"], 0.0), vnew[b]) for b in bs]
    kv = [_mm_tn(pr[b]["ks"] * pr[b]["kdec"], vnew[b]) for b in bs]
    for b in bs:
        o = _unstack_heads(qw[b][0:n] + av[b], nh)
        sn = pr[b]["sdec"] * s[b] + kv[b]
        s_ref[b] = sn
        for h in range(nh):
            sout_ref[b, h] = sn[h * GDN_DK:(h + 1) * GDN_DK, h * GDN_DV:(h + 1) * GDN_DV]
        y_ref[b] = _head_rms(o, gn_ref[...], nh, GDN_DV) * _silu(pr[b]["zg"])


def _mixer_group(batch, c):
    return min(batch, max(8, 128 // c))


def _gdn_mixer(z, s0, buf0, layer, cw, alog, dtb, gn, batch, seq):
    c = min(seq, MIX_CHUNK)
    nt = seq // c
    g = _mixer_group(batch, c)
    hw = GDN_HEADS * GDN_DK
    zw = z.shape[2]
    full2 = lambda i, j: (0, 0)
    sblk = (g, GDN_HEADS, GDN_DK, GDN_DV)
    return pl.pallas_call(
        functools.partial(_gdn_kernel, c=c, g=g),
        grid=(batch // g, nt),
        in_specs=[pl.BlockSpec((g, c, zw), lambda i, j: (i, j, 0)),
                  pl.BlockSpec((None,) + sblk, lambda i, j: (layer, i, 0, 0, 0)),
                  pl.BlockSpec((None, g, CONV_K - 1, 3 * hw), lambda i, j: (layer, i, 0, 0)),
                  pl.BlockSpec((CONV_K, 3 * hw), full2),
                  pl.BlockSpec((1, LANES), full2),
                  pl.BlockSpec((1, LANES), full2),
                  pl.BlockSpec((1, hw), full2)],
        out_specs=[pl.BlockSpec((g, c, hw), lambda i, j: (i, j, 0)),
                   pl.BlockSpec(sblk, lambda i, j: (i, 0, 0, 0)),
                   pl.BlockSpec((g, CONV_K - 1, 3 * hw), lambda i, j: (i, 0, 0))],
        out_shape=[jax.ShapeDtypeStruct((batch, seq, hw), F32),
                   jax.ShapeDtypeStruct((batch, GDN_HEADS, GDN_DK, GDN_DV), F32),
                   jax.ShapeDtypeStruct((batch, CONV_K - 1, 3 * hw), F32)],
        scratch_shapes=[pltpu.VMEM((g, 8 + c, 3 * hw), F32), pltpu.VMEM((g, hw, hw), F32)],
        compiler_params=_cparams(2),
        name="gdn_mixer",
    )(z, s0, buf0, cw, alog, dtb, gn.reshape(1, hw))


def _outproj_kernel(x_ref, yl_ref, yg_ref, yd_ref, wo_ref, gq_ref, wq_ref, xn_ref, q_ref):
    lw = LRU_W
    gw = GLA_HEADS * GLA_DV
    y = _mm(yl_ref[...], wo_ref[0:lw, :])
    y = y + _mm(yg_ref[...], wo_ref[lw:lw + gw, :])
    y = y + _mm(yd_ref[...], wo_ref[lw + gw:, :])
    xn = x_ref[...] + y
    xn_ref[...] = xn
    q_ref[...] = _mm(_rms_rows(xn, gq_ref[...]), wq_ref[...])


def _outproj(x, yl, yg, yd, wo, gq, wq, tm=512, row_off=0):
    d = x.shape[1]
    t = yl.shape[0]
    tm = _row_tile(t, tm)
    off = row_off // tm
    row = lambda i: (i, 0)
    full2 = lambda i: (0, 0)
    return pl.pallas_call(
        _outproj_kernel,
        grid=(t // tm,),
        in_specs=[pl.BlockSpec((tm, d), lambda i: (i + off, 0)),
                  pl.BlockSpec((tm, yl.shape[1]), row),
                  pl.BlockSpec((tm, yg.shape[1]), row),
                  pl.BlockSpec((tm, yd.shape[1]), row),
                  pl.BlockSpec((d, d), full2),
                  pl.BlockSpec((1, d), full2),
                  pl.BlockSpec((d, d), full2)],
        out_specs=[pl.BlockSpec((tm, d), row), pl.BlockSpec((tm, d), row)],
        out_shape=[jax.ShapeDtypeStruct((t, d), F32), jax.ShapeDtypeStruct((t, d), F32)],
        compiler_params=_cparams(1),
        name="outproj_qproj",
    )(x, yl, yg, yd, wo, gq.reshape(1, d), wq)


def _attn_kernel(x_ref, q_ref, k_ref, v_ref, wo_ref, o_ref):
    hd = MEM_HD
    acc = x_ref[...]
    for h in range(MEM_HEADS):
        sl = slice(h * hd, (h + 1) * hd)
        s = _mm_nt(q_ref[:, sl], k_ref[:, sl]) * (hd ** -0.5)
        m = jnp.max(s, axis=-1, keepdims=True)
        p = jnp.exp(s - m)
        l = jnp.sum(p, axis=-1, keepdims=True)
        oh = _mm(p, v_ref[:, sl]) / l
        acc = acc + _mm(oh, wo_ref[sl, :])
    o_ref[...] = acc


def _attn_heads_kernel(q_ref, k_ref, v_ref, o_ref, *, gb, tl):
    nh, hd = MEM_HEADS, MEM_HD
    m = k_ref.shape[1]
    row_head = lax.broadcasted_iota(jnp.int32, (nh * tl, 1), 0) // tl
    col_head = lax.broadcasted_iota(jnp.int32, (1, m * nh), 1) % nh
    for b in range(gb):
        kf = k_ref[b].reshape(m * nh, hd)
        vf = v_ref[b].reshape(m * nh, hd)
        rows = slice(b * tl, (b + 1) * tl)
        qs = jnp.concatenate([q_ref[rows, h * hd:(h + 1) * hd] for h in range(nh)], axis=0)
        s = _mm_nt(qs, kf) * (hd ** -0.5)
        s = jnp.where(row_head == col_head, s, -jnp.inf)
        mx = jnp.max(s, axis=-1, keepdims=True)
        p = jnp.exp(s - mx)
        l = jnp.sum(p, axis=-1, keepdims=True)
        o = _mm(p, vf) / l
        for h in range(nh):
            o_ref[rows, h * hd:(h + 1) * hd] = o[h * tl:(h + 1) * tl]


def _oproj_kernel(x_ref, a_ref, wo_ref, *rest):
    o_ref = rest[-1]
    o_ref[...] = x_ref[...] + _mm(a_ref[...], wo_ref[...])


def _attn_first_kernel(x_ref, q_ref, k_ref, v_ref, wo_ref, o_ref, *, n_real):
    i = pl.program_id(0)

    @pl.when(i < n_real)
    def _():
        _attn_kernel(x_ref, q_ref, k_ref, v_ref, wo_ref, o_ref)

    @pl.when(i >= n_real)
    def _():
        o_ref[...] = jnp.zeros_like(o_ref)


def _attn_into_kernel(x_ref, q_ref, k_ref, v_ref, wo_ref, joint_ref, o_ref):
    _attn_kernel(x_ref, q_ref, k_ref, v_ref, wo_ref, o_ref)


def _attention(x, q, k, v, layer, wo, batch, seq, joint, row_off, total_rows):
    t, d = x.shape
    out_shape = jax.ShapeDtypeStruct((total_rows, d), F32)
    extra_specs = [] if joint is None else [pl.BlockSpec(memory_space=pl.ANY)]
    extra_args = () if joint is None else (joint,)
    if k.ndim == 4:
        tl = min(seq, 512)
        nl = seq // tl
        off = row_off // tl
        kblk = (None, None) + k.shape[2:]
        n_real = batch * nl
        if joint is None:
            assert row_off == 0
            n_steps = pl.cdiv(total_rows, tl)
            body = functools.partial(_attn_first_kernel, n_real=n_real)
        else:
            n_steps = n_real
            body = _attn_into_kernel
        kidx = lambda i: (layer, jnp.minimum(i // nl, batch - 1), 0, 0)
        row = lambda i: (jnp.minimum(i, n_real - 1), 0)
        return pl.pallas_call(
            body,
            grid=(n_steps,),
            in_specs=[pl.BlockSpec((tl, d), row),
                      pl.BlockSpec((tl, d), row),
                      pl.BlockSpec(kblk, kidx),
                      pl.BlockSpec(kblk, kidx),
                      pl.BlockSpec((d, d), lambda i: (0, 0))] + extra_specs,
            out_specs=pl.BlockSpec((tl, d), lambda i: (i + off, 0)),
            out_shape=out_shape,
            input_output_aliases={} if joint is None else {5: 0},
            compiler_params=_cparams(1),
            name="mem_attention",
        )(x, q, k, v, wo, *extra_args)
    gb = 4 if batch % 4 == 0 else 2
    kblk = (None, gb) + k.shape[2:]
    kidx = lambda i: (layer, i, 0, 0, 0)
    att = pl.pallas_call(
        functools.partial(_attn_heads_kernel, gb=gb, tl=seq),
        grid=(batch // gb,),
        in_specs=[pl.BlockSpec((gb * seq, d), lambda i: (i, 0)),
                  pl.BlockSpec(kblk, kidx),
                  pl.BlockSpec(kblk, kidx)],
        out_specs=pl.BlockSpec((gb * seq, d), lambda i: (i, 0)),
        out_shape=jax.ShapeDtypeStruct((t, d), F32),
        compiler_params=_cparams(1),
        name="mem_attention_heads",
    )(q, k, v)
    tm = _row_tile(t, 512)
    off = row_off // tm
    return pl.pallas_call(
        _oproj_kernel,
        grid=(t // tm,),
        in_specs=[pl.BlockSpec((tm, d), lambda i: (i, 0)),
                  pl.BlockSpec((tm, d), lambda i: (i, 0)),
                  pl.BlockSpec((d, d), lambda i: (0, 0))] + extra_specs,
        out_specs=pl.BlockSpec((tm, d), lambda i: (i + off, 0)),
        out_shape=out_shape,
        input_output_aliases={} if joint is None else {3: 0},
        compiler_params=_cparams(1),
        name="mem_oproj",
    )(x, att, wo, *extra_args)


def _ffn_kernel(x_ref, g_ref, wg_ref, wu_ref, wd_ref, o_ref, h_ref):
    f = pl.program_id(1)

    @pl.when(f == 0)
    def _():
        h_ref[...] = _rms_rows(x_ref[...], g_ref[...]).astype(BF16)

    def chunk():
        h = h_ref[...]
        a = _mm(h, wg_ref[...])
        u = _mm(h, wu_ref[...])
        return _mm(_silu(a) * u, wd_ref[...])

    @pl.when(f == 0)
    def _():
        o_ref[...] = x_ref[...] + chunk()

    @pl.when(f > 0)
    def _():
        o_ref[...] += chunk()


def _ffn_dense(x, g, wg, wu, wd, tm=1024, tf=512):
    t, d = x.shape
    dff = wg.shape[1]
    tm = _row_tile(t, tm)
    return pl.pallas_call(
        _ffn_kernel,
        grid=(t // tm, dff // tf),
        in_specs=[pl.BlockSpec((tm, d), lambda i, f: (i, 0)),
                  pl.BlockSpec((1, d), lambda i, f: (0, 0)),
                  pl.BlockSpec((d, tf), lambda i, f: (0, f)),
                  pl.BlockSpec((d, tf), lambda i, f: (0, f)),
                  pl.BlockSpec((tf, d), lambda i, f: (f, 0))],
        out_specs=pl.BlockSpec((tm, d), lambda i, f: (i, 0)),
        out_shape=jax.ShapeDtypeStruct((t, d), F32),
        scratch_shapes=[pltpu.VMEM((tm, d), BF16)],
        compiler_params=_cparams(2),
        name="ffn_dense",
    )(x, g.reshape(1, d), wg, wu, wd)


def _router_kernel(x_ref, g_ref, whi_ref, wlo_ref, tri_ref, info_ref, cnt_ref, carry_ref, *, tm):
    i = pl.program_id(0)

    @pl.when(i == 0)
    def _():
        carry_ref[...] = jnp.zeros_like(carry_ref)

    h = _rms_rows(x_ref[...], g_ref[...])
    hhi = h.astype(BF16)
    hlo = (h - hhi.astype(F32)).astype(BF16)
    whi = whi_ref[...]
    logits = (jnp.dot(hhi, whi, preferred_element_type=F32)
              + jnp.dot(hlo, whi, preferred_element_type=F32)
              + jnp.dot(hhi, wlo_ref[...], preferred_element_type=F32))
    lane = lax.broadcasted_iota(jnp.int32, (tm, LANES), 1)
    neg = jnp.float32(-jnp.inf)
    logits = jnp.where(lane < N_EXPERTS, logits, neg)
    m1 = jnp.max(logits, axis=-1, keepdims=True)
    i1 = jnp.min(jnp.where(logits == m1, lane, LANES), axis=-1, keepdims=True)
    rest = jnp.where(lane == i1, neg, logits)
    m2 = jnp.max(rest, axis=-1, keepdims=True)
    i2 = jnp.min(jnp.where(rest == m2, lane, LANES), axis=-1, keepdims=True)
    e = jnp.exp(m2 - m1)
    g1 = 1.0 / (1.0 + e)
    g2 = e / (1.0 + e)
    oh1 = jnp.where(lane == i1, 1.0, 0.0)
    oh2 = jnp.where(lane == i2, 1.0, 0.0)
    oh = oh1 + oh2
    before = _mm(tri_ref[...], oh) + carry_ref[0:1, :]
    r1 = jnp.sum(oh1 * before, axis=-1, keepdims=True)
    r2 = jnp.sum(oh2 * before, axis=-1, keepdims=True)
    carry = carry_ref[0:1, :] + jnp.sum(oh, axis=0, keepdims=True)
    carry_ref[...] = jnp.broadcast_to(carry, carry_ref.shape)
    cnt_ref[...] = jnp.broadcast_to(carry, cnt_ref.shape)
    info = jnp.where(lane == 0, i1.astype(F32), 0.0)
    info = jnp.where(lane == 1, i2.astype(F32), info)
    info = jnp.where(lane == 2, r1, info)
    info = jnp.where(lane == 3, r2, info)
    info = jnp.where(lane == 4, g1, info)
    info = jnp.where(lane == 5, g2, info)
    info_ref[...] = info


def _router(x, g, whi, wlo, tm):
    t, d = x.shape
    tri = jnp.tril(jnp.ones((tm, tm), BF16), -1)
    return pl.pallas_call(
        functools.partial(_router_kernel, tm=tm),
        grid=(t // tm,),
        in_specs=[pl.BlockSpec((tm, d), lambda i: (i, 0)),
                  pl.BlockSpec((1, d), lambda i: (0, 0)),
                  pl.BlockSpec((d, LANES), lambda i: (0, 0)),
                  pl.BlockSpec((d, LANES), lambda i: (0, 0)),
                  pl.BlockSpec((tm, tm), lambda i: (0, 0))],
        out_specs=[pl.BlockSpec((tm, LANES), lambda i: (i, 0)),
                   pl.BlockSpec((8, LANES), lambda i: (i, 0))],
        out_shape=[jax.ShapeDtypeStruct((t, LANES), F32),
                   jax.ShapeDtypeStruct((t // tm * 8, LANES), F32)],
        scratch_shapes=[pltpu.VMEM((8, LANES), F32)],
        compiler_params=_cparams(1),
        name="moe_router",
    )(x, g.reshape(1, d), whi, wlo, tri)


MOE_CHUNK = 32


def _dispatch_kernel(st_ref, nc_ref, pos_ref, zs_ref, zf_ref, x_ref, g_ref, ct_ref, xs_hbm, w_ref, zbuf_ref,
                     zsem, sem, *, rows, sub, n_blk, win):
    i = pl.program_id(0)

    @pl.when(i == 0)
    def _():
        zbuf_ref[...] = jnp.zeros_like(zbuf_ref)
        for e in range(N_EXPERTS):
            dst = xs_hbm.at[pl.ds(pl.multiple_of(zs_ref[e], 8), sub)]
            pltpu.make_async_copy(zbuf_ref, dst, zsem).start()
            pltpu.make_async_copy(zbuf_ref, dst, zsem).wait()

        def fill(j, carry):
            @pl.when(zf_ref[j] == 1)
            def _():
                dst = xs_hbm.at[pl.ds(pl.multiple_of(j * sub, sub), sub)]
                pltpu.make_async_copy(zbuf_ref, dst, zsem).start()
            return carry

        def drain(j, carry):
            @pl.when(zf_ref[j] == 1)
            def _():
                dst = xs_hbm.at[pl.ds(pl.multiple_of(j * sub, sub), sub)]
                pltpu.make_async_copy(zbuf_ref, dst, zsem).wait()
            return carry

        lax.fori_loop(0, n_blk, fill, 0)
        lax.fori_loop(0, n_blk, drain, 0)

    ck = MOE_CHUNK
    n_steps = pl.num_programs(0)

    def chunk_copy(step, e, c):
        t = step * N_EXPERTS + e
        src = w_ref.at[step % 2, pl.ds(pl.multiple_of(pos_ref[t] + c * ck, ck), ck)]
        dst = xs_hbm.at[pl.ds(pl.multiple_of(st_ref[t] + c * ck, 8), ck)]
        return pltpu.make_async_copy(src, dst, sem)

    def all_chunks(step, fn):
        for e in range(N_EXPERTS):
            def body(c, carry, e=e):
                fn(chunk_copy(step, e, c))
                return carry
            lax.fori_loop(0, nc_ref[step * N_EXPERTS + e], body, 0)

    h = _rms_rows(x_ref[...], g_ref[...])
    wrow = lax.broadcasted_iota(jnp.int32, (win, rows), 0)
    cols = ct_ref[...]
    onehot = jnp.where(wrow == cols[0:1, :], 1.0, 0.0) + jnp.where(wrow == cols[1:2, :], 1.0, 0.0)
    w_ref[i % 2] = _mm(onehot, h)

    @pl.when(i >= 1)
    def _():
        all_chunks(i - 1, lambda cp: cp.wait())

    all_chunks(i, lambda cp: cp.start())

    @pl.when(i == n_steps - 1)
    def _():
        all_chunks(i, lambda cp: cp.wait())


def _dispatch(seg_start, n_chunks, win_pos, zero_start, zero_blk, cols_t, x, g, sub, rows, win):
    t, d = x.shape
    n_blk = zero_blk.shape[0]
    n_rows = n_blk * sub
    return pl.pallas_call(
        functools.partial(_dispatch_kernel, rows=rows, sub=sub, n_blk=n_blk, win=win),
        grid_spec=pltpu.PrefetchScalarGridSpec(
            num_scalar_prefetch=5,
            grid=(t // rows,),
            in_specs=[pl.BlockSpec((rows, d), lambda i, *_: (i, 0)),
                      pl.BlockSpec((1, d), lambda i, *_: (0, 0)),
                      pl.BlockSpec((2, rows), lambda i, *_: (0, i))],
            out_specs=pl.BlockSpec(memory_space=pl.ANY),
            scratch_shapes=[pltpu.VMEM((2, win, d), F32), pltpu.VMEM((sub, d), F32),
                            pltpu.SemaphoreType.DMA(()), pltpu.SemaphoreType.DMA(())]),
        out_shape=jax.ShapeDtypeStruct((n_rows, d), F32),
        compiler_params=_cparams(1),
        name="moe_dispatch",
    )(seg_start, n_chunks, win_pos, zero_start, zero_blk, x, g.reshape(1, d), cols_t)


def _expert_kernel(be_ref, ns_ref, xi_ref, x_ref, wg_ref, wu_ref, wd_ref, o_ref, *, sub, n_sub):
    i = pl.program_id(0)
    f = pl.program_id(1)
    ns = ns_ref[i]

    for s in range(n_sub):
        @pl.when((f == 0) & (s >= ns))
        def _():
            o_ref[s * sub:(s + 1) * sub, :] = jnp.zeros((sub, o_ref.shape[1]), o_ref.dtype)

    def swiglu(rows, first):
        h = x_ref[rows, :]
        a = _mm(h, wg_ref[...])
        u = _mm(h, wu_ref[...])
        y = _mm(_silu(a) * u, wd_ref[...])
        if first:
            o_ref[rows, :] = y
        else:
            o_ref[rows, :] += y

    for first in (True, False):
        is_first = (f == 0) if first else (f > 0)
        for s in range(0, n_sub, 2):
            if s + 2 <= n_sub:
                @pl.when((s + 2 <= ns) & is_first)
                def _():
                    swiglu(slice(s * sub, (s + 2) * sub), first)

            @pl.when((s + 1 == ns) & is_first)
            def _():
                swiglu(slice(s * sub, (s + 1) * sub), first)


def _expert_ffn(blk_exp, n_valid_sub, x_blk, xs, wg, wu, wd, sb, sub, tf=512):
    p, d = xs.shape
    dff = wg.shape[2]
    n_super = n_valid_sub.shape[0]
    nf = dff // tf

    def fidx(i, f, ns):
        used = jnp.minimum(ns[i], 1)
        return f * used + (nf - 1) * (1 - used)

    return pl.pallas_call(
        functools.partial(_expert_kernel, sub=sub, n_sub=sb // sub),
        grid_spec=pltpu.PrefetchScalarGridSpec(
            num_scalar_prefetch=3,
            grid=(n_super, nf),
            in_specs=[pl.BlockSpec((sb, d), lambda i, f, be, ns, xi: (xi[i], 0)),
                      pl.BlockSpec((None, d, tf), lambda i, f, be, ns, xi: (be[i], 0, fidx(i, f, ns))),
                      pl.BlockSpec((None, d, tf), lambda i, f, be, ns, xi: (be[i], 0, fidx(i, f, ns))),
                      pl.BlockSpec((None, tf, d), lambda i, f, be, ns, xi: (be[i], fidx(i, f, ns), 0))],
            out_specs=pl.BlockSpec((sb, d), lambda i, f, be, ns, xi: (i, 0))),
        out_shape=jax.ShapeDtypeStruct((n_super * sb, d), F32),
        compiler_params=pltpu.CompilerParams(dimension_semantics=("arbitrary", "arbitrary"),
                                             vmem_limit_bytes=EXPERT_VMEM_LIMIT_BYTES),
        name="moe_experts",
    )(blk_exp, n_valid_sub, x_blk, xs, wg, wu, wd)


def _combine_kernel(st_ref, nc_ref, pos_ref, x_ref, info_ref, col_ref, ys_hbm, gf_ref, o_ref, buf_ref, sem,
                    *, rows, final_norm, blk_off, win):
    ck = MOE_CHUNK
    i = pl.program_id(0)

    def chunk_copy(step, e, c):
        t = (step + blk_off) * N_EXPERTS + e
        src = ys_hbm.at[pl.ds(pl.multiple_of(st_ref[t] + c * ck, 8), ck)]
        dst = buf_ref.at[step % 2, pl.ds(pl.multiple_of(pos_ref[t] + c * ck, ck), ck)]
        return pltpu.make_async_copy(src, dst, sem.at[step % 2])

    def all_chunks(step, fn):
        for e in range(N_EXPERTS):
            def body(c, carry, e=e):
                fn(chunk_copy(step, e, c))
                return carry
            lax.fori_loop(0, nc_ref[(step + blk_off) * N_EXPERTS + e], body, 0)

    @pl.when(i == 0)
    def _():
        buf_ref[...] = jnp.zeros_like(buf_ref)
        all_chunks(i, lambda cp: cp.start())

    @pl.when(i + 1 < pl.num_programs(0))
    def _():
        all_chunks(i + 1, lambda cp: cp.start())

    info = info_ref[...]
    cols = col_ref[...]
    lane = lax.broadcasted_iota(jnp.int32, (rows, win), 1)
    p = (jnp.where(cols[:, 0:1] == lane, info[:, 4:5], 0.0)
         + jnp.where(cols[:, 1:2] == lane, info[:, 5:6], 0.0)).astype(BF16)
    all_chunks(i, lambda cp: cp.wait())
    acc = x_ref[...] + _mm(p, buf_ref[i % 2])
    if final_norm:
        acc = _rms_rows(acc, gf_ref[...])
    o_ref[...] = acc


def _combine(seg_start, n_chunks, win_pos, cols, x, info, ys, gf, final_norm, row_off, n_rows, rows, win):
    d = x.shape[1]
    off = row_off // rows
    return pl.pallas_call(
        functools.partial(_combine_kernel, rows=rows, final_norm=final_norm, blk_off=off, win=win),
        grid_spec=pltpu.PrefetchScalarGridSpec(
            num_scalar_prefetch=3,
            grid=(n_rows // rows,),
            in_specs=[pl.BlockSpec((rows, d), lambda i, *_: (i + off, 0)),
                      pl.BlockSpec((rows, LANES), lambda i, *_: (i + off, 0)),
                      pl.BlockSpec((rows, 2), lambda i, *_: (i + off, 0)),
                      pl.BlockSpec(memory_space=pl.ANY),
                      pl.BlockSpec((1, d), lambda i, *_: (0, 0))],
            out_specs=pl.BlockSpec((rows, d), lambda i, *_: (i, 0)),
            scratch_shapes=[pltpu.VMEM((2, win, d), F32), pltpu.SemaphoreType.DMA((2,))]),
        out_shape=jax.ShapeDtypeStruct((n_rows, d), F32),
        compiler_params=_cparams(1),
        name="moe_combine",
    )(seg_start, n_chunks, win_pos, x, info, cols, ys, gf.reshape(1, d))


def _moe_ffn(x, g, whi, wlo, wg, wu, wd, gf, final_norm, groups, sub=512):
    t, d = x.shape
    sb = 2048 if 2 * t >= 16 * 1024 else sub
    tb = t
    for off, n in groups:
        tb = math.gcd(tb, math.gcd(off, n))
    tb = _row_tile(tb, 512)
    info, cnt = _router(x, g, whi, wlo, tb)
    e = info[:, 0:2].astype(jnp.int32)
    rank = info[:, 2:4].astype(jnp.int32)
    after = cnt[::8, :N_EXPERTS].astype(jnp.int32)
    before = jnp.concatenate([jnp.zeros((1, N_EXPERTS), jnp.int32), after[:-1]], axis=0)
    n_blocks = t // tb
    ck = MOE_CHUNK
    n_seg = after - before
    seg_len = (n_seg + 7) // 8 * 8
    seg_rel = jnp.cumsum(seg_len, axis=0) - seg_len
    counts = jnp.sum(seg_len, axis=0)
    n_sb = (counts + ck + sb - 1) // sb
    sb_end = jnp.cumsum(n_sb)
    sb_start = sb_end - n_sb
    row_start = sb_start * sb
    seg_start = row_start[None, :] + seg_rel
    n_chunks = (n_seg + ck - 1) // ck
    win_pos = (jnp.cumsum(n_chunks, axis=1) - n_chunks) * ck
    win = (2 * tb + N_EXPERTS * (ck - 1) + LANES - 1) // LANES * LANES
    eid = jnp.arange(N_EXPERTS, dtype=jnp.int32)
    tok_origin = jnp.repeat(win_pos - before, tb, axis=0)
    cols = rank + jnp.sum(jnp.where(e[:, :, None] == eid, tok_origin[:, None, :], 0), axis=-1)
    cols = cols.astype(jnp.int32)
    n_super = (2 * t + n_blocks * N_EXPERTS * 7) // sb + N_EXPERTS + 2
    blk = jnp.arange(n_super, dtype=jnp.int32)
    n_used = sb_end[-1]
    used = blk < n_used
    blk_c = jnp.minimum(blk, n_used - 1)
    be = jnp.minimum(jnp.sum((blk_c[:, None] >= sb_end[None, :]).astype(jnp.int32), axis=-1),
                     N_EXPERTS - 1)
    valid = jnp.clip(counts[be] - (blk_c - sb_start[be]) * sb, 0, sb)
    n_valid_sub = jnp.where(used, (valid + sub - 1) // sub, 0).astype(jnp.int32)
    zero_start = (row_start + counts).astype(jnp.int32)
    per = sb // sub
    sub_in_blk = jnp.arange(per, dtype=jnp.int32)
    zero_blk = (sub_in_blk[None, :] >= n_valid_sub[:, None]).astype(jnp.int32).reshape(-1)
    zero_blk = jnp.concatenate([zero_blk, jnp.ones((1,), jnp.int32)])
    tables = [a.reshape(-1).astype(jnp.int32) for a in (seg_start, n_chunks, win_pos)]
    xs = _dispatch(*tables, zero_start, zero_blk, cols.T, x, g, sub, tb, win)
    ys = _expert_ffn(be.astype(jnp.int32), n_valid_sub, blk_c.astype(jnp.int32), xs, wg, wu, wd, sb, sub)
    return [_combine(*tables, cols, x, info, ys, gf, final_norm, off, n, tb, win) for off, n in groups]


def _final_norm_kernel(x_ref, g_ref, o_ref):
    o_ref[...] = _rms_rows(x_ref[...], g_ref[...])


def _final_norm(x, g, row_off, n_rows, tm=512):
    d = x.shape[1]
    t = n_rows
    tm = _row_tile(t, tm)
    off = row_off // tm
    return pl.pallas_call(
        _final_norm_kernel,
        grid=(t // tm,),
        in_specs=[pl.BlockSpec((tm, d), lambda i: (i + off, 0)), pl.BlockSpec((1, d), lambda i: (0, 0))],
        out_specs=pl.BlockSpec((tm, d), lambda i: (i, 0)),
        out_shape=jax.ShapeDtypeStruct((t, d), F32),
        compiler_params=_cparams(1),
        name="final_norm",
    )(x, g.reshape(1, d))


def _block_diag(w):
    n, c, d = w.shape
    eye = jnp.eye(n, dtype=w.dtype)
    return jnp.einsum("ncd,nm->ncmd", w, eye).reshape(n * c, n * d)


def _pad_cols(w, n):
    return jnp.pad(w, ((0, 0), (0, n - w.shape[1])))


def _layer_params(l, p):
    (norm_mix, w_in, lru_conv_w, lru_conv_b, lru_a_w, lru_a_b, lru_x_w, lru_x_b, lru_lam, gla_gk_w2,
     gla_gk_b, gla_norm, gdn_conv_w, gdn_a_log, gdn_dt_bias, gdn_norm, w_out, norm_xq, norm_mem,
     w_mq, w_mk, w_mv, w_mo, norm_ffn) = [a[l] for a in p]
    lw = LRU_W
    gk, gv = GLA_HEADS * GLA_DK, GLA_HEADS * GLA_DV
    dh = GDN_HEADS * GDN_DK
    offs = [0]
    for s in (lw, lw, gk, gk, gv, GLA_RANK, gv, dh, dh, dh, GDN_HEADS, GDN_HEADS, dh):
        offs.append(offs[-1] + s)
    col = lambda i: w_in[:, offs[i]:offs[i + 1]]
    w_lru = jnp.concatenate([col(0), col(1)], axis=1)
    w_gla = jnp.concatenate([col(2), col(3), col(4), col(6), _pad_cols(col(5), LANES)], axis=1)
    w_gdn = jnp.concatenate(
        [col(7), col(8), col(9), col(12), _pad_cols(jnp.concatenate([col(10), col(11)], axis=1), LANES)],
        axis=1)
    w_cat = jnp.concatenate([w_lru, w_gla, w_gdn], axis=1).astype(BF16)
    widths = (w_lru.shape[1], w_gla.shape[1], w_gdn.shape[1])
    alog = jnp.zeros((1, LANES), F32).at[0, GDN_HEADS:2 * GDN_HEADS].set(gdn_a_log)
    dtb = jnp.zeros((1, LANES), F32).at[0, GDN_HEADS:2 * GDN_HEADS].set(gdn_dt_bias)
    return dict(
        norm_mix=norm_mix, w_cat=w_cat, widths=widths,
        lru_conv_w=lru_conv_w, lru_conv_b=lru_conv_b,
        lru_a=_block_diag(lru_a_w).astype(BF16), lru_a_b=lru_a_b,
        lru_x=_block_diag(lru_x_w).astype(BF16), lru_x_b=lru_x_b, lru_lam=lru_lam,
        gla_w2=jnp.pad(gla_gk_w2, ((0, LANES - GLA_RANK), (0, 0))).astype(BF16), gla_gk_b=gla_gk_b,
        gla_norm=jnp.tile(gla_norm, GLA_HEADS),
        gdn_conv_w=gdn_conv_w, gdn_alog=alog, gdn_dtb=dtb, gdn_norm=jnp.tile(gdn_norm, GDN_HEADS),
        w_out=w_out.astype(BF16), norm_xq=norm_xq, w_mq=w_mq.astype(BF16), w_mo=w_mo.astype(BF16),
        norm_ffn=norm_ffn)


def _mix_and_attend(x, x_off, grp, l, lp, joint, total_rows):
    batch, seq, n = grp["batch"], grp["seq"], grp["batch"] * grp["seq"]
    lru_h0, lru_buf0, gla_s0, gdn_s0, gdn_buf0 = grp["states"]
    sl = grp["state_layer"](l)
    z_lru, z_gla, z_gdn = _rms_matmul(x, lp["norm_mix"], lp["w_cat"], lp["widths"], "in_proj",
                                      row_off=x_off, n_rows=n)
    y_lru, lru_h, lru_buf = _lru_mixer(
        z_lru, lru_h0, lru_buf0, sl, lp["lru_conv_w"], lp["lru_conv_b"], lp["lru_a"], lp["lru_a_b"],
        lp["lru_x"], lp["lru_x_b"], lp["lru_lam"], batch, seq)
    y_gla, gla_s = _gla_mixer(z_gla.reshape(batch, seq, -1), gla_s0, sl, lp["gla_w2"],
                              lp["gla_gk_b"], lp["gla_norm"], batch, seq)
    y_gdn, gdn_s, gdn_buf = _gdn_mixer(z_gdn.reshape(batch, seq, -1), gdn_s0, gdn_buf0, sl,
                                       lp["gdn_conv_w"], lp["gdn_alog"], lp["gdn_dtb"],
                                       lp["gdn_norm"], batch, seq)
    xn, q = _outproj(x, y_lru, y_gla.reshape(n, -1), y_gdn.reshape(n, -1),
                     lp["w_out"], lp["norm_xq"], lp["w_mq"], row_off=x_off)
    mem_k, mem_v, mem_layer = grp["mem"][l]
    joint = _attention(xn, q, mem_k, mem_v, mem_layer, lp["w_mo"], batch, seq, joint, grp["row_off"],
                       total_rows)
    return joint, (lru_h.reshape(batch, LRU_W), lru_buf, gla_s, gdn_s, gdn_buf)


def _run_layers(groups, layers, ffn, norm_final):
    total_rows = sum(g["batch"] * g["seq"] for g in groups)
    spans = [(g["row_off"], g["batch"] * g["seq"]) for g in groups]
    xs = [(g["x"], 0) for g in groups]
    new_states = [[] for _ in groups]
    outs = None
    for l, lp in enumerate(layers):
        joint = None
        for gi, grp in enumerate(groups):
            joint, st = _mix_and_attend(xs[gi][0], xs[gi][1], grp, l, lp, joint, total_rows)
            new_states[gi].append(st)
        last = l == len(layers) - 1
        kind, fp = ffn[l]
        if kind == "dense":
            joint = _ffn_dense(joint, lp["norm_ffn"], *fp)
            if last:
                outs = [_final_norm(joint, norm_final, off, n) for off, n in spans]
        elif last:
            outs = _moe_ffn(joint, lp["norm_ffn"], *fp, norm_final, True, spans)
        else:
            joint = _moe_ffn(joint, lp["norm_ffn"], *fp, norm_final, False, [(0, total_rows)])[0]
        xs = [(joint, g["row_off"]) for g in groups]
    states = [[jnp.stack(s) for s in zip(*ns)] for ns in new_states]
    return outs, states


def kernel(x_prompt, x_sample, mem_prompt, state_lru_h, state_lru_conv, state_gla, state_gdn, state_gdn_conv, cache_mem_k, cache_mem_v, norm_mix, w_in, lru_conv_w, lru_conv_b, lru_a_w, lru_a_b, lru_x_w, lru_x_b, lru_lam, gla_gk_w2, gla_gk_b, gla_norm, gdn_conv_w, gdn_a_log, gdn_dt_bias, gdn_norm, w_out, norm_xq, norm_mem, w_mq, w_mk, w_mv, w_mo, norm_ffn, w_ff_gate, w_ff_up, w_ff_down, w_router, w_e_gate, w_e_up, w_e_down, norm_final):
    depth = norm_mix.shape[0]
    per_layer = (norm_mix, w_in, lru_conv_w, lru_conv_b, lru_a_w, lru_a_b, lru_x_w, lru_x_b, lru_lam,
                 gla_gk_w2, gla_gk_b, gla_norm, gdn_conv_w, gdn_a_log, gdn_dt_bias, gdn_norm, w_out,
                 norm_xq, norm_mem, w_mq, w_mk, w_mv, w_mo, norm_ffn)
    layers = [_layer_params(l, per_layer) for l in range(depth)]
    ffn = []
    for l in range(depth):
        j = l // 2
        if l % 2 == 0:
            ffn.append(("dense", (w_ff_gate[j], w_ff_up[j], w_ff_down[j])))
        else:
            wr = _pad_cols(w_router[j], LANES)
            whi = wr.astype(BF16)
            wlo = (wr - whi.astype(F32)).astype(BF16)
            ffn.append(("moe", (whi, wlo, w_e_gate[j], w_e_up[j], w_e_down[j])))

    bp, mlen, d = mem_prompt.shape
    w_mkv = jnp.concatenate([w_mk, w_mv], axis=2).astype(BF16)
    pk, pv, p_mem_k, p_mem_v = _mem_kv(mem_prompt, norm_mem, w_mkv)
    zero_state = (jnp.zeros((1, bp, LRU_W), F32), jnp.zeros((1, bp, CONV_K - 1, LRU_W), F32),
                  jnp.zeros((1, bp, GLA_HEADS, GLA_DK, GLA_DV), F32),
                  jnp.zeros((1, bp, GDN_HEADS, GDN_DK, GDN_DV), F32),
                  jnp.zeros((1, bp, CONV_K - 1, 3 * GDN_HEADS * GDN_DK), F32))
    sp = x_prompt.shape[1]
    bs, ss = x_sample.shape[0], x_sample.shape[1]
    groups = [
        dict(x=x_prompt.reshape(bp * sp, d), batch=bp, seq=sp, row_off=0, states=zero_state,
             state_layer=lambda l: 0, mem=[(pk, pv, l) for l in range(depth)]),
        dict(x=x_sample.reshape(bs * ss, d), batch=bs, seq=ss, row_off=bp * sp,
             states=(state_lru_h, state_lru_conv, state_gla, state_gdn, state_gdn_conv),
             state_layer=lambda l: l, mem=[(cache_mem_k, cache_mem_v, l) for l in range(depth)]),
    ]
    (y_p, y_s), (p_st, s_st) = _run_layers(groups, layers, ffn, norm_final)

    return (y_p.reshape(bp, sp, d), y_s.reshape(bs, ss, d), p_st[0], p_st[1], p_st[2], p_st[3], p_st[4],
            p_mem_k, p_mem_v, s_st[0], s_st[1], s_st[2], s_st[3], s_st[4])
```

```python
import functools
import math

import jax
import jax.numpy as jnp
from jax import lax
from jax.experimental import pallas as pl
from jax.experimental.pallas import tpu as pltpu

F32 = jnp.float32
BF16 = jnp.bfloat16
EPS = 1e-6

D_MODEL = 1024
LRU_W = 512
LRU_BLOCKS = 8
LRU_C = 8.0
CONV_K = 4
GLA_HEADS = 4
GLA_DK = 32
GLA_DV = 64
GLA_RANK = 16
GLA_TAU = 16.0
GLA_SUB = 16
GDN_HEADS = 4
GDN_DK = 64
GDN_DV = 64
MIX_CHUNK = 64
MEM_HEADS = 4
MEM_HD = 256
N_EXPERTS = 8
LANES = 128
VMEM_LIMIT_BYTES = 48 * 1024 * 1024
EXPERT_VMEM_LIMIT_BYTES = 56 * 1024 * 1024


def _cparams(n_axes):
    return pltpu.CompilerParams(dimension_semantics=("arbitrary",) * n_axes,
                                vmem_limit_bytes=VMEM_LIMIT_BYTES)


def _mm(a, b):
    return jnp.dot(a.astype(BF16), b.astype(BF16), preferred_element_type=F32)


def _mm_nt(a, b):
    return lax.dot_general(a.astype(BF16), b.astype(BF16), (((1,), (1,)), ((), ())),
                           preferred_element_type=F32)


def _mm_tn(a, b):
    return lax.dot_general(a.astype(BF16), b.astype(BF16), (((0,), (0,)), ((), ())),
                           preferred_element_type=F32)


def _rms_rows(x, g):
    ms = jnp.mean(x * x, axis=-1, keepdims=True)
    return (x * lax.rsqrt(ms + EPS)) * g


def _softplus(x):
    return jnp.maximum(x, 0.0) + jnp.log1p(jnp.exp(-jnp.abs(x)))


def _sigmoid(x):
    return 1.0 / (1.0 + jnp.exp(-x))


def _silu(x):
    return x * _sigmoid(x)


def _gelu_tanh(x):
    c = 0.7978845608028654
    return x * (0.5 * (1.0 + jnp.tanh(c * (x + 0.044715 * (x * x * x)))))


def _seg_cumsum_rows(x, seg):
    rows = x.shape[0]
    tpos = lax.broadcasted_iota(jnp.int32, (rows, 1), 0) & (seg - 1)
    d = 1
    while d < seg:
        x = x + jnp.where(tpos >= d, pltpu.roll(x, d, axis=0), 0.0)
        d *= 2
    return x


def _head_rms(o, gain, n_heads, width):
    lane_head = lax.broadcasted_iota(jnp.int32, (1, n_heads * width), 1) // width
    sq = o * o
    inv = jnp.zeros_like(o)
    for h in range(n_heads):
        m = lane_head == h
        ms = jnp.sum(jnp.where(m, sq, 0.0), axis=-1, keepdims=True) * (1.0 / width)
        inv = jnp.where(m, lax.rsqrt(ms + EPS), inv)
    return (o * inv) * gain


def _stack_heads(x, n_heads, width):
    c = x.shape[0]
    t = jnp.concatenate([x] * n_heads, axis=0)
    row_head = lax.broadcasted_iota(jnp.int32, (n_heads * c, 1), 0) // c
    lane_head = lax.broadcasted_iota(jnp.int32, (1, n_heads * width), 1) // width
    return jnp.where(row_head == lane_head, t, 0.0)


def _unstack_heads(x, n_heads):
    c = x.shape[0] // n_heads
    o = x[0:c]
    for h in range(1, n_heads):
        o = o + x[h * c:(h + 1) * c]
    return o


def _rms_matmul_kernel(x_ref, g_ref, w_ref, *o_refs):
    h = _rms_rows(x_ref[...], g_ref[...]).astype(BF16)
    start = 0
    for o_ref in o_refs:
        n = o_ref.shape[1]
        o_ref[...] = jnp.dot(h, w_ref[:, start:start + n], preferred_element_type=F32)
        start += n


def _row_tile(t, pref):
    tile = min(pref, t)
    while t % tile or tile % 8:
        tile -= 8
    return tile


def _rms_matmul(x, g, w, widths, name, tm=512, row_off=0, n_rows=None):
    d = x.shape[1]
    t = x.shape[0] if n_rows is None else n_rows
    n = w.shape[1]
    tm = _row_tile(t, tm)
    off = row_off // tm
    out_specs = [pl.BlockSpec((tm, wd), lambda i: (i, 0)) for wd in widths]
    out_shape = [jax.ShapeDtypeStruct((t, wd), F32) for wd in widths]
    return pl.pallas_call(
        _rms_matmul_kernel,
        grid=(t // tm,),
        in_specs=[pl.BlockSpec((tm, d), lambda i: (i + off, 0)),
                  pl.BlockSpec((1, d), lambda i: (0, 0)),
                  pl.BlockSpec((d, n), lambda i: (0, 0))],
        out_specs=out_specs,
        out_shape=out_shape,
        compiler_params=_cparams(1),
        name=name,
    )(x, g.reshape(1, d), w)


def _mem_kv_kernel(m_ref, g_ref, w_ref, k_ref, v_ref, k5_ref, v5_ref):
    d = m_ref.shape[1]
    h = _rms_rows(m_ref[...], g_ref[...])
    kv = _mm(h, w_ref[...])
    k_ref[...] = kv[:, :d].astype(k_ref.dtype)
    v_ref[...] = kv[:, d:].astype(v_ref.dtype)
    for hd in range(MEM_HEADS):
        sl = slice(hd * MEM_HD, (hd + 1) * MEM_HD)
        k5_ref[:, hd, :] = kv[:, sl]
        v5_ref[:, hd, :] = kv[:, d + hd * MEM_HD:d + (hd + 1) * MEM_HD]


def _mem_kv(mem, g, w):
    bp, mlen, d = mem.shape
    depth = w.shape[0]
    flat = jax.ShapeDtypeStruct((depth, bp, mlen, d), BF16)
    split = jax.ShapeDtypeStruct((depth, bp, mlen, MEM_HEADS, MEM_HD), F32)
    fidx = lambda l, b: (l, b, 0, 0)
    sidx = lambda l, b: (l, b, 0, 0, 0)
    return pl.pallas_call(
        _mem_kv_kernel,
        grid=(depth, bp),
        in_specs=[pl.BlockSpec((None, mlen, d), lambda l, b: (b, 0, 0)),
                  pl.BlockSpec((None, 1, d), lambda l, b: (l, 0, 0)),
                  pl.BlockSpec((None, d, 2 * d), lambda l, b: (l, 0, 0))],
        out_specs=[pl.BlockSpec((None, None, mlen, d), fidx),
                   pl.BlockSpec((None, None, mlen, d), fidx),
                   pl.BlockSpec((None, None, mlen, MEM_HEADS, MEM_HD), sidx),
                   pl.BlockSpec((None, None, mlen, MEM_HEADS, MEM_HD), sidx)],
        out_shape=[flat, flat, split, split],
        compiler_params=_cparams(2),
        name="mem_kv",
    )(mem, g.reshape(depth, 1, d), w)


def _lru_coeffs(xc, aw_ref, ab_ref, xw_ref, xb_ref, lam_ref):
    r = _sigmoid(_mm(xc, aw_ref[...]) + ab_ref[...])
    ig = _sigmoid(_mm(xc, xw_ref[...]) + xb_ref[...])
    log_a = (-LRU_C * r) * _softplus(-lam_ref[...])
    a = jnp.exp(log_a)
    th = jnp.tanh(log_a)
    return a, jnp.sqrt((-2.0 * th) / (1.0 - th)) * (ig * xc)


def _lru_kernel(z_ref, h0_ref, buf0_ref, cw_ref, cb_ref, aw_ref, ab_ref, xw_ref, xb_ref, lam_ref,
                y_ref, hout_ref, bufout_ref, xs_ref, hc_ref, *, bt, tl, nt):
    w = LRU_W
    rows = bt * tl
    j = pl.program_id(0) % nt

    @pl.when(j == 0)
    def _():
        xs_ref[:, 5:8, :] = buf0_ref[...]
        hc_ref[...] = h0_ref[...]

    xs_ref[:, 8:, :] = z_ref[:, :w].reshape(bt, tl, w)
    gate = z_ref[:, w:]
    cw = cw_ref[...]
    xc = cb_ref[...] + xs_ref[:, 5:5 + tl, :] * cw[0:1]
    for k in range(1, CONV_K):
        xc = xc + xs_ref[:, 5 + k:5 + k + tl, :] * cw[k:k + 1]
    tail = xs_ref[:, 5 + tl:8 + tl, :]
    xs_ref[:, 5:8, :] = tail
    bufout_ref[...] = tail

    a, b = _lru_coeffs(xc.reshape(rows, w), aw_ref, ab_ref, xw_ref, xb_ref, lam_ref)

    sub = 8
    gps = tl // sub
    a3 = a.reshape(rows // sub, sub, w)
    b3 = b.reshape(rows // sub, sub, w)
    spos = lax.broadcasted_iota(jnp.int32, (1, sub, 1), 1)
    d = 1
    while d < sub:
        m = spos >= d
        b3 = jnp.where(m, a3 * pltpu.roll(b3, d, axis=1) + b3, b3)
        a3 = jnp.where(m, a3 * pltpu.roll(a3, d, axis=1), a3)
        d *= 2
    a4 = a3.reshape(bt, gps, sub, w)
    b4 = b3.reshape(bt, gps, sub, w)
    carry = hc_ref[...]
    hs = []
    for r in range(gps):
        hr = b4[:, r] + a4[:, r] * carry
        hs.append(hr)
        carry = hr[:, sub - 1:sub, :]
    h = jnp.stack(hs, axis=1).reshape(rows, w)
    hlast = carry
    hc_ref[...] = hlast
    hout_ref[...] = hlast
    y_ref[...] = h * _gelu_tanh(gate)


def _lru_mixer(z, h0, buf0, layer, cw, cb, aw, ab, xw, xb, lam, batch, seq):
    w = LRU_W
    tl = min(seq, 512)
    bt = min(batch, max(1, 512 // seq))
    nt = seq // tl
    rows = bt * tl
    grid = (batch * seq // rows,)
    if nt > 1:
        sidx = lambda i: (i // nt, 0, 0)
        lidx = lambda i: (layer, i // nt, 0, 0)
    else:
        sidx = lambda i: (i, 0, 0)
        lidx = lambda i: (layer, i, 0, 0)
    full2 = lambda i: (0, 0)
    return pl.pallas_call(
        functools.partial(_lru_kernel, bt=bt, tl=tl, nt=nt),
        grid=grid,
        in_specs=[pl.BlockSpec((rows, 2 * w), lambda i: (i, 0)),
                  pl.BlockSpec((None, bt, 1, w), lidx),
                  pl.BlockSpec((None, bt, CONV_K - 1, w), lidx),
                  pl.BlockSpec((CONV_K, w), full2),
                  pl.BlockSpec((1, w), full2),
                  pl.BlockSpec((w, w), full2),
                  pl.BlockSpec((1, w), full2),
                  pl.BlockSpec((w, w), full2),
                  pl.BlockSpec((1, w), full2),
                  pl.BlockSpec((1, w), full2)],
        out_specs=[pl.BlockSpec((rows, w), lambda i: (i, 0)),
                   pl.BlockSpec((bt, 1, w), sidx),
                   pl.BlockSpec((bt, CONV_K - 1, w), sidx)],
        out_shape=[jax.ShapeDtypeStruct((batch * seq, w), F32),
                   jax.ShapeDtypeStruct((batch, 1, w), F32),
                   jax.ShapeDtypeStruct((batch, CONV_K - 1, w), F32)],
        scratch_shapes=[pltpu.VMEM((bt, 8 + tl, w), F32), pltpu.VMEM((bt, 1, w), F32)],
        compiler_params=_cparams(1),
        name="lru_mixer",
    )(z, h0.reshape(h0.shape[0], batch, 1, w), buf0, cw, cb.reshape(1, w), aw, ab.reshape(1, w), xw,
      xb.reshape(1, w), lam.reshape(1, w))


def _gla_kernel(z_ref, s0_ref, w2_ref, gb_ref, gn_ref, y_ref, sout_ref, s_ref, *, c, sc, g):
    nh = GLA_HEADS
    kw = nh * GLA_DK
    vw = nh * GLA_DV

    @pl.when(pl.program_id(1) == 0)
    def _():
        s_ref[...] = jnp.zeros_like(s_ref)
        for b in range(g):
            for h in range(nh):
                s_ref[b, h * GLA_DK:(h + 1) * GLA_DK, h * GLA_DV:(h + 1) * GLA_DV] = s0_ref[b, h]

    bs = range(g)
    q = [z_ref[b, :, 0:kw] * (GLA_DK ** -0.5) for b in bs]
    k = [z_ref[b, :, kw:2 * kw] for b in bs]
    v = [z_ref[b, :, 2 * kw:2 * kw + vw] for b in bs]
    gk = [-_softplus(-(_mm(z_ref[b, :, 2 * kw + 2 * vw:], w2_ref[...]) + gb_ref[...])) / GLA_TAU
          for b in bs]
    gcum = [_seg_cumsum_rows(gk[b], sc) for b in bs]
    qp = [q[b] * jnp.exp(gcum[b]) for b in bs]

    ri = lax.broadcasted_iota(jnp.int32, (nh * c, nh * c), 0)
    ci = lax.broadcasted_iota(jnp.int32, (nh * c, nh * c), 1)
    keep = (ri // sc == ci // sc) & (ci <= ri)
    a = [_mm_nt(_stack_heads(qp[b], nh, GLA_DK), _stack_heads(k[b] * jnp.exp(-gcum[b]), nh, GLA_DK))
         for b in bs]
    o = [_unstack_heads(_mm(jnp.where(keep, a[b], 0.0), _stack_heads(v[b], nh, GLA_DV)), nh)
         for b in bs]

    s = [s_ref[b] for b in bs]
    eye = (lax.broadcasted_iota(jnp.int32, (kw, kw), 0) ==
           lax.broadcasted_iota(jnp.int32, (kw, kw), 1))
    bd = (lax.broadcasted_iota(jnp.int32, (kw, vw), 0) // GLA_DK ==
          lax.broadcasted_iota(jnp.int32, (kw, vw), 1) // GLA_DV)
    o_inter = [[] for _ in bs]
    for i in range(c // sc):
        lo, hi = i * sc, (i + 1) * sc
        for b in bs:
            o_inter[b].append(_mm(qp[b][lo:hi], s[b]))
            glast = gcum[b][hi - 1:hi]
            kpp = k[b][lo:hi] * jnp.exp(glast - gcum[b][lo:hi])
            u = _mm_tn(kpp, v[b][lo:hi])
            dcol = jnp.sum(jnp.where(eye, jnp.exp(glast), 0.0), axis=1, keepdims=True)
            s[b] = dcol * s[b] + jnp.where(bd, u, 0.0)
    for b in bs:
        s_ref[b] = s[b]
        for h in range(nh):
            sout_ref[b, h] = s[b][h * GLA_DK:(h + 1) * GLA_DK, h * GLA_DV:(h + 1) * GLA_DV]
        ob = o[b] + jnp.concatenate(o_inter[b], axis=0)
        gate = z_ref[b, :, 2 * kw + vw:2 * kw + 2 * vw]
        y_ref[b] = _head_rms(ob, gn_ref[...], nh, GLA_DV) * _silu(gate)


def _gla_mixer(z, s0, layer, w2, gb, gn, batch, seq):
    c = min(seq, MIX_CHUNK)
    sc = min(c, GLA_SUB)
    nt = seq // c
    g = _mixer_group(batch, c)
    kw, vw = GLA_HEADS * GLA_DK, GLA_HEADS * GLA_DV
    zw = z.shape[2]
    full2 = lambda i, j: (0, 0)
    sblk = (g, GLA_HEADS, GLA_DK, GLA_DV)
    return pl.pallas_call(
        functools.partial(_gla_kernel, c=c, sc=sc, g=g),
        grid=(batch // g, nt),
        in_specs=[pl.BlockSpec((g, c, zw), lambda i, j: (i, j, 0)),
                  pl.BlockSpec((None,) + sblk, lambda i, j: (layer, i, 0, 0, 0)),
                  pl.BlockSpec((LANES, kw), full2),
                  pl.BlockSpec((1, kw), full2),
                  pl.BlockSpec((1, vw), full2)],
        out_specs=[pl.BlockSpec((g, c, vw), lambda i, j: (i, j, 0)),
                   pl.BlockSpec(sblk, lambda i, j: (i, 0, 0, 0))],
        out_shape=[jax.ShapeDtypeStruct((batch, seq, vw), F32),
                   jax.ShapeDtypeStruct((batch, GLA_HEADS, GLA_DK, GLA_DV), F32)],
        scratch_shapes=[pltpu.VMEM((g, kw, vw), F32)],
        compiler_params=_cparams(2),
        name="gla_mixer",
    )(z, s0, w2, gb.reshape(1, kw), gn.reshape(1, vw))


def _gdn_prep(z_ref, cw_ref, alog_ref, dtb_ref, bufout_ref, xs_ref, *, c):
    nh = GDN_HEADS
    hw = nh * GDN_DK
    cw3 = 3 * hw

    xs_ref[8:, :] = z_ref[:, 0:cw3]
    cw = cw_ref[...]
    qkv = xs_ref[5:5 + c, :] * cw[0:1]
    for kk in range(1, CONV_K):
        qkv = qkv + xs_ref[5 + kk:5 + kk + c, :] * cw[kk:kk + 1]
    tail = xs_ref[5 + c:8 + c, :]
    xs_ref[5:8, :] = tail
    bufout_ref[...] = tail
    qkv = _silu(qkv)
    zg = z_ref[:, cw3:cw3 + hw]
    sm = z_ref[:, cw3 + hw:]

    lane_head = lax.broadcasted_iota(jnp.int32, (1, hw), 1) // GDN_DK

    def l2n(x):
        sq = x * x
        inv = jnp.zeros_like(x)
        for h in range(nh):
            m = lane_head == h
            ss = jnp.sum(jnp.where(m, sq, 0.0), axis=-1, keepdims=True)
            inv = jnp.where(m, lax.rsqrt(ss + EPS), inv)
        return x * inv

    q = l2n(qkv[:, 0:hw]) * (GDN_DK ** -0.5)
    k = l2n(qkv[:, hw:2 * hw])
    v = qkv[:, 2 * hw:3 * hw]
    beta = _sigmoid(sm)
    glog = -jnp.exp(alog_ref[...]) * _softplus(sm + dtb_ref[...])
    gcum = _seg_cumsum_rows(glog, c)

    n = nh * c
    bcol = jnp.concatenate([beta[:, h:h + 1] for h in range(nh)], axis=0)
    gcol = jnp.concatenate([gcum[:, nh + h:nh + h + 1] for h in range(nh)], axis=0)
    glast = jnp.concatenate(
        [jnp.broadcast_to(gcum[c - 1:c, nh + h:nh + h + 1], (c, 1)) for h in range(nh)], axis=0)
    ri = lax.broadcasted_iota(jnp.int32, (n, n), 0)
    ci = lax.broadcasted_iota(jnp.int32, (n, n), 1)
    grow = jnp.sum(jnp.where(ri == ci, gcol, 0.0), axis=0, keepdims=True)
    incl = (ri // c == ci // c) & (ci <= ri)
    dec = jnp.where(incl, jnp.exp(jnp.where(incl, gcol - grow, 0.0)), 0.0)

    sdec = jnp.concatenate(
        [jnp.broadcast_to(jnp.exp(gcum[c - 1:c, nh + h:nh + h + 1]), (GDN_DK, 1)) for h in range(nh)],
        axis=0)
    return dict(ks=_stack_heads(k, nh, GDN_DK), qs=_stack_heads(q, nh, GDN_DK),
                vs=_stack_heads(v, nh, GDN_DV), bcol=bcol, egc=jnp.exp(gcol), dec=dec,
                kdec=jnp.exp(glast - gcol), sdec=sdec, zg=zg)


def _gdn_kernel(z_ref, s0_ref, buf0_ref, cw_ref, alog_ref, dtb_ref, gn_ref,
                y_ref, sout_ref, bufout_ref, xs_ref, s_ref, *, c, g):
    nh = GDN_HEADS
    hw = nh * GDN_DK
    n = nh * c

    @pl.when(pl.program_id(1) == 0)
    def _():
        xs_ref[:, 5:8, :] = buf0_ref[...]
        s_ref[...] = jnp.zeros_like(s_ref)
        for b in range(g):
            for h in range(nh):
                s_ref[b, h * GDN_DK:(h + 1) * GDN_DK, h * GDN_DV:(h + 1) * GDN_DV] = s0_ref[b, h]

    bs = range(g)
    pr = [_gdn_prep(z_ref.at[b], cw_ref, alog_ref, dtb_ref, bufout_ref.at[b], xs_ref.at[b], c=c)
          for b in bs]
    ri = lax.broadcasted_iota(jnp.int32, (n, n), 0)
    ci = lax.broadcasted_iota(jnp.int32, (n, n), 1)
    same = ri // c == ci // c
    incl = same & (ci <= ri)
    strict = same & (ci < ri)
    kq = [_mm_nt(jnp.concatenate([pr[b]["ks"], pr[b]["qs"]], axis=0), pr[b]["ks"]) for b in bs]

    p = [jnp.where(strict, -(pr[b]["bcol"] * kq[b][0:n]) * pr[b]["dec"], 0.0) for b in bs]
    tinv = [jnp.where(ri == ci, 1.0, 0.0) + p[b] for b in bs]
    span = 2
    while span < c:
        p = [_mm(p[b], p[b]) for b in bs]
        tinv = [tinv[b] + _mm(tinv[b], p[b]) for b in bs]
        span *= 2

    s = [s_ref[b] for b in bs]
    uw = [_mm(tinv[b], jnp.concatenate([pr[b]["vs"] * pr[b]["bcol"],
                                        pr[b]["ks"] * (pr[b]["bcol"] * pr[b]["egc"])], axis=1))
          for b in bs]
    qw = [_mm(jnp.concatenate([pr[b]["qs"] * pr[b]["egc"], uw[b][:, hw:]], axis=0), s[b])
          for b in bs]
    vnew = [uw[b][:, 0:hw] - qw[b][n:] for b in bs]
    av = [_mm(jnp.where(incl, kq[b][n:] * pr[b]["dec"], 0.0), vnew[b]) for b in bs]
    kv = [_mm_tn(pr[b]["ks"] * pr[b]["kdec"], vnew[b]) for b in bs]
    for b in bs:
        o = _unstack_heads(qw[b][0:n] + av[b], nh)
        sn = pr[b]["sdec"] * s[b] + kv[b]
        s_ref[b] = sn
        for h in range(nh):
            sout_ref[b, h] = sn[h * GDN_DK:(h + 1) * GDN_DK, h * GDN_DV:(h + 1) * GDN_DV]
        y_ref[b] = _head_rms(o, gn_ref[...], nh, GDN_DV) * _silu(pr[b]["zg"])


def _mixer_group(batch, c):
    return min(batch, max(8, 256 // c))


def _gdn_mixer(z, s0, buf0, layer, cw, alog, dtb, gn, batch, seq):
    c = min(seq, MIX_CHUNK)
    nt = seq // c
    g = _mixer_group(batch, c)
    hw = GDN_HEADS * GDN_DK
    zw = z.shape[2]
    full2 = lambda i, j: (0, 0)
    sblk = (g, GDN_HEADS, GDN_DK, GDN_DV)
    return pl.pallas_call(
        functools.partial(_gdn_kernel, c=c, g=g),
        grid=(batch // g, nt),
        in_specs=[pl.BlockSpec((g, c, zw), lambda i, j: (i, j, 0)),
                  pl.BlockSpec((None,) + sblk, lambda i, j: (layer, i, 0, 0, 0)),
                  pl.BlockSpec((None, g, CONV_K - 1, 3 * hw), lambda i, j: (layer, i, 0, 0)),
                  pl.BlockSpec((CONV_K, 3 * hw), full2),
                  pl.BlockSpec((1, LANES), full2),
                  pl.BlockSpec((1, LANES), full2),
                  pl.BlockSpec((1, hw), full2)],
        out_specs=[pl.BlockSpec((g, c, hw), lambda i, j: (i, j, 0)),
                   pl.BlockSpec(sblk, lambda i, j: (i, 0, 0, 0)),
                   pl.BlockSpec((g, CONV_K - 1, 3 * hw), lambda i, j: (i, 0, 0))],
        out_shape=[jax.ShapeDtypeStruct((batch, seq, hw), F32),
                   jax.ShapeDtypeStruct((batch, GDN_HEADS, GDN_DK, GDN_DV), F32),
                   jax.ShapeDtypeStruct((batch, CONV_K - 1, 3 * hw), F32)],
        scratch_shapes=[pltpu.VMEM((g, 8 + c, 3 * hw), F32), pltpu.VMEM((g, hw, hw), F32)],
        compiler_params=_cparams(2),
        name="gdn_mixer",
    )(z, s0, buf0, cw, alog, dtb, gn.reshape(1, hw))


def _outproj_kernel(x_ref, yl_ref, yg_ref, yd_ref, wo_ref, gq_ref, wq_ref, xn_ref, q_ref):
    lw = LRU_W
    gw = GLA_HEADS * GLA_DV
    y = _mm(yl_ref[...], wo_ref[0:lw, :])
    y = y + _mm(yg_ref[...], wo_ref[lw:lw + gw, :])
    y = y + _mm(yd_ref[...], wo_ref[lw + gw:, :])
    xn = x_ref[...] + y
    xn_ref[...] = xn
    q_ref[...] = _mm(_rms_rows(xn, gq_ref[...]), wq_ref[...])


def _outproj(x, yl, yg, yd, wo, gq, wq, tm=1024, row_off=0):
    d = x.shape[1]
    t = yl.shape[0]
    tm = _row_tile(t, tm)
    off = row_off // tm
    row = lambda i: (i, 0)
    full2 = lambda i: (0, 0)
    return pl.pallas_call(
        _outproj_kernel,
        grid=(t // tm,),
        in_specs=[pl.BlockSpec((tm, d), lambda i: (i + off, 0)),
                  pl.BlockSpec((tm, yl.shape[1]), row),
                  pl.BlockSpec((tm, yg.shape[1]), row),
                  pl.BlockSpec((tm, yd.shape[1]), row),
                  pl.BlockSpec((d, d), full2),
                  pl.BlockSpec((1, d), full2),
                  pl.BlockSpec((d, d), full2)],
        out_specs=[pl.BlockSpec((tm, d), row), pl.BlockSpec((tm, d), row)],
        out_shape=[jax.ShapeDtypeStruct((t, d), F32), jax.ShapeDtypeStruct((t, d), F32)],
        compiler_params=_cparams(1),
        name="outproj_qproj",
    )(x, yl, yg, yd, wo, gq.reshape(1, d), wq)


def _attn_kernel(x_ref, q_ref, k_ref, v_ref, wo_ref, o_ref):
    hd = MEM_HD
    acc = x_ref[...]
    for h in range(MEM_HEADS):
        sl = slice(h * hd, (h + 1) * hd)
        s = _mm_nt(q_ref[:, sl], k_ref[:, sl]) * (hd ** -0.5)
        m = jnp.max(s, axis=-1, keepdims=True)
        p = jnp.exp(s - m)
        l = jnp.sum(p, axis=-1, keepdims=True)
        oh = _mm(p, v_ref[:, sl]) / l
        acc = acc + _mm(oh, wo_ref[sl, :])
    o_ref[...] = acc


def _attn_heads_kernel(q_ref, k_ref, v_ref, o_ref, *, gb, tl):
    nh, hd = MEM_HEADS, MEM_HD
    m = k_ref.shape[1]
    row_head = lax.broadcasted_iota(jnp.int32, (nh * tl, 1), 0) // tl
    col_head = lax.broadcasted_iota(jnp.int32, (1, m * nh), 1) % nh
    for b in range(gb):
        kf = k_ref[b].reshape(m * nh, hd)
        vf = v_ref[b].reshape(m * nh, hd)
        rows = slice(b * tl, (b + 1) * tl)
        qs = jnp.concatenate([q_ref[rows, h * hd:(h + 1) * hd] for h in range(nh)], axis=0)
        s = _mm_nt(qs, kf) * (hd ** -0.5)
        s = jnp.where(row_head == col_head, s, -jnp.inf)
        mx = jnp.max(s, axis=-1, keepdims=True)
        p = jnp.exp(s - mx)
        l = jnp.sum(p, axis=-1, keepdims=True)
        o = _mm(p, vf) / l
        for h in range(nh):
            o_ref[rows, h * hd:(h + 1) * hd] = o[h * tl:(h + 1) * tl]


def _oproj_kernel(x_ref, a_ref, wo_ref, *rest):
    o_ref = rest[-1]
    o_ref[...] = x_ref[...] + _mm(a_ref[...], wo_ref[...])


def _attn_first_kernel(x_ref, q_ref, k_ref, v_ref, wo_ref, o_ref, *, n_real):
    i = pl.program_id(0)

    @pl.when(i < n_real)
    def _():
        _attn_kernel(x_ref, q_ref, k_ref, v_ref, wo_ref, o_ref)

    @pl.when(i >= n_real)
    def _():
        o_ref[...] = jnp.zeros_like(o_ref)


def _attn_into_kernel(x_ref, q_ref, k_ref, v_ref, wo_ref, joint_ref, o_ref):
    _attn_kernel(x_ref, q_ref, k_ref, v_ref, wo_ref, o_ref)


def _attention(x, q, k, v, layer, wo, batch, seq, joint, row_off, total_rows):
    t, d = x.shape
    out_shape = jax.ShapeDtypeStruct((total_rows, d), F32)
    extra_specs = [] if joint is None else [pl.BlockSpec(memory_space=pl.ANY)]
    extra_args = () if joint is None else (joint,)
    if k.ndim == 4:
        tl = min(seq, 512)
        nl = seq // tl
        off = row_off // tl
        kblk = (None, None) + k.shape[2:]
        n_real = batch * nl
        if joint is None:
            assert row_off == 0
            n_steps = pl.cdiv(total_rows, tl)
            body = functools.partial(_attn_first_kernel, n_real=n_real)
        else:
            n_steps = n_real
            body = _attn_into_kernel
        kidx = lambda i: (layer, jnp.minimum(i // nl, batch - 1), 0, 0)
        row = lambda i: (jnp.minimum(i, n_real - 1), 0)
        return pl.pallas_call(
            body,
            grid=(n_steps,),
            in_specs=[pl.BlockSpec((tl, d), row),
                      pl.BlockSpec((tl, d), row),
                      pl.BlockSpec(kblk, kidx),
                      pl.BlockSpec(kblk, kidx),
                      pl.BlockSpec((d, d), lambda i: (0, 0))] + extra_specs,
            out_specs=pl.BlockSpec((tl, d), lambda i: (i + off, 0)),
            out_shape=out_shape,
            input_output_aliases={} if joint is None else {5: 0},
            compiler_params=_cparams(1),
            name="mem_attention",
        )(x, q, k, v, wo, *extra_args)
    gb = 4 if batch % 4 == 0 else 2
    kblk = (None, gb) + k.shape[2:]
    kidx = lambda i: (layer, i, 0, 0, 0)
    att = pl.pallas_call(
        functools.partial(_attn_heads_kernel, gb=gb, tl=seq),
        grid=(batch // gb,),
        in_specs=[pl.BlockSpec((gb * seq, d), lambda i: (i, 0)),
                  pl.BlockSpec(kblk, kidx),
                  pl.BlockSpec(kblk, kidx)],
        out_specs=pl.BlockSpec((gb * seq, d), lambda i: (i, 0)),
        out_shape=jax.ShapeDtypeStruct((t, d), F32),
        compiler_params=_cparams(1),
        name="mem_attention_heads",
    )(q, k, v)
    tm = _row_tile(t, 512)
    off = row_off // tm
    return pl.pallas_call(
        _oproj_kernel,
        grid=(t // tm,),
        in_specs=[pl.BlockSpec((tm, d), lambda i: (i, 0)),
                  pl.BlockSpec((tm, d), lambda i: (i, 0)),
                  pl.BlockSpec((d, d), lambda i: (0, 0))] + extra_specs,
        out_specs=pl.BlockSpec((tm, d), lambda i: (i + off, 0)),
        out_shape=out_shape,
        input_output_aliases={} if joint is None else {3: 0},
        compiler_params=_cparams(1),
        name="mem_oproj",
    )(x, att, wo, *extra_args)


def _ffn_kernel(x_ref, g_ref, wg_ref, wu_ref, wd_ref, o_ref, h_ref):
    f = pl.program_id(1)

    @pl.when(f == 0)
    def _():
        h_ref[...] = _rms_rows(x_ref[...], g_ref[...]).astype(BF16)

    def chunk():
        h = h_ref[...]
        a = _mm(h, wg_ref[...])
        u = _mm(h, wu_ref[...])
        return _mm(_silu(a) * u, wd_ref[...])

    @pl.when(f == 0)
    def _():
        o_ref[...] = x_ref[...] + chunk()

    @pl.when(f > 0)
    def _():
        o_ref[...] += chunk()


def _ffn_dense(x, g, wg, wu, wd, tm=1024, tf=512):
    t, d = x.shape
    dff = wg.shape[1]
    tm = _row_tile(t, tm)
    return pl.pallas_call(
        _ffn_kernel,
        grid=(t // tm, dff // tf),
        in_specs=[pl.BlockSpec((tm, d), lambda i, f: (i, 0)),
                  pl.BlockSpec((1, d), lambda i, f: (0, 0)),
                  pl.BlockSpec((d, tf), lambda i, f: (0, f)),
                  pl.BlockSpec((d, tf), lambda i, f: (0, f)),
                  pl.BlockSpec((tf, d), lambda i, f: (f, 0))],
        out_specs=pl.BlockSpec((tm, d), lambda i, f: (i, 0)),
        out_shape=jax.ShapeDtypeStruct((t, d), F32),
        scratch_shapes=[pltpu.VMEM((tm, d), BF16)],
        compiler_params=_cparams(2),
        name="ffn_dense",
    )(x, g.reshape(1, d), wg, wu, wd)


def _router_kernel(x_ref, g_ref, whi_ref, wlo_ref, tri_ref, info_ref, cnt_ref, carry_ref, *, tm):
    i = pl.program_id(0)

    @pl.when(i == 0)
    def _():
        carry_ref[...] = jnp.zeros_like(carry_ref)

    h = _rms_rows(x_ref[...], g_ref[...])
    hhi = h.astype(BF16)
    hlo = (h - hhi.astype(F32)).astype(BF16)
    whi = whi_ref[...]
    logits = (jnp.dot(hhi, whi, preferred_element_type=F32)
              + jnp.dot(hlo, whi, preferred_element_type=F32)
              + jnp.dot(hhi, wlo_ref[...], preferred_element_type=F32))
    lane = lax.broadcasted_iota(jnp.int32, (tm, LANES), 1)
    neg = jnp.float32(-jnp.inf)
    logits = jnp.where(lane < N_EXPERTS, logits, neg)
    m1 = jnp.max(logits, axis=-1, keepdims=True)
    i1 = jnp.min(jnp.where(logits == m1, lane, LANES), axis=-1, keepdims=True)
    rest = jnp.where(lane == i1, neg, logits)
    m2 = jnp.max(rest, axis=-1, keepdims=True)
    i2 = jnp.min(jnp.where(rest == m2, lane, LANES), axis=-1, keepdims=True)
    e = jnp.exp(m2 - m1)
    g1 = 1.0 / (1.0 + e)
    g2 = e / (1.0 + e)
    oh1 = jnp.where(lane == i1, 1.0, 0.0)
    oh2 = jnp.where(lane == i2, 1.0, 0.0)
    oh = oh1 + oh2
    before = _mm(tri_ref[...], oh) + carry_ref[0:1, :]
    r1 = jnp.sum(oh1 * before, axis=-1, keepdims=True)
    r2 = jnp.sum(oh2 * before, axis=-1, keepdims=True)
    carry = carry_ref[0:1, :] + jnp.sum(oh, axis=0, keepdims=True)
    carry_ref[...] = jnp.broadcast_to(carry, carry_ref.shape)
    cnt_ref[...] = jnp.broadcast_to(carry, cnt_ref.shape)
    info = jnp.where(lane == 0, i1.astype(F32), 0.0)
    info = jnp.where(lane == 1, i2.astype(F32), info)
    info = jnp.where(lane == 2, r1, info)
    info = jnp.where(lane == 3, r2, info)
    info = jnp.where(lane == 4, g1, info)
    info = jnp.where(lane == 5, g2, info)
    info_ref[...] = info


def _router(x, g, whi, wlo, tm):
    t, d = x.shape
    tri = jnp.tril(jnp.ones((tm, tm), BF16), -1)
    return pl.pallas_call(
        functools.partial(_router_kernel, tm=tm),
        grid=(t // tm,),
        in_specs=[pl.BlockSpec((tm, d), lambda i: (i, 0)),
                  pl.BlockSpec((1, d), lambda i: (0, 0)),
                  pl.BlockSpec((d, LANES), lambda i: (0, 0)),
                  pl.BlockSpec((d, LANES), lambda i: (0, 0)),
                  pl.BlockSpec((tm, tm), lambda i: (0, 0))],
        out_specs=[pl.BlockSpec((tm, LANES), lambda i: (i, 0)),
                   pl.BlockSpec((8, LANES), lambda i: (i, 0))],
        out_shape=[jax.ShapeDtypeStruct((t, LANES), F32),
                   jax.ShapeDtypeStruct((t // tm * 8, LANES), F32)],
        scratch_shapes=[pltpu.VMEM((8, LANES), F32)],
        compiler_params=_cparams(1),
        name="moe_router",
    )(x, g.reshape(1, d), whi, wlo, tri)


MOE_CHUNK = 32


def _dispatch_kernel(st_ref, nc_ref, pos_ref, zs_ref, zf_ref, x_ref, g_ref, ct_ref, xs_hbm, w_ref, zbuf_ref,
                     zsem, sem, *, rows, sub, n_blk, win):
    i = pl.program_id(0)

    @pl.when(i == 0)
    def _():
        zbuf_ref[...] = jnp.zeros_like(zbuf_ref)
        for e in range(N_EXPERTS):
            dst = xs_hbm.at[pl.ds(pl.multiple_of(zs_ref[e], 8), sub)]
            pltpu.make_async_copy(zbuf_ref, dst, zsem).start()
            pltpu.make_async_copy(zbuf_ref, dst, zsem).wait()

        def fill(j, carry):
            @pl.when(zf_ref[j] == 1)
            def _():
                dst = xs_hbm.at[pl.ds(pl.multiple_of(j * sub, sub), sub)]
                pltpu.make_async_copy(zbuf_ref, dst, zsem).start()
            return carry

        def drain(j, carry):
            @pl.when(zf_ref[j] == 1)
            def _():
                dst = xs_hbm.at[pl.ds(pl.multiple_of(j * sub, sub), sub)]
                pltpu.make_async_copy(zbuf_ref, dst, zsem).wait()
            return carry

        lax.fori_loop(0, n_blk, fill, 0)
        lax.fori_loop(0, n_blk, drain, 0)

    ck = MOE_CHUNK
    n_steps = pl.num_programs(0)

    def chunk_copy(step, e, c):
        t = step * N_EXPERTS + e
        src = w_ref.at[step % 2, pl.ds(pl.multiple_of(pos_ref[t] + c * ck, ck), ck)]
        dst = xs_hbm.at[pl.ds(pl.multiple_of(st_ref[t] + c * ck, 8), ck)]
        return pltpu.make_async_copy(src, dst, sem)

    def all_chunks(step, fn):
        for e in range(N_EXPERTS):
            def body(c, carry, e=e):
                fn(chunk_copy(step, e, c))
                return carry
            lax.fori_loop(0, nc_ref[step * N_EXPERTS + e], body, 0)

    h = _rms_rows(x_ref[...], g_ref[...])
    wrow = lax.broadcasted_iota(jnp.int32, (win, rows), 0)
    cols = ct_ref[...]
    onehot = jnp.where(wrow == cols[0:1, :], 1.0, 0.0) + jnp.where(wrow == cols[1:2, :], 1.0, 0.0)
    w_ref[i % 2] = _mm(onehot, h)

    @pl.when(i >= 1)
    def _():
        all_chunks(i - 1, lambda cp: cp.wait())

    all_chunks(i, lambda cp: cp.start())

    @pl.when(i == n_steps - 1)
    def _():
        all_chunks(i, lambda cp: cp.wait())


def _dispatch(seg_start, n_chunks, win_pos, zero_start, zero_blk, cols_t, x, g, sub, rows, win):
    t, d = x.shape
    n_blk = zero_blk.shape[0]
    n_rows = n_blk * sub
    return pl.pallas_call(
        functools.partial(_dispatch_kernel, rows=rows, sub=sub, n_blk=n_blk, win=win),
        grid_spec=pltpu.PrefetchScalarGridSpec(
            num_scalar_prefetch=5,
            grid=(t // rows,),
            in_specs=[pl.BlockSpec((rows, d), lambda i, *_: (i, 0)),
                      pl.BlockSpec((1, d), lambda i, *_: (0, 0)),
                      pl.BlockSpec((2, rows), lambda i, *_: (0, i))],
            out_specs=pl.BlockSpec(memory_space=pl.ANY),
            scratch_shapes=[pltpu.VMEM((2, win, d), F32), pltpu.VMEM((sub, d), F32),
                            pltpu.SemaphoreType.DMA(()), pltpu.SemaphoreType.DMA(())]),
        out_shape=jax.ShapeDtypeStruct((n_rows, d), F32),
        compiler_params=_cparams(1),
        name="moe_dispatch",
    )(seg_start, n_chunks, win_pos, zero_start, zero_blk, x, g.reshape(1, d), cols_t)


def _expert_kernel(be_ref, ns_ref, xi_ref, x_ref, wg_ref, wu_ref, wd_ref, o_ref, *, sub, n_sub):
    i = pl.program_id(0)
    f = pl.program_id(1)
    ns = ns_ref[i]

    for s in range(n_sub):
        @pl.when((f == 0) & (s >= ns))
        def _():
            o_ref[s * sub:(s + 1) * sub, :] = jnp.zeros((sub, o_ref.shape[1]), o_ref.dtype)

    def swiglu(rows, first):
        h = x_ref[rows, :]
        a = _mm(h, wg_ref[...])
        u = _mm(h, wu_ref[...])
        y = _mm(_silu(a) * u, wd_ref[...])
        if first:
            o_ref[rows, :] = y
        else:
            o_ref[rows, :] += y

    for first in (True, False):
        is_first = (f == 0) if first else (f > 0)
        for s in range(0, n_sub, 2):
            if s + 2 <= n_sub:
                @pl.when((s + 2 <= ns) & is_first)
                def _():
                    swiglu(slice(s * sub, (s + 2) * sub), first)

            @pl.when((s + 1 == ns) & is_first)
            def _():
                swiglu(slice(s * sub, (s + 1) * sub), first)


def _expert_ffn(blk_exp, n_valid_sub, x_blk, xs, wg, wu, wd, sb, sub, tf=512):
    p, d = xs.shape
    dff = wg.shape[2]
    n_super = n_valid_sub.shape[0]
    nf = dff // tf

    def fidx(i, f, ns):
        used = jnp.minimum(ns[i], 1)
        return f * used + (nf - 1) * (1 - used)

    return pl.pallas_call(
        functools.partial(_expert_kernel, sub=sub, n_sub=sb // sub),
        grid_spec=pltpu.PrefetchScalarGridSpec(
            num_scalar_prefetch=3,
            grid=(n_super, nf),
            in_specs=[pl.BlockSpec((sb, d), lambda i, f, be, ns, xi: (xi[i], 0)),
                      pl.BlockSpec((None, d, tf), lambda i, f, be, ns, xi: (be[i], 0, fidx(i, f, ns))),
                      pl.BlockSpec((None, d, tf), lambda i, f, be, ns, xi: (be[i], 0, fidx(i, f, ns))),
                      pl.BlockSpec((None, tf, d), lambda i, f, be, ns, xi: (be[i], fidx(i, f, ns), 0))],
            out_specs=pl.BlockSpec((sb, d), lambda i, f, be, ns, xi: (i, 0))),
        out_shape=jax.ShapeDtypeStruct((n_super * sb, d), F32),
        compiler_params=pltpu.CompilerParams(dimension_semantics=("arbitrary", "arbitrary"),
                                             vmem_limit_bytes=EXPERT_VMEM_LIMIT_BYTES),
        name="moe_experts",
    )(blk_exp, n_valid_sub, x_blk, xs, wg, wu, wd)


def _combine_kernel(st_ref, nc_ref, pos_ref, x_ref, info_ref, col_ref, ys_hbm, gf_ref, o_ref, buf_ref, sem,
                    *, rows, final_norm, blk_off, win):
    ck = MOE_CHUNK
    i = pl.program_id(0)

    def chunk_copy(step, e, c):
        t = (step + blk_off) * N_EXPERTS + e
        src = ys_hbm.at[pl.ds(pl.multiple_of(st_ref[t] + c * ck, 8), ck)]
        dst = buf_ref.at[step % 2, pl.ds(pl.multiple_of(pos_ref[t] + c * ck, ck), ck)]
        return pltpu.make_async_copy(src, dst, sem.at[step % 2])

    def all_chunks(step, fn):
        for e in range(N_EXPERTS):
            def body(c, carry, e=e):
                fn(chunk_copy(step, e, c))
                return carry
            lax.fori_loop(0, nc_ref[(step + blk_off) * N_EXPERTS + e], body, 0)

    @pl.when(i == 0)
    def _():
        buf_ref[...] = jnp.zeros_like(buf_ref)
        all_chunks(i, lambda cp: cp.start())

    @pl.when(i + 1 < pl.num_programs(0))
    def _():
        all_chunks(i + 1, lambda cp: cp.start())

    info = info_ref[...]
    cols = col_ref[...]
    lane = lax.broadcasted_iota(jnp.int32, (rows, win), 1)
    p = (jnp.where(cols[:, 0:1] == lane, info[:, 4:5], 0.0)
         + jnp.where(cols[:, 1:2] == lane, info[:, 5:6], 0.0)).astype(BF16)
    all_chunks(i, lambda cp: cp.wait())
    acc = x_ref[...] + _mm(p, buf_ref[i % 2])
    if final_norm:
        acc = _rms_rows(acc, gf_ref[...])
    o_ref[...] = acc


def _combine(seg_start, n_chunks, win_pos, cols, x, info, ys, gf, final_norm, row_off, n_rows, rows, win):
    d = x.shape[1]
    off = row_off // rows
    return pl.pallas_call(
        functools.partial(_combine_kernel, rows=rows, final_norm=final_norm, blk_off=off, win=win),
        grid_spec=pltpu.PrefetchScalarGridSpec(
            num_scalar_prefetch=3,
            grid=(n_rows // rows,),
            in_specs=[pl.BlockSpec((rows, d), lambda i, *_: (i + off, 0)),
                      pl.BlockSpec((rows, LANES), lambda i, *_: (i + off, 0)),
                      pl.BlockSpec((rows, 2), lambda i, *_: (i + off, 0)),
                      pl.BlockSpec(memory_space=pl.ANY),
                      pl.BlockSpec((1, d), lambda i, *_: (0, 0))],
            out_specs=pl.BlockSpec((rows, d), lambda i, *_: (i, 0)),
            scratch_shapes=[pltpu.VMEM((2, win, d), F32), pltpu.SemaphoreType.DMA((2,))]),
        out_shape=jax.ShapeDtypeStruct((n_rows, d), F32),
        compiler_params=_cparams(1),
        name="moe_combine",
    )(seg_start, n_chunks, win_pos, x, info, cols, ys, gf.reshape(1, d))


def _moe_ffn(x, g, whi, wlo, wg, wu, wd, gf, final_norm, groups, sub=512):
    t, d = x.shape
    sb = 2048 if 2 * t >= 16 * 1024 else sub
    tb = t
    for off, n in groups:
        tb = math.gcd(tb, math.gcd(off, n))
    tb = _row_tile(tb, 512)
    info, cnt = _router(x, g, whi, wlo, tb)
    e = info[:, 0:2].astype(jnp.int32)
    rank = info[:, 2:4].astype(jnp.int32)
    after = cnt[::8, :N_EXPERTS].astype(jnp.int32)
    before = jnp.concatenate([jnp.zeros((1, N_EXPERTS), jnp.int32), after[:-1]], axis=0)
    n_blocks = t // tb
    ck = MOE_CHUNK
    n_seg = after - before
    seg_len = (n_seg + 7) // 8 * 8
    seg_rel = jnp.cumsum(seg_len, axis=0) - seg_len
    counts = jnp.sum(seg_len, axis=0)
    n_sb = (counts + ck + sb - 1) // sb
    sb_end = jnp.cumsum(n_sb)
    sb_start = sb_end - n_sb
    row_start = sb_start * sb
    seg_start = row_start[None, :] + seg_rel
    n_chunks = (n_seg + ck - 1) // ck
    win_pos = (jnp.cumsum(n_chunks, axis=1) - n_chunks) * ck
    win = (2 * tb + N_EXPERTS * (ck - 1) + LANES - 1) // LANES * LANES
    eid = jnp.arange(N_EXPERTS, dtype=jnp.int32)
    tok_origin = jnp.repeat(win_pos - before, tb, axis=0)
    cols = rank + jnp.sum(jnp.where(e[:, :, None] == eid, tok_origin[:, None, :], 0), axis=-1)
    cols = cols.astype(jnp.int32)
    n_super = (2 * t + n_blocks * N_EXPERTS * 7) // sb + N_EXPERTS + 2
    blk = jnp.arange(n_super, dtype=jnp.int32)
    n_used = sb_end[-1]
    used = blk < n_used
    blk_c = jnp.minimum(blk, n_used - 1)
    be = jnp.minimum(jnp.sum((blk_c[:, None] >= sb_end[None, :]).astype(jnp.int32), axis=-1),
                     N_EXPERTS - 1)
    valid = jnp.clip(counts[be] - (blk_c - sb_start[be]) * sb, 0, sb)
    n_valid_sub = jnp.where(used, (valid + sub - 1) // sub, 0).astype(jnp.int32)
    zero_start = (row_start + counts).astype(jnp.int32)
    per = sb // sub
    sub_in_blk = jnp.arange(per, dtype=jnp.int32)
    zero_blk = (sub_in_blk[None, :] >= n_valid_sub[:, None]).astype(jnp.int32).reshape(-1)
    zero_blk = jnp.concatenate([zero_blk, jnp.ones((1,), jnp.int32)])
    tables = [a.reshape(-1).astype(jnp.int32) for a in (seg_start, n_chunks, win_pos)]
    xs = _dispatch(*tables, zero_start, zero_blk, cols.T, x, g, sub, tb, win)
    ys = _expert_ffn(be.astype(jnp.int32), n_valid_sub, blk_c.astype(jnp.int32), xs, wg, wu, wd, sb, sub)
    return [_combine(*tables, cols, x, info, ys, gf, final_norm, off, n, tb, win) for off, n in groups]


def _final_norm_kernel(x_ref, g_ref, o_ref):
    o_ref[...] = _rms_rows(x_ref[...], g_ref[...])


def _final_norm(x, g, row_off, n_rows, tm=512):
    d = x.shape[1]
    t = n_rows
    tm = _row_tile(t, tm)
    off = row_off // tm
    return pl.pallas_call(
        _final_norm_kernel,
        grid=(t // tm,),
        in_specs=[pl.BlockSpec((tm, d), lambda i: (i + off, 0)), pl.BlockSpec((1, d), lambda i: (0, 0))],
        out_specs=pl.BlockSpec((tm, d), lambda i: (i, 0)),
        out_shape=jax.ShapeDtypeStruct((t, d), F32),
        compiler_params=_cparams(1),
        name="final_norm",
    )(x, g.reshape(1, d))


def _block_diag(w):
    n, c, d = w.shape
    eye = jnp.eye(n, dtype=w.dtype)
    return jnp.einsum("ncd,nm->ncmd", w, eye).reshape(n * c, n * d)


def _pad_cols(w, n):
    return jnp.pad(w, ((0, 0), (0, n - w.shape[1])))


def _layer_params(l, p):
    (norm_mix, w_in, lru_conv_w, lru_conv_b, lru_a_w, lru_a_b, lru_x_w, lru_x_b, lru_lam, gla_gk_w2,
     gla_gk_b, gla_norm, gdn_conv_w, gdn_a_log, gdn_dt_bias, gdn_norm, w_out, norm_xq, norm_mem,
     w_mq, w_mk, w_mv, w_mo, norm_ffn) = [a[l] for a in p]
    lw = LRU_W
    gk, gv = GLA_HEADS * GLA_DK, GLA_HEADS * GLA_DV
    dh = GDN_HEADS * GDN_DK
    offs = [0]
    for s in (lw, lw, gk, gk, gv, GLA_RANK, gv, dh, dh, dh, GDN_HEADS, GDN_HEADS, dh):
        offs.append(offs[-1] + s)
    col = lambda i: w_in[:, offs[i]:offs[i + 1]]
    w_lru = jnp.concatenate([col(0), col(1)], axis=1)
    w_gla = jnp.concatenate([col(2), col(3), col(4), col(6), _pad_cols(col(5), LANES)], axis=1)
    w_gdn = jnp.concatenate(
        [col(7), col(8), col(9), col(12), _pad_cols(jnp.concatenate([col(10), col(11)], axis=1), LANES)],
        axis=1)
    w_cat = jnp.concatenate([w_lru, w_gla, w_gdn], axis=1).astype(BF16)
    widths = (w_lru.shape[1], w_gla.shape[1], w_gdn.shape[1])
    alog = jnp.zeros((1, LANES), F32).at[0, GDN_HEADS:2 * GDN_HEADS].set(gdn_a_log)
    dtb = jnp.zeros((1, LANES), F32).at[0, GDN_HEADS:2 * GDN_HEADS].set(gdn_dt_bias)
    return dict(
        norm_mix=norm_mix, w_cat=w_cat, widths=widths,
        lru_conv_w=lru_conv_w, lru_conv_b=lru_conv_b,
        lru_a=_block_diag(lru_a_w).astype(BF16), lru_a_b=lru_a_b,
        lru_x=_block_diag(lru_x_w).astype(BF16), lru_x_b=lru_x_b, lru_lam=lru_lam,
        gla_w2=jnp.pad(gla_gk_w2, ((0, LANES - GLA_RANK), (0, 0))).astype(BF16), gla_gk_b=gla_gk_b,
        gla_norm=jnp.tile(gla_norm, GLA_HEADS),
        gdn_conv_w=gdn_conv_w, gdn_alog=alog, gdn_dtb=dtb, gdn_norm=jnp.tile(gdn_norm, GDN_HEADS),
        w_out=w_out.astype(BF16), norm_xq=norm_xq, w_mq=w_mq.astype(BF16), w_mo=w_mo.astype(BF16),
        norm_ffn=norm_ffn)


def _mix_and_attend(x, x_off, grp, l, lp, joint, total_rows):
    batch, seq, n = grp["batch"], grp["seq"], grp["batch"] * grp["seq"]
    lru_h0, lru_buf0, gla_s0, gdn_s0, gdn_buf0 = grp["states"]
    sl = grp["state_layer"](l)
    z_lru, z_gla, z_gdn = _rms_matmul(x, lp["norm_mix"], lp["w_cat"], lp["widths"], "in_proj",
                                      row_off=x_off, n_rows=n)
    y_lru, lru_h, lru_buf = _lru_mixer(
        z_lru, lru_h0, lru_buf0, sl, lp["lru_conv_w"], lp["lru_conv_b"], lp["lru_a"], lp["lru_a_b"],
        lp["lru_x"], lp["lru_x_b"], lp["lru_lam"], batch, seq)
    y_gla, gla_s = _gla_mixer(z_gla.reshape(batch, seq, -1), gla_s0, sl, lp["gla_w2"],
                              lp["gla_gk_b"], lp["gla_norm"], batch, seq)
    y_gdn, gdn_s, gdn_buf = _gdn_mixer(z_gdn.reshape(batch, seq, -1), gdn_s0, gdn_buf0, sl,
                                       lp["gdn_conv_w"], lp["gdn_alog"], lp["gdn_dtb"],
                                       lp["gdn_norm"], batch, seq)
    xn, q = _outproj(x, y_lru, y_gla.reshape(n, -1), y_gdn.reshape(n, -1),
                     lp["w_out"], lp["norm_xq"], lp["w_mq"], row_off=x_off)
    mem_k, mem_v, mem_layer = grp["mem"][l]
    joint = _attention(xn, q, mem_k, mem_v, mem_layer, lp["w_mo"], batch, seq, joint, grp["row_off"],
                       total_rows)
    return joint, (lru_h.reshape(batch, LRU_W), lru_buf, gla_s, gdn_s, gdn_buf)


def _run_layers(groups, layers, ffn, norm_final):
    total_rows = sum(g["batch"] * g["seq"] for g in groups)
    spans = [(g["row_off"], g["batch"] * g["seq"]) for g in groups]
    xs = [(g["x"], 0) for g in groups]
    new_states = [[] for _ in groups]
    outs = None
    for l, lp in enumerate(layers):
        joint = None
        for gi, grp in enumerate(groups):
            joint, st = _mix_and_attend(xs[gi][0], xs[gi][1], grp, l, lp, joint, total_rows)
            new_states[gi].append(st)
        last = l == len(layers) - 1
        kind, fp = ffn[l]
        if kind == "dense":
            joint = _ffn_dense(joint, lp["norm_ffn"], *fp)
            if last:
                outs = [_final_norm(joint, norm_final, off, n) for off, n in spans]
        elif last:
            outs = _moe_ffn(joint, lp["norm_ffn"], *fp, norm_final, True, spans)
        else:
            joint = _moe_ffn(joint, lp["norm_ffn"], *fp, norm_final, False, [(0, total_rows)])[0]
        xs = [(joint, g["row_off"]) for g in groups]
    states = [[jnp.stack(s) for s in zip(*ns)] for ns in new_states]
    return outs, states


def kernel(x_prompt, x_sample, mem_prompt, state_lru_h, state_lru_conv, state_gla, state_gdn, state_gdn_conv, cache_mem_k, cache_mem_v, norm_mix, w_in, lru_conv_w, lru_conv_b, lru_a_w, lru_a_b, lru_x_w, lru_x_b, lru_lam, gla_gk_w2, gla_gk_b, gla_norm, gdn_conv_w, gdn_a_log, gdn_dt_bias, gdn_norm, w_out, norm_xq, norm_mem, w_mq, w_mk, w_mv, w_mo, norm_ffn, w_ff_gate, w_ff_up, w_ff_down, w_router, w_e_gate, w_e_up, w_e_down, norm_final):
    depth = norm_mix.shape[0]
    per_layer = (norm_mix, w_in, lru_conv_w, lru_conv_b, lru_a_w, lru_a_b, lru_x_w, lru_x_b, lru_lam,
                 gla_gk_w2, gla_gk_b, gla_norm, gdn_conv_w, gdn_a_log, gdn_dt_bias, gdn_norm, w_out,
                 norm_xq, norm_mem, w_mq, w_mk, w_mv, w_mo, norm_ffn)
    layers = [_layer_params(l, per_layer) for l in range(depth)]
    ffn = []
    for l in range(depth):
        j = l // 2
        if l % 2 == 0:
            ffn.append(("dense", (w_ff_gate[j], w_ff_up[j], w_ff_down[j])))
        else:
            wr = _pad_cols(w_router[j], LANES)
            whi = wr.astype(BF16)
            wlo = (wr - whi.astype(F32)).astype(BF16)
            ffn.append(("moe", (whi, wlo, w_e_gate[j], w_e_up[j], w_e_down[j])))

    bp, mlen, d = mem_prompt.shape
    w_mkv = jnp.concatenate([w_mk, w_mv], axis=2).astype(BF16)
    pk, pv, p_mem_k, p_mem_v = _mem_kv(mem_prompt, norm_mem, w_mkv)
    zero_state = (jnp.zeros((1, bp, LRU_W), F32), jnp.zeros((1, bp, CONV_K - 1, LRU_W), F32),
                  jnp.zeros((1, bp, GLA_HEADS, GLA_DK, GLA_DV), F32),
                  jnp.zeros((1, bp, GDN_HEADS, GDN_DK, GDN_DV), F32),
                  jnp.zeros((1, bp, CONV_K - 1, 3 * GDN_HEADS * GDN_DK), F32))
    sp = x_prompt.shape[1]
    bs, ss = x_sample.shape[0], x_sample.shape[1]
    groups = [
        dict(x=x_prompt.reshape(bp * sp, d), batch=bp, seq=sp, row_off=0, states=zero_state,
             state_layer=lambda l: 0, mem=[(pk, pv, l) for l in range(depth)]),
        dict(x=x_sample.reshape(bs * ss, d), batch=bs, seq=ss, row_off=bp * sp,
             states=(state_lru_h, state_lru_conv, state_gla, state_gdn, state_gdn_conv),
             state_layer=lambda l: l, mem=[(cache_mem_k, cache_mem_v, l) for l in range(depth)]),
    ]
    (y_p, y_s), (p_st, s_st) = _run_layers(groups, layers, ffn, norm_final)

    return (y_p.reshape(bp, sp, d), y_s.reshape(bs, ss, d), p_st[0], p_st[1], p_st[2], p_st[3], p_st[4],
            p_mem_k, p_mem_v, s_st[0], s_st[1], s_st[2], s_st[3], s_st[4])
```

```python
import functools
import math

import jax
import jax.numpy as jnp
from jax import lax
from jax.experimental import pallas as pl
from jax.experimental.pallas import tpu as pltpu

F32 = jnp.float32
BF16 = jnp.bfloat16
EPS = 1e-6

D_MODEL = 1024
LRU_W = 512
LRU_BLOCKS = 8
LRU_C = 8.0
CONV_K = 4
GLA_HEADS = 4
GLA_DK = 32
GLA_DV = 64
GLA_RANK = 16
GLA_TAU = 16.0
GLA_SUB = 16
GDN_HEADS = 4
GDN_DK = 64
GDN_DV = 64
MIX_CHUNK = 64
MEM_HEADS = 4
MEM_HD = 256
N_EXPERTS = 8
LANES = 128
VMEM_LIMIT_BYTES = 48 * 1024 * 1024
EXPERT_VMEM_LIMIT_BYTES = 56 * 1024 * 1024


def _cparams(n_axes):
    return pltpu.CompilerParams(dimension_semantics=("arbitrary",) * n_axes,
                                vmem_limit_bytes=VMEM_LIMIT_BYTES)


def _mm(a, b):
    return jnp.dot(a.astype(BF16), b.astype(BF16), preferred_element_type=F32)


def _mm_nt(a, b):
    return lax.dot_general(a.astype(BF16), b.astype(BF16), (((1,), (1,)), ((), ())),
                           preferred_element_type=F32)


def _mm_tn(a, b):
    return lax.dot_general(a.astype(BF16), b.astype(BF16), (((0,), (0,)), ((), ())),
                           preferred_element_type=F32)


def _rms_rows(x, g):
    ms = jnp.mean(x * x, axis=-1, keepdims=True)
    return (x * lax.rsqrt(ms + EPS)) * g


def _softplus(x):
    return jnp.maximum(x, 0.0) + jnp.log1p(jnp.exp(-jnp.abs(x)))


def _sigmoid(x):
    return 1.0 / (1.0 + jnp.exp(-x))


def _silu(x):
    return x * _sigmoid(x)


def _gelu_tanh(x):
    c = 0.7978845608028654
    return x * (0.5 * (1.0 + jnp.tanh(c * (x + 0.044715 * (x * x * x)))))


def _seg_cumsum_rows(x, seg):
    rows = x.shape[0]
    tpos = lax.broadcasted_iota(jnp.int32, (rows, 1), 0) & (seg - 1)
    d = 1
    while d < seg:
        x = x + jnp.where(tpos >= d, pltpu.roll(x, d, axis=0), 0.0)
        d *= 2
    return x


def _head_rms(o, gain, n_heads, width):
    lane_head = lax.broadcasted_iota(jnp.int32, (1, n_heads * width), 1) // width
    sq = o * o
    inv = jnp.zeros_like(o)
    for h in range(n_heads):
        m = lane_head == h
        ms = jnp.sum(jnp.where(m, sq, 0.0), axis=-1, keepdims=True) * (1.0 / width)
        inv = jnp.where(m, lax.rsqrt(ms + EPS), inv)
    return (o * inv) * gain


def _stack_heads(x, n_heads, width):
    c = x.shape[0]
    t = jnp.concatenate([x] * n_heads, axis=0)
    row_head = lax.broadcasted_iota(jnp.int32, (n_heads * c, 1), 0) // c
    lane_head = lax.broadcasted_iota(jnp.int32, (1, n_heads * width), 1) // width
    return jnp.where(row_head == lane_head, t, 0.0)


def _unstack_heads(x, n_heads):
    c = x.shape[0] // n_heads
    o = x[0:c]
    for h in range(1, n_heads):
        o = o + x[h * c:(h + 1) * c]
    return o


def _rms_matmul_kernel(x_ref, g_ref, w_ref, *o_refs):
    h = _rms_rows(x_ref[...], g_ref[...]).astype(BF16)
    start = 0
    for o_ref in o_refs:
        n = o_ref.shape[1]
        o_ref[...] = jnp.dot(h, w_ref[:, start:start + n], preferred_element_type=F32)
        start += n


def _row_tile(t, pref):
    tile = min(pref, t)
    while t % tile or tile % 8:
        tile -= 8
    return tile


def _rms_matmul(x, g, w, widths, name, tm=512, row_off=0, n_rows=None):
    d = x.shape[1]
    t = x.shape[0] if n_rows is None else n_rows
    n = w.shape[1]
    tm = _row_tile(t, tm)
    off = row_off // tm
    out_specs = [pl.BlockSpec((tm, wd), lambda i: (i, 0)) for wd in widths]
    out_shape = [jax.ShapeDtypeStruct((t, wd), F32) for wd in widths]
    return pl.pallas_call(
        _rms_matmul_kernel,
        grid=(t // tm,),
        in_specs=[pl.BlockSpec((tm, d), lambda i: (i + off, 0)),
                  pl.BlockSpec((1, d), lambda i: (0, 0)),
                  pl.BlockSpec((d, n), lambda i: (0, 0))],
        out_specs=out_specs,
        out_shape=out_shape,
        compiler_params=_cparams(1),
        name=name,
    )(x, g.reshape(1, d), w)


def _mem_kv_kernel(m_ref, g_ref, w_ref, k_ref, v_ref, k5_ref, v5_ref):
    d = m_ref.shape[1]
    h = _rms_rows(m_ref[...], g_ref[...])
    kv = _mm(h, w_ref[...])
    k_ref[...] = kv[:, :d].astype(k_ref.dtype)
    v_ref[...] = kv[:, d:].astype(v_ref.dtype)
    for hd in range(MEM_HEADS):
        sl = slice(hd * MEM_HD, (hd + 1) * MEM_HD)
        k5_ref[:, hd, :] = kv[:, sl]
        v5_ref[:, hd, :] = kv[:, d + hd * MEM_HD:d + (hd + 1) * MEM_HD]


def _mem_kv(mem, g, w):
    bp, mlen, d = mem.shape
    depth = w.shape[0]
    flat = jax.ShapeDtypeStruct((depth, bp, mlen, d), BF16)
    split = jax.ShapeDtypeStruct((depth, bp, mlen, MEM_HEADS, MEM_HD), F32)
    fidx = lambda l, b: (l, b, 0, 0)
    sidx = lambda l, b: (l, b, 0, 0, 0)
    return pl.pallas_call(
        _mem_kv_kernel,
        grid=(depth, bp),
        in_specs=[pl.BlockSpec((None, mlen, d), lambda l, b: (b, 0, 0)),
                  pl.BlockSpec((None, 1, d), lambda l, b: (l, 0, 0)),
                  pl.BlockSpec((None, d, 2 * d), lambda l, b: (l, 0, 0))],
        out_specs=[pl.BlockSpec((None, None, mlen, d), fidx),
                   pl.BlockSpec((None, None, mlen, d), fidx),
                   pl.BlockSpec((None, None, mlen, MEM_HEADS, MEM_HD), sidx),
                   pl.BlockSpec((None, None, mlen, MEM_HEADS, MEM_HD), sidx)],
        out_shape=[flat, flat, split, split],
        compiler_params=_cparams(2),
        name="mem_kv",
    )(mem, g.reshape(depth, 1, d), w)


def _lru_coeffs(xc, aw_ref, ab_ref, xw_ref, xb_ref, lam_ref):
    r = _sigmoid(_mm(xc, aw_ref[...]) + ab_ref[...])
    ig = _sigmoid(_mm(xc, xw_ref[...]) + xb_ref[...])
    log_a = (-LRU_C * r) * _softplus(-lam_ref[...])
    a = jnp.exp(log_a)
    th = jnp.tanh(log_a)
    return a, jnp.sqrt((-2.0 * th) / (1.0 - th)) * (ig * xc)


def _lru_kernel(z_ref, h0_ref, buf0_ref, cw_ref, cb_ref, aw_ref, ab_ref, xw_ref, xb_ref, lam_ref,
                y_ref, hout_ref, bufout_ref, xs_ref, hc_ref, *, bt, tl, nt):
    w = LRU_W
    rows = bt * tl
    j = pl.program_id(0) % nt

    @pl.when(j == 0)
    def _():
        xs_ref[:, 5:8, :] = buf0_ref[...]
        hc_ref[...] = h0_ref[...]

    xs_ref[:, 8:, :] = z_ref[:, :w].reshape(bt, tl, w)
    gate = z_ref[:, w:]
    cw = cw_ref[...]
    xc = cb_ref[...] + xs_ref[:, 5:5 + tl, :] * cw[0:1]
    for k in range(1, CONV_K):
        xc = xc + xs_ref[:, 5 + k:5 + k + tl, :] * cw[k:k + 1]
    tail = xs_ref[:, 5 + tl:8 + tl, :]
    xs_ref[:, 5:8, :] = tail
    bufout_ref[...] = tail

    a, b = _lru_coeffs(xc.reshape(rows, w), aw_ref, ab_ref, xw_ref, xb_ref, lam_ref)

    sub = 8
    gps = tl // sub
    a3 = a.reshape(rows // sub, sub, w)
    b3 = b.reshape(rows // sub, sub, w)
    spos = lax.broadcasted_iota(jnp.int32, (1, sub, 1), 1)
    d = 1
    while d < sub:
        m = spos >= d
        b3 = jnp.where(m, a3 * pltpu.roll(b3, d, axis=1) + b3, b3)
        a3 = jnp.where(m, a3 * pltpu.roll(a3, d, axis=1), a3)
        d *= 2
    a4 = a3.reshape(bt, gps, sub, w)
    b4 = b3.reshape(bt, gps, sub, w)
    carry = hc_ref[...]
    hs = []
    for r in range(gps):
        hr = b4[:, r] + a4[:, r] * carry
        hs.append(hr)
        carry = hr[:, sub - 1:sub, :]
    h = jnp.stack(hs, axis=1).reshape(rows, w)
    hlast = carry
    hc_ref[...] = hlast
    hout_ref[...] = hlast
    y_ref[...] = h * _gelu_tanh(gate)


def _lru_mixer(z, h0, buf0, layer, cw, cb, aw, ab, xw, xb, lam, batch, seq):
    w = LRU_W
    tl = min(seq, 512)
    bt = min(batch, max(1, 512 // seq))
    nt = seq // tl
    rows = bt * tl
    grid = (batch * seq // rows,)
    if nt > 1:
        sidx = lambda i: (i // nt, 0, 0)
        lidx = lambda i: (layer, i // nt, 0, 0)
    else:
        sidx = lambda i: (i, 0, 0)
        lidx = lambda i: (layer, i, 0, 0)
    full2 = lambda i: (0, 0)
    return pl.pallas_call(
        functools.partial(_lru_kernel, bt=bt, tl=tl, nt=nt),
        grid=grid,
        in_specs=[pl.BlockSpec((rows, 2 * w), lambda i: (i, 0)),
                  pl.BlockSpec((None, bt, 1, w), lidx),
                  pl.BlockSpec((None, bt, CONV_K - 1, w), lidx),
                  pl.BlockSpec((CONV_K, w), full2),
                  pl.BlockSpec((1, w), full2),
                  pl.BlockSpec((w, w), full2),
                  pl.BlockSpec((1, w), full2),
                  pl.BlockSpec((w, w), full2),
                  pl.BlockSpec((1, w), full2),
                  pl.BlockSpec((1, w), full2)],
        out_specs=[pl.BlockSpec((rows, w), lambda i: (i, 0)),
                   pl.BlockSpec((bt, 1, w), sidx),
                   pl.BlockSpec((bt, CONV_K - 1, w), sidx)],
        out_shape=[jax.ShapeDtypeStruct((batch * seq, w), F32),
                   jax.ShapeDtypeStruct((batch, 1, w), F32),
                   jax.ShapeDtypeStruct((batch, CONV_K - 1, w), F32)],
        scratch_shapes=[pltpu.VMEM((bt, 8 + tl, w), F32), pltpu.VMEM((bt, 1, w), F32)],
        compiler_params=_cparams(1),
        name="lru_mixer",
    )(z, h0.reshape(h0.shape[0], batch, 1, w), buf0, cw, cb.reshape(1, w), aw, ab.reshape(1, w), xw,
      xb.reshape(1, w), lam.reshape(1, w))


def _gla_kernel(z_ref, s0_ref, w2_ref, gb_ref, gn_ref, y_ref, sout_ref, s_ref, *, c, sc, g):
    nh = GLA_HEADS
    kw = nh * GLA_DK
    vw = nh * GLA_DV

    @pl.when(pl.program_id(1) == 0)
    def _():
        s_ref[...] = jnp.zeros_like(s_ref)
        for b in range(g):
            for h in range(nh):
                s_ref[b, h * GLA_DK:(h + 1) * GLA_DK, h * GLA_DV:(h + 1) * GLA_DV] = s0_ref[b, h]

    bs = range(g)
    q = [z_ref[b, :, 0:kw] * (GLA_DK ** -0.5) for b in bs]
    k = [z_ref[b, :, kw:2 * kw] for b in bs]
    v = [z_ref[b, :, 2 * kw:2 * kw + vw] for b in bs]
    gk = [-_softplus(-(_mm(z_ref[b, :, 2 * kw + 2 * vw:], w2_ref[...]) + gb_ref[...])) / GLA_TAU
          for b in bs]
    gcum = [_seg_cumsum_rows(gk[b], sc) for b in bs]
    qp = [q[b] * jnp.exp(gcum[b]) for b in bs]

    ri = lax.broadcasted_iota(jnp.int32, (nh * c, nh * c), 0)
    ci = lax.broadcasted_iota(jnp.int32, (nh * c, nh * c), 1)
    keep = (ri // sc == ci // sc) & (ci <= ri)
    a = [_mm_nt(_stack_heads(qp[b], nh, GLA_DK), _stack_heads(k[b] * jnp.exp(-gcum[b]), nh, GLA_DK))
         for b in bs]
    o = [_unstack_heads(_mm(jnp.where(keep, a[b], 0.0), _stack_heads(v[b], nh, GLA_DV)), nh)
         for b in bs]

    s = [s_ref[b] for b in bs]
    eye = (lax.broadcasted_iota(jnp.int32, (kw, kw), 0) ==
           lax.broadcasted_iota(jnp.int32, (kw, kw), 1))
    bd = (lax.broadcasted_iota(jnp.int32, (kw, vw), 0) // GLA_DK ==
          lax.broadcasted_iota(jnp.int32, (kw, vw), 1) // GLA_DV)
    o_inter = [[] for _ in bs]
    for i in range(c // sc):
        lo, hi = i * sc, (i + 1) * sc
        for b in bs:
            o_inter[b].append(_mm(qp[b][lo:hi], s[b]))
            glast = gcum[b][hi - 1:hi]
            kpp = k[b][lo:hi] * jnp.exp(glast - gcum[b][lo:hi])
            u = _mm_tn(kpp, v[b][lo:hi])
            dcol = jnp.sum(jnp.where(eye, jnp.exp(glast), 0.0), axis=1, keepdims=True)
            s[b] = dcol * s[b] + jnp.where(bd, u, 0.0)
    for b in bs:
        s_ref[b] = s[b]
        for h in range(nh):
            sout_ref[b, h] = s[b][h * GLA_DK:(h + 1) * GLA_DK, h * GLA_DV:(h + 1) * GLA_DV]
        ob = o[b] + jnp.concatenate(o_inter[b], axis=0)
        gate = z_ref[b, :, 2 * kw + vw:2 * kw + 2 * vw]
        y_ref[b] = _head_rms(ob, gn_ref[...], nh, GLA_DV) * _silu(gate)


def _gla_mixer(z, s0, layer, w2, gb, gn, batch, seq):
    c = min(seq, MIX_CHUNK)
    sc = min(c, GLA_SUB)
    nt = seq // c
    g = _mixer_group(batch, c)
    kw, vw = GLA_HEADS * GLA_DK, GLA_HEADS * GLA_DV
    zw = z.shape[2]
    full2 = lambda i, j: (0, 0)
    sblk = (g, GLA_HEADS, GLA_DK, GLA_DV)
    return pl.pallas_call(
        functools.partial(_gla_kernel, c=c, sc=sc, g=g),
        grid=(batch // g, nt),
        in_specs=[pl.BlockSpec((g, c, zw), lambda i, j: (i, j, 0)),
                  pl.BlockSpec((None,) + sblk, lambda i, j: (layer, i, 0, 0, 0)),
                  pl.BlockSpec((LANES, kw), full2),
                  pl.BlockSpec((1, kw), full2),
                  pl.BlockSpec((1, vw), full2)],
        out_specs=[pl.BlockSpec((g, c, vw), lambda i, j: (i, j, 0)),
                   pl.BlockSpec(sblk, lambda i, j: (i, 0, 0, 0))],
        out_shape=[jax.ShapeDtypeStruct((batch, seq, vw), F32),
                   jax.ShapeDtypeStruct((batch, GLA_HEADS, GLA_DK, GLA_DV), F32)],
        scratch_shapes=[pltpu.VMEM((g, kw, vw), F32)],
        compiler_params=_cparams(2),
        name="gla_mixer",
    )(z, s0, w2, gb.reshape(1, kw), gn.reshape(1, vw))


def _gdn_prep(z_ref, cw_ref, alog_ref, dtb_ref, bufout_ref, xs_ref, *, c):
    nh = GDN_HEADS
    hw = nh * GDN_DK
    cw3 = 3 * hw

    xs_ref[8:, :] = z_ref[:, 0:cw3]
    cw = cw_ref[...]
    qkv = xs_ref[5:5 + c, :] * cw[0:1]
    for kk in range(1, CONV_K):
        qkv = qkv + xs_ref[5 + kk:5 + kk + c, :] * cw[kk:kk + 1]
    tail = xs_ref[5 + c:8 + c, :]
    xs_ref[5:8, :] = tail
    bufout_ref[...] = tail
    qkv = _silu(qkv)
    zg = z_ref[:, cw3:cw3 + hw]
    sm = z_ref[:, cw3 + hw:]

    lane_head = lax.broadcasted_iota(jnp.int32, (1, hw), 1) // GDN_DK

    def l2n(x):
        sq = x * x
        inv = jnp.zeros_like(x)
        for h in range(nh):
            m = lane_head == h
            ss = jnp.sum(jnp.where(m, sq, 0.0), axis=-1, keepdims=True)
            inv = jnp.where(m, lax.rsqrt(ss + EPS), inv)
        return x * inv

    q = l2n(qkv[:, 0:hw]) * (GDN_DK ** -0.5)
    k = l2n(qkv[:, hw:2 * hw])
    v = qkv[:, 2 * hw:3 * hw]
    beta = _sigmoid(sm)
    glog = -jnp.exp(alog_ref[...]) * _softplus(sm + dtb_ref[...])
    gcum = _seg_cumsum_rows(glog, c)

    n = nh * c
    bcol = jnp.concatenate([beta[:, h:h + 1] for h in range(nh)], axis=0)
    gcol = jnp.concatenate([gcum[:, nh + h:nh + h + 1] for h in range(nh)], axis=0)
    glast = jnp.concatenate(
        [jnp.broadcast_to(gcum[c - 1:c, nh + h:nh + h + 1], (c, 1)) for h in range(nh)], axis=0)
    ri = lax.broadcasted_iota(jnp.int32, (n, n), 0)
    ci = lax.broadcasted_iota(jnp.int32, (n, n), 1)
    grow = jnp.sum(jnp.where(ri == ci, gcol, 0.0), axis=0, keepdims=True)
    incl = (ri // c == ci // c) & (ci <= ri)
    dec = jnp.where(incl, jnp.exp(jnp.where(incl, gcol - grow, 0.0)), 0.0)

    sdec = jnp.concatenate(
        [jnp.broadcast_to(jnp.exp(gcum[c - 1:c, nh + h:nh + h + 1]), (GDN_DK, 1)) for h in range(nh)],
        axis=0)
    return dict(ks=_stack_heads(k, nh, GDN_DK), qs=_stack_heads(q, nh, GDN_DK),
                vs=_stack_heads(v, nh, GDN_DV), bcol=bcol, egc=jnp.exp(gcol), dec=dec,
                kdec=jnp.exp(glast - gcol), sdec=sdec, zg=zg)


def _gdn_kernel(z_ref, s0_ref, buf0_ref, cw_ref, alog_ref, dtb_ref, gn_ref,
                y_ref, sout_ref, bufout_ref, xs_ref, s_ref, *, c, g):
    nh = GDN_HEADS
    hw = nh * GDN_DK
    n = nh * c

    @pl.when(pl.program_id(1) == 0)
    def _():
        xs_ref[:, 5:8, :] = buf0_ref[...]
        s_ref[...] = jnp.zeros_like(s_ref)
        for b in range(g):
            for h in range(nh):
                s_ref[b, h * GDN_DK:(h + 1) * GDN_DK, h * GDN_DV:(h + 1) * GDN_DV] = s0_ref[b, h]

    bs = range(g)
    pr = [_gdn_prep(z_ref.at[b], cw_ref, alog_ref, dtb_ref, bufout_ref.at[b], xs_ref.at[b], c=c)
          for b in bs]
    ri = lax.broadcasted_iota(jnp.int32, (n, n), 0)
    ci = lax.broadcasted_iota(jnp.int32, (n, n), 1)
    same = ri // c == ci // c
    incl = same & (ci <= ri)
    strict = same & (ci < ri)
    kq = [_mm_nt(jnp.concatenate([pr[b]["ks"], pr[b]["qs"]], axis=0), pr[b]["ks"]) for b in bs]

    p = [jnp.where(strict, -(pr[b]["bcol"] * kq[b][0:n]) * pr[b]["dec"], 0.0) for b in bs]
    tinv = [jnp.where(ri == ci, 1.0, 0.0) + p[b] for b in bs]
    span = 2
    while span < c:
        p = [_mm(p[b], p[b]) for b in bs]
        tinv = [tinv[b] + _mm(tinv[b], p[b]) for b in bs]
        span *= 2

    s = [s_ref[b] for b in bs]
    uw = [_mm(tinv[b], jnp.concatenate([pr[b]["vs"] * pr[b]["bcol"],
                                        pr[b]["ks"] * (pr[b]["bcol"] * pr[b]["egc"])], axis=1))
          for b in bs]
    qw = [_mm(jnp.concatenate([pr[b]["qs"] * pr[b]["egc"], uw[b][:, hw:]], axis=0), s[b])
          for b in bs]
    vnew = [uw[b][:, 0:hw] - qw[b][n:] for b in bs]
    av = [_mm(jnp.where(incl, kq[b][n:] * pr[b]["dec"], 0.0), vnew[b]) for b in bs]
    kv = [_mm_tn(pr[b]["ks"] * pr[b]["kdec"], vnew[b]) for b in bs]
    for b in bs:
        o = _unstack_heads(qw[b][0:n] + av[b], nh)
        sn = pr[b]["sdec"] * s[b] + kv[b]
        s_ref[b] = sn
        for h in range(nh):
            sout_ref[b, h] = sn[h * GDN_DK:(h + 1) * GDN_DK, h * GDN_DV:(h + 1) * GDN_DV]
        y_ref[b] = _head_rms(o, gn_ref[...], nh, GDN_DV) * _silu(pr[b]["zg"])


def _mixer_group(batch, c):
    return min(batch, max(8, 256 // c))


def _gdn_mixer(z, s0, buf0, layer, cw, alog, dtb, gn, batch, seq):
    c = min(seq, MIX_CHUNK)
    nt = seq // c
    g = _mixer_group(batch, c)
    hw = GDN_HEADS * GDN_DK
    zw = z.shape[2]
    full2 = lambda i, j: (0, 0)
    sblk = (g, GDN_HEADS, GDN_DK, GDN_DV)
    return pl.pallas_call(
        functools.partial(_gdn_kernel, c=c, g=g),
        grid=(batch // g, nt),
        in_specs=[pl.BlockSpec((g, c, zw), lambda i, j: (i, j, 0)),
                  pl.BlockSpec((None,) + sblk, lambda i, j: (layer, i, 0, 0, 0)),
                  pl.BlockSpec((None, g, CONV_K - 1, 3 * hw), lambda i, j: (layer, i, 0, 0)),
                  pl.BlockSpec((CONV_K, 3 * hw), full2),
                  pl.BlockSpec((1, LANES), full2),
                  pl.BlockSpec((1, LANES), full2),
                  pl.BlockSpec((1, hw), full2)],
        out_specs=[pl.BlockSpec((g, c, hw), lambda i, j: (i, j, 0)),
                   pl.BlockSpec(sblk, lambda i, j: (i, 0, 0, 0)),
                   pl.BlockSpec((g, CONV_K - 1, 3 * hw), lambda i, j: (i, 0, 0))],
        out_shape=[jax.ShapeDtypeStruct((batch, seq, hw), F32),
                   jax.ShapeDtypeStruct((batch, GDN_HEADS, GDN_DK, GDN_DV), F32),
                   jax.ShapeDtypeStruct((batch, CONV_K - 1, 3 * hw), F32)],
        scratch_shapes=[pltpu.VMEM((g, 8 + c, 3 * hw), F32), pltpu.VMEM((g, hw, hw), F32)],
        compiler_params=_cparams(2),
        name="gdn_mixer",
    )(z, s0, buf0, cw, alog, dtb, gn.reshape(1, hw))


def _outproj_kernel(x_ref, yl_ref, yg_ref, yd_ref, wo_ref, gq_ref, wq_ref, xn_ref, q_ref):
    lw = LRU_W
    gw = GLA_HEADS * GLA_DV
    y = _mm(yl_ref[...], wo_ref[0:lw, :])
    y = y + _mm(yg_ref[...], wo_ref[lw:lw + gw, :])
    y = y + _mm(yd_ref[...], wo_ref[lw + gw:, :])
    xn = x_ref[...] + y
    xn_ref[...] = xn
    q_ref[...] = _mm(_rms_rows(xn, gq_ref[...]), wq_ref[...])


def _outproj(x, yl, yg, yd, wo, gq, wq, tm=1024, row_off=0):
    d = x.shape[1]
    t = yl.shape[0]
    tm = _row_tile(t, tm)
    off = row_off // tm
    row = lambda i: (i, 0)
    full2 = lambda i: (0, 0)
    return pl.pallas_call(
        _outproj_kernel,
        grid=(t // tm,),
        in_specs=[pl.BlockSpec((tm, d), lambda i: (i + off, 0)),
                  pl.BlockSpec((tm, yl.shape[1]), row),
                  pl.BlockSpec((tm, yg.shape[1]), row),
                  pl.BlockSpec((tm, yd.shape[1]), row),
                  pl.BlockSpec((d, d), full2),
                  pl.BlockSpec((1, d), full2),
                  pl.BlockSpec((d, d), full2)],
        out_specs=[pl.BlockSpec((tm, d), row), pl.BlockSpec((tm, d), row)],
        out_shape=[jax.ShapeDtypeStruct((t, d), F32), jax.ShapeDtypeStruct((t, d), F32)],
        compiler_params=_cparams(1),
        name="outproj_qproj",
    )(x, yl, yg, yd, wo, gq.reshape(1, d), wq)


def _attn_kernel(x_ref, q_ref, k_ref, v_ref, wo_ref, o_ref):
    hd = MEM_HD
    acc = x_ref[...]
    for h in range(MEM_HEADS):
        sl = slice(h * hd, (h + 1) * hd)
        s = _mm_nt(q_ref[:, sl], k_ref[:, sl]) * (hd ** -0.5)
        m = jnp.max(s, axis=-1, keepdims=True)
        p = jnp.exp(s - m)
        l = jnp.sum(p, axis=-1, keepdims=True)
        oh = _mm(p, v_ref[:, sl]) / l
        acc = acc + _mm(oh, wo_ref[sl, :])
    o_ref[...] = acc


def _attn_heads_kernel(q_ref, k_ref, v_ref, o_ref, *, gb, tl):
    nh, hd = MEM_HEADS, MEM_HD
    m = k_ref.shape[1]
    row_head = lax.broadcasted_iota(jnp.int32, (nh * tl, 1), 0) // tl
    col_head = lax.broadcasted_iota(jnp.int32, (1, m * nh), 1) % nh
    for b in range(gb):
        kf = k_ref[b].reshape(m * nh, hd)
        vf = v_ref[b].reshape(m * nh, hd)
        rows = slice(b * tl, (b + 1) * tl)
        qs = jnp.concatenate([q_ref[rows, h * hd:(h + 1) * hd] for h in range(nh)], axis=0)
        s = _mm_nt(qs, kf) * (hd ** -0.5)
        s = jnp.where(row_head == col_head, s, -jnp.inf)
        mx = jnp.max(s, axis=-1, keepdims=True)
        p = jnp.exp(s - mx)
        l = jnp.sum(p, axis=-1, keepdims=True)
        o = _mm(p, vf) / l
        for h in range(nh):
            o_ref[rows, h * hd:(h + 1) * hd] = o[h * tl:(h + 1) * tl]


def _oproj_kernel(x_ref, a_ref, wo_ref, *rest):
    o_ref = rest[-1]
    o_ref[...] = x_ref[...] + _mm(a_ref[...], wo_ref[...])


def _attn_first_kernel(x_ref, q_ref, k_ref, v_ref, wo_ref, o_ref, *, n_real):
    i = pl.program_id(0)

    @pl.when(i < n_real)
    def _():
        _attn_kernel(x_ref, q_ref, k_ref, v_ref, wo_ref, o_ref)

    @pl.when(i >= n_real)
    def _():
        o_ref[...] = jnp.zeros_like(o_ref)


def _attn_into_kernel(x_ref, q_ref, k_ref, v_ref, wo_ref, joint_ref, o_ref):
    _attn_kernel(x_ref, q_ref, k_ref, v_ref, wo_ref, o_ref)


def _attention(x, q, k, v, layer, wo, batch, seq, joint, row_off, total_rows):
    t, d = x.shape
    out_shape = jax.ShapeDtypeStruct((total_rows, d), F32)
    extra_specs = [] if joint is None else [pl.BlockSpec(memory_space=pl.ANY)]
    extra_args = () if joint is None else (joint,)
    if k.ndim == 4:
        tl = min(seq, 1024)
        nl = seq // tl
        off = row_off // tl
        kblk = (None, None) + k.shape[2:]
        n_real = batch * nl
        if joint is None:
            assert row_off == 0
            n_steps = pl.cdiv(total_rows, tl)
            body = functools.partial(_attn_first_kernel, n_real=n_real)
        else:
            n_steps = n_real
            body = _attn_into_kernel
        kidx = lambda i: (layer, jnp.minimum(i // nl, batch - 1), 0, 0)
        row = lambda i: (jnp.minimum(i, n_real - 1), 0)
        return pl.pallas_call(
            body,
            grid=(n_steps,),
            in_specs=[pl.BlockSpec((tl, d), row),
                      pl.BlockSpec((tl, d), row),
                      pl.BlockSpec(kblk, kidx),
                      pl.BlockSpec(kblk, kidx),
                      pl.BlockSpec((d, d), lambda i: (0, 0))] + extra_specs,
            out_specs=pl.BlockSpec((tl, d), lambda i: (i + off, 0)),
            out_shape=out_shape,
            input_output_aliases={} if joint is None else {5: 0},
            compiler_params=_cparams(1),
            name="mem_attention",
        )(x, q, k, v, wo, *extra_args)
    gb = 4 if batch % 4 == 0 else 2
    kblk = (None, gb) + k.shape[2:]
    kidx = lambda i: (layer, i, 0, 0, 0)
    att = pl.pallas_call(
        functools.partial(_attn_heads_kernel, gb=gb, tl=seq),
        grid=(batch // gb,),
        in_specs=[pl.BlockSpec((gb * seq, d), lambda i: (i, 0)),
                  pl.BlockSpec(kblk, kidx),
                  pl.BlockSpec(kblk, kidx)],
        out_specs=pl.BlockSpec((gb * seq, d), lambda i: (i, 0)),
        out_shape=jax.ShapeDtypeStruct((t, d), F32),
        compiler_params=_cparams(1),
        name="mem_attention_heads",
    )(q, k, v)
    tm = _row_tile(t, 512)
    off = row_off // tm
    return pl.pallas_call(
        _oproj_kernel,
        grid=(t // tm,),
        in_specs=[pl.BlockSpec((tm, d), lambda i: (i, 0)),
                  pl.BlockSpec((tm, d), lambda i: (i, 0)),
                  pl.BlockSpec((d, d), lambda i: (0, 0))] + extra_specs,
        out_specs=pl.BlockSpec((tm, d), lambda i: (i + off, 0)),
        out_shape=out_shape,
        input_output_aliases={} if joint is None else {3: 0},
        compiler_params=_cparams(1),
        name="mem_oproj",
    )(x, att, wo, *extra_args)


def _ffn_kernel(x_ref, g_ref, wg_ref, wu_ref, wd_ref, o_ref, h_ref):
    f = pl.program_id(1)

    @pl.when(f == 0)
    def _():
        h_ref[...] = _rms_rows(x_ref[...], g_ref[...]).astype(BF16)

    def chunk():
        h = h_ref[...]
        a = _mm(h, wg_ref[...])
        u = _mm(h, wu_ref[...])
        return _mm(_silu(a) * u, wd_ref[...])

    @pl.when(f == 0)
    def _():
        o_ref[...] = x_ref[...] + chunk()

    @pl.when(f > 0)
    def _():
        o_ref[...] += chunk()


def _ffn_dense(x, g, wg, wu, wd, tm=1024, tf=512):
    t, d = x.shape
    dff = wg.shape[1]
    tm = _row_tile(t, tm)
    return pl.pallas_call(
        _ffn_kernel,
        grid=(t // tm, dff // tf),
        in_specs=[pl.BlockSpec((tm, d), lambda i, f: (i, 0)),
                  pl.BlockSpec((1, d), lambda i, f: (0, 0)),
                  pl.BlockSpec((d, tf), lambda i, f: (0, f)),
                  pl.BlockSpec((d, tf), lambda i, f: (0, f)),
                  pl.BlockSpec((tf, d), lambda i, f: (f, 0))],
        out_specs=pl.BlockSpec((tm, d), lambda i, f: (i, 0)),
        out_shape=jax.ShapeDtypeStruct((t, d), F32),
        scratch_shapes=[pltpu.VMEM((tm, d), BF16)],
        compiler_params=_cparams(2),
        name="ffn_dense",
    )(x, g.reshape(1, d), wg, wu, wd)


def _router_kernel(x_ref, g_ref, whi_ref, wlo_ref, tri_ref, info_ref, cnt_ref, carry_ref, *, tm):
    i = pl.program_id(0)

    @pl.when(i == 0)
    def _():
        carry_ref[...] = jnp.zeros_like(carry_ref)

    h = _rms_rows(x_ref[...], g_ref[...])
    hhi = h.astype(BF16)
    hlo = (h - hhi.astype(F32)).astype(BF16)
    whi = whi_ref[...]
    logits = (jnp.dot(hhi, whi, preferred_element_type=F32)
              + jnp.dot(hlo, whi, preferred_element_type=F32)
              + jnp.dot(hhi, wlo_ref[...], preferred_element_type=F32))
    lane = lax.broadcasted_iota(jnp.int32, (tm, LANES), 1)
    neg = jnp.float32(-jnp.inf)
    logits = jnp.where(lane < N_EXPERTS, logits, neg)
    m1 = jnp.max(logits, axis=-1, keepdims=True)
    i1 = jnp.min(jnp.where(logits == m1, lane, LANES), axis=-1, keepdims=True)
    rest = jnp.where(lane == i1, neg, logits)
    m2 = jnp.max(rest, axis=-1, keepdims=True)
    i2 = jnp.min(jnp.where(rest == m2, lane, LANES), axis=-1, keepdims=True)
    e = jnp.exp(m2 - m1)
    g1 = 1.0 / (1.0 + e)
    g2 = e / (1.0 + e)
    oh1 = jnp.where(lane == i1, 1.0, 0.0)
    oh2 = jnp.where(lane == i2, 1.0, 0.0)
    oh = oh1 + oh2
    before = _mm(tri_ref[...], oh) + carry_ref[0:1, :]
    r1 = jnp.sum(oh1 * before, axis=-1, keepdims=True)
    r2 = jnp.sum(oh2 * before, axis=-1, keepdims=True)
    carry = carry_ref[0:1, :] + jnp.sum(oh, axis=0, keepdims=True)
    carry_ref[...] = jnp.broadcast_to(carry, carry_ref.shape)
    cnt_ref[...] = jnp.broadcast_to(carry, cnt_ref.shape)
    info = jnp.where(lane == 0, i1.astype(F32), 0.0)
    info = jnp.where(lane == 1, i2.astype(F32), info)
    info = jnp.where(lane == 2, r1, info)
    info = jnp.where(lane == 3, r2, info)
    info = jnp.where(lane == 4, g1, info)
    info = jnp.where(lane == 5, g2, info)
    info_ref[...] = info


def _router(x, g, whi, wlo, tm):
    t, d = x.shape
    tri = jnp.tril(jnp.ones((tm, tm), BF16), -1)
    return pl.pallas_call(
        functools.partial(_router_kernel, tm=tm),
        grid=(t // tm,),
        in_specs=[pl.BlockSpec((tm, d), lambda i: (i, 0)),
                  pl.BlockSpec((1, d), lambda i: (0, 0)),
                  pl.BlockSpec((d, LANES), lambda i: (0, 0)),
                  pl.BlockSpec((d, LANES), lambda i: (0, 0)),
                  pl.BlockSpec((tm, tm), lambda i: (0, 0))],
        out_specs=[pl.BlockSpec((tm, LANES), lambda i: (i, 0)),
                   pl.BlockSpec((8, LANES), lambda i: (i, 0))],
        out_shape=[jax.ShapeDtypeStruct((t, LANES), F32),
                   jax.ShapeDtypeStruct((t // tm * 8, LANES), F32)],
        scratch_shapes=[pltpu.VMEM((8, LANES), F32)],
        compiler_params=_cparams(1),
        name="moe_router",
    )(x, g.reshape(1, d), whi, wlo, tri)


MOE_CHUNK = 32


def _dispatch_kernel(st_ref, nc_ref, pos_ref, zs_ref, zf_ref, x_ref, g_ref, ct_ref, xs_hbm, w_ref, zbuf_ref,
                     zsem, sem, *, rows, sub, n_blk, win):
    i = pl.program_id(0)

    @pl.when(i == 0)
    def _():
        zbuf_ref[...] = jnp.zeros_like(zbuf_ref)
        for e in range(N_EXPERTS):
            dst = xs_hbm.at[pl.ds(pl.multiple_of(zs_ref[e], 8), sub)]
            pltpu.make_async_copy(zbuf_ref, dst, zsem).start()
            pltpu.make_async_copy(zbuf_ref, dst, zsem).wait()

        def fill(j, carry):
            @pl.when(zf_ref[j] == 1)
            def _():
                dst = xs_hbm.at[pl.ds(pl.multiple_of(j * sub, sub), sub)]
                pltpu.make_async_copy(zbuf_ref, dst, zsem).start()
            return carry

        def drain(j, carry):
            @pl.when(zf_ref[j] == 1)
            def _():
                dst = xs_hbm.at[pl.ds(pl.multiple_of(j * sub, sub), sub)]
                pltpu.make_async_copy(zbuf_ref, dst, zsem).wait()
            return carry

        lax.fori_loop(0, n_blk, fill, 0)
        lax.fori_loop(0, n_blk, drain, 0)

    ck = MOE_CHUNK
    n_steps = pl.num_programs(0)

    def chunk_copy(step, e, c):
        t = step * N_EXPERTS + e
        src = w_ref.at[step % 2, pl.ds(pl.multiple_of(pos_ref[t] + c * ck, ck), ck)]
        dst = xs_hbm.at[pl.ds(pl.multiple_of(st_ref[t] + c * ck, 8), ck)]
        return pltpu.make_async_copy(src, dst, sem)

    def all_chunks(step, fn):
        for e in range(N_EXPERTS):
            def body(c, carry, e=e):
                fn(chunk_copy(step, e, c))
                return carry
            lax.fori_loop(0, nc_ref[step * N_EXPERTS + e], body, 0)

    h = _rms_rows(x_ref[...], g_ref[...])
    wrow = lax.broadcasted_iota(jnp.int32, (win, rows), 0)
    cols = ct_ref[...]
    onehot = jnp.where(wrow == cols[0:1, :], 1.0, 0.0) + jnp.where(wrow == cols[1:2, :], 1.0, 0.0)
    w_ref[i % 2] = _mm(onehot, h)

    @pl.when(i >= 1)
    def _():
        all_chunks(i - 1, lambda cp: cp.wait())

    all_chunks(i, lambda cp: cp.start())

    @pl.when(i == n_steps - 1)
    def _():
        all_chunks(i, lambda cp: cp.wait())


def _dispatch(seg_start, n_chunks, win_pos, zero_start, zero_blk, cols_t, x, g, sub, rows, win):
    t, d = x.shape
    n_blk = zero_blk.shape[0]
    n_rows = n_blk * sub
    return pl.pallas_call(
        functools.partial(_dispatch_kernel, rows=rows, sub=sub, n_blk=n_blk, win=win),
        grid_spec=pltpu.PrefetchScalarGridSpec(
            num_scalar_prefetch=5,
            grid=(t // rows,),
            in_specs=[pl.BlockSpec((rows, d), lambda i, *_: (i, 0)),
                      pl.BlockSpec((1, d), lambda i, *_: (0, 0)),
                      pl.BlockSpec((2, rows), lambda i, *_: (0, i))],
            out_specs=pl.BlockSpec(memory_space=pl.ANY),
            scratch_shapes=[pltpu.VMEM((2, win, d), F32), pltpu.VMEM((sub, d), F32),
                            pltpu.SemaphoreType.DMA(()), pltpu.SemaphoreType.DMA(())]),
        out_shape=jax.ShapeDtypeStruct((n_rows, d), F32),
        compiler_params=_cparams(1),
        name="moe_dispatch",
    )(seg_start, n_chunks, win_pos, zero_start, zero_blk, x, g.reshape(1, d), cols_t)


def _expert_kernel(be_ref, ns_ref, xi_ref, x_ref, wg_ref, wu_ref, wd_ref, o_ref, *, sub, n_sub):
    i = pl.program_id(0)
    f = pl.program_id(1)
    ns = ns_ref[i]

    for s in range(n_sub):
        @pl.when((f == 0) & (s >= ns))
        def _():
            o_ref[s * sub:(s + 1) * sub, :] = jnp.zeros((sub, o_ref.shape[1]), o_ref.dtype)

    def swiglu(rows, first):
        h = x_ref[rows, :]
        a = _mm(h, wg_ref[...])
        u = _mm(h, wu_ref[...])
        y = _mm(_silu(a) * u, wd_ref[...])
        if first:
            o_ref[rows, :] = y
        else:
            o_ref[rows, :] += y

    for first in (True, False):
        is_first = (f == 0) if first else (f > 0)
        for s in range(0, n_sub, 2):
            if s + 2 <= n_sub:
                @pl.when((s + 2 <= ns) & is_first)
                def _():
                    swiglu(slice(s * sub, (s + 2) * sub), first)

            @pl.when((s + 1 == ns) & is_first)
            def _():
                swiglu(slice(s * sub, (s + 1) * sub), first)


def _expert_ffn(blk_exp, n_valid_sub, x_blk, xs, wg, wu, wd, sb, sub, tf=512):
    p, d = xs.shape
    dff = wg.shape[2]
    n_super = n_valid_sub.shape[0]
    nf = dff // tf

    def fidx(i, f, ns):
        used = jnp.minimum(ns[i], 1)
        return f * used + (nf - 1) * (1 - used)

    return pl.pallas_call(
        functools.partial(_expert_kernel, sub=sub, n_sub=sb // sub),
        grid_spec=pltpu.PrefetchScalarGridSpec(
            num_scalar_prefetch=3,
            grid=(n_super, nf),
            in_specs=[pl.BlockSpec((sb, d), lambda i, f, be, ns, xi: (xi[i], 0)),
                      pl.BlockSpec((None, d, tf), lambda i, f, be, ns, xi: (be[i], 0, fidx(i, f, ns))),
                      pl.BlockSpec((None, d, tf), lambda i, f, be, ns, xi: (be[i], 0, fidx(i, f, ns))),
                      pl.BlockSpec((None, tf, d), lambda i, f, be, ns, xi: (be[i], fidx(i, f, ns), 0))],
            out_specs=pl.BlockSpec((sb, d), lambda i, f, be, ns, xi: (i, 0))),
        out_shape=jax.ShapeDtypeStruct((n_super * sb, d), F32),
        compiler_params=pltpu.CompilerParams(dimension_semantics=("arbitrary", "arbitrary"),
                                             vmem_limit_bytes=EXPERT_VMEM_LIMIT_BYTES),
        name="moe_experts",
    )(blk_exp, n_valid_sub, x_blk, xs, wg, wu, wd)


def _combine_kernel(st_ref, nc_ref, pos_ref, x_ref, info_ref, col_ref, ys_hbm, gf_ref, o_ref, buf_ref, sem,
                    *, rows, final_norm, blk_off, win):
    ck = MOE_CHUNK
    i = pl.program_id(0)

    def chunk_copy(step, e, c):
        t = (step + blk_off) * N_EXPERTS + e
        src = ys_hbm.at[pl.ds(pl.multiple_of(st_ref[t] + c * ck, 8), ck)]
        dst = buf_ref.at[step % 2, pl.ds(pl.multiple_of(pos_ref[t] + c * ck, ck), ck)]
        return pltpu.make_async_copy(src, dst, sem.at[step % 2])

    def all_chunks(step, fn):
        for e in range(N_EXPERTS):
            def body(c, carry, e=e):
                fn(chunk_copy(step, e, c))
                return carry
            lax.fori_loop(0, nc_ref[(step + blk_off) * N_EXPERTS + e], body, 0)

    @pl.when(i == 0)
    def _():
        buf_ref[...] = jnp.zeros_like(buf_ref)
        all_chunks(i, lambda cp: cp.start())

    @pl.when(i + 1 < pl.num_programs(0))
    def _():
        all_chunks(i + 1, lambda cp: cp.start())

    info = info_ref[...]
    cols = col_ref[...]
    lane = lax.broadcasted_iota(jnp.int32, (rows, win), 1)
    p = (jnp.where(cols[:, 0:1] == lane, info[:, 4:5], 0.0)
         + jnp.where(cols[:, 1:2] == lane, info[:, 5:6], 0.0)).astype(BF16)
    all_chunks(i, lambda cp: cp.wait())
    acc = x_ref[...] + _mm(p, buf_ref[i % 2])
    if final_norm:
        acc = _rms_rows(acc, gf_ref[...])
    o_ref[...] = acc


def _combine(seg_start, n_chunks, win_pos, cols, x, info, ys, gf, final_norm, row_off, n_rows, rows, win):
    d = x.shape[1]
    off = row_off // rows
    return pl.pallas_call(
        functools.partial(_combine_kernel, rows=rows, final_norm=final_norm, blk_off=off, win=win),
        grid_spec=pltpu.PrefetchScalarGridSpec(
            num_scalar_prefetch=3,
            grid=(n_rows // rows,),
            in_specs=[pl.BlockSpec((rows, d), lambda i, *_: (i + off, 0)),
                      pl.BlockSpec((rows, LANES), lambda i, *_: (i + off, 0)),
                      pl.BlockSpec((rows, 2), lambda i, *_: (i + off, 0)),
                      pl.BlockSpec(memory_space=pl.ANY),
                      pl.BlockSpec((1, d), lambda i, *_: (0, 0))],
            out_specs=pl.BlockSpec((rows, d), lambda i, *_: (i, 0)),
            scratch_shapes=[pltpu.VMEM((2, win, d), F32), pltpu.SemaphoreType.DMA((2,))]),
        out_shape=jax.ShapeDtypeStruct((n_rows, d), F32),
        compiler_params=_cparams(1),
        name="moe_combine",
    )(seg_start, n_chunks, win_pos, x, info, cols, ys, gf.reshape(1, d))


def _moe_ffn(x, g, whi, wlo, wg, wu, wd, gf, final_norm, groups, sub=512):
    t, d = x.shape
    sb = 2048 if 2 * t >= 16 * 1024 else sub
    tb = t
    for off, n in groups:
        tb = math.gcd(tb, math.gcd(off, n))
    tb = _row_tile(tb, 512)
    info, cnt = _router(x, g, whi, wlo, tb)
    e = info[:, 0:2].astype(jnp.int32)
    rank = info[:, 2:4].astype(jnp.int32)
    after = cnt[::8, :N_EXPERTS].astype(jnp.int32)
    before = jnp.concatenate([jnp.zeros((1, N_EXPERTS), jnp.int32), after[:-1]], axis=0)
    n_blocks = t // tb
    ck = MOE_CHUNK
    n_seg = after - before
    seg_len = (n_seg + 7) // 8 * 8
    seg_rel = jnp.cumsum(seg_len, axis=0) - seg_len
    counts = jnp.sum(seg_len, axis=0)
    n_sb = (counts + ck + sb - 1) // sb
    sb_end = jnp.cumsum(n_sb)
    sb_start = sb_end - n_sb
    row_start = sb_start * sb
    seg_start = row_start[None, :] + seg_rel
    n_chunks = (n_seg + ck - 1) // ck
    win_pos = (jnp.cumsum(n_chunks, axis=1) - n_chunks) * ck
    win = (2 * tb + N_EXPERTS * (ck - 1) + LANES - 1) // LANES * LANES
    eid = jnp.arange(N_EXPERTS, dtype=jnp.int32)
    tok_origin = jnp.repeat(win_pos - before, tb, axis=0)
    cols = rank + jnp.sum(jnp.where(e[:, :, None] == eid, tok_origin[:, None, :], 0), axis=-1)
    cols = cols.astype(jnp.int32)
    n_super = (2 * t + n_blocks * N_EXPERTS * 7) // sb + N_EXPERTS + 2
    blk = jnp.arange(n_super, dtype=jnp.int32)
    n_used = sb_end[-1]
    used = blk < n_used
    blk_c = jnp.minimum(blk, n_used - 1)
    be = jnp.minimum(jnp.sum((blk_c[:, None] >= sb_end[None, :]).astype(jnp.int32), axis=-1),
                     N_EXPERTS - 1)
    valid = jnp.clip(counts[be] - (blk_c - sb_start[be]) * sb, 0, sb)
    n_valid_sub = jnp.where(used, (valid + sub - 1) // sub, 0).astype(jnp.int32)
    zero_start = (row_start + counts).astype(jnp.int32)
    per = sb // sub
    sub_in_blk = jnp.arange(per, dtype=jnp.int32)
    zero_blk = (sub_in_blk[None, :] >= n_valid_sub[:, None]).astype(jnp.int32).reshape(-1)
    zero_blk = jnp.concatenate([zero_blk, jnp.ones((1,), jnp.int32)])
    tables = [a.reshape(-1).astype(jnp.int32) for a in (seg_start, n_chunks, win_pos)]
    xs = _dispatch(*tables, zero_start, zero_blk, cols.T, x, g, sub, tb, win)
    ys = _expert_ffn(be.astype(jnp.int32), n_valid_sub, blk_c.astype(jnp.int32), xs, wg, wu, wd, sb, sub)
    return [_combine(*tables, cols, x, info, ys, gf, final_norm, off, n, tb, win) for off, n in groups]


def _final_norm_kernel(x_ref, g_ref, o_ref):
    o_ref[...] = _rms_rows(x_ref[...], g_ref[...])


def _final_norm(x, g, row_off, n_rows, tm=512):
    d = x.shape[1]
    t = n_rows
    tm = _row_tile(t, tm)
    off = row_off // tm
    return pl.pallas_call(
        _final_norm_kernel,
        grid=(t // tm,),
        in_specs=[pl.BlockSpec((tm, d), lambda i: (i + off, 0)), pl.BlockSpec((1, d), lambda i: (0, 0))],
        out_specs=pl.BlockSpec((tm, d), lambda i: (i, 0)),
        out_shape=jax.ShapeDtypeStruct((t, d), F32),
        compiler_params=_cparams(1),
        name="final_norm",
    )(x, g.reshape(1, d))


def _block_diag(w):
    n, c, d = w.shape
    eye = jnp.eye(n, dtype=w.dtype)
    return jnp.einsum("ncd,nm->ncmd", w, eye).reshape(n * c, n * d)


def _pad_cols(w, n):
    return jnp.pad(w, ((0, 0), (0, n - w.shape[1])))


def _layer_params(l, p):
    (norm_mix, w_in, lru_conv_w, lru_conv_b, lru_a_w, lru_a_b, lru_x_w, lru_x_b, lru_lam, gla_gk_w2,
     gla_gk_b, gla_norm, gdn_conv_w, gdn_a_log, gdn_dt_bias, gdn_norm, w_out, norm_xq, norm_mem,
     w_mq, w_mk, w_mv, w_mo, norm_ffn) = [a[l] for a in p]
    lw = LRU_W
    gk, gv = GLA_HEADS * GLA_DK, GLA_HEADS * GLA_DV
    dh = GDN_HEADS * GDN_DK
    offs = [0]
    for s in (lw, lw, gk, gk, gv, GLA_RANK, gv, dh, dh, dh, GDN_HEADS, GDN_HEADS, dh):
        offs.append(offs[-1] + s)
    col = lambda i: w_in[:, offs[i]:offs[i + 1]]
    w_lru = jnp.concatenate([col(0), col(1)], axis=1)
    w_gla = jnp.concatenate([col(2), col(3), col(4), col(6), _pad_cols(col(5), LANES)], axis=1)
    w_gdn = jnp.concatenate(
        [col(7), col(8), col(9), col(12), _pad_cols(jnp.concatenate([col(10), col(11)], axis=1), LANES)],
        axis=1)
    w_cat = jnp.concatenate([w_lru, w_gla, w_gdn], axis=1).astype(BF16)
    widths = (w_lru.shape[1], w_gla.shape[1], w_gdn.shape[1])
    alog = jnp.zeros((1, LANES), F32).at[0, GDN_HEADS:2 * GDN_HEADS].set(gdn_a_log)
    dtb = jnp.zeros((1, LANES), F32).at[0, GDN_HEADS:2 * GDN_HEADS].set(gdn_dt_bias)
    return dict(
        norm_mix=norm_mix, w_cat=w_cat, widths=widths,
        lru_conv_w=lru_conv_w, lru_conv_b=lru_conv_b,
        lru_a=_block_diag(lru_a_w).astype(BF16), lru_a_b=lru_a_b,
        lru_x=_block_diag(lru_x_w).astype(BF16), lru_x_b=lru_x_b, lru_lam=lru_lam,
        gla_w2=jnp.pad(gla_gk_w2, ((0, LANES - GLA_RANK), (0, 0))).astype(BF16), gla_gk_b=gla_gk_b,
        gla_norm=jnp.tile(gla_norm, GLA_HEADS),
        gdn_conv_w=gdn_conv_w, gdn_alog=alog, gdn_dtb=dtb, gdn_norm=jnp.tile(gdn_norm, GDN_HEADS),
        w_out=w_out.astype(BF16), norm_xq=norm_xq, w_mq=w_mq.astype(BF16), w_mo=w_mo.astype(BF16),
        norm_ffn=norm_ffn)


def _mix_and_attend(x, x_off, grp, l, lp, joint, total_rows):
    batch, seq, n = grp["batch"], grp["seq"], grp["batch"] * grp["seq"]
    lru_h0, lru_buf0, gla_s0, gdn_s0, gdn_buf0 = grp["states"]
    sl = grp["state_layer"](l)
    z_lru, z_gla, z_gdn = _rms_matmul(x, lp["norm_mix"], lp["w_cat"], lp["widths"], "in_proj",
                                      row_off=x_off, n_rows=n)
    y_lru, lru_h, lru_buf = _lru_mixer(
        z_lru, lru_h0, lru_buf0, sl, lp["lru_conv_w"], lp["lru_conv_b"], lp["lru_a"], lp["lru_a_b"],
        lp["lru_x"], lp["lru_x_b"], lp["lru_lam"], batch, seq)
    y_gla, gla_s = _gla_mixer(z_gla.reshape(batch, seq, -1), gla_s0, sl, lp["gla_w2"],
                              lp["gla_gk_b"], lp["gla_norm"], batch, seq)
    y_gdn, gdn_s, gdn_buf = _gdn_mixer(z_gdn.reshape(batch, seq, -1), gdn_s0, gdn_buf0, sl,
                                       lp["gdn_conv_w"], lp["gdn_alog"], lp["gdn_dtb"],
                                       lp["gdn_norm"], batch, seq)
    xn, q = _outproj(x, y_lru, y_gla.reshape(n, -1), y_gdn.reshape(n, -1),
                     lp["w_out"], lp["norm_xq"], lp["w_mq"], row_off=x_off)
    mem_k, mem_v, mem_layer = grp["mem"][l]
    joint = _attention(xn, q, mem_k, mem_v, mem_layer, lp["w_mo"], batch, seq, joint, grp["row_off"],
                       total_rows)
    return joint, (lru_h.reshape(batch, LRU_W), lru_buf, gla_s, gdn_s, gdn_buf)


def _run_layers(groups, layers, ffn, norm_final):
    total_rows = sum(g["batch"] * g["seq"] for g in groups)
    spans = [(g["row_off"], g["batch"] * g["seq"]) for g in groups]
    xs = [(g["x"], 0) for g in groups]
    new_states = [[] for _ in groups]
    outs = None
    for l, lp in enumerate(layers):
        joint = None
        for gi, grp in enumerate(groups):
            joint, st = _mix_and_attend(xs[gi][0], xs[gi][1], grp, l, lp, joint, total_rows)
            new_states[gi].append(st)
        last = l == len(layers) - 1
        kind, fp = ffn[l]
        if kind == "dense":
            joint = _ffn_dense(joint, lp["norm_ffn"], *fp)
            if last:
                outs = [_final_norm(joint, norm_final, off, n) for off, n in spans]
        elif last:
            outs = _moe_ffn(joint, lp["norm_ffn"], *fp, norm_final, True, spans)
        else:
            joint = _moe_ffn(joint, lp["norm_ffn"], *fp, norm_final, False, [(0, total_rows)])[0]
        xs = [(joint, g["row_off"]) for g in groups]
    states = [[jnp.stack(s) for s in zip(*ns)] for ns in new_states]
    return outs, states


def kernel(x_prompt, x_sample, mem_prompt, state_lru_h, state_lru_conv, state_gla, state_gdn, state_gdn_conv, cache_mem_k, cache_mem_v, norm_mix, w_in, lru_conv_w, lru_conv_b, lru_a_w, lru_a_b, lru_x_w, lru_x_b, lru_lam, gla_gk_w2, gla_gk_b, gla_norm, gdn_conv_w, gdn_a_log, gdn_dt_bias, gdn_norm, w_out, norm_xq, norm_mem, w_mq, w_mk, w_mv, w_mo, norm_ffn, w_ff_gate, w_ff_up, w_ff_down, w_router, w_e_gate, w_e_up, w_e_down, norm_final):
    depth = norm_mix.shape[0]
    per_layer = (norm_mix, w_in, lru_conv_w, lru_conv_b, lru_a_w, lru_a_b, lru_x_w, lru_x_b, lru_lam,
                 gla_gk_w2, gla_gk_b, gla_norm, gdn_conv_w, gdn_a_log, gdn_dt_bias, gdn_norm, w_out,
                 norm_xq, norm_mem, w_mq, w_mk, w_mv, w_mo, norm_ffn)
    layers = [_layer_params(l, per_layer) for l in range(depth)]
    ffn = []
    for l in range(depth):
        j = l // 2
        if l % 2 == 0:
            ffn.append(("dense", (w_ff_gate[j], w_ff_up[j], w_ff_down[j])))
        else:
            wr = _pad_cols(w_router[j], LANES)
            whi = wr.astype(BF16)
            wlo = (wr - whi.astype(F32)).astype(BF16)
            ffn.append(("moe", (whi, wlo, w_e_gate[j], w_e_up[j], w_e_down[j])))

    bp, mlen, d = mem_prompt.shape
    w_mkv = jnp.concatenate([w_mk, w_mv], axis=2).astype(BF16)
    pk, pv, p_mem_k, p_mem_v = _mem_kv(mem_prompt, norm_mem, w_mkv)
    zero_state = (jnp.zeros((1, bp, LRU_W), F32), jnp.zeros((1, bp, CONV_K - 1, LRU_W), F32),
                  jnp.zeros((1, bp, GLA_HEADS, GLA_DK, GLA_DV), F32),
                  jnp.zeros((1, bp, GDN_HEADS, GDN_DK, GDN_DV), F32),
                  jnp.zeros((1, bp, CONV_K - 1, 3 * GDN_HEADS * GDN_DK), F32))
    sp = x_prompt.shape[1]
    bs, ss = x_sample.shape[0], x_sample.shape[1]
    groups = [
        dict(x=x_prompt.reshape(bp * sp, d), batch=bp, seq=sp, row_off=0, states=zero_state,
             state_layer=lambda l: 0, mem=[(pk, pv, l) for l in range(depth)]),
        dict(x=x_sample.reshape(bs * ss, d), batch=bs, seq=ss, row_off=bp * sp,
             states=(state_lru_h, state_lru_conv, state_gla, state_gdn, state_gdn_conv),
             state_layer=lambda l: l, mem=[(cache_mem_k, cache_mem_v, l) for l in range(depth)]),
    ]
    (y_p, y_s), (p_st, s_st) = _run_layers(groups, layers, ffn, norm_final)

    return (y_p.reshape(bp, sp, d), y_s.reshape(bs, ss, d), p_st[0], p_st[1], p_st[2], p_st[3], p_st[4],
            p_mem_k, p_mem_v, s_st[0], s_st[1], s_st[2], s_st[3], s_st[4])
```
